```python
import math
import jax, jax.numpy as jnp
from jax import lax
import numpy as np

D_MODEL = 1024
BATCH = 16
SEQ = 4096
DEPTH = 1

HEAD_DIM = 64
SWA_Q_HEADS = 8
SWA_KV_HEADS = 2
SWA_GROUP = SWA_Q_HEADS // SWA_KV_HEADS
SWA_WINDOW = 128
SWA_BLOCK = 128
MOBA_HEADS = 8
MOBA_BLOCK = 256
MOBA_TOPK = 3
MOBA_QCHUNK = 32
N_BRANCHES = 2
REL_BUCKETS = 32
REL_MAX_DIST = 128
N_REL_HEADS = SWA_Q_HEADS + MOBA_HEADS
N_GROUPS = 4
EXPERTS_PER_GROUP = 8
N_EXPERTS = N_GROUPS * EXPERTS_PER_GROUP
EXPERT_TOPK = 2
D_EXPERT = 512
MOE_BLOCK = 256
LN_EPS = 1e-5
DEEPNORM_ALPHA = (2.0 * DEPTH) ** 0.25
DEEPNORM_BETA = (8.0 * DEPTH) ** -0.25
NEG_INF = -1e30
ATTN_SCALE = HEAD_DIM ** -0.5

SWA_Q_W = SWA_Q_HEADS * HEAD_DIM
SWA_KV_W = SWA_KV_HEADS * HEAD_DIM
MOBA_W = MOBA_HEADS * HEAD_DIM
GATE_W = N_BRANCHES * D_MODEL
OFF_SWA_Q = 0
OFF_SWA_K = OFF_SWA_Q + SWA_Q_W
OFF_SWA_V = OFF_SWA_K + SWA_KV_W
OFF_MOBA_Q = OFF_SWA_V + SWA_KV_W
OFF_MOBA_K = OFF_MOBA_Q + MOBA_W
OFF_MOBA_V = OFF_MOBA_K + MOBA_W
OFF_GATE = OFF_MOBA_V + MOBA_W
D_IN_PROJ = OFF_GATE + GATE_W

kernel_name = "hybrid_swa_sinks_moba_gated_deepnorm_hmoe"


def layer_norm(x, gain, bias):
    xf = x.astype(jnp.float32)
    mu = jnp.mean(xf, axis=-1, keepdims=True)
    var = jnp.mean(jnp.square(xf - mu), axis=-1, keepdims=True)
    y = (xf - mu) * lax.rsqrt(var + LN_EPS) * gain.astype(jnp.float32) + bias.astype(jnp.float32)
    return y.astype(x.dtype)


def rel_bucket(dist):
    n = jnp.maximum(dist, 0)
    max_exact = REL_BUCKETS // 2
    nf = jnp.maximum(n, 1).astype(jnp.float32)
    large = max_exact + (jnp.log(nf / max_exact) / math.log(REL_MAX_DIST / max_exact)
                         * (REL_BUCKETS - max_exact)).astype(jnp.int32)
    large = jnp.minimum(large, REL_BUCKETS - 1)
    return jnp.where(n < max_exact, n, large)


def sliding_window_attention(q, k, v, sinks, rel_table_a):
    B, S, _ = q.shape
    nb = S // SWA_BLOCK
    q = q.reshape(B, nb, SWA_BLOCK, SWA_KV_HEADS, SWA_GROUP, HEAD_DIM)
    k = k.reshape(B, nb, SWA_BLOCK, SWA_KV_HEADS, HEAD_DIM)
    v = v.reshape(B, nb, SWA_BLOCK, SWA_KV_HEADS, HEAD_DIM)
    pad = jnp.zeros_like(k[:, :1])
    k_cat = jnp.concatenate([jnp.concatenate([pad, k[:, :-1]], axis=1), k], axis=2)
    v_cat = jnp.concatenate([jnp.concatenate([pad, v[:, :-1]], axis=1), v], axis=2)
    logits = jnp.einsum('bnqkgd,bnskd->bnkgqs', q, k_cat).astype(jnp.float32) * ATTN_SCALE
    qi = jnp.arange(SWA_BLOCK)
    kj = jnp.arange(2 * SWA_BLOCK)
    dist = SWA_BLOCK + qi[:, None] - kj[None, :]
    bias = rel_table_a.astype(jnp.float32)[rel_bucket(dist)]
    bias = bias.reshape(SWA_BLOCK, 2 * SWA_BLOCK, SWA_KV_HEADS, SWA_GROUP).transpose(2, 3, 0, 1)
    k_abs = jnp.arange(nb)[:, None] * SWA_BLOCK - SWA_BLOCK + kj[None, :]
    visible = ((dist >= 0) & (dist < SWA_WINDOW))[None] & (k_abs >= 0)[:, None, :]
    logits = jnp.where(visible[None, :, None, None], logits + bias[None, None], NEG_INF)
    sink = jnp.broadcast_to(
        sinks.astype(jnp.float32).reshape(SWA_KV_HEADS, SWA_GROUP)[None, None, :, :, None, None],
        logits.shape[:-1] + (1,))
    probs = jax.nn.softmax(jnp.concatenate([logits, sink], axis=-1), axis=-1)[..., :-1]
    out = jnp.einsum('bnkgqs,bnskd->bnqkgd', probs.astype(v.dtype), v_cat)
    return out.reshape(B, S, SWA_Q_W)


def moba_attention(q, k, v, rel_table_b):
    B, S, _ = q.shape
    nb = -(-S // MOBA_BLOCK)
    s_pad = nb * MOBA_BLOCK

    def heads(t):
        t = jnp.pad(t, ((0, 0), (0, s_pad - S), (0, 0)))
        return t.reshape(B, s_pad, MOBA_HEADS, HEAD_DIM).transpose(0, 2, 1, 3)

    q, k, v = heads(q), heads(k), heads(v)
    kb = k.reshape(B, MOBA_HEADS, nb, MOBA_BLOCK, HEAD_DIM)
    vb = v.reshape(B, MOBA_HEADS, nb, MOBA_BLOCK, HEAD_DIM)
    k_mean = jnp.mean(kb, axis=3)
    q_blk = jnp.arange(s_pad) // MOBA_BLOCK
    gate = jnp.einsum('bhsd,bhnd->bhsn', q, k_mean).astype(jnp.float32)
    past = jnp.arange(nb)[None, :] < q_blk[:, None]
    gate = jnp.where(past, gate, NEG_INF)
    k_sel = min(MOBA_TOPK, nb)
    _, top_idx = lax.top_k(gate, k_sel)
    sel_valid = jnp.arange(k_sel)[None, :] < q_blk[:, None]
    b_ix = jnp.arange(B)[:, None, None, None]
    h_ix = jnp.arange(MOBA_HEADS)[None, :, None, None]
    table_h = rel_table_b.astype(jnp.float32).T

    def one_chunk(c):
        start = c * MOBA_QCHUNK
        qc = lax.dynamic_slice_in_dim(q, start, MOBA_QCHUNK, axis=2)
        idx = lax.dynamic_slice_in_dim(top_idx, start, MOBA_QCHUNK, axis=2)
        valid = lax.dynamic_slice_in_dim(sel_valid, start, MOBA_QCHUNK, axis=0)
        blk = start // MOBA_BLOCK
        k_own = lax.dynamic_index_in_dim(kb, blk, axis=2, keepdims=False)
        v_own = lax.dynamic_index_in_dim(vb, blk, axis=2, keepdims=False)
        k_g = kb[b_ix, h_ix, idx]
        v_g = vb[b_ix, h_ix, idx]
        q_pos = start + jnp.arange(MOBA_QCHUNK)
        d_own = q_pos[:, None] - (blk * MOBA_BLOCK + jnp.arange(MOBA_BLOCK))[None, :]
        l_own = jnp.einsum('bhcd,bhtd->bhct', qc, k_own).astype(jnp.float32) * ATTN_SCALE
        l_own = jnp.where(d_own >= 0, l_own + table_h[:, rel_bucket(d_own)][None], NEG_INF)
        sel_pos = idx[..., None] * MOBA_BLOCK + jnp.arange(MOBA_BLOCK)
        d_sel = q_pos[None, None, :, None, None] - sel_pos
        l_sel = jnp.einsum('bhcd,bhcktd->bhckt', qc, k_g).astype(jnp.float32) * ATTN_SCALE
        l_sel = l_sel + table_h[h_ix[..., None], rel_bucket(d_sel)]
        l_sel = jnp.where(valid[None, None, :, :, None], l_sel, NEG_INF)
        logits = jnp.concatenate([l_own, l_sel.reshape(B, MOBA_HEADS, MOBA_QCHUNK, k_sel * MOBA_BLOCK)], axis=-1)
        p = jax.nn.softmax(logits, axis=-1).astype(v.dtype)
        p_own = p[..., :MOBA_BLOCK]
        p_sel = p[..., MOBA_BLOCK:].reshape(B, MOBA_HEADS, MOBA_QCHUNK, k_sel, MOBA_BLOCK)
        return (jnp.einsum('bhct,bhtd->bhcd', p_own, v_own)
                + jnp.einsum('bhckt,bhcktd->bhcd', p_sel, v_g))

    out = lax.map(one_chunk, jnp.arange(s_pad // MOBA_QCHUNK))
    out = out.transpose(1, 0, 3, 2, 4).reshape(B, s_pad, MOBA_W)
    return out[:, :S]


def hierarchical_moe(h, w_group, b_group, w_expert, b_expert, w_gate, w_up, w_down):
    B, S, D = h.shape
    T = B * S
    hf = h.reshape(T, D)
    g_logits = (hf @ w_group + b_group).astype(jnp.float32)
    g_idx = jnp.argmax(g_logits, axis=-1)
    g_prob = jnp.take_along_axis(jax.nn.softmax(g_logits, axis=-1), g_idx[:, None], axis=1)
    e_logits = (hf @ w_expert + b_expert).astype(jnp.float32).reshape(T, N_GROUPS, EXPERTS_PER_GROUP)
    e_logits = jnp.take_along_axis(e_logits, g_idx[:, None, None], axis=1)[:, 0]
    e_top, e_loc = lax.top_k(e_logits, EXPERT_TOPK)
    weights = g_prob * jax.nn.softmax(e_top, axis=-1)
    flat_e = (g_idx[:, None] * EXPERTS_PER_GROUP + e_loc).reshape(-1)
    flat_w = weights.reshape(-1)
    n_assign = T * EXPERT_TOPK
    order = jnp.argsort(flat_e)
    sorted_e = flat_e[order]
    tok = order // EXPERT_TOPK
    sizes = jnp.bincount(flat_e, length=N_EXPERTS)
    start_unpadded = jnp.cumsum(sizes) - sizes
    padded = ((sizes + MOE_BLOCK - 1) // MOE_BLOCK) * MOE_BLOCK
    padded_end = jnp.cumsum(padded)
    padded_start = padded_end - padded
    dest = padded_start[sorted_e] + jnp.arange(n_assign) - start_unpadded[sorted_e]
    cap = -(-n_assign // MOE_BLOCK) * MOE_BLOCK + N_EXPERTS * MOE_BLOCK
    n_blocks = cap // MOE_BLOCK
    x_buf = jnp.zeros((cap, D), h.dtype).at[dest].set(hf[tok])
    blk_expert = jnp.minimum(
        jnp.searchsorted(padded_end, jnp.arange(n_blocks) * MOE_BLOCK, side='right'), N_EXPERTS - 1)

    def expert_block(args):
        xb, e = args
        act = jax.nn.silu(xb @ w_gate[e]) * (xb @ w_up[e])
        return act @ w_down[e]

    y_buf = lax.map(expert_block, (x_buf.reshape(n_blocks, MOE_BLOCK, D), blk_expert)).reshape(cap, D)
    contrib = y_buf[dest] * flat_w[order][:, None].astype(h.dtype)
    y = jnp.zeros_like(hf).at[tok].add(contrib)
    return y.reshape(B, S, D)


def setup_inputs(seed: int = 0) -> dict:
    key = jax.random.key(seed)
    ks = jax.random.split(key, 24)
    f32 = jnp.float32

    def nrm(k, shape, scale):
        return jax.random.normal(k, shape, f32) * scale

    s_d = D_MODEL ** -0.5
    beta = DEEPNORM_BETA
    x = jax.random.normal(ks[0], (BATCH, SEQ, D_MODEL), f32)
    w_in = jnp.concatenate([
        nrm(ks[1], (DEPTH, D_MODEL, SWA_Q_W), s_d),
        nrm(ks[2], (DEPTH, D_MODEL, SWA_KV_W), s_d),
        nrm(ks[3], (DEPTH, D_MODEL, SWA_KV_W), s_d * beta),
        nrm(ks[4], (DEPTH, D_MODEL, MOBA_W), s_d),
        nrm(ks[5], (DEPTH, D_MODEL, MOBA_W), s_d),
        nrm(ks[6], (DEPTH, D_MODEL, MOBA_W), s_d * beta),
        nrm(ks[7], (DEPTH, D_MODEL, GATE_W), s_d),
    ], axis=-1)
    b_in = nrm(ks[8], (DEPTH, D_IN_PROJ), 0.02)
    attn_sinks = nrm(ks[9], (DEPTH, SWA_Q_HEADS), 0.5)
    rel_bias_table = nrm(ks[10], (REL_BUCKETS, N_REL_HEADS), 0.2)
    w_branch_swa = nrm(ks[11], (DEPTH, SWA_Q_W, D_MODEL), SWA_Q_W ** -0.5 * beta)
    w_branch_moba = nrm(ks[12], (DEPTH, MOBA_W, D_MODEL), MOBA_W ** -0.5 * beta)
    w_out = nrm(ks[13], (DEPTH, D_MODEL, D_MODEL), s_d * beta)
    ln1_gain = 1.0 + nrm(ks[14], (DEPTH, D_MODEL), 0.02)
    ln1_bias = nrm(ks[15], (DEPTH, D_MODEL), 0.02)
    w_group_router = nrm(ks[16], (DEPTH, D_MODEL, N_GROUPS), s_d)
    b_group_router = nrm(ks[17], (DEPTH, N_GROUPS), 0.01)
    w_expert_router = nrm(ks[18], (DEPTH, D_MODEL, N_EXPERTS), s_d)
    b_expert_router = nrm(ks[19], (DEPTH, N_EXPERTS), 0.01)
    w_expert_gate = nrm(ks[20], (DEPTH, N_EXPERTS, D_MODEL, D_EXPERT), s_d * beta)
    w_expert_up = nrm(ks[21], (DEPTH, N_EXPERTS, D_MODEL, D_EXPERT), s_d * beta)
    w_expert_down = nrm(ks[22], (DEPTH, N_EXPERTS, D_EXPERT, D_MODEL), D_EXPERT ** -0.5 * beta)
    k_ln = jax.random.split(ks[23], 2)
    ln2_gain = 1.0 + nrm(k_ln[0], (DEPTH, D_MODEL), 0.02)
    ln2_bias = nrm(k_ln[1], (DEPTH, D_MODEL), 0.02)
    return {"x": x, "w_in": w_in, "b_in": b_in, "attn_sinks": attn_sinks,
            "rel_bias_table": rel_bias_table, "w_branch_swa": w_branch_swa,
            "w_branch_moba": w_branch_moba, "w_out": w_out, "ln1_gain": ln1_gain,
            "ln1_bias": ln1_bias, "w_group_router": w_group_router,
            "b_group_router": b_group_router, "w_expert_router": w_expert_router,
            "b_expert_router": b_expert_router, "w_expert_gate": w_expert_gate,
            "w_expert_up": w_expert_up, "w_expert_down": w_expert_down,
            "ln2_gain": ln2_gain, "ln2_bias": ln2_bias}


def reference(x, w_in, b_in, attn_sinks, rel_bias_table, w_branch_swa, w_branch_moba, w_out,
              ln1_gain, ln1_bias, w_group_router, b_group_router, w_expert_router, b_expert_router,
              w_expert_gate, w_expert_up, w_expert_down, ln2_gain, ln2_bias):
    rel_a = rel_bias_table[:, :SWA_Q_HEADS]
    rel_b = rel_bias_table[:, SWA_Q_HEADS:]
    for layer in range(DEPTH):
        proj = x @ w_in[layer] + b_in[layer]
        y_a = sliding_window_attention(proj[..., OFF_SWA_Q:OFF_SWA_K], proj[..., OFF_SWA_K:OFF_SWA_V],
                                       proj[..., OFF_SWA_V:OFF_MOBA_Q], attn_sinks[layer], rel_a)
        y_b = moba_attention(proj[..., OFF_MOBA_Q:OFF_MOBA_K], proj[..., OFF_MOBA_K:OFF_MOBA_V],
                             proj[..., OFF_MOBA_V:OFF_GATE], rel_b)
        gates = jax.nn.sigmoid(proj[..., OFF_GATE:])
        merged = (gates[..., :D_MODEL] * (y_a @ w_branch_swa[layer])
                  + gates[..., D_MODEL:] * (y_b @ w_branch_moba[layer]))
        mixed = merged @ w_out[layer]
        x = layer_norm(DEEPNORM_ALPHA * x + mixed, ln1_gain[layer], ln1_bias[layer])
        moe = hierarchical_moe(x, w_group_router[layer], b_group_router[layer], w_expert_router[layer],
                               b_expert_router[layer], w_expert_gate[layer], w_expert_up[layer],
                               w_expert_down[layer])
        x = layer_norm(DEEPNORM_ALPHA * x + moe, ln2_gain[layer], ln2_bias[layer])
    return x
```

```python
import functools
import math

import numpy as np
import jax
import jax.numpy as jnp
from jax import lax
from jax.experimental import pallas as pl
from jax.experimental.pallas import tpu as pltpu

D_MODEL = 1024
HEAD_DIM = 64
SWA_Q_HEADS = 8
SWA_KV_HEADS = 2
SWA_GROUP = SWA_Q_HEADS // SWA_KV_HEADS
SWA_WINDOW = 128
SWA_BLOCK = 128
MOBA_HEADS = 8
MOBA_BLOCK = 256
MOBA_TOPK = 3
REL_BUCKETS = 32
REL_MAX_DIST = 128
N_GROUPS = 4
EXPERTS_PER_GROUP = 8
N_EXPERTS = N_GROUPS * EXPERTS_PER_GROUP
EXPERT_TOPK = 2
D_EXPERT = 512
LN_EPS = 1e-5
DEPTH = 1
DEEPNORM_ALPHA = (2.0 * DEPTH) ** 0.25
NEG_INF = -1e30
ATTN_SCALE = HEAD_DIM ** -0.5

SWA_Q_W = SWA_Q_HEADS * HEAD_DIM
SWA_KV_W = SWA_KV_HEADS * HEAD_DIM
MOBA_W = MOBA_HEADS * HEAD_DIM
OFF_SWA_Q = 0
OFF_SWA_K = OFF_SWA_Q + SWA_Q_W
OFF_SWA_V = OFF_SWA_K + SWA_KV_W
OFF_MOBA_Q = OFF_SWA_V + SWA_KV_W
OFF_MOBA_K = OFF_MOBA_Q + MOBA_W
OFF_MOBA_V = OFF_MOBA_K + MOBA_W
OFF_GATE = OFF_MOBA_V + MOBA_W

LANES = 128
QK_SWA_Q = 0
QK_SWA_K = QK_SWA_Q + SWA_Q_W
QK_MOBA_Q = QK_SWA_K + SWA_KV_HEADS * LANES
QK_MOBA_K = QK_MOBA_Q + MOBA_W
QK_W = QK_MOBA_K + MOBA_W
VT_SWA = 0
VT_MOBA = VT_SWA + SWA_KV_W
VT_W = VT_MOBA + MOBA_W

MOE_TM = 256
ROUTER_ROWS = 8 + N_EXPERTS
VMEM_LIMIT = 56 * 1024 * 1024

_NT = (((1,), (1,)), ((), ()))


def _rel_bucket_np(dist):
    n = np.maximum(dist, 0)
    max_exact = REL_BUCKETS // 2
    nf = np.maximum(n, 1).astype(np.float32)
    large = max_exact + (np.log(nf / np.float32(max_exact)) / np.float32(math.log(REL_MAX_DIST / max_exact))
                         * np.float32(REL_BUCKETS - max_exact)).astype(np.int32)
    large = np.minimum(large, REL_BUCKETS - 1)
    return np.where(n < max_exact, n, large).astype(np.int32)


def _inproj_kernel(x_ref, wn_ref, bn_ref, wt_ref, bt_ref, qk_ref, vt_ref):
    xb = x_ref[0].astype(jnp.bfloat16)
    qk = jnp.dot(xb, wn_ref[...], preferred_element_type=jnp.float32) + bn_ref[...]
    qk_ref[0] = qk.astype(jnp.bfloat16)
    vt = lax.dot_general(wt_ref[...], xb, _NT, preferred_element_type=jnp.float32) + bt_ref[...]
    vt_ref[0] = vt.astype(jnp.bfloat16)


def _inproj(x, wn, bn, wt, bt, tm=512):
    B, S, D = x.shape
    return pl.pallas_call(
        _inproj_kernel,
        grid=(B, S // tm),
        in_specs=[
            pl.BlockSpec((1, tm, D), lambda b, i: (b, i, 0)),
            pl.BlockSpec((D, QK_W), lambda b, i: (0, 0)),
            pl.BlockSpec((1, QK_W), lambda b, i: (0, 0)),
            pl.BlockSpec((VT_W, D), lambda b, i: (0, 0)),
            pl.BlockSpec((VT_W, 1), lambda b, i: (0, 0)),
        ],
        out_specs=[
            pl.BlockSpec((1, tm, QK_W), lambda b, i: (b, i, 0)),
            pl.BlockSpec((1, VT_W, tm), lambda b, i: (b, 0, i)),
        ],
        out_shape=[
            jax.ShapeDtypeStruct((B, S, QK_W), jnp.bfloat16),
            jax.ShapeDtypeStruct((B, VT_W, S), jnp.bfloat16),
        ],
        compiler_params=pltpu.CompilerParams(
            dimension_semantics=("parallel", "parallel"), vmem_limit_bytes=VMEM_LIMIT),
        name="inproj",
    )(x, wn, bn, wt, bt)


def _swa_kernel(sink_ref, q_ref, k_ref, vt_ref, bias_ref, o_ref, vaug_ref):
    S = q_ref.shape[1]
    nblk = S // SWA_BLOCK
    half = HEAD_DIM
    vt = vt_ref[0]
    rows = lax.broadcasted_iota(jnp.int32, vt.shape, 0)
    one = jnp.ones_like(vt)
    vaug_ref[0] = jnp.where(rows < half, vt, one)
    vaug_ref[1] = jnp.where(rows < half, one, vt)
    lane = lax.broadcasted_iota(jnp.int32, (SWA_BLOCK, LANES), 1)
    orow = lax.broadcasted_iota(jnp.int32, (LANES, SWA_BLOCK), 0)

    def block(qs, ks, nk, boff):
        outs = []
        for pair in range(SWA_Q_HEADS // 2):
            qblk = q_ref[0, pl.ds(qs, SWA_BLOCK), pair * LANES:(pair + 1) * LANES]
            kv = (2 * pair) // SWA_GROUP
            kblk = k_ref[0, pl.ds(ks, nk), kv * LANES:(kv + 1) * LANES]
            vaug = vaug_ref[kv, :, pl.ds(ks, nk)]
            res = []
            for sub in range(2):
                h = 2 * pair + sub
                qh = jnp.where((lane < half) if sub == 0 else (lane >= half), qblk, jnp.zeros_like(qblk))
                s = lax.dot_general(kblk, qh, _NT, preferred_element_type=jnp.float32)
                s = s + bias_ref[h, boff:boff + nk, :]
                sink = sink_ref[h]
                m = jnp.maximum(jnp.max(s, axis=0, keepdims=True), sink)
                p = jnp.exp(s - m)
                r = jnp.dot(vaug, p.astype(jnp.bfloat16), preferred_element_type=jnp.float32)
                if kv == 0:
                    l = r[half:half + 1, :]
                else:
                    l = r[0:1, :]
                l = l + jnp.exp(sink - m)
                res.append((r / l, kv))
            (r0, kv0), (r1, _) = res
            if kv0 == 0:
                top = r0[0:half, :]
                bot = r1[0:half, :]
            else:
                top = r0[half:, :]
                bot = r1[half:, :]
            outs.append(jnp.concatenate([top, bot], axis=0))
        ot = jnp.concatenate(outs, axis=0)
        o_ref[0, pl.ds(qs, SWA_BLOCK), :] = ot.T.astype(jnp.bfloat16)

    del orow
    block(0, 0, SWA_BLOCK, SWA_BLOCK)

    def body(n, carry):
        qs = pl.multiple_of(n * SWA_BLOCK, SWA_BLOCK)
        ks = pl.multiple_of(qs - SWA_BLOCK, SWA_BLOCK)
        block(qs, ks, 2 * SWA_BLOCK, 0)
        return carry

    lax.fori_loop(1, nblk, body, 0)


def _swa(sinks, qk, vt, bias):
    B, S, _ = qk.shape
    nq = SWA_Q_W // LANES
    return pl.pallas_call(
        _swa_kernel,
        grid_spec=pltpu.PrefetchScalarGridSpec(
            num_scalar_prefetch=1,
            grid=(B,),
            in_specs=[
                pl.BlockSpec((1, S, SWA_Q_W), lambda b, s: (b, 0, QK_SWA_Q // SWA_Q_W)),
                pl.BlockSpec((1, S, 2 * LANES), lambda b, s: (b, 0, QK_SWA_K // (2 * LANES))),
                pl.BlockSpec((1, SWA_KV_W, S), lambda b, s: (b, VT_SWA // SWA_KV_W, 0)),
                pl.BlockSpec((SWA_Q_HEADS, 2 * SWA_BLOCK, SWA_BLOCK), lambda b, s: (0, 0, 0)),
            ],
            out_specs=pl.BlockSpec((1, S, SWA_Q_W), lambda b, s: (b, 0, 0)),
            scratch_shapes=[pltpu.VMEM((SWA_KV_HEADS, SWA_KV_W, S), jnp.bfloat16)],
        ),
        out_shape=jax.ShapeDtypeStruct((B, S, SWA_Q_W), jnp.bfloat16),
        compiler_params=pltpu.CompilerParams(
            dimension_semantics=("parallel",), vmem_limit_bytes=VMEM_LIMIT),
        name="swa",
    )(sinks, qk, qk, vt, bias)


def _moba_kernel(q_ref, k_ref, vt_ref, bias_ref, o_ref, vaug_ref, gate_ref, rank_ref, sel_ref):
    S = q_ref.shape[1]
    nb = S // MOBA_BLOCK
    half = HEAD_DIM
    BLK = MOBA_BLOCK

    vt = vt_ref[0]
    rows = lax.broadcasted_iota(jnp.int32, vt.shape, 0)
    one = jnp.ones_like(vt)
    vaug_ref[0] = jnp.where(rows < half, vt, one)
    vaug_ref[1] = jnp.where(rows < half, one, vt)

    kf = k_ref[0].astype(jnp.float32).reshape(nb, BLK, LANES)
    kmean = jnp.sum(kf, axis=1) * (1.0 / BLK)
    k_hi = kmean.astype(jnp.bfloat16)
    k_lo = (kmean - k_hi.astype(jnp.float32)).astype(jnp.bfloat16)
    kcat = jnp.concatenate([k_hi, k_lo], axis=0)

    lane_q = lax.broadcasted_iota(jnp.int32, (S, LANES), 1)
    brow = lax.broadcasted_iota(jnp.int32, (nb, S), 0)
    qblk_of = lax.broadcasted_iota(jnp.int32, (nb, S), 1) // BLK
    qall = q_ref[0]
    for h in range(2):
        qh = jnp.where((lane_q < half) if h == 0 else (lane_q >= half), qall, jnp.zeros_like(qall))
        g2 = lax.dot_general(kcat, qh, _NT, preferred_element_type=jnp.float32)
        gate_ref[...] = g2[0:nb] + g2[nb:2 * nb]
        rank_ref[...] = jnp.zeros_like(rank_ref)
        for m in range(nb - 1):
            lo = (m + 1) * BLK
            G = gate_ref[:, lo:]
            gm = gate_ref[m:m + 1, lo:]
            ge = jnp.where(gm >= G, 1.0, 0.0)
            gt = jnp.where(gm > G, 1.0, 0.0)
            brow_m = lax.broadcasted_iota(jnp.int32, (nb, S - lo), 0)
            rank_ref[:, lo:] += jnp.where(brow_m > m, ge, gt)
        sel = (brow < qblk_of) & (rank_ref[...] < float(MOBA_TOPK))
        sel_ref[h] = jnp.where(sel, 0.0, NEG_INF)

    lane = lax.broadcasted_iota(jnp.int32, (BLK, LANES), 1)
    orow = lax.broadcasted_iota(jnp.int32, (LANES, BLK), 0)

    def tile(h, qh, ks, m, acc, bias, selrow):
        kblk = k_ref[0, pl.ds(ks, BLK), :]
        s = lax.dot_general(kblk, qh, _NT, preferred_element_type=jnp.float32)
        if bias is not None:
            s = s + bias
        cm = jnp.max(s, axis=0, keepdims=True)
        if selrow is not None:
            cm = cm + selrow
        m_new = jnp.maximum(m, cm)
        off = m_new if selrow is None else m_new - selrow
        p = jnp.exp(s - off)
        alpha = jnp.exp(m - m_new)
        pv = jnp.dot(vaug_ref[h, :, pl.ds(ks, BLK)], p.astype(jnp.bfloat16),
                     preferred_element_type=jnp.float32)
        return m_new, acc * alpha + pv

    def qblock(qb, qs, with_prev):
        qblk = q_ref[0, pl.ds(qs, BLK), :]
        accs = []
        for h in range(2):
            qh = jnp.where((lane < half) if h == 0 else (lane >= half), qblk, jnp.zeros_like(qblk))
            m = jnp.full((1, BLK), NEG_INF, jnp.float32)
            acc = jnp.zeros((LANES, BLK), jnp.float32)
            m, acc = tile(h, qh, qs, m, acc, bias_ref[h, 0], None)
            if with_prev:
                ps = pl.multiple_of(qs - BLK, BLK)
                selrow = sel_ref[h, pl.ds(qb - 1, 1), pl.ds(qs, BLK)]
                m, acc = tile(h, qh, ps, m, acc, bias_ref[h, 1], selrow)

                def far(kb, carry):
                    ks = pl.multiple_of(kb * BLK, BLK)
                    srow = sel_ref[h, pl.ds(kb, 1), pl.ds(qs, BLK)]
                    return tile(h, qh, ks, carry[0], carry[1], None, srow)

                m, acc = lax.fori_loop(0, qb - 1, far, (m, acc))
            accs.append(acc)
        l0 = accs[0][half:half + 1, :]
        l1 = accs[1][0:1, :]
        ot = jnp.where(orow < half, accs[0] / l0, accs[1] / l1)
        o_ref[0, pl.ds(qs, BLK), :] = ot.T.astype(jnp.bfloat16)

    qblock(0, 0, False)

    def body(qb, carry):
        qblock(qb, pl.multiple_of(qb * BLK, BLK), True)
        return carry

    lax.fori_loop(1, nb, body, 0)


def _moba(qk, vt, bias):
    B, S, _ = qk.shape
    nb = S // MOBA_BLOCK
    npair = MOBA_W // LANES
    return pl.pallas_call(
        _moba_kernel,
        grid=(B, npair),
        in_specs=[
            pl.BlockSpec((1, S, LANES), lambda b, p: (b, 0, QK_MOBA_Q // LANES + p)),
            pl.BlockSpec((1, S, LANES), lambda b, p: (b, 0, QK_MOBA_K // LANES + p)),
            pl.BlockSpec((1, LANES, S), lambda b, p: (b, VT_MOBA // LANES + p, 0)),
            pl.BlockSpec((2, 2, MOBA_BLOCK, MOBA_BLOCK), lambda b, p: (p, 0, 0, 0)),
        ],
        out_specs=pl.BlockSpec((1, S, LANES), lambda b, p: (b, 0, p)),
        out_shape=jax.ShapeDtypeStruct((B, S, MOBA_W), jnp.bfloat16),
        scratch_shapes=[
            pltpu.VMEM((2, LANES, S), jnp.bfloat16),
            pltpu.VMEM((nb, S), jnp.float32),
            pltpu.VMEM((nb, S), jnp.float32),
            pltpu.VMEM((2, nb, S), jnp.float32),
        ],
        compiler_params=pltpu.CompilerParams(
            dimension_semantics=("parallel", "parallel"), vmem_limit_bytes=VMEM_LIMIT),
        name="moba",
    )(qk, qk, vt, bias)


def _layer_norm(h, gain, bias):
    mu = jnp.mean(h, axis=-1, keepdims=True)
    c = h - mu
    var = jnp.mean(c * c, axis=-1, keepdims=True)
    return c * lax.rsqrt(var + LN_EPS) * gain + bias


def _merge_kernel(x_ref, ya_ref, yb_ref, wg_ref, bg_ref, wa_ref, wb_ref, wo_ref, g1_ref, b1_ref,
                  wr_ref, br_ref, x1_ref, x1b_ref, r_ref):
    x = x_ref[...]
    xb = x.astype(jnp.bfloat16)
    z = jnp.dot(xb, wg_ref[...], preferred_element_type=jnp.float32) + bg_ref[...]
    gates = 1.0 / (1.0 + jnp.exp(-z))
    pa = jnp.dot(ya_ref[...], wa_ref[...], preferred_element_type=jnp.float32)
    pb = jnp.dot(yb_ref[...], wb_ref[...], preferred_element_type=jnp.float32)
    merged = gates[:, :D_MODEL] * pa + gates[:, D_MODEL:] * pb
    mixed = jnp.dot(merged.astype(jnp.bfloat16), wo_ref[...], preferred_element_type=jnp.float32)
    x1 = _layer_norm(DEEPNORM_ALPHA * x + mixed, g1_ref[...], b1_ref[...])
    x1_ref[...] = x1
    x1_hi = x1.astype(jnp.bfloat16)
    x1b_ref[...] = x1_hi
    x1_lo = (x1 - x1_hi.astype(jnp.float32)).astype(jnp.bfloat16)

    R = ROUTER_ROWS
    l1 = lax.dot_general(wr_ref[...], x1_hi, _NT, preferred_element_type=jnp.float32)
    l2 = lax.dot_general(wr_ref[0:R, :], x1_lo, _NT, preferred_element_type=jnp.float32)
    L = l1[0:R] + l1[R:2 * R] + l2 + br_ref[...]
    tm = x.shape[0]
    row = lax.broadcasted_iota(jnp.int32, (8, tm), 0)
    big = jnp.float32(-3e38)
    gl = jnp.where(row < N_GROUPS, L[0:8], big)
    gmax = jnp.max(gl, axis=0, keepdims=True)
    g_idx = jnp.min(jnp.where(gl == gmax, row, 8), axis=0, keepdims=True)
    gsum = jnp.sum(jnp.where(row < N_GROUPS, jnp.exp(gl - gmax), 0.0), axis=0, keepdims=True)
    g_prob = 1.0 / gsum
    E = L[8 + 8 * (N_GROUPS - 1):8 + 8 * N_GROUPS]
    for g in range(N_GROUPS - 2, -1, -1):
        E = jnp.where(g_idx == g, L[8 + 8 * g:16 + 8 * g], E)
    t0 = jnp.max(E, axis=0, keepdims=True)
    loc0 = jnp.min(jnp.where(E == t0, row, 8), axis=0, keepdims=True)
    E2 = jnp.where(row == loc0, big, E)
    t1 = jnp.max(E2, axis=0, keepdims=True)
    loc1 = jnp.min(jnp.where(E2 == t1, row, 8), axis=0, keepdims=True)
    ex = jnp.exp(t1 - t0)
    w0 = g_prob / (1.0 + ex)
    w1 = g_prob * ex / (1.0 + ex)
    e0 = (g_idx * EXPERTS_PER_GROUP + loc0).astype(jnp.float32)
    e1 = (g_idx * EXPERTS_PER_GROUP + loc1).astype(jnp.float32)
    r_ref[...] = jnp.where(row == 0, e0, jnp.where(row == 1, e1, jnp.where(row == 2, w0, jnp.where(row == 3, w1, 0.0))))


def _merge(x2, ya, yb, wg, bg, wa, wb, wo, g1, b1, wr, br, tm=512):
    T, D = x2.shape
    const = lambda i: (0, 0)
    return pl.pallas_call(
        _merge_kernel,
        grid=(T // tm,),
        in_specs=[
            pl.BlockSpec((tm, D), lambda i: (i, 0)),
            pl.BlockSpec((tm, SWA_Q_W), lambda i: (i, 0)),
            pl.BlockSpec((tm, MOBA_W), lambda i: (i, 0)),
            pl.BlockSpec((D, 2 * D), const),
            pl.BlockSpec((1, 2 * D), const),
            pl.BlockSpec((SWA_Q_W, D), const),
            pl.BlockSpec((MOBA_W, D), const),
            pl.BlockSpec((D, D), const),
            pl.BlockSpec((1, D), const),
            pl.BlockSpec((1, D), const),
            pl.BlockSpec((2 * ROUTER_ROWS, D), const),
            pl.BlockSpec((ROUTER_ROWS, 1), const),
        ],
        out_specs=[
            pl.BlockSpec((tm, D), lambda i: (i, 0)),
            pl.BlockSpec((tm, D), lambda i: (i, 0)),
            pl.BlockSpec((8, tm), lambda i: (0, i)),
        ],
        out_shape=[
            jax.ShapeDtypeStruct((T, D), jnp.float32),
            jax.ShapeDtypeStruct((T, D), jnp.bfloat16),
            jax.ShapeDtypeStruct((8, T), jnp.float32),
        ],
        compiler_params=pltpu.CompilerParams(
            dimension_semantics=("parallel",), vmem_limit_bytes=VMEM_LIMIT),
        name="merge",
    )(x2, ya, yb, wg, bg, wa, wb, wo, g1, b1, wr, br)


def _expert_kernel(be_ref, nu_ref, x_ref, wgu_ref, wd_ref, y_ref):
    i = pl.program_id(0)

    @pl.when(i < nu_ref[0])
    def _():
        gu = jnp.dot(x_ref[...], wgu_ref[0], preferred_element_type=jnp.float32)
        g = gu[:, :D_EXPERT]
        u = gu[:, D_EXPERT:]
        act = (g / (1.0 + jnp.exp(-g))) * u
        y = jnp.dot(act.astype(jnp.bfloat16), wd_ref[0], preferred_element_type=jnp.float32)
        y_ref[...] = y.astype(jnp.bfloat16)

    @pl.when(i >= nu_ref[0])
    def _():
        y_ref[...] = jnp.zeros_like(y_ref)


def _experts(blk_expert, n_used, xs, wgu, wd):
    cap, D = xs.shape
    n_blocks = cap // MOE_TM
    return pl.pallas_call(
        _expert_kernel,
        grid_spec=pltpu.PrefetchScalarGridSpec(
            num_scalar_prefetch=2,
            grid=(n_blocks,),
            in_specs=[
                pl.BlockSpec((MOE_TM, D), lambda i, be, nu: (i, 0)),
                pl.BlockSpec((1, D, 2 * D_EXPERT), lambda i, be, nu: (be[i], 0, 0)),
                pl.BlockSpec((1, D_EXPERT, D), lambda i, be, nu: (be[i], 0, 0)),
            ],
            out_specs=pl.BlockSpec((MOE_TM, D), lambda i, be, nu: (i, 0)),
        ),
        out_shape=jax.ShapeDtypeStruct((cap, D), jnp.bfloat16),
        compiler_params=pltpu.CompilerParams(
            dimension_semantics=("arbitrary",), vmem_limit_bytes=VMEM_LIMIT),
        name="experts",
    )(blk_expert, n_used, xs, wgu, wd)


def _final_kernel(x1_ref, yp_ref, w_ref, g2_ref, b2_ref, o_ref):
    w = w_ref[...]
    y0 = yp_ref[:, :D_MODEL].astype(jnp.float32)
    y1 = yp_ref[:, D_MODEL:].astype(jnp.float32)
    moe = y0 * w[:, 0:1] + y1 * w[:, 1:2]
    o_ref[...] = _layer_norm(DEEPNORM_ALPHA * x1_ref[...] + moe, g2_ref[...], b2_ref[...])


def _final(x1, ypair, w2, g2, b2, tm=1024):
    T, D = x1.shape
    const = lambda i: (0, 0)
    return pl.pallas_call(
        _final_kernel,
        grid=(T // tm,),
        in_specs=[
            pl.BlockSpec((tm, D), lambda i: (i, 0)),
            pl.BlockSpec((tm, 2 * D), lambda i: (i, 0)),
            pl.BlockSpec((tm, EXPERT_TOPK), lambda i: (i, 0)),
            pl.BlockSpec((1, D), const),
            pl.BlockSpec((1, D), const),
        ],
        out_specs=pl.BlockSpec((tm, D), lambda i: (i, 0)),
        out_shape=jax.ShapeDtypeStruct((T, D), jnp.float32),
        compiler_params=pltpu.CompilerParams(
            dimension_semantics=("parallel",), vmem_limit_bytes=VMEM_LIMIT),
        name="final",
    )(x1, ypair, w2, g2, b2)


def _attention_bias_tables(rel_bias_table):
    rel = rel_bias_table.astype(jnp.float32)
    rel_a = rel[:, :SWA_Q_HEADS]
    rel_b = rel[:, SWA_Q_HEADS:]
    kj = np.arange(2 * SWA_BLOCK)[:, None]
    qi = np.arange(SWA_BLOCK)[None, :]
    dist = SWA_BLOCK + qi - kj
    vis = (dist >= 0) & (dist < SWA_WINDOW)
    bias_a = jnp.where(vis[None], rel_a.T[:, _rel_bucket_np(dist)], NEG_INF)
    j = np.arange(MOBA_BLOCK)[:, None]
    i = np.arange(MOBA_BLOCK)[None, :]
    d_own = i - j
    d_prev = MOBA_BLOCK + i - j
    tb = rel_b.T
    far = tb[:, REL_BUCKETS - 1][:, None, None]
    own = jnp.where((d_own >= 0)[None], tb[:, _rel_bucket_np(d_own)] - far, NEG_INF)
    prev = tb[:, _rel_bucket_np(d_prev)] - far
    bias_b = jnp.stack([own, prev], axis=1)
    return bias_a, bias_b


def _dispatch_plan(e0, e1, n_tok):
    n_assign = n_tok * EXPERT_TOPK
    flat_e = jnp.stack([e0, e1], axis=1).reshape(-1)
    order = jnp.argsort(flat_e)
    sorted_e = flat_e[order]
    sizes = jnp.sum(flat_e[:, None] == jnp.arange(N_EXPERTS)[None, :], axis=0).astype(jnp.int32)
    start_unpadded = jnp.cumsum(sizes) - sizes
    padded = ((sizes + MOE_TM - 1) // MOE_TM) * MOE_TM
    padded_end = jnp.cumsum(padded)
    padded_start = padded_end - padded
    cap = -(-n_assign // MOE_TM) * MOE_TM + N_EXPERTS * MOE_TM
    n_blocks = cap // MOE_TM
    blk_expert = jnp.minimum(
        jnp.searchsorted(padded_end, jnp.arange(n_blocks, dtype=jnp.int32) * MOE_TM, side='right'),
        N_EXPERTS - 1).astype(jnp.int32)
    slot_e = jnp.repeat(blk_expert, MOE_TM)
    pos = jnp.arange(cap, dtype=jnp.int32) - padded_start[slot_e]
    valid = pos < sizes[slot_e]
    src = order[jnp.clip(start_unpadded[slot_e] + pos, 0, n_assign - 1)]
    slot_tok = jnp.where(valid, src // EXPERT_TOPK, 0).astype(jnp.int32)
    dest_sorted = padded_start[sorted_e] + jnp.arange(n_assign, dtype=jnp.int32) - start_unpadded[sorted_e]
    dest = jnp.zeros((n_assign,), jnp.int32).at[order].set(dest_sorted.astype(jnp.int32))
    n_used = (padded_end[-1] // MOE_TM).astype(jnp.int32).reshape(1)
    return slot_tok, dest, blk_expert, n_used


def kernel(x, w_in, b_in, attn_sinks, rel_bias_table, w_branch_swa, w_branch_moba, w_out, ln1_gain, ln1_bias,
           w_group_router, b_group_router, w_expert_router, b_expert_router, w_expert_gate, w_expert_up,
           w_expert_down, ln2_gain, ln2_bias):
    assert w_in.shape[0] == DEPTH == 1
    B, S, D = x.shape
    T = B * S
    bf16 = jnp.bfloat16
    f32 = jnp.float32
    w = w_in[0]
    b = b_in[0]

    def cols(off, width):
        return w[:, off:off + width], b[off:off + width]

    wq_a, bq_a = cols(OFF_SWA_Q, SWA_Q_W)
    wk_a, bk_a = cols(OFF_SWA_K, SWA_KV_W)
    wv_a, bv_a = cols(OFF_SWA_V, SWA_KV_W)
    wq_b, bq_b = cols(OFF_MOBA_Q, MOBA_W)
    wk_b, bk_b = cols(OFF_MOBA_K, MOBA_W)
    wv_b, bv_b = cols(OFF_MOBA_V, MOBA_W)

    def dup_kv(t):
        parts = [t[..., i * HEAD_DIM:(i + 1) * HEAD_DIM] for i in range(SWA_KV_HEADS)]
        return jnp.concatenate([p for p in parts for _ in range(2)], axis=-1)

    wn = jnp.concatenate([wq_a * ATTN_SCALE, dup_kv(wk_a), wq_b * ATTN_SCALE, wk_b], axis=1).astype(bf16)
    bn = jnp.concatenate([bq_a * ATTN_SCALE, dup_kv(bk_a), bq_b * ATTN_SCALE, bk_b])[None, :].astype(f32)
    wt = jnp.concatenate([wv_a, wv_b], axis=1).T.astype(bf16)
    bt = jnp.concatenate([bv_a, bv_b])[:, None].astype(f32)

    qk, vt = _inproj(x, wn, bn, wt, bt)

    bias_a, bias_b = _attention_bias_tables(rel_bias_table)
    y_a = _swa(attn_sinks[0].astype(f32), qk, vt, bias_a)
    y_b = _moba(qk, vt, bias_b)

    wg, bg = cols(OFF_GATE, 2 * D_MODEL)
    wr = jnp.zeros((ROUTER_ROWS, D), f32)
    wr = wr.at[0:N_GROUPS].set(w_group_router[0].T).at[8:8 + N_EXPERTS].set(w_expert_router[0].T)
    wr_hi = wr.astype(bf16)
    wr_lo = (wr - wr_hi.astype(f32)).astype(bf16)
    br = jnp.zeros((ROUTER_ROWS,), f32)
    br = br.at[0:N_GROUPS].set(b_group_router[0]).at[8:8 + N_EXPERTS].set(b_expert_router[0])[:, None]
    x1, x1b, rinfo = _merge(
        x.reshape(T, D), y_a.reshape(T, SWA_Q_W), y_b.reshape(T, MOBA_W),
        wg.astype(bf16), bg[None, :].astype(f32), w_branch_swa[0].astype(bf16), w_branch_moba[0].astype(bf16),
        w_out[0].astype(bf16), ln1_gain[0][None, :].astype(f32), ln1_bias[0][None, :].astype(f32),
        jnp.concatenate([wr_hi, wr_lo], axis=0), br)

    e0 = rinfo[0].astype(jnp.int32)
    e1 = rinfo[1].astype(jnp.int32)
    w2 = jnp.stack([rinfo[2], rinfo[3]], axis=1)
    slot_tok, dest, blk_expert, n_used = _dispatch_plan(e0, e1, T)
    xs = jnp.take(x1b, slot_tok, axis=0)
    wgu = jnp.concatenate([w_expert_gate[0], w_expert_up[0]], axis=-1).astype(bf16)
    y_buf = _experts(blk_expert, n_used, xs, wgu, w_expert_down[0].astype(bf16))
    ypair = jnp.take(y_buf, dest, axis=0).reshape(T, EXPERT_TOPK * D)
    out = _final(x1, ypair, w2, ln2_gain[0][None, :].astype(f32), ln2_bias[0][None, :].astype(f32))
    return out.reshape(B, S, D)
```

```python
import functools
import math

import numpy as np
import jax
import jax.numpy as jnp
from jax import lax
from jax.experimental import pallas as pl
from jax.experimental.pallas import tpu as pltpu

D_MODEL = 1024
HEAD_DIM = 64
SWA_Q_HEADS = 8
SWA_KV_HEADS = 2
SWA_GROUP = SWA_Q_HEADS // SWA_KV_HEADS
SWA_WINDOW = 128
SWA_BLOCK = 128
MOBA_HEADS = 8
MOBA_BLOCK = 256
MOBA_TOPK = 3
REL_BUCKETS = 32
REL_MAX_DIST = 128
N_GROUPS = 4
EXPERTS_PER_GROUP = 8
N_EXPERTS = N_GROUPS * EXPERTS_PER_GROUP
EXPERT_TOPK = 2
D_EXPERT = 512
LN_EPS = 1e-5
DEPTH = 1
DEEPNORM_ALPHA = (2.0 * DEPTH) ** 0.25
NEG_INF = -1e30
ATTN_SCALE = HEAD_DIM ** -0.5

SWA_Q_W = SWA_Q_HEADS * HEAD_DIM
SWA_KV_W = SWA_KV_HEADS * HEAD_DIM
MOBA_W = MOBA_HEADS * HEAD_DIM
OFF_SWA_Q = 0
OFF_SWA_K = OFF_SWA_Q + SWA_Q_W
OFF_SWA_V = OFF_SWA_K + SWA_KV_W
OFF_MOBA_Q = OFF_SWA_V + SWA_KV_W
OFF_MOBA_K = OFF_MOBA_Q + MOBA_W
OFF_MOBA_V = OFF_MOBA_K + MOBA_W
OFF_GATE = OFF_MOBA_V + MOBA_W

LANES = 128
QK_SWA_Q = 0
QK_SWA_K = QK_SWA_Q + SWA_Q_W
QK_MOBA_Q = QK_SWA_K + SWA_KV_HEADS * LANES
QK_MOBA_K = QK_MOBA_Q + MOBA_W
QK_W = QK_MOBA_K + MOBA_W
VT_SWA = 0
VT_MOBA = VT_SWA + SWA_KV_W
VT_W = VT_MOBA + MOBA_W

MOE_TM = 256
ROUTER_ROWS = 8 + N_EXPERTS
VMEM_LIMIT = 56 * 1024 * 1024

_NT = (((1,), (1,)), ((), ()))


def _rel_bucket_np(dist):
    n = np.maximum(dist, 0)
    max_exact = REL_BUCKETS // 2
    nf = np.maximum(n, 1).astype(np.float32)
    large = max_exact + (np.log(nf / np.float32(max_exact)) / np.float32(math.log(REL_MAX_DIST / max_exact))
                         * np.float32(REL_BUCKETS - max_exact)).astype(np.int32)
    large = np.minimum(large, REL_BUCKETS - 1)
    return np.where(n < max_exact, n, large).astype(np.int32)


def _inproj_kernel(x_ref, wn_ref, bn_ref, wt_ref, bt_ref, qk_ref, vt_ref):
    xb = x_ref[0].astype(jnp.bfloat16)
    qk = jnp.dot(xb, wn_ref[...], preferred_element_type=jnp.float32) + bn_ref[...]
    qk_ref[0] = qk.astype(jnp.bfloat16)
    vt = lax.dot_general(wt_ref[...], xb, _NT, preferred_element_type=jnp.float32) + bt_ref[...]
    vt_ref[0] = vt.astype(jnp.bfloat16)


def _inproj(x, wn, bn, wt, bt, tm=512):
    B, S, D = x.shape
    return pl.pallas_call(
        _inproj_kernel,
        grid=(B, S // tm),
        in_specs=[
            pl.BlockSpec((1, tm, D), lambda b, i: (b, i, 0)),
            pl.BlockSpec((D, QK_W), lambda b, i: (0, 0)),
            pl.BlockSpec((1, QK_W), lambda b, i: (0, 0)),
            pl.BlockSpec((VT_W, D), lambda b, i: (0, 0)),
            pl.BlockSpec((VT_W, 1), lambda b, i: (0, 0)),
        ],
        out_specs=[
            pl.BlockSpec((1, tm, QK_W), lambda b, i: (b, i, 0)),
            pl.BlockSpec((1, VT_W, tm), lambda b, i: (b, 0, i)),
        ],
        out_shape=[
            jax.ShapeDtypeStruct((B, S, QK_W), jnp.bfloat16),
            jax.ShapeDtypeStruct((B, VT_W, S), jnp.bfloat16),
        ],
        compiler_params=pltpu.CompilerParams(
            dimension_semantics=("parallel", "parallel"), vmem_limit_bytes=VMEM_LIMIT),
        name="inproj",
    )(x, wn, bn, wt, bt)


def _swa_kernel(sink_ref, q_ref, k_ref, vt_ref, bias_ref, o_ref, vaug_ref):
    S = q_ref.shape[1]
    nblk = S // SWA_BLOCK
    half = HEAD_DIM
    vt = vt_ref[0]
    rows = lax.broadcasted_iota(jnp.int32, vt.shape, 0)
    one = jnp.ones_like(vt)
    vaug_ref[0] = jnp.where(rows < half, vt, one)
    vaug_ref[1] = jnp.where(rows < half, one, vt)
    lane = lax.broadcasted_iota(jnp.int32, (SWA_BLOCK, LANES), 1)
    orow = lax.broadcasted_iota(jnp.int32, (LANES, SWA_BLOCK), 0)

    def block(qs, ks, nk, boff):
        outs = []
        for pair in range(SWA_Q_HEADS // 2):
            qblk = q_ref[0, pl.ds(qs, SWA_BLOCK), pair * LANES:(pair + 1) * LANES]
            kv = (2 * pair) // SWA_GROUP
            kblk = k_ref[0, pl.ds(ks, nk), kv * LANES:(kv + 1) * LANES]
            vaug = vaug_ref[kv, :, pl.ds(ks, nk)]
            res = []
            for sub in range(2):
                h = 2 * pair + sub
                qh = jnp.where((lane < half) if sub == 0 else (lane >= half), qblk, jnp.zeros_like(qblk))
                s = lax.dot_general(kblk, qh, _NT, preferred_element_type=jnp.float32)
                s = s + bias_ref[h, boff:boff + nk, :]
                sink = sink_ref[h]
                m = jnp.maximum(jnp.max(s, axis=0, keepdims=True), sink)
                p = jnp.exp(s - m)
                r = jnp.dot(vaug, p.astype(jnp.bfloat16), preferred_element_type=jnp.float32)
                if kv == 0:
                    l = r[half:half + 1, :]
                else:
                    l = r[0:1, :]
                l = l + jnp.exp(sink - m)
                res.append((r / l, kv))
            (r0, kv0), (r1, _) = res
            if kv0 == 0:
                top = r0[0:half, :]
                bot = r1[0:half, :]
            else:
                top = r0[half:, :]
                bot = r1[half:, :]
            outs.append(jnp.concatenate([top, bot], axis=0))
        ot = jnp.concatenate(outs, axis=0)
        o_ref[0, pl.ds(qs, SWA_BLOCK), :] = ot.T.astype(jnp.bfloat16)

    del orow
    block(0, 0, SWA_BLOCK, SWA_BLOCK)

    def body(n, carry):
        qs = pl.multiple_of(n * SWA_BLOCK, SWA_BLOCK)
        ks = pl.multiple_of(qs - SWA_BLOCK, SWA_BLOCK)
        block(qs, ks, 2 * SWA_BLOCK, 0)
        return carry

    lax.fori_loop(1, nblk, body, 0)


def _swa(sinks, qk, vt, bias):
    B, S, _ = qk.shape
    nq = SWA_Q_W // LANES
    return pl.pallas_call(
        _swa_kernel,
        grid_spec=pltpu.PrefetchScalarGridSpec(
            num_scalar_prefetch=1,
            grid=(B,),
            in_specs=[
                pl.BlockSpec((1, S, SWA_Q_W), lambda b, s: (b, 0, QK_SWA_Q // SWA_Q_W)),
                pl.BlockSpec((1, S, 2 * LANES), lambda b, s: (b, 0, QK_SWA_K // (2 * LANES))),
                pl.BlockSpec((1, SWA_KV_W, S), lambda b, s: (b, VT_SWA // SWA_KV_W, 0)),
                pl.BlockSpec((SWA_Q_HEADS, 2 * SWA_BLOCK, SWA_BLOCK), lambda b, s: (0, 0, 0)),
            ],
            out_specs=pl.BlockSpec((1, S, SWA_Q_W), lambda b, s: (b, 0, 0)),
            scratch_shapes=[pltpu.VMEM((SWA_KV_HEADS, SWA_KV_W, S), jnp.bfloat16)],
        ),
        out_shape=jax.ShapeDtypeStruct((B, S, SWA_Q_W), jnp.bfloat16),
        compiler_params=pltpu.CompilerParams(
            dimension_semantics=("parallel",), vmem_limit_bytes=VMEM_LIMIT),
        name="swa",
    )(sinks, qk, qk, vt, bias)


def _colmax(s):
    while s.shape[0] > 8:
        h = s.shape[0] // 2
        s = jnp.maximum(s[:h], s[h:])
    return jnp.max(s, axis=0, keepdims=True)


def _moba_kernel(fqb_ref, fc_ref, q_ref, k_ref, vt_ref, bias_ref, o_ref,
                 vaug_ref, gate_ref, rank_ref, sel_ref, far_ref, m_scr, acc_scr):
    S = q_ref.shape[1]
    nb = S // MOBA_BLOCK
    half = HEAD_DIM
    BLK = MOBA_BLOCK

    vt = vt_ref[0]
    rows = lax.broadcasted_iota(jnp.int32, vt.shape, 0)
    one = jnp.ones_like(vt)
    vaug_ref[0] = jnp.where(rows < half, vt, one)
    vaug_ref[1] = jnp.where(rows < half, one, vt)

    kf = k_ref[0].astype(jnp.float32).reshape(nb, BLK, LANES)
    kmean = jnp.sum(kf, axis=1) * (1.0 / BLK)
    k_hi = kmean.astype(jnp.bfloat16)
    k_lo = (kmean - k_hi.astype(jnp.float32)).astype(jnp.bfloat16)
    kcat = jnp.concatenate([k_hi, k_lo], axis=0)

    lane_q = lax.broadcasted_iota(jnp.int32, (S, LANES), 1)
    brow = lax.broadcasted_iota(jnp.int32, (nb, S), 0)
    qblk_of = lax.broadcasted_iota(jnp.int32, (nb, S), 1) // BLK
    qall = q_ref[0]
    for h in range(2):
        qh = jnp.where((lane_q < half) if h == 0 else (lane_q >= half), qall, jnp.zeros_like(qall))
        g2 = lax.dot_general(kcat, qh, _NT, preferred_element_type=jnp.float32)
        gate_ref[...] = g2[0:nb] + g2[nb:2 * nb]
        rank_ref[...] = jnp.zeros_like(rank_ref)
        for m in range(nb - 1):
            lo = (m + 1) * BLK
            G = gate_ref[:, lo:]
            gm = gate_ref[m:m + 1, lo:]
            ge = jnp.where(gm >= G, 1.0, 0.0)
            gt = jnp.where(gm > G, 1.0, 0.0)
            brow_m = lax.broadcasted_iota(jnp.int32, (nb, S - lo), 0)
            rank_ref[:, lo:] += jnp.where(brow_m > m, ge, gt)
        top = rank_ref[...] < float(MOBA_TOPK)
        sel_ref[h] = jnp.where((brow < qblk_of) & top, 0.0, NEG_INF)
        far_ref[h] = jnp.where((brow < qblk_of - 1) & top, 0.0, NEG_INF)

    lane = lax.broadcasted_iota(jnp.int32, (BLK, LANES), 1)
    orow = lax.broadcasted_iota(jnp.int32, (LANES, BLK), 0)

    def head_q(qs, h):
        qblk = q_ref[0, pl.ds(qs, BLK), :]
        return jnp.where((lane < half) if h == 0 else (lane >= half), qblk, jnp.zeros_like(qblk))

    m0 = jnp.full((1, BLK), NEG_INF, jnp.float32)
    acc0 = jnp.zeros((LANES, BLK), jnp.float32)

    def item(h, qb, qs, ks, nk, bias, sel_a, sel_b, m, acc):
        return dict(h=h, qb=qb, qs=qs, ks=ks, nk=nk, bias=bias, sel_a=sel_a, sel_b=sel_b, m=m, acc=acc)

    def near_item(qb, h):
        qs = pl.multiple_of(qb * BLK, BLK)
        ps = pl.multiple_of(qs - BLK, BLK)
        sel_prev = sel_ref[h, pl.ds(qb - 1, 1), pl.ds(qs, BLK)]
        return item(h, qb, qs, ps, 2 * BLK, bias_ref[h], sel_prev, None, m0, acc0)

    def far_item(i, h):
        qb = fqb_ref[i]
        c = fc_ref[i]
        qs = pl.multiple_of(qb * BLK, BLK)
        ks = pl.multiple_of(c * (2 * BLK), 2 * BLK)
        sel_a = far_ref[h, pl.ds(2 * c, 1), pl.ds(qs, BLK)]
        sel_b = far_ref[h, pl.ds(2 * c + 1, 1), pl.ds(qs, BLK)]
        return item(h, qb, qs, ks, 2 * BLK, None, sel_a, sel_b, m_scr[h, qb], acc_scr[h, qb])

    def scores(it):
        kslab = k_ref[0, pl.ds(it["ks"], it["nk"]), :]
        s = lax.dot_general(kslab, head_q(it["qs"], it["h"]), _NT,
                            preferred_element_type=jnp.float32)
        return s if it["bias"] is None else s + it["bias"]

    def probs(it, s):
        m, sel_a, sel_b = it["m"], it["sel_a"], it["sel_b"]
        if it["nk"] == BLK:
            m_new = jnp.maximum(m, _colmax(s))
            p = jnp.exp(s - m_new)
        else:
            cm_a = _colmax(s[:BLK]) + sel_a
            cm_b = _colmax(s[BLK:])
            if sel_b is not None:
                cm_b = cm_b + sel_b
            m_new = jnp.maximum(m, jnp.maximum(cm_a, cm_b))
            p_a = jnp.exp(s[:BLK] - (m_new - sel_a))
            p_b = jnp.exp(s[BLK:] - (m_new if sel_b is None else m_new - sel_b))
            p = jnp.concatenate([p_a, p_b], axis=0)
        return m_new, jnp.exp(m - m_new), p.astype(jnp.bfloat16)

    def run(items):
        ss = [scores(it) for it in items]
        ps = [probs(it, s) for it, s in zip(items, ss)]
        pvs = [jnp.dot(vaug_ref[it["h"], :, pl.ds(it["ks"], it["nk"])], p, preferred_element_type=jnp.float32)
               for it, (_, _, p) in zip(items, ps)]
        for it, (m_new, alpha, _), pv in zip(items, ps, pvs):
            m_scr[it["h"], it["qb"]] = m_new
            acc_scr[it["h"], it["qb"]] = it["acc"] * alpha + pv

    run([item(h, 0, 0, 0, BLK, bias_ref[h, BLK:, :], None, None, m0, acc0) for h in range(2)]
        + [near_item(nb - 1, h) for h in range(2)])

    near_pairs = nb // 2 - 1

    def near_body(i, carry):
        run([near_item(qb, h) for qb in (i, i + near_pairs) for h in range(2)])
        return carry

    lax.fori_loop(1, near_pairs + 1, near_body, 0)

    n_far = fqb_ref.shape[0]

    def far_body(i, carry):
        run([far_item(j, h) for j in (i, i + n_far // 2) for h in range(2)])
        return carry

    lax.fori_loop(0, n_far // 2, far_body, 0)

    def out_body(qb, carry):
        qs = pl.multiple_of(qb * BLK, BLK)
        a0 = acc_scr[0, qb]
        a1 = acc_scr[1, qb]
        ot = jnp.where(orow < half, a0 / a0[half:half + 1, :], a1 / a1[0:1, :])
        o_ref[0, pl.ds(qs, BLK), :] = ot.T.astype(jnp.bfloat16)
        return carry

    lax.fori_loop(0, nb, out_body, 0)


def _moba_far_items(nb):
    items = [(qb, c) for qb in range(2, nb) for c in range(qb // 2)]
    n = len(items)
    assert n % 2 == 0 and all(items[i][0] != items[i + n // 2][0] for i in range(n // 2))
    return np.array([it[0] for it in items], np.int32), np.array([it[1] for it in items], np.int32)


def _moba(qk, vt, bias):
    B, S, _ = qk.shape
    nb = S // MOBA_BLOCK
    npair = MOBA_W // LANES
    far_qb, far_c = _moba_far_items(nb)
    return pl.pallas_call(
        _moba_kernel,
        grid_spec=pltpu.PrefetchScalarGridSpec(
            num_scalar_prefetch=2,
            grid=(B, npair),
            in_specs=[
                pl.BlockSpec((1, S, LANES), lambda b, p, fq, fc: (b, 0, QK_MOBA_Q // LANES + p)),
                pl.BlockSpec((1, S, LANES), lambda b, p, fq, fc: (b, 0, QK_MOBA_K // LANES + p)),
                pl.BlockSpec((1, LANES, S), lambda b, p, fq, fc: (b, VT_MOBA // LANES + p, 0)),
                pl.BlockSpec((2, 2 * MOBA_BLOCK, MOBA_BLOCK), lambda b, p, fq, fc: (p, 0, 0)),
            ],
            out_specs=pl.BlockSpec((1, S, LANES), lambda b, p, fq, fc: (b, 0, p)),
            scratch_shapes=[
                pltpu.VMEM((2, LANES, S), jnp.bfloat16),
                pltpu.VMEM((nb, S), jnp.float32),
                pltpu.VMEM((nb, S), jnp.float32),
                pltpu.VMEM((2, nb, S), jnp.float32),
                pltpu.VMEM((2, nb, S), jnp.float32),
                pltpu.VMEM((2, nb, 1, MOBA_BLOCK), jnp.float32),
                pltpu.VMEM((2, nb, LANES, MOBA_BLOCK), jnp.float32),
            ],
        ),
        out_shape=jax.ShapeDtypeStruct((B, S, MOBA_W), jnp.bfloat16),
        compiler_params=pltpu.CompilerParams(
            dimension_semantics=("parallel", "parallel"), vmem_limit_bytes=VMEM_LIMIT),
        name="moba",
    )(jnp.asarray(far_qb), jnp.asarray(far_c), qk, qk, vt, bias)


def _layer_norm(h, gain, bias):
    mu = jnp.mean(h, axis=-1, keepdims=True)
    c = h - mu
    var = jnp.mean(c * c, axis=-1, keepdims=True)
    return c * lax.rsqrt(var + LN_EPS) * gain + bias


def _merge_kernel(x_ref, ya_ref, yb_ref, wg_ref, bg_ref, wa_ref, wb_ref, wo_ref, g1_ref, b1_ref,
                  wr_ref, br_ref, x1_ref, x1b_ref, r_ref):
    x = x_ref[...]
    xb = x.astype(jnp.bfloat16)
    z = jnp.dot(xb, wg_ref[...], preferred_element_type=jnp.float32) + bg_ref[...]
    gates = 1.0 / (1.0 + jnp.exp(-z))
    pa = jnp.dot(ya_ref[...], wa_ref[...], preferred_element_type=jnp.float32)
    pb = jnp.dot(yb_ref[...], wb_ref[...], preferred_element_type=jnp.float32)
    merged = gates[:, :D_MODEL] * pa + gates[:, D_MODEL:] * pb
    mixed = jnp.dot(merged.astype(jnp.bfloat16), wo_ref[...], preferred_element_type=jnp.float32)
    x1 = _layer_norm(DEEPNORM_ALPHA * x + mixed, g1_ref[...], b1_ref[...])
    x1_ref[...] = x1
    x1_hi = x1.astype(jnp.bfloat16)
    x1b_ref[...] = x1_hi
    x1_lo = (x1 - x1_hi.astype(jnp.float32)).astype(jnp.bfloat16)

    R = ROUTER_ROWS
    l1 = lax.dot_general(wr_ref[...], x1_hi, _NT, preferred_element_type=jnp.float32)
    l2 = lax.dot_general(wr_ref[0:R, :], x1_lo, _NT, preferred_element_type=jnp.float32)
    L = l1[0:R] + l1[R:2 * R] + l2 + br_ref[...]
    tm = x.shape[0]
    row = lax.broadcasted_iota(jnp.int32, (8, tm), 0)
    big = jnp.float32(-3e38)
    gl = jnp.where(row < N_GROUPS, L[0:8], big)
    gmax = jnp.max(gl, axis=0, keepdims=True)
    g_idx = jnp.min(jnp.where(gl == gmax, row, 8), axis=0, keepdims=True)
    gsum = jnp.sum(jnp.where(row < N_GROUPS, jnp.exp(gl - gmax), 0.0), axis=0, keepdims=True)
    g_prob = 1.0 / gsum
    E = L[8 + 8 * (N_GROUPS - 1):8 + 8 * N_GROUPS]
    for g in range(N_GROUPS - 2, -1, -1):
        E = jnp.where(g_idx == g, L[8 + 8 * g:16 + 8 * g], E)
    t0 = jnp.max(E, axis=0, keepdims=True)
    loc0 = jnp.min(jnp.where(E == t0, row, 8), axis=0, keepdims=True)
    E2 = jnp.where(row == loc0, big, E)
    t1 = jnp.max(E2, axis=0, keepdims=True)
    loc1 = jnp.min(jnp.where(E2 == t1, row, 8), axis=0, keepdims=True)
    ex = jnp.exp(t1 - t0)
    w0 = g_prob / (1.0 + ex)
    w1 = g_prob * ex / (1.0 + ex)
    e0 = (g_idx * EXPERTS_PER_GROUP + loc0).astype(jnp.float32)
    e1 = (g_idx * EXPERTS_PER_GROUP + loc1).astype(jnp.float32)
    r_ref[...] = jnp.where(row == 0, e0, jnp.where(row == 1, e1, jnp.where(row == 2, w0, jnp.where(row == 3, w1, 0.0))))


def _merge(x2, ya, yb, wg, bg, wa, wb, wo, g1, b1, wr, br, tm=512):
    T, D = x2.shape
    const = lambda i: (0, 0)
    return pl.pallas_call(
        _merge_kernel,
        grid=(T // tm,),
        in_specs=[
            pl.BlockSpec((tm, D), lambda i: (i, 0)),
            pl.BlockSpec((tm, SWA_Q_W), lambda i: (i, 0)),
            pl.BlockSpec((tm, MOBA_W), lambda i: (i, 0)),
            pl.BlockSpec((D, 2 * D), const),
            pl.BlockSpec((1, 2 * D), const),
            pl.BlockSpec((SWA_Q_W, D), const),
            pl.BlockSpec((MOBA_W, D), const),
            pl.BlockSpec((D, D), const),
            pl.BlockSpec((1, D), const),
            pl.BlockSpec((1, D), const),
            pl.BlockSpec((2 * ROUTER_ROWS, D), const),
            pl.BlockSpec((ROUTER_ROWS, 1), const),
        ],
        out_specs=[
            pl.BlockSpec((tm, D), lambda i: (i, 0)),
            pl.BlockSpec((tm, D), lambda i: (i, 0)),
            pl.BlockSpec((8, tm), lambda i: (0, i)),
        ],
        out_shape=[
            jax.ShapeDtypeStruct((T, D), jnp.float32),
            jax.ShapeDtypeStruct((T, D), jnp.bfloat16),
            jax.ShapeDtypeStruct((8, T), jnp.float32),
        ],
        compiler_params=pltpu.CompilerParams(
            dimension_semantics=("parallel",), vmem_limit_bytes=VMEM_LIMIT),
        name="merge",
    )(x2, ya, yb, wg, bg, wa, wb, wo, g1, b1, wr, br)


def _expert_kernel(be_ref, nu_ref, x_ref, wgu_ref, wd_ref, y_ref):
    i = pl.program_id(0)

    @pl.when(i < nu_ref[0])
    def _():
        gu = jnp.dot(x_ref[...], wgu_ref[0], preferred_element_type=jnp.float32)
        g = gu[:, :D_EXPERT]
        u = gu[:, D_EXPERT:]
        act = (g / (1.0 + jnp.exp(-g))) * u
        y = jnp.dot(act.astype(jnp.bfloat16), wd_ref[0], preferred_element_type=jnp.float32)
        y_ref[...] = y.astype(jnp.bfloat16)

    @pl.when(i >= nu_ref[0])
    def _():
        y_ref[...] = jnp.zeros_like(y_ref)


def _experts(blk_expert, n_used, xs, wgu, wd):
    cap, D = xs.shape
    n_blocks = cap // MOE_TM
    return pl.pallas_call(
        _expert_kernel,
        grid_spec=pltpu.PrefetchScalarGridSpec(
            num_scalar_prefetch=2,
            grid=(n_blocks,),
            in_specs=[
                pl.BlockSpec((MOE_TM, D), lambda i, be, nu: (i, 0)),
                pl.BlockSpec((1, D, 2 * D_EXPERT), lambda i, be, nu: (be[i], 0, 0)),
                pl.BlockSpec((1, D_EXPERT, D), lambda i, be, nu: (be[i], 0, 0)),
            ],
            out_specs=pl.BlockSpec((MOE_TM, D), lambda i, be, nu: (i, 0)),
        ),
        out_shape=jax.ShapeDtypeStruct((cap, D), jnp.bfloat16),
        compiler_params=pltpu.CompilerParams(
            dimension_semantics=("arbitrary",), vmem_limit_bytes=VMEM_LIMIT),
        name="experts",
    )(blk_expert, n_used, xs, wgu, wd)


def _final_kernel(x1_ref, y0_ref, y1_ref, w_ref, g2_ref, b2_ref, o_ref):
    w = w_ref[...]
    moe = y0_ref[...].astype(jnp.float32) * w[:, 0:1] + y1_ref[...].astype(jnp.float32) * w[:, 1:2]
    o_ref[...] = _layer_norm(DEEPNORM_ALPHA * x1_ref[...] + moe, g2_ref[...], b2_ref[...])


def _final(x1, y0, y1, w2, g2, b2, tm=1024):
    T, D = x1.shape
    const = lambda i: (0, 0)
    return pl.pallas_call(
        _final_kernel,
        grid=(T // tm,),
        in_specs=[
            pl.BlockSpec((tm, D), lambda i: (i, 0)),
            pl.BlockSpec((tm, D), lambda i: (i, 0)),
            pl.BlockSpec((tm, D), lambda i: (i, 0)),
            pl.BlockSpec((tm, EXPERT_TOPK), lambda i: (i, 0)),
            pl.BlockSpec((1, D), const),
            pl.BlockSpec((1, D), const),
        ],
        out_specs=pl.BlockSpec((tm, D), lambda i: (i, 0)),
        out_shape=jax.ShapeDtypeStruct((T, D), jnp.float32),
        compiler_params=pltpu.CompilerParams(
            dimension_semantics=("parallel",), vmem_limit_bytes=VMEM_LIMIT),
        name="final",
    )(x1, y0, y1, w2, g2, b2)


def _attention_bias_tables(rel_bias_table):
    rel = rel_bias_table.astype(jnp.float32)
    rel_a = rel[:, :SWA_Q_HEADS]
    rel_b = rel[:, SWA_Q_HEADS:]
    kj = np.arange(2 * SWA_BLOCK)[:, None]
    qi = np.arange(SWA_BLOCK)[None, :]
    dist = SWA_BLOCK + qi - kj
    vis = (dist >= 0) & (dist < SWA_WINDOW)
    bias_a = jnp.where(vis[None], rel_a.T[:, _rel_bucket_np(dist)], NEG_INF)
    j = np.arange(MOBA_BLOCK)[:, None]
    i = np.arange(MOBA_BLOCK)[None, :]
    d_own = i - j
    d_prev = MOBA_BLOCK + i - j
    tb = rel_b.T
    far = tb[:, REL_BUCKETS - 1][:, None, None]
    own = jnp.where((d_own >= 0)[None], tb[:, _rel_bucket_np(d_own)] - far, NEG_INF)
    prev = tb[:, _rel_bucket_np(d_prev)] - far
    bias_b = jnp.concatenate([prev, own], axis=1)
    return bias_a, bias_b


def _dispatch_plan(e0, e1, n_tok):
    n_assign = n_tok * EXPERT_TOPK
    flat_e = jnp.stack([e0, e1], axis=1).reshape(-1)
    order = jnp.argsort(flat_e)
    sorted_e = flat_e[order]
    sizes = jnp.sum(flat_e[:, None] == jnp.arange(N_EXPERTS)[None, :], axis=0).astype(jnp.int32)
    start_unpadded = jnp.cumsum(sizes) - sizes
    padded = ((sizes + MOE_TM - 1) // MOE_TM) * MOE_TM
    padded_end = jnp.cumsum(padded)
    padded_start = padded_end - padded
    cap = -(-n_assign // MOE_TM) * MOE_TM + N_EXPERTS * MOE_TM
    n_blocks = cap // MOE_TM
    blk_start = jnp.arange(n_blocks, dtype=jnp.int32) * MOE_TM
    blk_expert = jnp.minimum(
        jnp.sum(padded_end[None, :] <= blk_start[:, None], axis=1), N_EXPERTS - 1).astype(jnp.int32)
    slot_e = jnp.repeat(blk_expert, MOE_TM)
    pos = jnp.arange(cap, dtype=jnp.int32) - padded_start[slot_e]
    valid = pos < sizes[slot_e]
    src = order[jnp.clip(start_unpadded[slot_e] + pos, 0, n_assign - 1)]
    slot_tok = jnp.where(valid, src // EXPERT_TOPK, 0).astype(jnp.int32)
    dest_sorted = padded_start[sorted_e] + jnp.arange(n_assign, dtype=jnp.int32) - start_unpadded[sorted_e]
    dest = jnp.zeros((n_assign,), jnp.int32).at[order].set(dest_sorted.astype(jnp.int32))
    n_used = (padded_end[-1] // MOE_TM).astype(jnp.int32).reshape(1)
    return slot_tok, dest, blk_expert, n_used


def kernel(x, w_in, b_in, attn_sinks, rel_bias_table, w_branch_swa, w_branch_moba, w_out, ln1_gain, ln1_bias,
           w_group_router, b_group_router, w_expert_router, b_expert_router, w_expert_gate, w_expert_up,
           w_expert_down, ln2_gain, ln2_bias):
    assert w_in.shape[0] == DEPTH == 1
    B, S, D = x.shape
    T = B * S
    bf16 = jnp.bfloat16
    f32 = jnp.float32
    w = w_in[0]
    b = b_in[0]

    def cols(off, width):
        return w[:, off:off + width], b[off:off + width]

    wq_a, bq_a = cols(OFF_SWA_Q, SWA_Q_W)
    wk_a, bk_a = cols(OFF_SWA_K, SWA_KV_W)
    wv_a, bv_a = cols(OFF_SWA_V, SWA_KV_W)
    wq_b, bq_b = cols(OFF_MOBA_Q, MOBA_W)
    wk_b, bk_b = cols(OFF_MOBA_K, MOBA_W)
    wv_b, bv_b = cols(OFF_MOBA_V, MOBA_W)

    def dup_kv(t):
        parts = [t[..., i * HEAD_DIM:(i + 1) * HEAD_DIM] for i in range(SWA_KV_HEADS)]
        return jnp.concatenate([p for p in parts for _ in range(2)], axis=-1)

    wn = jnp.concatenate([wq_a * ATTN_SCALE, dup_kv(wk_a), wq_b * ATTN_SCALE, wk_b], axis=1).astype(bf16)
    bn = jnp.concatenate([bq_a * ATTN_SCALE, dup_kv(bk_a), bq_b * ATTN_SCALE, bk_b])[None, :].astype(f32)
    wt = jnp.concatenate([wv_a, wv_b], axis=1).T.astype(bf16)
    bt = jnp.concatenate([bv_a, bv_b])[:, None].astype(f32)

    qk, vt = _inproj(x, wn, bn, wt, bt)

    bias_a, bias_b = _attention_bias_tables(rel_bias_table)
    y_a = _swa(attn_sinks[0].astype(f32), qk, vt, bias_a)
    y_b = _moba(qk, vt, bias_b)

    wg, bg = cols(OFF_GATE, 2 * D_MODEL)
    wr = jnp.zeros((ROUTER_ROWS, D), f32)
    wr = wr.at[0:N_GROUPS].set(w_group_router[0].T).at[8:8 + N_EXPERTS].set(w_expert_router[0].T)
    wr_hi = wr.astype(bf16)
    wr_lo = (wr - wr_hi.astype(f32)).astype(bf16)
    br = jnp.zeros((ROUTER_ROWS,), f32)
    br = br.at[0:N_GROUPS].set(b_group_router[0]).at[8:8 + N_EXPERTS].set(b_expert_router[0])[:, None]
    x1, x1b, rinfo = _merge(
        x.reshape(T, D), y_a.reshape(T, SWA_Q_W), y_b.reshape(T, MOBA_W),
        wg.astype(bf16), bg[None, :].astype(f32), w_branch_swa[0].astype(bf16), w_branch_moba[0].astype(bf16),
        w_out[0].astype(bf16), ln1_gain[0][None, :].astype(f32), ln1_bias[0][None, :].astype(f32),
        jnp.concatenate([wr_hi, wr_lo], axis=0), br)

    e0 = rinfo[0].astype(jnp.int32)
    e1 = rinfo[1].astype(jnp.int32)
    w2 = jnp.stack([rinfo[2], rinfo[3]], axis=1)
    slot_tok, dest, blk_expert, n_used = _dispatch_plan(e0, e1, T)
    xs = jnp.take(x1b, slot_tok, axis=0)
    wgu = jnp.concatenate([w_expert_gate[0], w_expert_up[0]], axis=-1).astype(bf16)
    y_buf = _experts(blk_expert, n_used, xs, wgu, w_expert_down[0].astype(bf16))
    dest2 = dest.reshape(T, EXPERT_TOPK)
    y0 = jnp.take(y_buf, dest2[:, 0], axis=0)
    y1 = jnp.take(y_buf, dest2[:, 1], axis=0)
    out = _final(x1, y0, y1, w2, ln2_gain[0][None, :].astype(f32), ln2_bias[0][None, :].astype(f32))
    return out.reshape(B, S, D)
```

```python
import functools
import math

import numpy as np
import jax
import jax.numpy as jnp
from jax import lax
from jax.experimental import pallas as pl
from jax.experimental.pallas import tpu as pltpu
from jax.experimental.pallas import tpu_sc as plsc

D_MODEL = 1024
HEAD_DIM = 64
SWA_Q_HEADS = 8
SWA_KV_HEADS = 2
SWA_GROUP = SWA_Q_HEADS // SWA_KV_HEADS
SWA_WINDOW = 128
SWA_BLOCK = 128
MOBA_HEADS = 8
MOBA_BLOCK = 256
MOBA_TOPK = 3
REL_BUCKETS = 32
REL_MAX_DIST = 128
N_GROUPS = 4
EXPERTS_PER_GROUP = 8
N_EXPERTS = N_GROUPS * EXPERTS_PER_GROUP
EXPERT_TOPK = 2
D_EXPERT = 512
LN_EPS = 1e-5
DEPTH = 1
DEEPNORM_ALPHA = (2.0 * DEPTH) ** 0.25
NEG_INF = -1e30
ATTN_SCALE = HEAD_DIM ** -0.5

SWA_Q_W = SWA_Q_HEADS * HEAD_DIM
SWA_KV_W = SWA_KV_HEADS * HEAD_DIM
MOBA_W = MOBA_HEADS * HEAD_DIM
OFF_SWA_Q = 0
OFF_SWA_K = OFF_SWA_Q + SWA_Q_W
OFF_SWA_V = OFF_SWA_K + SWA_KV_W
OFF_MOBA_Q = OFF_SWA_V + SWA_KV_W
OFF_MOBA_K = OFF_MOBA_Q + MOBA_W
OFF_MOBA_V = OFF_MOBA_K + MOBA_W
OFF_GATE = OFF_MOBA_V + MOBA_W

LANES = 128
QK_SWA_Q = 0
QK_SWA_K = QK_SWA_Q + SWA_Q_W
QK_MOBA_Q = QK_SWA_K + SWA_KV_HEADS * LANES
QK_MOBA_K = QK_MOBA_Q + MOBA_W
QK_W = QK_MOBA_K + MOBA_W
VT_SWA = 0
VT_MOBA = VT_SWA + SWA_KV_W
VT_W = VT_MOBA + MOBA_W

MOE_TM = 256
SC_CHUNK = 128
ROUTER_ROWS = 8 + N_EXPERTS
VMEM_LIMIT = 56 * 1024 * 1024

_NT = (((1,), (1,)), ((), ()))


def _rel_bucket_np(dist):
    n = np.maximum(dist, 0)
    max_exact = REL_BUCKETS // 2
    nf = np.maximum(n, 1).astype(np.float32)
    large = max_exact + (np.log(nf / np.float32(max_exact)) / np.float32(math.log(REL_MAX_DIST / max_exact))
                         * np.float32(REL_BUCKETS - max_exact)).astype(np.int32)
    large = np.minimum(large, REL_BUCKETS - 1)
    return np.where(n < max_exact, n, large).astype(np.int32)


def _inproj_kernel(x_ref, wn_ref, bn_ref, wt_ref, bt_ref, qk_ref, vt_ref):
    xb = x_ref[0].astype(jnp.bfloat16)
    qk = jnp.dot(xb, wn_ref[...], preferred_element_type=jnp.float32) + bn_ref[...]
    qk_ref[0] = qk.astype(jnp.bfloat16)
    vt = lax.dot_general(wt_ref[...], xb, _NT, preferred_element_type=jnp.float32) + bt_ref[...]
    vt_ref[0] = vt.astype(jnp.bfloat16)


def _inproj(x, wn, bn, wt, bt, tm=512):
    B, S, D = x.shape
    return pl.pallas_call(
        _inproj_kernel,
        grid=(B, S // tm),
        in_specs=[
            pl.BlockSpec((1, tm, D), lambda b, i: (b, i, 0)),
            pl.BlockSpec((D, QK_W), lambda b, i: (0, 0)),
            pl.BlockSpec((1, QK_W), lambda b, i: (0, 0)),
            pl.BlockSpec((VT_W, D), lambda b, i: (0, 0)),
            pl.BlockSpec((VT_W, 1), lambda b, i: (0, 0)),
        ],
        out_specs=[
            pl.BlockSpec((1, tm, QK_W), lambda b, i: (b, i, 0)),
            pl.BlockSpec((1, VT_W, tm), lambda b, i: (b, 0, i)),
        ],
        out_shape=[
            jax.ShapeDtypeStruct((B, S, QK_W), jnp.bfloat16),
            jax.ShapeDtypeStruct((B, VT_W, S), jnp.bfloat16),
        ],
        compiler_params=pltpu.CompilerParams(
            dimension_semantics=("parallel", "parallel"), vmem_limit_bytes=VMEM_LIMIT),
        name="inproj",
    )(x, wn, bn, wt, bt)


def _swa_kernel(sink_ref, q_ref, k_ref, vt_ref, bias_ref, o_ref, vaug_ref):
    S = q_ref.shape[1]
    nblk = S // SWA_BLOCK
    half = HEAD_DIM
    vt = vt_ref[0]
    rows = lax.broadcasted_iota(jnp.int32, vt.shape, 0)
    one = jnp.ones_like(vt)
    vaug_ref[0] = jnp.where(rows < half, vt, one)
    vaug_ref[1] = jnp.where(rows < half, one, vt)
    lane = lax.broadcasted_iota(jnp.int32, (SWA_BLOCK, LANES), 1)
    orow = lax.broadcasted_iota(jnp.int32, (LANES, SWA_BLOCK), 0)

    def block(qs, ks, nk, boff):
        outs = []
        for pair in range(SWA_Q_HEADS // 2):
            qblk = q_ref[0, pl.ds(qs, SWA_BLOCK), pair * LANES:(pair + 1) * LANES]
            kv = (2 * pair) // SWA_GROUP
            kblk = k_ref[0, pl.ds(ks, nk), kv * LANES:(kv + 1) * LANES]
            vaug = vaug_ref[kv, :, pl.ds(ks, nk)]
            res = []
            for sub in range(2):
                h = 2 * pair + sub
                qh = jnp.where((lane < half) if sub == 0 else (lane >= half), qblk, jnp.zeros_like(qblk))
                s = lax.dot_general(kblk, qh, _NT, preferred_element_type=jnp.float32)
                s = s + bias_ref[h, boff:boff + nk, :]
                sink = sink_ref[h]
                m = jnp.maximum(jnp.max(s, axis=0, keepdims=True), sink)
                p = jnp.exp(s - m)
                r = jnp.dot(vaug, p.astype(jnp.bfloat16), preferred_element_type=jnp.float32)
                if kv == 0:
                    l = r[half:half + 1, :]
                else:
                    l = r[0:1, :]
                l = l + jnp.exp(sink - m)
                res.append((r / l, kv))
            (r0, kv0), (r1, _) = res
            if kv0 == 0:
                top = r0[0:half, :]
                bot = r1[0:half, :]
            else:
                top = r0[half:, :]
                bot = r1[half:, :]
            outs.append(jnp.concatenate([top, bot], axis=0))
        ot = jnp.concatenate(outs, axis=0)
        o_ref[0, pl.ds(qs, SWA_BLOCK), :] = ot.T.astype(jnp.bfloat16)

    del orow
    block(0, 0, SWA_BLOCK, SWA_BLOCK)

    def body(n, carry):
        qs = pl.multiple_of(n * SWA_BLOCK, SWA_BLOCK)
        ks = pl.multiple_of(qs - SWA_BLOCK, SWA_BLOCK)
        block(qs, ks, 2 * SWA_BLOCK, 0)
        return carry

    lax.fori_loop(1, nblk, body, 0)


def _swa(sinks, qk, vt, bias):
    B, S, _ = qk.shape
    nq = SWA_Q_W // LANES
    return pl.pallas_call(
        _swa_kernel,
        grid_spec=pltpu.PrefetchScalarGridSpec(
            num_scalar_prefetch=1,
            grid=(B,),
            in_specs=[
                pl.BlockSpec((1, S, SWA_Q_W), lambda b, s: (b, 0, QK_SWA_Q // SWA_Q_W)),
                pl.BlockSpec((1, S, 2 * LANES), lambda b, s: (b, 0, QK_SWA_K // (2 * LANES))),
                pl.BlockSpec((1, SWA_KV_W, S), lambda b, s: (b, VT_SWA // SWA_KV_W, 0)),
                pl.BlockSpec((SWA_Q_HEADS, 2 * SWA_BLOCK, SWA_BLOCK), lambda b, s: (0, 0, 0)),
            ],
            out_specs=pl.BlockSpec((1, S, SWA_Q_W), lambda b, s: (b, 0, 0)),
            scratch_shapes=[pltpu.VMEM((SWA_KV_HEADS, SWA_KV_W, S), jnp.bfloat16)],
        ),
        out_shape=jax.ShapeDtypeStruct((B, S, SWA_Q_W), jnp.bfloat16),
        compiler_params=pltpu.CompilerParams(
            dimension_semantics=("parallel",), vmem_limit_bytes=VMEM_LIMIT),
        name="swa",
    )(sinks, qk, qk, vt, bias)


def _colmax(s):
    while s.shape[0] > 8:
        h = s.shape[0] // 2
        s = jnp.maximum(s[:h], s[h:])
    return jnp.max(s, axis=0, keepdims=True)


def _moba_kernel(fqb_ref, fc_ref, q_ref, k_ref, vt_ref, bias_ref, o_ref,
                 vaug_ref, gate_ref, rank_ref, sel_ref, far_ref, m_scr, acc_scr):
    S = q_ref.shape[1]
    nb = S // MOBA_BLOCK
    half = HEAD_DIM
    BLK = MOBA_BLOCK

    vt = vt_ref[0]
    rows = lax.broadcasted_iota(jnp.int32, vt.shape, 0)
    one = jnp.ones_like(vt)
    vaug_ref[0] = jnp.where(rows < half, vt, one)
    vaug_ref[1] = jnp.where(rows < half, one, vt)

    kf = k_ref[0].astype(jnp.float32).reshape(nb, BLK, LANES)
    kmean = jnp.sum(kf, axis=1) * (1.0 / BLK)
    k_hi = kmean.astype(jnp.bfloat16)
    k_lo = (kmean - k_hi.astype(jnp.float32)).astype(jnp.bfloat16)
    kcat = jnp.concatenate([k_hi, k_lo], axis=0)

    lane_q = lax.broadcasted_iota(jnp.int32, (S, LANES), 1)
    brow = lax.broadcasted_iota(jnp.int32, (nb, S), 0)
    qblk_of = lax.broadcasted_iota(jnp.int32, (nb, S), 1) // BLK
    qall = q_ref[0]
    for h in range(2):
        qh = jnp.where((lane_q < half) if h == 0 else (lane_q >= half), qall, jnp.zeros_like(qall))
        g2 = lax.dot_general(kcat, qh, _NT, preferred_element_type=jnp.float32)
        gate_ref[...] = g2[0:nb] + g2[nb:2 * nb]
        rank_ref[...] = jnp.zeros_like(rank_ref)
        for m in range(nb - 1):
            lo = (m + 1) * BLK
            G = gate_ref[:, lo:]
            gm = gate_ref[m:m + 1, lo:]
            ge = jnp.where(gm >= G, 1.0, 0.0)
            gt = jnp.where(gm > G, 1.0, 0.0)
            brow_m = lax.broadcasted_iota(jnp.int32, (nb, S - lo), 0)
            rank_ref[:, lo:] += jnp.where(brow_m > m, ge, gt)
        top = rank_ref[...] < float(MOBA_TOPK)
        sel_ref[h] = jnp.where((brow < qblk_of) & top, 0.0, NEG_INF)
        far_ref[h] = jnp.where((brow < qblk_of - 1) & top, 0.0, NEG_INF)

    lane = lax.broadcasted_iota(jnp.int32, (BLK, LANES), 1)
    orow = lax.broadcasted_iota(jnp.int32, (LANES, BLK), 0)

    def head_q(qs, h):
        qblk = q_ref[0, pl.ds(qs, BLK), :]
        return jnp.where((lane < half) if h == 0 else (lane >= half), qblk, jnp.zeros_like(qblk))

    m0 = jnp.full((1, BLK), NEG_INF, jnp.float32)
    acc0 = jnp.zeros((LANES, BLK), jnp.float32)

    def item(h, qb, qs, ks, nk, bias, sel_a, sel_b, m, acc):
        return dict(h=h, qb=qb, qs=qs, ks=ks, nk=nk, bias=bias, sel_a=sel_a, sel_b=sel_b, m=m, acc=acc)

    def near_item(qb, h):
        qs = pl.multiple_of(qb * BLK, BLK)
        ps = pl.multiple_of(qs - BLK, BLK)
        sel_prev = sel_ref[h, pl.ds(qb - 1, 1), pl.ds(qs, BLK)]
        return item(h, qb, qs, ps, 2 * BLK, bias_ref[h], sel_prev, None, m0, acc0)

    def far_item(i, h):
        qb = fqb_ref[i]
        c = fc_ref[i]
        qs = pl.multiple_of(qb * BLK, BLK)
        ks = pl.multiple_of(c * (2 * BLK), 2 * BLK)
        sel_a = far_ref[h, pl.ds(2 * c, 1), pl.ds(qs, BLK)]
        sel_b = far_ref[h, pl.ds(2 * c + 1, 1), pl.ds(qs, BLK)]
        return item(h, qb, qs, ks, 2 * BLK, None, sel_a, sel_b, m_scr[h, qb], acc_scr[h, qb])

    def scores(it):
        kslab = k_ref[0, pl.ds(it["ks"], it["nk"]), :]
        s = lax.dot_general(kslab, head_q(it["qs"], it["h"]), _NT,
                            preferred_element_type=jnp.float32)
        return s if it["bias"] is None else s + it["bias"]

    def probs(it, s):
        m, sel_a, sel_b = it["m"], it["sel_a"], it["sel_b"]
        if it["nk"] == BLK:
            m_new = jnp.maximum(m, _colmax(s))
            p = jnp.exp(s - m_new)
        else:
            cm_a = _colmax(s[:BLK]) + sel_a
            cm_b = _colmax(s[BLK:])
            if sel_b is not None:
                cm_b = cm_b + sel_b
            m_new = jnp.maximum(m, jnp.maximum(cm_a, cm_b))
            p_a = jnp.exp(s[:BLK] - (m_new - sel_a))
            p_b = jnp.exp(s[BLK:] - (m_new if sel_b is None else m_new - sel_b))
            p = jnp.concatenate([p_a, p_b], axis=0)
        return m_new, jnp.exp(m - m_new), p.astype(jnp.bfloat16)

    def run(items):
        ss = [scores(it) for it in items]
        ps = [probs(it, s) for it, s in zip(items, ss)]
        pvs = [jnp.dot(vaug_ref[it["h"], :, pl.ds(it["ks"], it["nk"])], p, preferred_element_type=jnp.float32)
               for it, (_, _, p) in zip(items, ps)]
        for it, (m_new, alpha, _), pv in zip(items, ps, pvs):
            m_scr[it["h"], it["qb"]] = m_new
            acc_scr[it["h"], it["qb"]] = it["acc"] * alpha + pv

    run([item(h, 0, 0, 0, BLK, bias_ref[h, BLK:, :], None, None, m0, acc0) for h in range(2)]
        + [near_item(nb - 1, h) for h in range(2)])

    near_pairs = nb // 2 - 1

    def near_body(i, carry):
        run([near_item(qb, h) for qb in (i, i + near_pairs) for h in range(2)])
        return carry

    lax.fori_loop(1, near_pairs + 1, near_body, 0)

    n_far = fqb_ref.shape[0]

    def far_body(i, carry):
        run([far_item(j, h) for j in (i, i + n_far // 2) for h in range(2)])
        return carry

    lax.fori_loop(0, n_far // 2, far_body, 0)

    def out_body(qb, carry):
        qs = pl.multiple_of(qb * BLK, BLK)
        a0 = acc_scr[0, qb]
        a1 = acc_scr[1, qb]
        ot = jnp.where(orow < half, a0 / a0[half:half + 1, :], a1 / a1[0:1, :])
        o_ref[0, pl.ds(qs, BLK), :] = ot.T.astype(jnp.bfloat16)
        return carry

    lax.fori_loop(0, nb, out_body, 0)


def _moba_far_items(nb):
    items = [(qb, c) for qb in range(2, nb) for c in range(qb // 2)]
    n = len(items)
    assert n % 2 == 0 and all(items[i][0] != items[i + n // 2][0] for i in range(n // 2))
    return np.array([it[0] for it in items], np.int32), np.array([it[1] for it in items], np.int32)


def _moba(qk, vt, bias):
    B, S, _ = qk.shape
    nb = S // MOBA_BLOCK
    npair = MOBA_W // LANES
    far_qb, far_c = _moba_far_items(nb)
    return pl.pallas_call(
        _moba_kernel,
        grid_spec=pltpu.PrefetchScalarGridSpec(
            num_scalar_prefetch=2,
            grid=(B, npair),
            in_specs=[
                pl.BlockSpec((1, S, LANES), lambda b, p, fq, fc: (b, 0, QK_MOBA_Q // LANES + p)),
                pl.BlockSpec((1, S, LANES), lambda b, p, fq, fc: (b, 0, QK_MOBA_K // LANES + p)),
                pl.BlockSpec((1, LANES, S), lambda b, p, fq, fc: (b, VT_MOBA // LANES + p, 0)),
                pl.BlockSpec((2, 2 * MOBA_BLOCK, MOBA_BLOCK), lambda b, p, fq, fc: (p, 0, 0)),
            ],
            out_specs=pl.BlockSpec((1, S, LANES), lambda b, p, fq, fc: (b, 0, p)),
            scratch_shapes=[
                pltpu.VMEM((2, LANES, S), jnp.bfloat16),
                pltpu.VMEM((nb, S), jnp.float32),
                pltpu.VMEM((nb, S), jnp.float32),
                pltpu.VMEM((2, nb, S), jnp.float32),
                pltpu.VMEM((2, nb, S), jnp.float32),
                pltpu.VMEM((2, nb, 1, MOBA_BLOCK), jnp.float32),
                pltpu.VMEM((2, nb, LANES, MOBA_BLOCK), jnp.float32),
            ],
        ),
        out_shape=jax.ShapeDtypeStruct((B, S, MOBA_W), jnp.bfloat16),
        compiler_params=pltpu.CompilerParams(
            dimension_semantics=("parallel", "parallel"), vmem_limit_bytes=VMEM_LIMIT),
        name="moba",
    )(jnp.asarray(far_qb), jnp.asarray(far_c), qk, qk, vt, bias)


def _layer_norm(h, gain, bias):
    mu = jnp.mean(h, axis=-1, keepdims=True)
    c = h - mu
    var = jnp.mean(c * c, axis=-1, keepdims=True)
    return c * lax.rsqrt(var + LN_EPS) * gain + bias


def _pack_bf16_pair(a, b):
    ia = lax.bitcast_convert_type(a.astype(jnp.bfloat16).astype(jnp.float32), jnp.int32)
    ib = lax.bitcast_convert_type(b.astype(jnp.bfloat16).astype(jnp.float32), jnp.int32)
    return lax.shift_right_logical(ia, 16) | ib


def _unpack_bf16_pair(w):
    lo = lax.bitcast_convert_type(lax.shift_left(w, 16), jnp.float32)
    hi = lax.bitcast_convert_type(w & jnp.int32(-65536), jnp.float32)
    return lo.astype(jnp.bfloat16), hi.astype(jnp.bfloat16)


def _merge_kernel(x_ref, ya_ref, yb_ref, wg_ref, bg_ref, wa_ref, wb_ref, wo_ref, g1_ref, b1_ref,
                  wr_ref, br_ref, tri_ref, x1_ref, x1p_ref, r_ref, cnt_ref):
    x = x_ref[...]
    xb = x.astype(jnp.bfloat16)
    z = jnp.dot(xb, wg_ref[...], preferred_element_type=jnp.float32) + bg_ref[...]
    gates = 1.0 / (1.0 + jnp.exp(-z))
    pa = jnp.dot(ya_ref[...], wa_ref[...], preferred_element_type=jnp.float32)
    pb = jnp.dot(yb_ref[...], wb_ref[...], preferred_element_type=jnp.float32)
    merged = gates[:, :D_MODEL] * pa + gates[:, D_MODEL:] * pb
    mixed = jnp.dot(merged.astype(jnp.bfloat16), wo_ref[...], preferred_element_type=jnp.float32)
    x1 = _layer_norm(DEEPNORM_ALPHA * x + mixed, g1_ref[...], b1_ref[...])
    x1_ref[...] = x1
    x1_hi = x1.astype(jnp.bfloat16)
    x1p_ref[...] = _pack_bf16_pair(x1[:, :D_MODEL // 2], x1[:, D_MODEL // 2:])
    x1_lo = (x1 - x1_hi.astype(jnp.float32)).astype(jnp.bfloat16)

    R = ROUTER_ROWS
    l1 = lax.dot_general(wr_ref[...], x1_hi, _NT, preferred_element_type=jnp.float32)
    l2 = lax.dot_general(wr_ref[0:R, :], x1_lo, _NT, preferred_element_type=jnp.float32)
    L = l1[0:R] + l1[R:2 * R] + l2 + br_ref[...]
    tm = x.shape[0]
    row = lax.broadcasted_iota(jnp.int32, (8, tm), 0)
    big = jnp.float32(-3e38)
    gl = jnp.where(row < N_GROUPS, L[0:8], big)
    gmax = jnp.max(gl, axis=0, keepdims=True)
    g_idx = jnp.min(jnp.where(gl == gmax, row, 8), axis=0, keepdims=True)
    gsum = jnp.sum(jnp.where(row < N_GROUPS, jnp.exp(gl - gmax), 0.0), axis=0, keepdims=True)
    g_prob = 1.0 / gsum
    E = L[8 + 8 * (N_GROUPS - 1):8 + 8 * N_GROUPS]
    for g in range(N_GROUPS - 2, -1, -1):
        E = jnp.where(g_idx == g, L[8 + 8 * g:16 + 8 * g], E)
    t0 = jnp.max(E, axis=0, keepdims=True)
    loc0 = jnp.min(jnp.where(E == t0, row, 8), axis=0, keepdims=True)
    E2 = jnp.where(row == loc0, big, E)
    t1 = jnp.max(E2, axis=0, keepdims=True)
    loc1 = jnp.min(jnp.where(E2 == t1, row, 8), axis=0, keepdims=True)
    ex = jnp.exp(t1 - t0)
    w0 = g_prob / (1.0 + ex)
    w1 = g_prob * ex / (1.0 + ex)
    e0i = g_idx * EXPERTS_PER_GROUP + loc0
    e1i = g_idx * EXPERTS_PER_GROUP + loc1

    erow = lax.broadcasted_iota(jnp.int32, (N_EXPERTS, tm), 0)
    oh0 = jnp.where(erow == e0i, 1.0, 0.0)
    oh1 = jnp.where(erow == e1i, 1.0, 0.0)
    both = oh0 + oh1

    @pl.when(pl.program_id(0) == 0)
    def _():
        cnt_ref[...] = jnp.zeros_like(cnt_ref)

    prefix = jnp.dot(both.astype(jnp.bfloat16), tri_ref[...], preferred_element_type=jnp.float32)
    prefix = prefix + cnt_ref[:, 0:1]
    rank0 = jnp.sum(oh0 * prefix, axis=0, keepdims=True)
    rank1 = jnp.sum(oh1 * prefix, axis=0, keepdims=True)
    cnt_ref[...] = cnt_ref[...] + jnp.sum(both, axis=1, keepdims=True)

    vals = (e0i.astype(jnp.float32), e1i.astype(jnp.float32), w0, w1, rank0, rank1)
    out = jnp.zeros((8, tm), jnp.float32)
    for k, v in enumerate(vals):
        out = jnp.where(row == k, v, out)
    r_ref[...] = out


def _merge(x2, ya, yb, wg, bg, wa, wb, wo, g1, b1, wr, br, tm=512):
    T, D = x2.shape
    const = lambda i: (0, 0)
    tri = jnp.triu(jnp.ones((tm, tm), jnp.bfloat16), k=1)
    return pl.pallas_call(
        _merge_kernel,
        grid=(T // tm,),
        in_specs=[
            pl.BlockSpec((tm, D), lambda i: (i, 0)),
            pl.BlockSpec((tm, SWA_Q_W), lambda i: (i, 0)),
            pl.BlockSpec((tm, MOBA_W), lambda i: (i, 0)),
            pl.BlockSpec((D, 2 * D), const),
            pl.BlockSpec((1, 2 * D), const),
            pl.BlockSpec((SWA_Q_W, D), const),
            pl.BlockSpec((MOBA_W, D), const),
            pl.BlockSpec((D, D), const),
            pl.BlockSpec((1, D), const),
            pl.BlockSpec((1, D), const),
            pl.BlockSpec((2 * ROUTER_ROWS, D), const),
            pl.BlockSpec((ROUTER_ROWS, 1), const),
            pl.BlockSpec((tm, tm), const),
        ],
        out_specs=[
            pl.BlockSpec((tm, D), lambda i: (i, 0)),
            pl.BlockSpec((tm, D // 2), lambda i: (i, 0)),
            pl.BlockSpec((8, tm), lambda i: (0, i)),
            pl.BlockSpec((N_EXPERTS, LANES), const),
        ],
        out_shape=[
            jax.ShapeDtypeStruct((T, D), jnp.float32),
            jax.ShapeDtypeStruct((T, D // 2), jnp.int32),
            jax.ShapeDtypeStruct((8, T), jnp.float32),
            jax.ShapeDtypeStruct((N_EXPERTS, LANES), jnp.float32),
        ],
        compiler_params=pltpu.CompilerParams(
            dimension_semantics=("arbitrary",), vmem_limit_bytes=VMEM_LIMIT),
        name="merge",
    )(x2, ya, yb, wg, bg, wa, wb, wo, g1, b1, wr, br, tri)


def _dest_kernel(r_ref, ps_ref, d_ref):
    tt = r_ref.shape[1]
    erow = lax.broadcasted_iota(jnp.int32, (N_EXPERTS, tt), 0)
    row = lax.broadcasted_iota(jnp.int32, (8, tt), 0)
    ps = ps_ref[...]
    out = jnp.zeros((8, tt), jnp.float32)
    for k in range(EXPERT_TOPK):
        e = r_ref[k:k + 1, :].astype(jnp.int32)
        start = jnp.sum(jnp.where(erow == e, ps, 0.0), axis=0, keepdims=True)
        out = jnp.where(row == k, start + r_ref[4 + k:5 + k, :], out)
    d_ref[...] = out.astype(jnp.int32)


def _dest(rinfo, padded_start, tt=8192):
    T = rinfo.shape[1]
    tt = min(tt, T)
    return pl.pallas_call(
        _dest_kernel,
        grid=(T // tt,),
        in_specs=[pl.BlockSpec((8, tt), lambda i: (0, i)), pl.BlockSpec((N_EXPERTS, 1), lambda i: (0, 0))],
        out_specs=pl.BlockSpec((8, tt), lambda i: (0, i)),
        out_shape=jax.ShapeDtypeStruct((8, T), jnp.int32),
        compiler_params=pltpu.CompilerParams(dimension_semantics=("parallel",)),
        name="dest",
    )(rinfo, padded_start.astype(jnp.float32)[:, None])


def _sc_workers():
    info = plsc.get_sparse_core_info()
    return info.num_cores, info.num_subcores


def _sc_scatter_rows(src, dest0, dest1, cap):
    T, W = src.shape
    nc, ns = _sc_workers()
    per_w = T // (nc * ns)
    assert per_w * nc * ns == T and per_w % SC_CHUNK == 0
    mesh = plsc.VectorSubcoreMesh(core_axis_name="c", subcore_axis_name="s")

    @functools.partial(
        pl.kernel, mesh=mesh,
        out_type=jax.ShapeDtypeStruct((cap, W), src.dtype),
        scratch_types=[pltpu.VMEM((SC_CHUNK,), jnp.int32), pltpu.VMEM((SC_CHUNK, W), src.dtype)],
    )
    def scatter(src_hbm, d0_hbm, d1_hbm, out_hbm, idx_v, rows_v):
        wid = lax.axis_index("s") * nc + lax.axis_index("c")
        base = wid * per_w

        @pl.loop(0, per_w // SC_CHUNK)
        def _(c):
            off = pl.multiple_of(base + c * SC_CHUNK, SC_CHUNK)
            pltpu.sync_copy(src_hbm.at[pl.ds(off, SC_CHUNK)], rows_v)
            for d_hbm in (d0_hbm, d1_hbm):
                pltpu.sync_copy(d_hbm.at[pl.ds(off, SC_CHUNK)], idx_v)
                pltpu.sync_copy(rows_v, out_hbm.at[idx_v])

    return scatter(src, dest0, dest1)


def _sc_gather_rows(table, idx):
    N = idx.shape[0]
    W = table.shape[1]
    nc, ns = _sc_workers()
    per_w = N // (nc * ns)
    assert per_w * nc * ns == N and per_w % SC_CHUNK == 0
    mesh = plsc.VectorSubcoreMesh(core_axis_name="c", subcore_axis_name="s")

    @functools.partial(
        pl.kernel, mesh=mesh,
        out_type=jax.ShapeDtypeStruct((N, W), table.dtype),
        scratch_types=[pltpu.VMEM((SC_CHUNK,), jnp.int32), pltpu.VMEM((SC_CHUNK, W), table.dtype)],
    )
    def gather(table_hbm, idx_hbm, out_hbm, idx_v, rows_v):
        wid = lax.axis_index("s") * nc + lax.axis_index("c")
        base = wid * per_w

        @pl.loop(0, per_w // SC_CHUNK)
        def _(c):
            off = pl.multiple_of(base + c * SC_CHUNK, SC_CHUNK)
            pltpu.sync_copy(idx_hbm.at[pl.ds(off, SC_CHUNK)], idx_v)
            pltpu.sync_copy(table_hbm.at[idx_v], rows_v)
            pltpu.sync_copy(rows_v, out_hbm.at[pl.ds(off, SC_CHUNK)])

    return gather(table, idx)


def _expert_kernel(be_ref, nv_ref, x_ref, wgu_ref, wd_ref, y_ref):
    i = pl.program_id(0)
    nv = nv_ref[i]

    @pl.when(nv > 0)
    def _():
        lo, hi = _unpack_bf16_pair(x_ref[...])
        xb = jnp.concatenate([lo, hi], axis=1)
        rows = lax.broadcasted_iota(jnp.int32, xb.shape, 0)
        xb = jnp.where(rows < nv, xb, jnp.zeros_like(xb))
        gu = jnp.dot(xb, wgu_ref[0], preferred_element_type=jnp.float32)
        g = gu[:, :D_EXPERT]
        u = gu[:, D_EXPERT:]
        act = (g / (1.0 + jnp.exp(-g))) * u
        y = jnp.dot(act.astype(jnp.bfloat16), wd_ref[0], preferred_element_type=jnp.float32)
        y_ref[...] = _pack_bf16_pair(y[:, :D_MODEL // 2], y[:, D_MODEL // 2:])

    @pl.when(nv <= 0)
    def _():
        y_ref[...] = jnp.zeros_like(y_ref)


def _experts(blk_expert, blk_valid, xs, wgu, wd):
    cap, DW = xs.shape
    D = 2 * DW
    n_blocks = cap // MOE_TM
    return pl.pallas_call(
        _expert_kernel,
        grid_spec=pltpu.PrefetchScalarGridSpec(
            num_scalar_prefetch=2,
            grid=(n_blocks,),
            in_specs=[
                pl.BlockSpec((MOE_TM, DW), lambda i, be, nv: (i, 0)),
                pl.BlockSpec((1, D, 2 * D_EXPERT), lambda i, be, nv: (be[i], 0, 0)),
                pl.BlockSpec((1, D_EXPERT, D), lambda i, be, nv: (be[i], 0, 0)),
            ],
            out_specs=pl.BlockSpec((MOE_TM, DW), lambda i, be, nv: (i, 0)),
        ),
        out_shape=jax.ShapeDtypeStruct((cap, DW), jnp.int32),
        compiler_params=pltpu.CompilerParams(
            dimension_semantics=("arbitrary",), vmem_limit_bytes=VMEM_LIMIT),
        name="experts",
    )(blk_expert, blk_valid, xs, wgu, wd)


def _final_kernel(x1_ref, y0_ref, y1_ref, w_ref, g2_ref, b2_ref, o_ref):
    w = w_ref[...]
    halves = []
    for part in range(2):
        y0 = _unpack_bf16_pair(y0_ref[...])[part].astype(jnp.float32)
        y1 = _unpack_bf16_pair(y1_ref[...])[part].astype(jnp.float32)
        halves.append(y0 * w[:, 0:1] + y1 * w[:, 1:2])
    moe = jnp.concatenate(halves, axis=1)
    o_ref[...] = _layer_norm(DEEPNORM_ALPHA * x1_ref[...] + moe, g2_ref[...], b2_ref[...])


def _final(x1, ypair, w2, g2, b2, tm=1024):
    T, D = x1.shape
    const = lambda i: (0, 0)
    nt = T // tm
    return pl.pallas_call(
        _final_kernel,
        grid=(nt,),
        in_specs=[
            pl.BlockSpec((tm, D), lambda i: (i, 0)),
            pl.BlockSpec((tm, D // 2), lambda i: (i, 0)),
            pl.BlockSpec((tm, D // 2), lambda i: (i + nt, 0)),
            pl.BlockSpec((tm, EXPERT_TOPK), lambda i: (i, 0)),
            pl.BlockSpec((1, D), const),
            pl.BlockSpec((1, D), const),
        ],
        out_specs=pl.BlockSpec((tm, D), lambda i: (i, 0)),
        out_shape=jax.ShapeDtypeStruct((T, D), jnp.float32),
        compiler_params=pltpu.CompilerParams(
            dimension_semantics=("parallel",), vmem_limit_bytes=VMEM_LIMIT),
        name="final",
    )(x1, ypair, ypair, w2, g2, b2)


def _attention_bias_tables(rel_bias_table):
    rel = rel_bias_table.astype(jnp.float32)
    rel_a = rel[:, :SWA_Q_HEADS]
    rel_b = rel[:, SWA_Q_HEADS:]
    kj = np.arange(2 * SWA_BLOCK)[:, None]
    qi = np.arange(SWA_BLOCK)[None, :]
    dist = SWA_BLOCK + qi - kj
    vis = (dist >= 0) & (dist < SWA_WINDOW)
    bias_a = jnp.where(vis[None], rel_a.T[:, _rel_bucket_np(dist)], NEG_INF)
    j = np.arange(MOBA_BLOCK)[:, None]
    i = np.arange(MOBA_BLOCK)[None, :]
    d_own = i - j
    d_prev = MOBA_BLOCK + i - j
    tb = rel_b.T
    far = tb[:, REL_BUCKETS - 1][:, None, None]
    own = jnp.where((d_own >= 0)[None], tb[:, _rel_bucket_np(d_own)] - far, NEG_INF)
    prev = tb[:, _rel_bucket_np(d_prev)] - far
    bias_b = jnp.concatenate([prev, own], axis=1)
    return bias_a, bias_b


def _block_plan(sizes, n_tok):
    n_assign = n_tok * EXPERT_TOPK
    padded = ((sizes + MOE_TM - 1) // MOE_TM) * MOE_TM
    padded_end = jnp.cumsum(padded)
    padded_start = padded_end - padded
    cap = -(-n_assign // MOE_TM) * MOE_TM + N_EXPERTS * MOE_TM
    blk_start = jnp.arange(cap // MOE_TM, dtype=jnp.int32) * MOE_TM
    blk_expert = jnp.minimum(
        jnp.sum(padded_end[None, :] <= blk_start[:, None], axis=1), N_EXPERTS - 1).astype(jnp.int32)
    used = blk_start < padded_end[-1]
    blk_valid = jnp.clip(sizes[blk_expert] - (blk_start - padded_start[blk_expert]), 0, MOE_TM)
    blk_valid = jnp.where(used, blk_valid, 0).astype(jnp.int32)
    return padded_start, blk_expert, blk_valid, cap


def kernel(x, w_in, b_in, attn_sinks, rel_bias_table, w_branch_swa, w_branch_moba, w_out, ln1_gain, ln1_bias,
           w_group_router, b_group_router, w_expert_router, b_expert_router, w_expert_gate, w_expert_up,
           w_expert_down, ln2_gain, ln2_bias):
    assert w_in.shape[0] == DEPTH == 1
    B, S, D = x.shape
    T = B * S
    bf16 = jnp.bfloat16
    f32 = jnp.float32
    w = w_in[0]
    b = b_in[0]

    def cols(off, width):
        return w[:, off:off + width], b[off:off + width]

    wq_a, bq_a = cols(OFF_SWA_Q, SWA_Q_W)
    wk_a, bk_a = cols(OFF_SWA_K, SWA_KV_W)
    wv_a, bv_a = cols(OFF_SWA_V, SWA_KV_W)
    wq_b, bq_b = cols(OFF_MOBA_Q, MOBA_W)
    wk_b, bk_b = cols(OFF_MOBA_K, MOBA_W)
    wv_b, bv_b = cols(OFF_MOBA_V, MOBA_W)

    def dup_kv(t):
        parts = [t[..., i * HEAD_DIM:(i + 1) * HEAD_DIM] for i in range(SWA_KV_HEADS)]
        return jnp.concatenate([p for p in parts for _ in range(2)], axis=-1)

    wn = jnp.concatenate([wq_a * ATTN_SCALE, dup_kv(wk_a), wq_b * ATTN_SCALE, wk_b], axis=1).astype(bf16)
    bn = jnp.concatenate([bq_a * ATTN_SCALE, dup_kv(bk_a), bq_b * ATTN_SCALE, bk_b])[None, :].astype(f32)
    wt = jnp.concatenate([wv_a, wv_b], axis=1).T.astype(bf16)
    bt = jnp.concatenate([bv_a, bv_b])[:, None].astype(f32)

    qk, vt = _inproj(x, wn, bn, wt, bt)

    bias_a, bias_b = _attention_bias_tables(rel_bias_table)
    y_a = _swa(attn_sinks[0].astype(f32), qk, vt, bias_a)
    y_b = _moba(qk, vt, bias_b)

    wg, bg = cols(OFF_GATE, 2 * D_MODEL)
    wr = jnp.zeros((ROUTER_ROWS, D), f32)
    wr = wr.at[0:N_GROUPS].set(w_group_router[0].T).at[8:8 + N_EXPERTS].set(w_expert_router[0].T)
    wr_hi = wr.astype(bf16)
    wr_lo = (wr - wr_hi.astype(f32)).astype(bf16)
    br = jnp.zeros((ROUTER_ROWS,), f32)
    br = br.at[0:N_GROUPS].set(b_group_router[0]).at[8:8 + N_EXPERTS].set(b_expert_router[0])[:, None]
    x1, x1p, rinfo, counts = _merge(
        x.reshape(T, D), y_a.reshape(T, SWA_Q_W), y_b.reshape(T, MOBA_W),
        wg.astype(bf16), bg[None, :].astype(f32), w_branch_swa[0].astype(bf16), w_branch_moba[0].astype(bf16),
        w_out[0].astype(bf16), ln1_gain[0][None, :].astype(f32), ln1_bias[0][None, :].astype(f32),
        jnp.concatenate([wr_hi, wr_lo], axis=0), br)

    w2 = jnp.stack([rinfo[2], rinfo[3]], axis=1)
    sizes = counts[:, 0].astype(jnp.int32)
    padded_start, blk_expert, blk_valid, cap = _block_plan(sizes, T)
    dest = _dest(rinfo, padded_start)
    xs = _sc_scatter_rows(x1p, dest[0], dest[1], cap)
    wgu = jnp.concatenate([w_expert_gate[0], w_expert_up[0]], axis=-1).astype(bf16)
    y_buf = _experts(blk_expert, blk_valid, xs, wgu, w_expert_down[0].astype(bf16))
    ypair = _sc_gather_rows(y_buf, dest[0:EXPERT_TOPK].reshape(-1))
    out = _final(x1, ypair, w2, ln2_gain[0][None, :].astype(f32), ln2_bias[0][None, :].astype(f32))
    return out.reshape(B, S, D)
```

```python
import functools
import math

import numpy as np
import jax
import jax.numpy as jnp
from jax import lax
from jax.experimental import pallas as pl
from jax.experimental.pallas import tpu as pltpu
from jax.experimental.pallas import tpu_sc as plsc

D_MODEL = 1024
HEAD_DIM = 64
SWA_Q_HEADS = 8
SWA_KV_HEADS = 2
SWA_GROUP = SWA_Q_HEADS // SWA_KV_HEADS
SWA_WINDOW = 128
SWA_BLOCK = 128
MOBA_HEADS = 8
MOBA_BLOCK = 256
MOBA_TOPK = 3
REL_BUCKETS = 32
REL_MAX_DIST = 128
N_GROUPS = 4
EXPERTS_PER_GROUP = 8
N_EXPERTS = N_GROUPS * EXPERTS_PER_GROUP
EXPERT_TOPK = 2
D_EXPERT = 512
LN_EPS = 1e-5
DEPTH = 1
DEEPNORM_ALPHA = (2.0 * DEPTH) ** 0.25
NEG_INF = -1e30
ATTN_SCALE = HEAD_DIM ** -0.5

SWA_Q_W = SWA_Q_HEADS * HEAD_DIM
SWA_KV_W = SWA_KV_HEADS * HEAD_DIM
MOBA_W = MOBA_HEADS * HEAD_DIM
OFF_SWA_Q = 0
OFF_SWA_K = OFF_SWA_Q + SWA_Q_W
OFF_SWA_V = OFF_SWA_K + SWA_KV_W
OFF_MOBA_Q = OFF_SWA_V + SWA_KV_W
OFF_MOBA_K = OFF_MOBA_Q + MOBA_W
OFF_MOBA_V = OFF_MOBA_K + MOBA_W
OFF_GATE = OFF_MOBA_V + MOBA_W

LANES = 128
QK_SWA_Q = 0
QK_SWA_K = QK_SWA_Q + SWA_Q_W
QK_MOBA_Q = QK_SWA_K + SWA_KV_HEADS * LANES
QK_MOBA_K = QK_MOBA_Q + MOBA_W
QK_W = QK_MOBA_K + MOBA_W
VT_SWA = 0
VT_MOBA = VT_SWA + SWA_KV_W
VT_W = VT_MOBA + MOBA_W

MOE_TM = 512
SC_CHUNK = 128
ROUTER_ROWS = 8 + N_EXPERTS
VMEM_LIMIT = 56 * 1024 * 1024

_NT = (((1,), (1,)), ((), ()))


def _rel_bucket_np(dist):
    n = np.maximum(dist, 0)
    max_exact = REL_BUCKETS // 2
    nf = np.maximum(n, 1).astype(np.float32)
    large = max_exact + (np.log(nf / np.float32(max_exact)) / np.float32(math.log(REL_MAX_DIST / max_exact))
                         * np.float32(REL_BUCKETS - max_exact)).astype(np.int32)
    large = np.minimum(large, REL_BUCKETS - 1)
    return np.where(n < max_exact, n, large).astype(np.int32)


def _inproj_kernel(x_ref, wn_ref, bn_ref, wt_ref, bt_ref, qk_ref, vt_ref):
    xb = x_ref[0].astype(jnp.bfloat16)
    qk = jnp.dot(xb, wn_ref[...], preferred_element_type=jnp.float32) + bn_ref[...]
    qk_ref[0] = qk.astype(jnp.bfloat16)
    vt = lax.dot_general(wt_ref[...], xb, _NT, preferred_element_type=jnp.float32) + bt_ref[...]
    vt_ref[0] = vt.astype(jnp.bfloat16)


def _inproj(x, wn, bn, wt, bt, tm=512):
    B, S, D = x.shape
    return pl.pallas_call(
        _inproj_kernel,
        grid=(B, S // tm),
        in_specs=[
            pl.BlockSpec((1, tm, D), lambda b, i: (b, i, 0)),
            pl.BlockSpec((D, QK_W), lambda b, i: (0, 0)),
            pl.BlockSpec((1, QK_W), lambda b, i: (0, 0)),
            pl.BlockSpec((VT_W, D), lambda b, i: (0, 0)),
            pl.BlockSpec((VT_W, 1), lambda b, i: (0, 0)),
        ],
        out_specs=[
            pl.BlockSpec((1, tm, QK_W), lambda b, i: (b, i, 0)),
            pl.BlockSpec((1, VT_W, tm), lambda b, i: (b, 0, i)),
        ],
        out_shape=[
            jax.ShapeDtypeStruct((B, S, QK_W), jnp.bfloat16),
            jax.ShapeDtypeStruct((B, VT_W, S), jnp.bfloat16),
        ],
        compiler_params=pltpu.CompilerParams(
            dimension_semantics=("parallel", "parallel"), vmem_limit_bytes=VMEM_LIMIT),
        name="inproj",
    )(x, wn, bn, wt, bt)


def _swa_kernel(sink_ref, q_ref, k_ref, vt_ref, bias_ref, o_ref, vaug_ref):
    S = q_ref.shape[1]
    nblk = S // SWA_BLOCK
    half = HEAD_DIM
    vt = vt_ref[0]
    rows = lax.broadcasted_iota(jnp.int32, vt.shape, 0)
    one = jnp.ones_like(vt)
    vaug_ref[0] = jnp.where(rows < half, vt, one)
    vaug_ref[1] = jnp.where(rows < half, one, vt)
    lane = lax.broadcasted_iota(jnp.int32, (SWA_BLOCK, LANES), 1)
    col = lax.broadcasted_iota(jnp.int32, (1, 2 * SWA_BLOCK), 1)
    npair = SWA_Q_HEADS // 2

    def blocks(specs):
        items = [(qs, ks, nk, boff, pair, (2 * pair) // SWA_GROUP) for qs, ks, nk, boff in specs for pair in range(npair)]
        ss = []
        for qs, ks, nk, boff, pair, kv in items:
            qblk = q_ref[0, pl.ds(qs, SWA_BLOCK), pair * LANES:(pair + 1) * LANES]
            zero = jnp.zeros_like(qblk)
            q2 = jnp.concatenate([jnp.where(lane < half, qblk, zero), jnp.where(lane >= half, qblk, zero)], axis=0)
            kblk = k_ref[0, pl.ds(ks, nk), kv * LANES:(kv + 1) * LANES]
            s = lax.dot_general(kblk, q2, _NT, preferred_element_type=jnp.float32)
            ss.append(s + bias_ref[pair, boff:boff + nk, :])
        stats = []
        for (qs, ks, nk, boff, pair, kv), s in zip(items, ss):
            sink = jnp.where(col < SWA_BLOCK, sink_ref[2 * pair], sink_ref[2 * pair + 1])
            m = jnp.maximum(_colmax(s), sink)
            stats.append((jnp.exp(s - m).astype(jnp.bfloat16), jnp.exp(sink - m)))
        rs = [jnp.dot(vaug_ref[kv, :, pl.ds(ks, nk)], p, preferred_element_type=jnp.float32)
              for (qs, ks, nk, boff, pair, kv), (p, _) in zip(items, stats)]
        outs = []
        for (qs, ks, nk, boff, pair, kv), (_, esink), r in zip(items, stats, rs):
            l = (r[half:half + 1, :] if kv == 0 else r[0:1, :]) + esink
            o = (r[0:half, :] if kv == 0 else r[half:, :]) / l
            outs.append(jnp.concatenate([o[:, :SWA_BLOCK], o[:, SWA_BLOCK:]], axis=0))
        for i, (qs, _, _, _) in enumerate(specs):
            ot = jnp.concatenate(outs[i * npair:(i + 1) * npair], axis=0)
            o_ref[0, pl.ds(qs, SWA_BLOCK), :] = ot.T.astype(jnp.bfloat16)

    blocks([(0, 0, SWA_BLOCK, SWA_BLOCK), (SWA_BLOCK, 0, 2 * SWA_BLOCK, 0)])

    def body(t, carry):
        specs = []
        for n in (2 * t, 2 * t + 1):
            qs = pl.multiple_of(n * SWA_BLOCK, SWA_BLOCK)
            specs.append((qs, pl.multiple_of(qs - SWA_BLOCK, SWA_BLOCK), 2 * SWA_BLOCK, 0))
        blocks(specs)
        return carry

    assert nblk % 2 == 0
    lax.fori_loop(1, nblk // 2, body, 0)


def _swa(sinks, qk, vt, bias):
    B, S, _ = qk.shape
    nq = SWA_Q_W // LANES
    return pl.pallas_call(
        _swa_kernel,
        grid_spec=pltpu.PrefetchScalarGridSpec(
            num_scalar_prefetch=1,
            grid=(B,),
            in_specs=[
                pl.BlockSpec((1, S, SWA_Q_W), lambda b, s: (b, 0, QK_SWA_Q // SWA_Q_W)),
                pl.BlockSpec((1, S, 2 * LANES), lambda b, s: (b, 0, QK_SWA_K // (2 * LANES))),
                pl.BlockSpec((1, SWA_KV_W, S), lambda b, s: (b, VT_SWA // SWA_KV_W, 0)),
                pl.BlockSpec((SWA_Q_HEADS // 2, 2 * SWA_BLOCK, 2 * SWA_BLOCK), lambda b, s: (0, 0, 0)),
            ],
            out_specs=pl.BlockSpec((1, S, SWA_Q_W), lambda b, s: (b, 0, 0)),
            scratch_shapes=[pltpu.VMEM((SWA_KV_HEADS, SWA_KV_W, S), jnp.bfloat16)],
        ),
        out_shape=jax.ShapeDtypeStruct((B, S, SWA_Q_W), jnp.bfloat16),
        compiler_params=pltpu.CompilerParams(
            dimension_semantics=("parallel",), vmem_limit_bytes=VMEM_LIMIT),
        name="swa",
    )(sinks, qk, qk, vt, bias)


def _colmax(s):
    while s.shape[0] > 8:
        h = s.shape[0] // 2
        s = jnp.maximum(s[:h], s[h:])
    return jnp.max(s, axis=0, keepdims=True)


def _moba_kernel(fqb_ref, fc_ref, q_ref, k_ref, vt_ref, bias_ref, o_ref,
                 vaug_ref, gate_ref, rank_ref, sel_ref, far_ref, m_scr, acc_scr, sa_scr, sb_scr):
    S = q_ref.shape[1]
    nb = S // MOBA_BLOCK
    half = HEAD_DIM
    BLK = MOBA_BLOCK

    vt = vt_ref[0]
    rows = lax.broadcasted_iota(jnp.int32, vt.shape, 0)
    one = jnp.ones_like(vt)
    vaug_ref[0] = jnp.where(rows < half, vt, one)
    vaug_ref[1] = jnp.where(rows < half, one, vt)

    kf = k_ref[0].astype(jnp.float32).reshape(nb, BLK, LANES)
    kmean = jnp.sum(kf, axis=1) * (1.0 / BLK)
    k_hi = kmean.astype(jnp.bfloat16)
    k_lo = (kmean - k_hi.astype(jnp.float32)).astype(jnp.bfloat16)
    kcat = jnp.concatenate([k_hi, k_lo], axis=0)

    lane_q = lax.broadcasted_iota(jnp.int32, (S, LANES), 1)
    brow = lax.broadcasted_iota(jnp.int32, (nb, S), 0)
    qblk_of = lax.broadcasted_iota(jnp.int32, (nb, S), 1) // BLK
    qall = q_ref[0]
    for h in range(2):
        qh = jnp.where((lane_q < half) if h == 0 else (lane_q >= half), qall, jnp.zeros_like(qall))
        g2 = lax.dot_general(kcat, qh, _NT, preferred_element_type=jnp.float32)
        gate_ref[...] = g2[0:nb] + g2[nb:2 * nb]
        rank_ref[...] = jnp.zeros_like(rank_ref)
        for m in range(nb - 1):
            lo = (m + 1) * BLK
            G = gate_ref[:, lo:]
            gm = gate_ref[m:m + 1, lo:]
            ge = jnp.where(gm >= G, 1.0, 0.0)
            gt = jnp.where(gm > G, 1.0, 0.0)
            brow_m = lax.broadcasted_iota(jnp.int32, (nb, S - lo), 0)
            rank_ref[:, lo:] += jnp.where(brow_m > m, ge, gt)
        top = rank_ref[...] < float(MOBA_TOPK)
        sel_ref[h] = jnp.where((brow < qblk_of) & top, 0.0, NEG_INF)
        far_ref[h] = jnp.where((brow < qblk_of - 1) & top, 0.0, NEG_INF)

    lane = lax.broadcasted_iota(jnp.int32, (BLK, LANES), 1)
    orow = lax.broadcasted_iota(jnp.int32, (LANES, BLK), 0)

    def head_q(qs, h):
        qblk = q_ref[0, pl.ds(qs, BLK), :]
        return jnp.where((lane < half) if h == 0 else (lane >= half), qblk, jnp.zeros_like(qblk))

    m0 = jnp.full((1, BLK), NEG_INF, jnp.float32)
    acc0 = jnp.zeros((LANES, BLK), jnp.float32)

    def item(h, qb, qs, ks, nk, bias, sel_a, sel_b, m, acc):
        return dict(h=h, qb=qb, qs=qs, ks=ks, nk=nk, bias=bias, sel_a=sel_a, sel_b=sel_b, m=m, acc=acc)

    def near_item(qb, h):
        qs = pl.multiple_of(qb * BLK, BLK)
        ps = pl.multiple_of(qs - BLK, BLK)
        sel_prev = sel_ref[h, pl.ds(qb - 1, 1), pl.ds(qs, BLK)]
        return item(h, qb, qs, ps, 2 * BLK, bias_ref[h], sel_prev, None, m0, acc0)

    def far_item(i, h):
        qb = fqb_ref[i]
        c = fc_ref[i]
        qs = pl.multiple_of(qb * BLK, BLK)
        ks = pl.multiple_of(c * (2 * BLK), 2 * BLK)
        sel_a = far_ref[h, pl.ds(2 * c, 1), pl.ds(qs, BLK)]
        sel_b = far_ref[h, pl.ds(2 * c + 1, 1), pl.ds(qs, BLK)]
        return item(h, qb, qs, ks, 2 * BLK, None, sel_a, sel_b, m_scr[h, qb], acc_scr[h, qb])

    def scores(it):
        kslab = k_ref[0, pl.ds(it["ks"], it["nk"]), :]
        s = lax.dot_general(kslab, head_q(it["qs"], it["h"]), _NT,
                            preferred_element_type=jnp.float32)
        return s if it["bias"] is None else s + it["bias"]

    def probs(it, s):
        m, sel_a, sel_b = it["m"], it["sel_a"], it["sel_b"]
        if it["nk"] == BLK:
            m_new = jnp.maximum(m, _colmax(s))
            p = jnp.exp(s - m_new)
        else:
            cm_a = _colmax(s[:BLK]) + sel_a
            cm_b = _colmax(s[BLK:])
            if sel_b is not None:
                cm_b = cm_b + sel_b
            m_new = jnp.maximum(m, jnp.maximum(cm_a, cm_b))
            p_a = jnp.exp(s[:BLK] - (m_new - sel_a))
            p_b = jnp.exp(s[BLK:] - (m_new if sel_b is None else m_new - sel_b))
            p = jnp.concatenate([p_a, p_b], axis=0)
        return m_new, jnp.exp(m - m_new), p.astype(jnp.bfloat16)

    def run(items):
        ss = [scores(it) for it in items]
        ps = [probs(it, s) for it, s in zip(items, ss)]
        pvs = [jnp.dot(vaug_ref[it["h"], :, pl.ds(it["ks"], it["nk"])], p, preferred_element_type=jnp.float32)
               for it, (_, _, p) in zip(items, ps)]
        for it, (m_new, alpha, _), pv in zip(items, ps, pvs):
            m_scr[it["h"], it["qb"]] = m_new
            acc_scr[it["h"], it["qb"]] = it["acc"] * alpha + pv

    run([item(h, 0, 0, 0, BLK, bias_ref[h, BLK:, :], None, None, m0, acc0) for h in range(2)]
        + [near_item(nb - 1, h) for h in range(2)])

    near_pairs = nb // 2 - 1

    def near_body(i, carry):
        run([near_item(qb, h) for qb in (i, i + near_pairs) for h in range(2)])
        return carry

    lax.fori_loop(1, near_pairs + 1, near_body, 0)

    n_far = fqb_ref.shape[0]
    n_groups = n_far // 2

    def group(g):
        return [(j, h) for j in (g, g + n_groups) for h in range(2)]

    def score_stage(g, s_buf):
        for k, (j, h) in enumerate(group(g)):
            qs = pl.multiple_of(fqb_ref[j] * BLK, BLK)
            ks = pl.multiple_of(fc_ref[j] * (2 * BLK), 2 * BLK)
            s_buf[k] = lax.dot_general(k_ref[0, pl.ds(ks, 2 * BLK), :], head_q(qs, h), _NT,
                                       preferred_element_type=jnp.float32)

    def finish_stage(g, s_buf):
        items = [far_item(j, h) for j, h in group(g)]
        ps = [probs(it, s_buf[k]) for k, it in enumerate(items)]
        pvs = [jnp.dot(vaug_ref[it["h"], :, pl.ds(it["ks"], it["nk"])], p, preferred_element_type=jnp.float32)
               for it, (_, _, p) in zip(items, ps)]
        for it, (m_new, alpha, _), pv in zip(items, ps, pvs):
            m_scr[it["h"], it["qb"]] = m_new
            acc_scr[it["h"], it["qb"]] = it["acc"] * alpha + pv

    score_stage(0, sa_scr)

    def far_body(t, carry):
        g = 2 * t
        score_stage(g + 1, sb_scr)
        finish_stage(g, sa_scr)
        score_stage(jnp.minimum(g + 2, n_groups - 1), sa_scr)
        finish_stage(g + 1, sb_scr)
        return carry

    assert n_groups % 2 == 0
    lax.fori_loop(0, n_groups // 2, far_body, 0)

    def out_body(qb, carry):
        qs = pl.multiple_of(qb * BLK, BLK)
        a0 = acc_scr[0, qb]
        a1 = acc_scr[1, qb]
        ot = jnp.where(orow < half, a0 / a0[half:half + 1, :], a1 / a1[0:1, :])
        o_ref[0, pl.ds(qs, BLK), :] = ot.T.astype(jnp.bfloat16)
        return carry

    lax.fori_loop(0, nb, out_body, 0)


def _moba_far_items(nb):
    items = [(qb, c) for qb in range(2, nb) for c in range(qb // 2)]
    n = len(items)
    assert n % 2 == 0 and all(items[i][0] != items[i + n // 2][0] for i in range(n // 2))
    return np.array([it[0] for it in items], np.int32), np.array([it[1] for it in items], np.int32)


def _moba(qk, vt, bias):
    B, S, _ = qk.shape
    nb = S // MOBA_BLOCK
    npair = MOBA_W // LANES
    far_qb, far_c = _moba_far_items(nb)
    return pl.pallas_call(
        _moba_kernel,
        grid_spec=pltpu.PrefetchScalarGridSpec(
            num_scalar_prefetch=2,
            grid=(B, npair),
            in_specs=[
                pl.BlockSpec((1, S, LANES), lambda b, p, fq, fc: (b, 0, QK_MOBA_Q // LANES + p)),
                pl.BlockSpec((1, S, LANES), lambda b, p, fq, fc: (b, 0, QK_MOBA_K // LANES + p)),
                pl.BlockSpec((1, LANES, S), lambda b, p, fq, fc: (b, VT_MOBA // LANES + p, 0)),
                pl.BlockSpec((2, 2 * MOBA_BLOCK, MOBA_BLOCK), lambda b, p, fq, fc: (p, 0, 0)),
            ],
            out_specs=pl.BlockSpec((1, S, LANES), lambda b, p, fq, fc: (b, 0, p)),
            scratch_shapes=[
                pltpu.VMEM((2, LANES, S), jnp.bfloat16),
                pltpu.VMEM((nb, S), jnp.float32),
                pltpu.VMEM((nb, S), jnp.float32),
                pltpu.VMEM((2, nb, S), jnp.float32),
                pltpu.VMEM((2, nb, S), jnp.float32),
                pltpu.VMEM((2, nb, 1, MOBA_BLOCK), jnp.float32),
                pltpu.VMEM((2, nb, LANES, MOBA_BLOCK), jnp.float32),
                pltpu.VMEM((4, 2 * MOBA_BLOCK, MOBA_BLOCK), jnp.float32),
                pltpu.VMEM((4, 2 * MOBA_BLOCK, MOBA_BLOCK), jnp.float32),
            ],
        ),
        out_shape=jax.ShapeDtypeStruct((B, S, MOBA_W), jnp.bfloat16),
        compiler_params=pltpu.CompilerParams(
            dimension_semantics=("parallel", "parallel"), vmem_limit_bytes=VMEM_LIMIT),
        name="moba",
    )(jnp.asarray(far_qb), jnp.asarray(far_c), qk, qk, vt, bias)


def _layer_norm(h, gain, bias):
    mu = jnp.mean(h, axis=-1, keepdims=True)
    c = h - mu
    var = jnp.mean(c * c, axis=-1, keepdims=True)
    return c * lax.rsqrt(var + LN_EPS) * gain + bias


def _pack_bf16_pair(a, b):
    ia = lax.bitcast_convert_type(a.astype(jnp.bfloat16).astype(jnp.float32), jnp.int32)
    ib = lax.bitcast_convert_type(b.astype(jnp.bfloat16).astype(jnp.float32), jnp.int32)
    return lax.shift_right_logical(ia, 16) | ib


def _unpack_bf16_pair(w):
    lo = lax.bitcast_convert_type(lax.shift_left(w, 16), jnp.float32)
    hi = lax.bitcast_convert_type(w & jnp.int32(-65536), jnp.float32)
    return lo.astype(jnp.bfloat16), hi.astype(jnp.bfloat16)


def _merge_kernel(x_ref, ya_ref, yb_ref, wg_ref, bg_ref, wa_ref, wb_ref, wo_ref, g1_ref, b1_ref,
                  wr_ref, br_ref, tri_ref, x1_ref, x1p_ref, r_ref, cnt_ref):
    x = x_ref[...]
    xb = x.astype(jnp.bfloat16)
    z = jnp.dot(xb, wg_ref[...], preferred_element_type=jnp.float32) + bg_ref[...]
    gates = 1.0 / (1.0 + jnp.exp(-z))
    pa = jnp.dot(ya_ref[...], wa_ref[...], preferred_element_type=jnp.float32)
    pb = jnp.dot(yb_ref[...], wb_ref[...], preferred_element_type=jnp.float32)
    merged = gates[:, :D_MODEL] * pa + gates[:, D_MODEL:] * pb
    mixed = jnp.dot(merged.astype(jnp.bfloat16), wo_ref[...], preferred_element_type=jnp.float32)
    x1 = _layer_norm(DEEPNORM_ALPHA * x + mixed, g1_ref[...], b1_ref[...])
    x1_ref[...] = x1
    x1_hi = x1.astype(jnp.bfloat16)
    x1p_ref[...] = _pack_bf16_pair(x1[:, :D_MODEL // 2], x1[:, D_MODEL // 2:])
    x1_lo = (x1 - x1_hi.astype(jnp.float32)).astype(jnp.bfloat16)

    R = ROUTER_ROWS
    l1 = lax.dot_general(wr_ref[...], x1_hi, _NT, preferred_element_type=jnp.float32)
    l2 = lax.dot_general(wr_ref[0:R, :], x1_lo, _NT, preferred_element_type=jnp.float32)
    L = l1[0:R] + l1[R:2 * R] + l2 + br_ref[...]
    tm = x.shape[0]
    row = lax.broadcasted_iota(jnp.int32, (8, tm), 0)
    big = jnp.float32(-3e38)
    gl = jnp.where(row < N_GROUPS, L[0:8], big)
    gmax = jnp.max(gl, axis=0, keepdims=True)
    g_idx = jnp.min(jnp.where(gl == gmax, row, 8), axis=0, keepdims=True)
    gsum = jnp.sum(jnp.where(row < N_GROUPS, jnp.exp(gl - gmax), 0.0), axis=0, keepdims=True)
    g_prob = 1.0 / gsum
    E = L[8 + 8 * (N_GROUPS - 1):8 + 8 * N_GROUPS]
    for g in range(N_GROUPS - 2, -1, -1):
        E = jnp.where(g_idx == g, L[8 + 8 * g:16 + 8 * g], E)
    t0 = jnp.max(E, axis=0, keepdims=True)
    loc0 = jnp.min(jnp.where(E == t0, row, 8), axis=0, keepdims=True)
    E2 = jnp.where(row == loc0, big, E)
    t1 = jnp.max(E2, axis=0, keepdims=True)
    loc1 = jnp.min(jnp.where(E2 == t1, row, 8), axis=0, keepdims=True)
    ex = jnp.exp(t1 - t0)
    w0 = g_prob / (1.0 + ex)
    w1 = g_prob * ex / (1.0 + ex)
    e0i = g_idx * EXPERTS_PER_GROUP + loc0
    e1i = g_idx * EXPERTS_PER_GROUP + loc1

    erow = lax.broadcasted_iota(jnp.int32, (N_EXPERTS, tm), 0)
    oh0 = jnp.where(erow == e0i, 1.0, 0.0)
    oh1 = jnp.where(erow == e1i, 1.0, 0.0)
    both = oh0 + oh1

    @pl.when(pl.program_id(0) == 0)
    def _():
        cnt_ref[...] = jnp.zeros_like(cnt_ref)

    prefix = jnp.dot(both.astype(jnp.bfloat16), tri_ref[...], preferred_element_type=jnp.float32)
    prefix = prefix + cnt_ref[:, 0:1]
    rank0 = jnp.sum(oh0 * prefix, axis=0, keepdims=True)
    rank1 = jnp.sum(oh1 * prefix, axis=0, keepdims=True)
    cnt_ref[...] = cnt_ref[...] + jnp.sum(both, axis=1, keepdims=True)

    vals = (e0i.astype(jnp.float32), e1i.astype(jnp.float32), w0, w1, rank0, rank1)
    out = jnp.zeros((8, tm), jnp.float32)
    for k, v in enumerate(vals):
        out = jnp.where(row == k, v, out)
    r_ref[...] = out


def _merge(x2, ya, yb, wg, bg, wa, wb, wo, g1, b1, wr, br, tm=512):
    T, D = x2.shape
    const = lambda i: (0, 0)
    tri = jnp.triu(jnp.ones((tm, tm), jnp.bfloat16), k=1)
    return pl.pallas_call(
        _merge_kernel,
        grid=(T // tm,),
        in_specs=[
            pl.BlockSpec((tm, D), lambda i: (i, 0)),
            pl.BlockSpec((tm, SWA_Q_W), lambda i: (i, 0)),
            pl.BlockSpec((tm, MOBA_W), lambda i: (i, 0)),
            pl.BlockSpec((D, 2 * D), const),
            pl.BlockSpec((1, 2 * D), const),
            pl.BlockSpec((SWA_Q_W, D), const),
            pl.BlockSpec((MOBA_W, D), const),
            pl.BlockSpec((D, D), const),
            pl.BlockSpec((1, D), const),
            pl.BlockSpec((1, D), const),
            pl.BlockSpec((2 * ROUTER_ROWS, D), const),
            pl.BlockSpec((ROUTER_ROWS, 1), const),
            pl.BlockSpec((tm, tm), const),
        ],
        out_specs=[
            pl.BlockSpec((tm, D), lambda i: (i, 0)),
            pl.BlockSpec((tm, D // 2), lambda i: (i, 0)),
            pl.BlockSpec((8, tm), lambda i: (0, i)),
            pl.BlockSpec((N_EXPERTS, LANES), const),
        ],
        out_shape=[
            jax.ShapeDtypeStruct((T, D), jnp.float32),
            jax.ShapeDtypeStruct((T, D // 2), jnp.int32),
            jax.ShapeDtypeStruct((8, T), jnp.float32),
            jax.ShapeDtypeStruct((N_EXPERTS, LANES), jnp.float32),
        ],
        compiler_params=pltpu.CompilerParams(
            dimension_semantics=("arbitrary",), vmem_limit_bytes=VMEM_LIMIT),
        name="merge",
    )(x2, ya, yb, wg, bg, wa, wb, wo, g1, b1, wr, br, tri)


def _dest_kernel(r_ref, ps_ref, d_ref):
    tt = r_ref.shape[1]
    erow = lax.broadcasted_iota(jnp.int32, (N_EXPERTS, tt), 0)
    row = lax.broadcasted_iota(jnp.int32, (8, tt), 0)
    ps = ps_ref[...]
    out = jnp.zeros((8, tt), jnp.float32)
    for k in range(EXPERT_TOPK):
        e = r_ref[k:k + 1, :].astype(jnp.int32)
        start = jnp.sum(jnp.where(erow == e, ps, 0.0), axis=0, keepdims=True)
        out = jnp.where(row == k, start + r_ref[4 + k:5 + k, :], out)
    d_ref[...] = out.astype(jnp.int32)


def _dest(rinfo, padded_start, tt=8192):
    T = rinfo.shape[1]
    tt = min(tt, T)
    return pl.pallas_call(
        _dest_kernel,
        grid=(T // tt,),
        in_specs=[pl.BlockSpec((8, tt), lambda i: (0, i)), pl.BlockSpec((N_EXPERTS, 1), lambda i: (0, 0))],
        out_specs=pl.BlockSpec((8, tt), lambda i: (0, i)),
        out_shape=jax.ShapeDtypeStruct((8, T), jnp.int32),
        compiler_params=pltpu.CompilerParams(dimension_semantics=("parallel",)),
        name="dest",
    )(rinfo, padded_start.astype(jnp.float32)[:, None])


def _sc_workers():
    info = plsc.get_sparse_core_info()
    return info.num_cores, info.num_subcores


def _sc_scatter_rows(src, dest0, dest1, cap):
    T, W = src.shape
    nc, ns = _sc_workers()
    per_w = T // (nc * ns)
    assert per_w * nc * ns == T and per_w % SC_CHUNK == 0
    mesh = plsc.VectorSubcoreMesh(core_axis_name="c", subcore_axis_name="s")

    @functools.partial(
        pl.kernel, mesh=mesh,
        out_type=jax.ShapeDtypeStruct((cap, W), src.dtype),
        scratch_types=[pltpu.VMEM((SC_CHUNK,), jnp.int32), pltpu.VMEM((SC_CHUNK, W), src.dtype)],
    )
    def scatter(src_hbm, d0_hbm, d1_hbm, out_hbm, idx_v, rows_v):
        wid = lax.axis_index("s") * nc + lax.axis_index("c")
        base = wid * per_w

        @pl.loop(0, per_w // SC_CHUNK)
        def _(c):
            off = pl.multiple_of(base + c * SC_CHUNK, SC_CHUNK)
            pltpu.sync_copy(src_hbm.at[pl.ds(off, SC_CHUNK)], rows_v)
            for d_hbm in (d0_hbm, d1_hbm):
                pltpu.sync_copy(d_hbm.at[pl.ds(off, SC_CHUNK)], idx_v)
                pltpu.sync_copy(rows_v, out_hbm.at[idx_v])

    return scatter(src, dest0, dest1)


def _sc_gather_rows(table, idx):
    N = idx.shape[0]
    W = table.shape[1]
    nc, ns = _sc_workers()
    per_w = N // (nc * ns)
    assert per_w * nc * ns == N and per_w % SC_CHUNK == 0
    mesh = plsc.VectorSubcoreMesh(core_axis_name="c", subcore_axis_name="s")

    @functools.partial(
        pl.kernel, mesh=mesh,
        out_type=jax.ShapeDtypeStruct((N, W), table.dtype),
        scratch_types=[pltpu.VMEM((SC_CHUNK,), jnp.int32), pltpu.VMEM((SC_CHUNK, W), table.dtype)],
    )
    def gather(table_hbm, idx_hbm, out_hbm, idx_v, rows_v):
        wid = lax.axis_index("s") * nc + lax.axis_index("c")
        base = wid * per_w

        @pl.loop(0, per_w // SC_CHUNK)
        def _(c):
            off = pl.multiple_of(base + c * SC_CHUNK, SC_CHUNK)
            pltpu.sync_copy(idx_hbm.at[pl.ds(off, SC_CHUNK)], idx_v)
            pltpu.sync_copy(table_hbm.at[idx_v], rows_v)
            pltpu.sync_copy(rows_v, out_hbm.at[pl.ds(off, SC_CHUNK)])

    return gather(table, idx)


def _expert_kernel(be_ref, nv_ref, x_ref, wgu_ref, wd_ref, y_ref):
    i = pl.program_id(0)
    nv = nv_ref[i]

    @pl.when(nv > 0)
    def _():
        lo, hi = _unpack_bf16_pair(x_ref[...])
        xb = jnp.concatenate([lo, hi], axis=1)
        rows = lax.broadcasted_iota(jnp.int32, xb.shape, 0)
        xb = jnp.where(rows < nv, xb, jnp.zeros_like(xb))
        gu = jnp.dot(xb, wgu_ref[0], preferred_element_type=jnp.float32)
        g = gu[:, :D_EXPERT]
        u = gu[:, D_EXPERT:]
        act = (g / (1.0 + jnp.exp(-g))) * u
        y = jnp.dot(act.astype(jnp.bfloat16), wd_ref[0], preferred_element_type=jnp.float32)
        y_ref[...] = _pack_bf16_pair(y[:, :D_MODEL // 2], y[:, D_MODEL // 2:])

    @pl.when(nv <= 0)
    def _():
        y_ref[...] = jnp.zeros_like(y_ref)


def _experts(blk_expert, blk_valid, xs, wgu, wd):
    cap, DW = xs.shape
    D = 2 * DW
    n_blocks = cap // MOE_TM
    return pl.pallas_call(
        _expert_kernel,
        grid_spec=pltpu.PrefetchScalarGridSpec(
            num_scalar_prefetch=2,
            grid=(n_blocks,),
            in_specs=[
                pl.BlockSpec((MOE_TM, DW), lambda i, be, nv: (i, 0)),
                pl.BlockSpec((1, D, 2 * D_EXPERT), lambda i, be, nv: (be[i], 0, 0)),
                pl.BlockSpec((1, D_EXPERT, D), lambda i, be, nv: (be[i], 0, 0)),
            ],
            out_specs=pl.BlockSpec((MOE_TM, DW), lambda i, be, nv: (i, 0)),
        ),
        out_shape=jax.ShapeDtypeStruct((cap, DW), jnp.int32),
        compiler_params=pltpu.CompilerParams(
            dimension_semantics=("arbitrary",), vmem_limit_bytes=VMEM_LIMIT),
        name="experts",
    )(blk_expert, blk_valid, xs, wgu, wd)


def _final_kernel(x1_ref, y0_ref, y1_ref, w_ref, g2_ref, b2_ref, o_ref):
    w = w_ref[...]
    halves = []
    for part in range(2):
        y0 = _unpack_bf16_pair(y0_ref[...])[part].astype(jnp.float32)
        y1 = _unpack_bf16_pair(y1_ref[...])[part].astype(jnp.float32)
        halves.append(y0 * w[:, 0:1] + y1 * w[:, 1:2])
    moe = jnp.concatenate(halves, axis=1)
    o_ref[...] = _layer_norm(DEEPNORM_ALPHA * x1_ref[...] + moe, g2_ref[...], b2_ref[...])


def _final(x1, ypair, w2, g2, b2, tm=1024):
    T, D = x1.shape
    const = lambda i: (0, 0)
    nt = T // tm
    return pl.pallas_call(
        _final_kernel,
        grid=(nt,),
        in_specs=[
            pl.BlockSpec((tm, D), lambda i: (i, 0)),
            pl.BlockSpec((tm, D // 2), lambda i: (i, 0)),
            pl.BlockSpec((tm, D // 2), lambda i: (i + nt, 0)),
            pl.BlockSpec((tm, EXPERT_TOPK), lambda i: (i, 0)),
            pl.BlockSpec((1, D), const),
            pl.BlockSpec((1, D), const),
        ],
        out_specs=pl.BlockSpec((tm, D), lambda i: (i, 0)),
        out_shape=jax.ShapeDtypeStruct((T, D), jnp.float32),
        compiler_params=pltpu.CompilerParams(
            dimension_semantics=("parallel",), vmem_limit_bytes=VMEM_LIMIT),
        name="final",
    )(x1, ypair, ypair, w2, g2, b2)


def _bias_kernel(tab_ref, sub_ref, idx_ref, o_ref):
    h = pl.program_id(0)
    idx = idx_ref[...]
    acc = jnp.full(idx.shape, NEG_INF, jnp.float32)
    for b in range(REL_BUCKETS):
        acc = jnp.where(idx == b, tab_ref[h, b] - sub_ref[h], acc)
    o_ref[0] = acc


def _bias_expand(table_hb, sub_h, idx):
    H = table_hb.shape[0]
    K, Q = idx.shape
    return pl.pallas_call(
        _bias_kernel,
        grid_spec=pltpu.PrefetchScalarGridSpec(
            num_scalar_prefetch=2,
            grid=(H,),
            in_specs=[pl.BlockSpec((K, Q), lambda h, t, s: (0, 0))],
            out_specs=pl.BlockSpec((1, K, Q), lambda h, t, s: (h, 0, 0)),
        ),
        out_shape=jax.ShapeDtypeStruct((H, K, Q), jnp.float32),
        compiler_params=pltpu.CompilerParams(dimension_semantics=("parallel",)),
        name="bias",
    )(table_hb, sub_h, jnp.asarray(idx))


def _attention_bias_tables(rel_bias_table):
    rel = rel_bias_table.astype(jnp.float32)
    tab_a = rel[:, :SWA_Q_HEADS].T
    tab_b = rel[:, SWA_Q_HEADS:].T
    kj = np.arange(2 * SWA_BLOCK)[:, None]
    qi = np.arange(SWA_BLOCK)[None, :]
    dist = SWA_BLOCK + qi - kj
    idx_a = np.where((dist >= 0) & (dist < SWA_WINDOW), _rel_bucket_np(dist), -1).astype(np.int32)
    bias_a = _bias_expand(tab_a, jnp.zeros((SWA_Q_HEADS,), jnp.float32), idx_a)
    bias_a = bias_a.reshape(SWA_Q_HEADS // 2, 2, 2 * SWA_BLOCK, SWA_BLOCK).transpose(0, 2, 1, 3)
    bias_a = bias_a.reshape(SWA_Q_HEADS // 2, 2 * SWA_BLOCK, 2 * SWA_BLOCK)
    j = np.arange(MOBA_BLOCK)[:, None]
    i = np.arange(MOBA_BLOCK)[None, :]
    d_own = i - j
    idx_own = np.where(d_own >= 0, _rel_bucket_np(d_own), -1)
    idx_prev = _rel_bucket_np(MOBA_BLOCK + i - j)
    idx_b = np.concatenate([idx_prev, idx_own], axis=0).astype(np.int32)
    bias_b = _bias_expand(tab_b, tab_b[:, REL_BUCKETS - 1], idx_b)
    return bias_a, bias_b


def _block_plan(sizes, n_tok):
    n_assign = n_tok * EXPERT_TOPK
    padded = ((sizes + MOE_TM - 1) // MOE_TM) * MOE_TM
    padded_end = jnp.cumsum(padded)
    padded_start = padded_end - padded
    cap = -(-n_assign // MOE_TM) * MOE_TM + N_EXPERTS * MOE_TM
    blk_start = jnp.arange(cap // MOE_TM, dtype=jnp.int32) * MOE_TM
    blk_expert = jnp.minimum(
        jnp.sum(padded_end[None, :] <= blk_start[:, None], axis=1), N_EXPERTS - 1).astype(jnp.int32)
    used = blk_start < padded_end[-1]
    blk_valid = jnp.clip(sizes[blk_expert] - (blk_start - padded_start[blk_expert]), 0, MOE_TM)
    blk_valid = jnp.where(used, blk_valid, 0).astype(jnp.int32)
    return padded_start, blk_expert, blk_valid, cap


def kernel(x, w_in, b_in, attn_sinks, rel_bias_table, w_branch_swa, w_branch_moba, w_out, ln1_gain, ln1_bias,
           w_group_router, b_group_router, w_expert_router, b_expert_router, w_expert_gate, w_expert_up,
           w_expert_down, ln2_gain, ln2_bias):
    assert w_in.shape[0] == DEPTH == 1
    B, S, D = x.shape
    T = B * S
    bf16 = jnp.bfloat16
    f32 = jnp.float32
    w = w_in[0]
    b = b_in[0]

    def cols(off, width):
        return w[:, off:off + width], b[off:off + width]

    wq_a, bq_a = cols(OFF_SWA_Q, SWA_Q_W)
    wk_a, bk_a = cols(OFF_SWA_K, SWA_KV_W)
    wv_a, bv_a = cols(OFF_SWA_V, SWA_KV_W)
    wq_b, bq_b = cols(OFF_MOBA_Q, MOBA_W)
    wk_b, bk_b = cols(OFF_MOBA_K, MOBA_W)
    wv_b, bv_b = cols(OFF_MOBA_V, MOBA_W)

    def dup_kv(t):
        parts = [t[..., i * HEAD_DIM:(i + 1) * HEAD_DIM] for i in range(SWA_KV_HEADS)]
        return jnp.concatenate([p for p in parts for _ in range(2)], axis=-1)

    wn = jnp.concatenate([wq_a * ATTN_SCALE, dup_kv(wk_a), wq_b * ATTN_SCALE, wk_b], axis=1).astype(bf16)
    bn = jnp.concatenate([bq_a * ATTN_SCALE, dup_kv(bk_a), bq_b * ATTN_SCALE, bk_b])[None, :].astype(f32)
    wt = jnp.concatenate([wv_a, wv_b], axis=1).T.astype(bf16)
    bt = jnp.concatenate([bv_a, bv_b])[:, None].astype(f32)

    qk, vt = _inproj(x, wn, bn, wt, bt)

    bias_a, bias_b = _attention_bias_tables(rel_bias_table)
    y_a = _swa(attn_sinks[0].astype(f32), qk, vt, bias_a)
    y_b = _moba(qk, vt, bias_b)

    wg, bg = cols(OFF_GATE, 2 * D_MODEL)
    wr = jnp.zeros((ROUTER_ROWS, D), f32)
    wr = wr.at[0:N_GROUPS].set(w_group_router[0].T).at[8:8 + N_EXPERTS].set(w_expert_router[0].T)
    wr_hi = wr.astype(bf16)
    wr_lo = (wr - wr_hi.astype(f32)).astype(bf16)
    br = jnp.zeros((ROUTER_ROWS,), f32)
    br = br.at[0:N_GROUPS].set(b_group_router[0]).at[8:8 + N_EXPERTS].set(b_expert_router[0])[:, None]
    x1, x1p, rinfo, counts = _merge(
        x.reshape(T, D), y_a.reshape(T, SWA_Q_W), y_b.reshape(T, MOBA_W),
        wg.astype(bf16), bg[None, :].astype(f32), w_branch_swa[0].astype(bf16), w_branch_moba[0].astype(bf16),
        w_out[0].astype(bf16), ln1_gain[0][None, :].astype(f32), ln1_bias[0][None, :].astype(f32),
        jnp.concatenate([wr_hi, wr_lo], axis=0), br)

    w2 = jnp.stack([rinfo[2], rinfo[3]], axis=1)
    sizes = counts[:, 0].astype(jnp.int32)
    padded_start, blk_expert, blk_valid, cap = _block_plan(sizes, T)
    dest = _dest(rinfo, padded_start)
    xs = _sc_scatter_rows(x1p, dest[0], dest[1], cap)
    wgu = jnp.concatenate([w_expert_gate[0], w_expert_up[0]], axis=-1).astype(bf16)
    y_buf = _experts(blk_expert, blk_valid, xs, wgu, w_expert_down[0].astype(bf16))
    ypair = _sc_gather_rows(y_buf, dest[0:EXPERT_TOPK].reshape(-1))
    out = _final(x1, ypair, w2, ln2_gain[0][None, :].astype(f32), ln2_bias[0][None, :].astype(f32))
    return out.reshape(B, S, D)
```

```python
import functools
import math

import numpy as np
import jax
import jax.numpy as jnp
from jax import lax
from jax.experimental import pallas as pl
from jax.experimental.pallas import tpu as pltpu
from jax.experimental.pallas import tpu_sc as plsc

D_MODEL = 1024
HEAD_DIM = 64
SWA_Q_HEADS = 8
SWA_KV_HEADS = 2
SWA_GROUP = SWA_Q_HEADS // SWA_KV_HEADS
SWA_WINDOW = 128
SWA_BLOCK = 128
MOBA_HEADS = 8
MOBA_BLOCK = 256
MOBA_TOPK = 3
REL_BUCKETS = 32
REL_MAX_DIST = 128
N_GROUPS = 4
EXPERTS_PER_GROUP = 8
N_EXPERTS = N_GROUPS * EXPERTS_PER_GROUP
EXPERT_TOPK = 2
D_EXPERT = 512
LN_EPS = 1e-5
DEPTH = 1
DEEPNORM_ALPHA = (2.0 * DEPTH) ** 0.25
NEG_INF = -1e30
ATTN_SCALE = HEAD_DIM ** -0.5

SWA_Q_W = SWA_Q_HEADS * HEAD_DIM
SWA_KV_W = SWA_KV_HEADS * HEAD_DIM
MOBA_W = MOBA_HEADS * HEAD_DIM
OFF_SWA_Q = 0
OFF_SWA_K = OFF_SWA_Q + SWA_Q_W
OFF_SWA_V = OFF_SWA_K + SWA_KV_W
OFF_MOBA_Q = OFF_SWA_V + SWA_KV_W
OFF_MOBA_K = OFF_MOBA_Q + MOBA_W
OFF_MOBA_V = OFF_MOBA_K + MOBA_W
OFF_GATE = OFF_MOBA_V + MOBA_W

LANES = 128
QK_SWA_Q = 0
QK_SWA_K = QK_SWA_Q + SWA_Q_W
QK_MOBA_Q = QK_SWA_K + SWA_KV_HEADS * LANES
QK_MOBA_K = QK_MOBA_Q + MOBA_W
QK_W = QK_MOBA_K + MOBA_W
VT_SWA = 0
VT_MOBA = VT_SWA + SWA_KV_W
VT_W = VT_MOBA + MOBA_W

MOE_TM = 512
MOE_PARTS = 2
SC_CHUNK = 128
ROUTER_ROWS = 8 + N_EXPERTS
VMEM_LIMIT = 56 * 1024 * 1024

_NT = (((1,), (1,)), ((), ()))


def _rel_bucket_np(dist):
    n = np.maximum(dist, 0)
    max_exact = REL_BUCKETS // 2
    nf = np.maximum(n, 1).astype(np.float32)
    large = max_exact + (np.log(nf / np.float32(max_exact)) / np.float32(math.log(REL_MAX_DIST / max_exact))
                         * np.float32(REL_BUCKETS - max_exact)).astype(np.int32)
    large = np.minimum(large, REL_BUCKETS - 1)
    return np.where(n < max_exact, n, large).astype(np.int32)


def _inproj_kernel(x_ref, wn_ref, bn_ref, wt_ref, bt_ref, qk_ref, vt_ref):
    xb = x_ref[0].astype(jnp.bfloat16)
    qk = jnp.dot(xb, wn_ref[...], preferred_element_type=jnp.float32) + bn_ref[...]
    qk_ref[0] = qk.astype(jnp.bfloat16)
    vt = lax.dot_general(wt_ref[...], xb, _NT, preferred_element_type=jnp.float32) + bt_ref[...]
    vt_ref[0] = vt.astype(jnp.bfloat16)


def _inproj(x, wn, bn, wt, bt, tm=512):
    B, S, D = x.shape
    return pl.pallas_call(
        _inproj_kernel,
        grid=(B, S // tm),
        in_specs=[
            pl.BlockSpec((1, tm, D), lambda b, i: (b, i, 0)),
            pl.BlockSpec((D, QK_W), lambda b, i: (0, 0)),
            pl.BlockSpec((1, QK_W), lambda b, i: (0, 0)),
            pl.BlockSpec((VT_W, D), lambda b, i: (0, 0)),
            pl.BlockSpec((VT_W, 1), lambda b, i: (0, 0)),
        ],
        out_specs=[
            pl.BlockSpec((1, tm, QK_W), lambda b, i: (b, i, 0)),
            pl.BlockSpec((1, VT_W, tm), lambda b, i: (b, 0, i)),
        ],
        out_shape=[
            jax.ShapeDtypeStruct((B, S, QK_W), jnp.bfloat16),
            jax.ShapeDtypeStruct((B, VT_W, S), jnp.bfloat16),
        ],
        compiler_params=pltpu.CompilerParams(
            dimension_semantics=("parallel", "parallel"), vmem_limit_bytes=VMEM_LIMIT),
        name="inproj",
    )(x, wn, bn, wt, bt)


def _swa_kernel(sink_ref, q_ref, k_ref, vt_ref, bias_ref, o_ref, vaug_ref):
    S = q_ref.shape[1]
    nblk = S // SWA_BLOCK
    half = HEAD_DIM
    vt = vt_ref[0]
    rows = lax.broadcasted_iota(jnp.int32, vt.shape, 0)
    one = jnp.ones_like(vt)
    vaug_ref[0] = jnp.where(rows < half, vt, one)
    vaug_ref[1] = jnp.where(rows < half, one, vt)
    lane = lax.broadcasted_iota(jnp.int32, (SWA_BLOCK, LANES), 1)
    col = lax.broadcasted_iota(jnp.int32, (1, 2 * SWA_BLOCK), 1)
    npair = SWA_Q_HEADS // 2

    def blocks(specs):
        items = [(qs, ks, nk, boff, pair, (2 * pair) // SWA_GROUP) for qs, ks, nk, boff in specs for pair in range(npair)]
        ss = []
        for qs, ks, nk, boff, pair, kv in items:
            qblk = q_ref[0, pl.ds(qs, SWA_BLOCK), pair * LANES:(pair + 1) * LANES]
            zero = jnp.zeros_like(qblk)
            q2 = jnp.concatenate([jnp.where(lane < half, qblk, zero), jnp.where(lane >= half, qblk, zero)], axis=0)
            kblk = k_ref[0, pl.ds(ks, nk), kv * LANES:(kv + 1) * LANES]
            s = lax.dot_general(kblk, q2, _NT, preferred_element_type=jnp.float32)
            ss.append(s + bias_ref[pair, boff:boff + nk, :])
        stats = []
        for (qs, ks, nk, boff, pair, kv), s in zip(items, ss):
            sink = jnp.where(col < SWA_BLOCK, sink_ref[2 * pair], sink_ref[2 * pair + 1])
            m = jnp.maximum(_colmax(s), sink)
            stats.append((jnp.exp(s - m).astype(jnp.bfloat16), jnp.exp(sink - m)))
        rs = [jnp.dot(vaug_ref[kv, :, pl.ds(ks, nk)], p, preferred_element_type=jnp.float32)
              for (qs, ks, nk, boff, pair, kv), (p, _) in zip(items, stats)]
        outs = []
        for (qs, ks, nk, boff, pair, kv), (_, esink), r in zip(items, stats, rs):
            l = (r[half:half + 1, :] if kv == 0 else r[0:1, :]) + esink
            o = (r[0:half, :] if kv == 0 else r[half:, :]) / l
            outs.append(jnp.concatenate([o[:, :SWA_BLOCK], o[:, SWA_BLOCK:]], axis=0))
        for i, (qs, _, _, _) in enumerate(specs):
            ot = jnp.concatenate(outs[i * npair:(i + 1) * npair], axis=0)
            o_ref[0, pl.ds(qs, SWA_BLOCK), :] = ot.T.astype(jnp.bfloat16)

    blocks([(0, 0, SWA_BLOCK, SWA_BLOCK), (SWA_BLOCK, 0, 2 * SWA_BLOCK, 0)])

    def body(t, carry):
        specs = []
        for n in (2 * t, 2 * t + 1):
            qs = pl.multiple_of(n * SWA_BLOCK, SWA_BLOCK)
            specs.append((qs, pl.multiple_of(qs - SWA_BLOCK, SWA_BLOCK), 2 * SWA_BLOCK, 0))
        blocks(specs)
        return carry

    assert nblk % 2 == 0
    lax.fori_loop(1, nblk // 2, body, 0)


def _swa(sinks, qk, vt, bias):
    B, S, _ = qk.shape
    nq = SWA_Q_W // LANES
    return pl.pallas_call(
        _swa_kernel,
        grid_spec=pltpu.PrefetchScalarGridSpec(
            num_scalar_prefetch=1,
            grid=(B,),
            in_specs=[
                pl.BlockSpec((1, S, SWA_Q_W), lambda b, s: (b, 0, QK_SWA_Q // SWA_Q_W)),
                pl.BlockSpec((1, S, 2 * LANES), lambda b, s: (b, 0, QK_SWA_K // (2 * LANES))),
                pl.BlockSpec((1, SWA_KV_W, S), lambda b, s: (b, VT_SWA // SWA_KV_W, 0)),
                pl.BlockSpec((SWA_Q_HEADS // 2, 2 * SWA_BLOCK, 2 * SWA_BLOCK), lambda b, s: (0, 0, 0)),
            ],
            out_specs=pl.BlockSpec((1, S, SWA_Q_W), lambda b, s: (b, 0, 0)),
            scratch_shapes=[pltpu.VMEM((SWA_KV_HEADS, SWA_KV_W, S), jnp.bfloat16)],
        ),
        out_shape=jax.ShapeDtypeStruct((B, S, SWA_Q_W), jnp.bfloat16),
        compiler_params=pltpu.CompilerParams(
            dimension_semantics=("parallel",), vmem_limit_bytes=VMEM_LIMIT),
        name="swa",
    )(sinks, qk, qk, vt, bias)


def _colmax(s):
    while s.shape[0] > 8:
        h = s.shape[0] // 2
        s = jnp.maximum(s[:h], s[h:])
    return jnp.max(s, axis=0, keepdims=True)


def _moba_kernel(fqb_ref, fc_ref, q_ref, k_ref, vt_ref, bias_ref, o_ref,
                 vaug_ref, gate_ref, rank_ref, sel_ref, far_ref, m_scr, acc_scr, sa_scr, sb_scr):
    S = q_ref.shape[1]
    nb = S // MOBA_BLOCK
    half = HEAD_DIM
    BLK = MOBA_BLOCK

    vt = vt_ref[0]
    rows = lax.broadcasted_iota(jnp.int32, vt.shape, 0)
    one = jnp.ones_like(vt)
    vaug_ref[0] = jnp.where(rows < half, vt, one)
    vaug_ref[1] = jnp.where(rows < half, one, vt)

    kf = k_ref[0].astype(jnp.float32).reshape(nb, BLK, LANES)
    kmean = jnp.sum(kf, axis=1) * (1.0 / BLK)
    k_hi = kmean.astype(jnp.bfloat16)
    k_lo = (kmean - k_hi.astype(jnp.float32)).astype(jnp.bfloat16)
    kcat = jnp.concatenate([k_hi, k_lo], axis=0)

    lane_q = lax.broadcasted_iota(jnp.int32, (S, LANES), 1)
    brow = lax.broadcasted_iota(jnp.int32, (nb, S), 0)
    qblk_of = lax.broadcasted_iota(jnp.int32, (nb, S), 1) // BLK
    qall = q_ref[0]
    for h in range(2):
        qh = jnp.where((lane_q < half) if h == 0 else (lane_q >= half), qall, jnp.zeros_like(qall))
        g2 = lax.dot_general(kcat, qh, _NT, preferred_element_type=jnp.float32)
        gate_ref[...] = g2[0:nb] + g2[nb:2 * nb]
        rank_ref[...] = jnp.zeros_like(rank_ref)
        for m in range(nb - 1):
            lo = (m + 1) * BLK
            G = gate_ref[:, lo:]
            gm = gate_ref[m:m + 1, lo:]
            ge = jnp.where(gm >= G, 1.0, 0.0)
            gt = jnp.where(gm > G, 1.0, 0.0)
            brow_m = lax.broadcasted_iota(jnp.int32, (nb, S - lo), 0)
            rank_ref[:, lo:] += jnp.where(brow_m > m, ge, gt)
        top = rank_ref[...] < float(MOBA_TOPK)
        sel_ref[h] = jnp.where((brow < qblk_of) & top, 0.0, NEG_INF)
        far_ref[h] = jnp.where((brow < qblk_of - 1) & top, 0.0, NEG_INF)

    lane = lax.broadcasted_iota(jnp.int32, (BLK, LANES), 1)
    orow = lax.broadcasted_iota(jnp.int32, (LANES, BLK), 0)

    def head_q(qs, h):
        qblk = q_ref[0, pl.ds(qs, BLK), :]
        return jnp.where((lane < half) if h == 0 else (lane >= half), qblk, jnp.zeros_like(qblk))

    m0 = jnp.full((1, BLK), NEG_INF, jnp.float32)
    acc0 = jnp.zeros((LANES, BLK), jnp.float32)

    def item(h, qb, qs, ks, nk, bias, sel_a, sel_b, m, acc):
        return dict(h=h, qb=qb, qs=qs, ks=ks, nk=nk, bias=bias, sel_a=sel_a, sel_b=sel_b, m=m, acc=acc)

    def near_item(qb, h):
        qs = pl.multiple_of(qb * BLK, BLK)
        ps = pl.multiple_of(qs - BLK, BLK)
        sel_prev = sel_ref[h, pl.ds(qb - 1, 1), pl.ds(qs, BLK)]
        return item(h, qb, qs, ps, 2 * BLK, bias_ref[h], sel_prev, None, m0, acc0)

    def far_item(i, h):
        qb = fqb_ref[i]
        c = fc_ref[i]
        qs = pl.multiple_of(qb * BLK, BLK)
        ks = pl.multiple_of(c * (2 * BLK), 2 * BLK)
        sel_a = far_ref[h, pl.ds(2 * c, 1), pl.ds(qs, BLK)]
        sel_b = far_ref[h, pl.ds(2 * c + 1, 1), pl.ds(qs, BLK)]
        return item(h, qb, qs, ks, 2 * BLK, None, sel_a, sel_b, m_scr[h, qb], acc_scr[h, qb])

    def scores(it):
        kslab = k_ref[0, pl.ds(it["ks"], it["nk"]), :]
        s = lax.dot_general(kslab, head_q(it["qs"], it["h"]), _NT,
                            preferred_element_type=jnp.float32)
        return s if it["bias"] is None else s + it["bias"]

    def probs(it, s):
        m, sel_a, sel_b = it["m"], it["sel_a"], it["sel_b"]
        if it["nk"] == BLK:
            m_new = jnp.maximum(m, _colmax(s))
            p = jnp.exp(s - m_new)
        else:
            cm_a = _colmax(s[:BLK]) + sel_a
            cm_b = _colmax(s[BLK:])
            if sel_b is not None:
                cm_b = cm_b + sel_b
            m_new = jnp.maximum(m, jnp.maximum(cm_a, cm_b))
            p_a = jnp.exp(s[:BLK] - (m_new - sel_a))
            p_b = jnp.exp(s[BLK:] - (m_new if sel_b is None else m_new - sel_b))
            p = jnp.concatenate([p_a, p_b], axis=0)
        return m_new, jnp.exp(m - m_new), p.astype(jnp.bfloat16)

    def run(items):
        ss = [scores(it) for it in items]
        ps = [probs(it, s) for it, s in zip(items, ss)]
        pvs = [jnp.dot(vaug_ref[it["h"], :, pl.ds(it["ks"], it["nk"])], p, preferred_element_type=jnp.float32)
               for it, (_, _, p) in zip(items, ps)]
        for it, (m_new, alpha, _), pv in zip(items, ps, pvs):
            m_scr[it["h"], it["qb"]] = m_new
            acc_scr[it["h"], it["qb"]] = it["acc"] * alpha + pv

    run([item(h, 0, 0, 0, BLK, bias_ref[h, BLK:, :], None, None, m0, acc0) for h in range(2)]
        + [near_item(nb - 1, h) for h in range(2)])

    near_pairs = nb // 2 - 1

    def near_body(i, carry):
        run([near_item(qb, h) for qb in (i, i + near_pairs) for h in range(2)])
        return carry

    lax.fori_loop(1, near_pairs + 1, near_body, 0)

    n_far = fqb_ref.shape[0]
    n_groups = n_far // 2

    def group(g):
        return [(j, h) for j in (g, g + n_groups) for h in range(2)]

    def score_stage(g, s_buf):
        for k, (j, h) in enumerate(group(g)):
            qs = pl.multiple_of(fqb_ref[j] * BLK, BLK)
            ks = pl.multiple_of(fc_ref[j] * (2 * BLK), 2 * BLK)
            s_buf[k] = lax.dot_general(k_ref[0, pl.ds(ks, 2 * BLK), :], head_q(qs, h), _NT,
                                       preferred_element_type=jnp.float32)

    def finish_stage(g, s_buf):
        items = [far_item(j, h) for j, h in group(g)]
        ps = [probs(it, s_buf[k]) for k, it in enumerate(items)]
        pvs = [jnp.dot(vaug_ref[it["h"], :, pl.ds(it["ks"], it["nk"])], p, preferred_element_type=jnp.float32)
               for it, (_, _, p) in zip(items, ps)]
        for it, (m_new, alpha, _), pv in zip(items, ps, pvs):
            m_scr[it["h"], it["qb"]] = m_new
            acc_scr[it["h"], it["qb"]] = it["acc"] * alpha + pv

    score_stage(0, sa_scr)

    def far_body(t, carry):
        g = 2 * t
        score_stage(g + 1, sb_scr)
        finish_stage(g, sa_scr)
        score_stage(jnp.minimum(g + 2, n_groups - 1), sa_scr)
        finish_stage(g + 1, sb_scr)
        return carry

    assert n_groups % 2 == 0
    lax.fori_loop(0, n_groups // 2, far_body, 0)

    def out_body(qb, carry):
        qs = pl.multiple_of(qb * BLK, BLK)
        a0 = acc_scr[0, qb]
        a1 = acc_scr[1, qb]
        ot = jnp.where(orow < half, a0 / a0[half:half + 1, :], a1 / a1[0:1, :])
        o_ref[0, pl.ds(qs, BLK), :] = ot.T.astype(jnp.bfloat16)
        return carry

    lax.fori_loop(0, nb, out_body, 0)


def _moba_far_items(nb):
    items = [(qb, c) for qb in range(2, nb) for c in range(qb // 2)]
    n = len(items)
    assert n % 2 == 0 and all(items[i][0] != items[i + n // 2][0] for i in range(n // 2))
    return np.array([it[0] for it in items], np.int32), np.array([it[1] for it in items], np.int32)


def _moba(qk, vt, bias):
    B, S, _ = qk.shape
    nb = S // MOBA_BLOCK
    npair = MOBA_W // LANES
    far_qb, far_c = _moba_far_items(nb)
    return pl.pallas_call(
        _moba_kernel,
        grid_spec=pltpu.PrefetchScalarGridSpec(
            num_scalar_prefetch=2,
            grid=(B, npair),
            in_specs=[
                pl.BlockSpec((1, S, LANES), lambda b, p, fq, fc: (b, 0, QK_MOBA_Q // LANES + p)),
                pl.BlockSpec((1, S, LANES), lambda b, p, fq, fc: (b, 0, QK_MOBA_K // LANES + p)),
                pl.BlockSpec((1, LANES, S), lambda b, p, fq, fc: (b, VT_MOBA // LANES + p, 0)),
                pl.BlockSpec((2, 2 * MOBA_BLOCK, MOBA_BLOCK), lambda b, p, fq, fc: (p, 0, 0)),
            ],
            out_specs=pl.BlockSpec((1, S, LANES), lambda b, p, fq, fc: (b, 0, p)),
            scratch_shapes=[
                pltpu.VMEM((2, LANES, S), jnp.bfloat16),
                pltpu.VMEM((nb, S), jnp.float32),
                pltpu.VMEM((nb, S), jnp.float32),
                pltpu.VMEM((2, nb, S), jnp.float32),
                pltpu.VMEM((2, nb, S), jnp.float32),
                pltpu.VMEM((2, nb, 1, MOBA_BLOCK), jnp.float32),
                pltpu.VMEM((2, nb, LANES, MOBA_BLOCK), jnp.float32),
                pltpu.VMEM((4, 2 * MOBA_BLOCK, MOBA_BLOCK), jnp.float32),
                pltpu.VMEM((4, 2 * MOBA_BLOCK, MOBA_BLOCK), jnp.float32),
            ],
        ),
        out_shape=jax.ShapeDtypeStruct((B, S, MOBA_W), jnp.bfloat16),
        compiler_params=pltpu.CompilerParams(
            dimension_semantics=("parallel", "parallel"), vmem_limit_bytes=VMEM_LIMIT),
        name="moba",
    )(jnp.asarray(far_qb), jnp.asarray(far_c), qk, qk, vt, bias)


def _layer_norm(h, gain, bias):
    mu = jnp.mean(h, axis=-1, keepdims=True)
    c = h - mu
    var = jnp.mean(c * c, axis=-1, keepdims=True)
    return c * lax.rsqrt(var + LN_EPS) * gain + bias


def _pack_bf16_pair(a, b):
    ia = lax.bitcast_convert_type(a.astype(jnp.bfloat16).astype(jnp.float32), jnp.int32)
    ib = lax.bitcast_convert_type(b.astype(jnp.bfloat16).astype(jnp.float32), jnp.int32)
    return lax.shift_right_logical(ia, 16) | ib


def _unpack_bf16_pair(w):
    lo = lax.bitcast_convert_type(lax.shift_left(w, 16), jnp.float32)
    hi = lax.bitcast_convert_type(w & jnp.int32(-65536), jnp.float32)
    return lo.astype(jnp.bfloat16), hi.astype(jnp.bfloat16)


def _merge_kernel(x_ref, ya_ref, yb_ref, wg_ref, bg_ref, wa_ref, wb_ref, wo_ref, g1_ref, b1_ref,
                  wr_ref, br_ref, tri_ref, x1_ref, x1p_ref, r_ref, cnt_ref):
    x = x_ref[...]
    xb = x.astype(jnp.bfloat16)
    z = jnp.dot(xb, wg_ref[...], preferred_element_type=jnp.float32) + bg_ref[...]
    gates = 1.0 / (1.0 + jnp.exp(-z))
    pa = jnp.dot(ya_ref[...], wa_ref[...], preferred_element_type=jnp.float32)
    pb = jnp.dot(yb_ref[...], wb_ref[...], preferred_element_type=jnp.float32)
    merged = gates[:, :D_MODEL] * pa + gates[:, D_MODEL:] * pb
    mixed = jnp.dot(merged.astype(jnp.bfloat16), wo_ref[...], preferred_element_type=jnp.float32)
    x1 = _layer_norm(DEEPNORM_ALPHA * x + mixed, g1_ref[...], b1_ref[...])
    x1_ref[...] = x1
    x1_hi = x1.astype(jnp.bfloat16)
    x1p_ref[...] = _pack_bf16_pair(x1[:, :D_MODEL // 2], x1[:, D_MODEL // 2:])
    x1_lo = (x1 - x1_hi.astype(jnp.float32)).astype(jnp.bfloat16)

    R = ROUTER_ROWS
    l1 = lax.dot_general(wr_ref[...], x1_hi, _NT, preferred_element_type=jnp.float32)
    l2 = lax.dot_general(wr_ref[0:R, :], x1_lo, _NT, preferred_element_type=jnp.float32)
    L = l1[0:R] + l1[R:2 * R] + l2 + br_ref[...]
    tm = x.shape[0]
    row = lax.broadcasted_iota(jnp.int32, (8, tm), 0)
    big = jnp.float32(-3e38)
    gl = jnp.where(row < N_GROUPS, L[0:8], big)
    gmax = jnp.max(gl, axis=0, keepdims=True)
    g_idx = jnp.min(jnp.where(gl == gmax, row, 8), axis=0, keepdims=True)
    gsum = jnp.sum(jnp.where(row < N_GROUPS, jnp.exp(gl - gmax), 0.0), axis=0, keepdims=True)
    g_prob = 1.0 / gsum
    E = L[8 + 8 * (N_GROUPS - 1):8 + 8 * N_GROUPS]
    for g in range(N_GROUPS - 2, -1, -1):
        E = jnp.where(g_idx == g, L[8 + 8 * g:16 + 8 * g], E)
    t0 = jnp.max(E, axis=0, keepdims=True)
    loc0 = jnp.min(jnp.where(E == t0, row, 8), axis=0, keepdims=True)
    E2 = jnp.where(row == loc0, big, E)
    t1 = jnp.max(E2, axis=0, keepdims=True)
    loc1 = jnp.min(jnp.where(E2 == t1, row, 8), axis=0, keepdims=True)
    ex = jnp.exp(t1 - t0)
    w0 = g_prob / (1.0 + ex)
    w1 = g_prob * ex / (1.0 + ex)
    e0i = g_idx * EXPERTS_PER_GROUP + loc0
    e1i = g_idx * EXPERTS_PER_GROUP + loc1

    erow = lax.broadcasted_iota(jnp.int32, (N_EXPERTS, tm), 0)
    oh0 = jnp.where(erow == e0i, 1.0, 0.0)
    oh1 = jnp.where(erow == e1i, 1.0, 0.0)
    both = oh0 + oh1

    @pl.when(pl.program_id(0) == 0)
    def _():
        cnt_ref[...] = jnp.zeros_like(cnt_ref)

    prefix = jnp.dot(both.astype(jnp.bfloat16), tri_ref[...], preferred_element_type=jnp.float32)
    prefix = prefix + cnt_ref[:, 0:1]
    rank0 = jnp.sum(oh0 * prefix, axis=0, keepdims=True)
    rank1 = jnp.sum(oh1 * prefix, axis=0, keepdims=True)
    cnt_ref[...] = cnt_ref[...] + jnp.sum(both, axis=1, keepdims=True)

    vals = (e0i.astype(jnp.float32), e1i.astype(jnp.float32), w0, w1, rank0, rank1)
    out = jnp.zeros((8, tm), jnp.float32)
    for k, v in enumerate(vals):
        out = jnp.where(row == k, v, out)
    r_ref[...] = out


def _merge(x2, ya, yb, wg, bg, wa, wb, wo, g1, b1, wr, br, part, n_parts, tm=512):
    D = x2.shape[1]
    T = x2.shape[0] // n_parts
    off = part * (T // tm)
    const = lambda i: (0, 0)
    tri = jnp.triu(jnp.ones((tm, tm), jnp.bfloat16), k=1)
    return pl.pallas_call(
        _merge_kernel,
        grid=(T // tm,),
        in_specs=[
            pl.BlockSpec((tm, D), lambda i: (i + off, 0)),
            pl.BlockSpec((tm, SWA_Q_W), lambda i: (i + off, 0)),
            pl.BlockSpec((tm, MOBA_W), lambda i: (i + off, 0)),
            pl.BlockSpec((D, 2 * D), const),
            pl.BlockSpec((1, 2 * D), const),
            pl.BlockSpec((SWA_Q_W, D), const),
            pl.BlockSpec((MOBA_W, D), const),
            pl.BlockSpec((D, D), const),
            pl.BlockSpec((1, D), const),
            pl.BlockSpec((1, D), const),
            pl.BlockSpec((2 * ROUTER_ROWS, D), const),
            pl.BlockSpec((ROUTER_ROWS, 1), const),
            pl.BlockSpec((tm, tm), const),
        ],
        out_specs=[
            pl.BlockSpec((tm, D), lambda i: (i, 0)),
            pl.BlockSpec((tm, D // 2), lambda i: (i, 0)),
            pl.BlockSpec((8, tm), lambda i: (0, i)),
            pl.BlockSpec((N_EXPERTS, LANES), const),
        ],
        out_shape=[
            jax.ShapeDtypeStruct((T, D), jnp.float32),
            jax.ShapeDtypeStruct((T, D // 2), jnp.int32),
            jax.ShapeDtypeStruct((8, T), jnp.float32),
            jax.ShapeDtypeStruct((N_EXPERTS, LANES), jnp.float32),
        ],
        compiler_params=pltpu.CompilerParams(
            dimension_semantics=("arbitrary",), vmem_limit_bytes=VMEM_LIMIT),
        name="merge",
    )(x2, ya, yb, wg, bg, wa, wb, wo, g1, b1, wr, br, tri)


def _dest_kernel(r_ref, ps_ref, d_ref):
    tt = r_ref.shape[1]
    erow = lax.broadcasted_iota(jnp.int32, (N_EXPERTS, tt), 0)
    row = lax.broadcasted_iota(jnp.int32, (8, tt), 0)
    ps = ps_ref[...]
    out = jnp.zeros((8, tt), jnp.float32)
    for k in range(EXPERT_TOPK):
        e = r_ref[k:k + 1, :].astype(jnp.int32)
        start = jnp.sum(jnp.where(erow == e, ps, 0.0), axis=0, keepdims=True)
        out = jnp.where(row == k, start + r_ref[4 + k:5 + k, :], out)
    d_ref[...] = out.astype(jnp.int32)


def _dest(rinfo, padded_start, tt=8192):
    T = rinfo.shape[1]
    tt = min(tt, T)
    return pl.pallas_call(
        _dest_kernel,
        grid=(T // tt,),
        in_specs=[pl.BlockSpec((8, tt), lambda i: (0, i)), pl.BlockSpec((N_EXPERTS, 1), lambda i: (0, 0))],
        out_specs=pl.BlockSpec((8, tt), lambda i: (0, i)),
        out_shape=jax.ShapeDtypeStruct((8, T), jnp.int32),
        compiler_params=pltpu.CompilerParams(dimension_semantics=("parallel",)),
        name="dest",
    )(rinfo, padded_start.astype(jnp.float32)[:, None])


def _sc_workers():
    info = plsc.get_sparse_core_info()
    return info.num_cores, info.num_subcores


def _sc_scatter_rows(src, dest0, dest1, cap):
    T, W = src.shape
    nc, ns = _sc_workers()
    per_w = T // (nc * ns)
    assert per_w * nc * ns == T and per_w % SC_CHUNK == 0
    mesh = plsc.VectorSubcoreMesh(core_axis_name="c", subcore_axis_name="s")

    @functools.partial(
        pl.kernel, mesh=mesh,
        out_type=jax.ShapeDtypeStruct((cap, W), src.dtype),
        scratch_types=[pltpu.VMEM((SC_CHUNK,), jnp.int32), pltpu.VMEM((SC_CHUNK, W), src.dtype)],
    )
    def scatter(src_hbm, d0_hbm, d1_hbm, out_hbm, idx_v, rows_v):
        wid = lax.axis_index("s") * nc + lax.axis_index("c")
        base = wid * per_w

        @pl.loop(0, per_w // SC_CHUNK)
        def _(c):
            off = pl.multiple_of(base + c * SC_CHUNK, SC_CHUNK)
            pltpu.sync_copy(src_hbm.at[pl.ds(off, SC_CHUNK)], rows_v)
            for d_hbm in (d0_hbm, d1_hbm):
                pltpu.sync_copy(d_hbm.at[pl.ds(off, SC_CHUNK)], idx_v)
                pltpu.sync_copy(rows_v, out_hbm.at[idx_v])

    return scatter(src, dest0, dest1)


def _sc_gather_rows(table, idx):
    N = idx.shape[0]
    W = table.shape[1]
    nc, ns = _sc_workers()
    per_w = N // (nc * ns)
    assert per_w * nc * ns == N and per_w % SC_CHUNK == 0
    mesh = plsc.VectorSubcoreMesh(core_axis_name="c", subcore_axis_name="s")

    @functools.partial(
        pl.kernel, mesh=mesh,
        out_type=jax.ShapeDtypeStruct((N, W), table.dtype),
        scratch_types=[pltpu.VMEM((SC_CHUNK,), jnp.int32), pltpu.VMEM((SC_CHUNK, W), table.dtype)],
    )
    def gather(table_hbm, idx_hbm, out_hbm, idx_v, rows_v):
        wid = lax.axis_index("s") * nc + lax.axis_index("c")
        base = wid * per_w

        @pl.loop(0, per_w // SC_CHUNK)
        def _(c):
            off = pl.multiple_of(base + c * SC_CHUNK, SC_CHUNK)
            pltpu.sync_copy(idx_hbm.at[pl.ds(off, SC_CHUNK)], idx_v)
            pltpu.sync_copy(table_hbm.at[idx_v], rows_v)
            pltpu.sync_copy(rows_v, out_hbm.at[pl.ds(off, SC_CHUNK)])

    return gather(table, idx)


def _expert_kernel(be_ref, nv_ref, x_ref, wgu_ref, wd_ref, y_ref):
    i = pl.program_id(0)
    nv = nv_ref[i]

    @pl.when(nv > 0)
    def _():
        lo, hi = _unpack_bf16_pair(x_ref[...])
        xb = jnp.concatenate([lo, hi], axis=1)
        rows = lax.broadcasted_iota(jnp.int32, xb.shape, 0)
        xb = jnp.where(rows < nv, xb, jnp.zeros_like(xb))
        gu = jnp.dot(xb, wgu_ref[0], preferred_element_type=jnp.float32)
        g = gu[:, :D_EXPERT]
        u = gu[:, D_EXPERT:]
        act = (g / (1.0 + jnp.exp(-g))) * u
        y = jnp.dot(act.astype(jnp.bfloat16), wd_ref[0], preferred_element_type=jnp.float32)
        y_ref[...] = _pack_bf16_pair(y[:, :D_MODEL // 2], y[:, D_MODEL // 2:])

    @pl.when(nv <= 0)
    def _():
        y_ref[...] = jnp.zeros_like(y_ref)


def _experts(blk_expert, blk_valid, xs, wgu, wd):
    cap, DW = xs.shape
    D = 2 * DW
    n_blocks = cap // MOE_TM
    return pl.pallas_call(
        _expert_kernel,
        grid_spec=pltpu.PrefetchScalarGridSpec(
            num_scalar_prefetch=2,
            grid=(n_blocks,),
            in_specs=[
                pl.BlockSpec((MOE_TM, DW), lambda i, be, nv: (i, 0)),
                pl.BlockSpec((1, D, 2 * D_EXPERT), lambda i, be, nv: (be[i], 0, 0)),
                pl.BlockSpec((1, D_EXPERT, D), lambda i, be, nv: (be[i], 0, 0)),
            ],
            out_specs=pl.BlockSpec((MOE_TM, DW), lambda i, be, nv: (i, 0)),
        ),
        out_shape=jax.ShapeDtypeStruct((cap, DW), jnp.int32),
        compiler_params=pltpu.CompilerParams(
            dimension_semantics=("arbitrary",), vmem_limit_bytes=VMEM_LIMIT),
        name="experts",
    )(blk_expert, blk_valid, xs, wgu, wd)


def _final_kernel(x1_ref, y0_ref, y1_ref, w_ref, g2_ref, b2_ref, *rest):
    o_ref = rest[-1]
    w = w_ref[...]
    halves = []
    for part in range(2):
        y0 = _unpack_bf16_pair(y0_ref[...])[part].astype(jnp.float32)
        y1 = _unpack_bf16_pair(y1_ref[...])[part].astype(jnp.float32)
        halves.append(y0 * w[:, 0:1] + y1 * w[:, 1:2])
    moe = jnp.concatenate(halves, axis=1)
    o_ref[...] = _layer_norm(DEEPNORM_ALPHA * x1_ref[...] + moe, g2_ref[...], b2_ref[...])


def _final(x1, ypair, w2, g2, b2, out_prev, part, n_parts, tm=1024):
    T, D = x1.shape
    const = lambda i: (0, 0)
    nt = T // tm
    off = part * nt
    in_specs = [
        pl.BlockSpec((tm, D), lambda i: (i, 0)),
        pl.BlockSpec((tm, D // 2), lambda i: (i, 0)),
        pl.BlockSpec((tm, D // 2), lambda i: (i + nt, 0)),
        pl.BlockSpec((tm, EXPERT_TOPK), lambda i: (i, 0)),
        pl.BlockSpec((1, D), const),
        pl.BlockSpec((1, D), const),
    ]
    args = [x1, ypair, ypair, w2, g2, b2]
    aliases = {}
    if out_prev is not None:
        in_specs.append(pl.BlockSpec(memory_space=pl.ANY))
        args.append(out_prev)
        aliases = {len(args) - 1: 0}
    return pl.pallas_call(
        _final_kernel,
        grid=(nt,),
        in_specs=in_specs,
        out_specs=pl.BlockSpec((tm, D), lambda i: (i + off, 0)),
        out_shape=jax.ShapeDtypeStruct((n_parts * T, D), jnp.float32),
        input_output_aliases=aliases,
        compiler_params=pltpu.CompilerParams(
            dimension_semantics=("parallel",), vmem_limit_bytes=VMEM_LIMIT),
        name="final",
    )(*args)


def _bias_kernel(tab_ref, sub_ref, idx_ref, o_ref):
    h = pl.program_id(0)
    idx = idx_ref[...]
    acc = jnp.full(idx.shape, NEG_INF, jnp.float32)
    for b in range(REL_BUCKETS):
        acc = jnp.where(idx == b, tab_ref[h, b] - sub_ref[h], acc)
    o_ref[0] = acc


def _bias_expand(table_hb, sub_h, idx):
    H = table_hb.shape[0]
    K, Q = idx.shape
    return pl.pallas_call(
        _bias_kernel,
        grid_spec=pltpu.PrefetchScalarGridSpec(
            num_scalar_prefetch=2,
            grid=(H,),
            in_specs=[pl.BlockSpec((K, Q), lambda h, t, s: (0, 0))],
            out_specs=pl.BlockSpec((1, K, Q), lambda h, t, s: (h, 0, 0)),
        ),
        out_shape=jax.ShapeDtypeStruct((H, K, Q), jnp.float32),
        compiler_params=pltpu.CompilerParams(dimension_semantics=("parallel",)),
        name="bias",
    )(table_hb, sub_h, jnp.asarray(idx))


def _attention_bias_tables(rel_bias_table):
    rel = rel_bias_table.astype(jnp.float32)
    tab_a = rel[:, :SWA_Q_HEADS].T
    tab_b = rel[:, SWA_Q_HEADS:].T
    kj = np.arange(2 * SWA_BLOCK)[:, None]
    qi = np.arange(SWA_BLOCK)[None, :]
    dist = SWA_BLOCK + qi - kj
    idx_a = np.where((dist >= 0) & (dist < SWA_WINDOW), _rel_bucket_np(dist), -1).astype(np.int32)
    bias_a = _bias_expand(tab_a, jnp.zeros((SWA_Q_HEADS,), jnp.float32), idx_a)
    bias_a = bias_a.reshape(SWA_Q_HEADS // 2, 2, 2 * SWA_BLOCK, SWA_BLOCK).transpose(0, 2, 1, 3)
    bias_a = bias_a.reshape(SWA_Q_HEADS // 2, 2 * SWA_BLOCK, 2 * SWA_BLOCK)
    j = np.arange(MOBA_BLOCK)[:, None]
    i = np.arange(MOBA_BLOCK)[None, :]
    d_own = i - j
    idx_own = np.where(d_own >= 0, _rel_bucket_np(d_own), -1)
    idx_prev = _rel_bucket_np(MOBA_BLOCK + i - j)
    idx_b = np.concatenate([idx_prev, idx_own], axis=0).astype(np.int32)
    bias_b = _bias_expand(tab_b, tab_b[:, REL_BUCKETS - 1], idx_b)
    return bias_a, bias_b


def _block_plan(sizes, n_tok):
    n_assign = n_tok * EXPERT_TOPK
    padded = ((sizes + MOE_TM - 1) // MOE_TM) * MOE_TM
    padded_end = jnp.cumsum(padded)
    padded_start = padded_end - padded
    cap = -(-n_assign // MOE_TM) * MOE_TM + N_EXPERTS * MOE_TM
    blk_start = jnp.arange(cap // MOE_TM, dtype=jnp.int32) * MOE_TM
    blk_expert = jnp.minimum(
        jnp.sum(padded_end[None, :] <= blk_start[:, None], axis=1), N_EXPERTS - 1).astype(jnp.int32)
    used = blk_start < padded_end[-1]
    blk_valid = jnp.clip(sizes[blk_expert] - (blk_start - padded_start[blk_expert]), 0, MOE_TM)
    blk_valid = jnp.where(used, blk_valid, 0).astype(jnp.int32)
    return padded_start, blk_expert, blk_valid, cap


def kernel(x, w_in, b_in, attn_sinks, rel_bias_table, w_branch_swa, w_branch_moba, w_out, ln1_gain, ln1_bias,
           w_group_router, b_group_router, w_expert_router, b_expert_router, w_expert_gate, w_expert_up,
           w_expert_down, ln2_gain, ln2_bias):
    assert w_in.shape[0] == DEPTH == 1
    B, S, D = x.shape
    T = B * S
    bf16 = jnp.bfloat16
    f32 = jnp.float32
    w = w_in[0]
    b = b_in[0]

    def cols(off, width):
        return w[:, off:off + width], b[off:off + width]

    wq_a, bq_a = cols(OFF_SWA_Q, SWA_Q_W)
    wk_a, bk_a = cols(OFF_SWA_K, SWA_KV_W)
    wv_a, bv_a = cols(OFF_SWA_V, SWA_KV_W)
    wq_b, bq_b = cols(OFF_MOBA_Q, MOBA_W)
    wk_b, bk_b = cols(OFF_MOBA_K, MOBA_W)
    wv_b, bv_b = cols(OFF_MOBA_V, MOBA_W)

    def dup_kv(t):
        parts = [t[..., i * HEAD_DIM:(i + 1) * HEAD_DIM] for i in range(SWA_KV_HEADS)]
        return jnp.concatenate([p for p in parts for _ in range(2)], axis=-1)

    wn = jnp.concatenate([wq_a * ATTN_SCALE, dup_kv(wk_a), wq_b * ATTN_SCALE, wk_b], axis=1).astype(bf16)
    bn = jnp.concatenate([bq_a * ATTN_SCALE, dup_kv(bk_a), bq_b * ATTN_SCALE, bk_b])[None, :].astype(f32)
    wt = jnp.concatenate([wv_a, wv_b], axis=1).T.astype(bf16)
    bt = jnp.concatenate([bv_a, bv_b])[:, None].astype(f32)

    qk, vt = _inproj(x, wn, bn, wt, bt)

    bias_a, bias_b = _attention_bias_tables(rel_bias_table)
    y_a = _swa(attn_sinks[0].astype(f32), qk, vt, bias_a)
    y_b = _moba(qk, vt, bias_b)

    wg, bg = cols(OFF_GATE, 2 * D_MODEL)
    wr = jnp.zeros((ROUTER_ROWS, D), f32)
    wr = wr.at[0:N_GROUPS].set(w_group_router[0].T).at[8:8 + N_EXPERTS].set(w_expert_router[0].T)
    wr_hi = wr.astype(bf16)
    wr_lo = (wr - wr_hi.astype(f32)).astype(bf16)
    br = jnp.zeros((ROUTER_ROWS,), f32)
    br = br.at[0:N_GROUPS].set(b_group_router[0]).at[8:8 + N_EXPERTS].set(b_expert_router[0])[:, None]
    merge_args = (
        x.reshape(T, D), y_a.reshape(T, SWA_Q_W), y_b.reshape(T, MOBA_W),
        wg.astype(bf16), bg[None, :].astype(f32), w_branch_swa[0].astype(bf16), w_branch_moba[0].astype(bf16),
        w_out[0].astype(bf16), ln1_gain[0][None, :].astype(f32), ln1_bias[0][None, :].astype(f32),
        jnp.concatenate([wr_hi, wr_lo], axis=0), br)
    wgu = jnp.concatenate([w_expert_gate[0], w_expert_up[0]], axis=-1).astype(bf16)
    wd = w_expert_down[0].astype(bf16)
    g2 = ln2_gain[0][None, :].astype(f32)
    b2 = ln2_bias[0][None, :].astype(f32)

    Tp = T // MOE_PARTS
    out = None
    for part in range(MOE_PARTS):
        x1, x1p, rinfo, counts = _merge(*merge_args, part, MOE_PARTS)
        w2 = jnp.stack([rinfo[2], rinfo[3]], axis=1)
        sizes = counts[:, 0].astype(jnp.int32)
        padded_start, blk_expert, blk_valid, cap = _block_plan(sizes, Tp)
        dest = _dest(rinfo, padded_start)
        xs = _sc_scatter_rows(x1p, dest[0], dest[1], cap)
        y_buf = _experts(blk_expert, blk_valid, xs, wgu, wd)
        ypair = _sc_gather_rows(y_buf, dest[0:EXPERT_TOPK].reshape(-1))
        out = _final(x1, ypair, w2, g2, b2, out, part, MOE_PARTS)
    return out.reshape(B, S, D)
```

```python
import functools
import math

import numpy as np
import jax
import jax.numpy as jnp
from jax import lax
from jax.experimental import pallas as pl
from jax.experimental.pallas import tpu as pltpu
from jax.experimental.pallas import tpu_sc as plsc

D_MODEL = 1024
HEAD_DIM = 64
SWA_Q_HEADS = 8
SWA_KV_HEADS = 2
SWA_GROUP = SWA_Q_HEADS // SWA_KV_HEADS
SWA_WINDOW = 128
SWA_BLOCK = 128
MOBA_HEADS = 8
MOBA_BLOCK = 256
MOBA_TOPK = 3
REL_BUCKETS = 32
REL_MAX_DIST = 128
N_GROUPS = 4
EXPERTS_PER_GROUP = 8
N_EXPERTS = N_GROUPS * EXPERTS_PER_GROUP
EXPERT_TOPK = 2
D_EXPERT = 512
LN_EPS = 1e-5
DEPTH = 1
DEEPNORM_ALPHA = (2.0 * DEPTH) ** 0.25
NEG_INF = -1e30
ATTN_SCALE = HEAD_DIM ** -0.5

SWA_Q_W = SWA_Q_HEADS * HEAD_DIM
SWA_KV_W = SWA_KV_HEADS * HEAD_DIM
MOBA_W = MOBA_HEADS * HEAD_DIM
OFF_SWA_Q = 0
OFF_SWA_K = OFF_SWA_Q + SWA_Q_W
OFF_SWA_V = OFF_SWA_K + SWA_KV_W
OFF_MOBA_Q = OFF_SWA_V + SWA_KV_W
OFF_MOBA_K = OFF_MOBA_Q + MOBA_W
OFF_MOBA_V = OFF_MOBA_K + MOBA_W
OFF_GATE = OFF_MOBA_V + MOBA_W

LANES = 128
QK_SWA_Q = 0
QK_SWA_K = QK_SWA_Q + SWA_Q_W
QK_MOBA_Q = QK_SWA_K + SWA_KV_HEADS * LANES
QK_MOBA_K = QK_MOBA_Q + MOBA_W
QK_W = QK_MOBA_K + MOBA_W
VT_SWA = 0
VT_MOBA = VT_SWA + SWA_KV_W
VT_W = VT_MOBA + MOBA_W

MOE_TM = 512
MOE_PARTS = 2
SC_CHUNK = 128
ROUTER_ROWS = 8 + N_EXPERTS
VMEM_LIMIT = 56 * 1024 * 1024

_NT = (((1,), (1,)), ((), ()))


def _rel_bucket_np(dist):
    n = np.maximum(dist, 0)
    max_exact = REL_BUCKETS // 2
    nf = np.maximum(n, 1).astype(np.float32)
    large = max_exact + (np.log(nf / np.float32(max_exact)) / np.float32(math.log(REL_MAX_DIST / max_exact))
                         * np.float32(REL_BUCKETS - max_exact)).astype(np.int32)
    large = np.minimum(large, REL_BUCKETS - 1)
    return np.where(n < max_exact, n, large).astype(np.int32)


def _inproj_kernel(x_ref, wn_ref, bn_ref, wt_ref, bt_ref, qk_ref, vt_ref):
    xb = x_ref[0].astype(jnp.bfloat16)
    qk = jnp.dot(xb, wn_ref[...], preferred_element_type=jnp.float32) + bn_ref[...]
    qk_ref[0] = qk.astype(jnp.bfloat16)
    vt = lax.dot_general(wt_ref[...], xb, _NT, preferred_element_type=jnp.float32) + bt_ref[...]
    vt_ref[0] = vt.astype(jnp.bfloat16)


def _inproj(x, wn, bn, wt, bt, tm=512):
    B, S, D = x.shape
    return pl.pallas_call(
        _inproj_kernel,
        grid=(B, S // tm),
        in_specs=[
            pl.BlockSpec((1, tm, D), lambda b, i: (b, i, 0)),
            pl.BlockSpec((D, QK_W), lambda b, i: (0, 0)),
            pl.BlockSpec((1, QK_W), lambda b, i: (0, 0)),
            pl.BlockSpec((VT_W, D), lambda b, i: (0, 0)),
            pl.BlockSpec((VT_W, 1), lambda b, i: (0, 0)),
        ],
        out_specs=[
            pl.BlockSpec((1, tm, QK_W), lambda b, i: (b, i, 0)),
            pl.BlockSpec((1, VT_W, tm), lambda b, i: (b, 0, i)),
        ],
        out_shape=[
            jax.ShapeDtypeStruct((B, S, QK_W), jnp.bfloat16),
            jax.ShapeDtypeStruct((B, VT_W, S), jnp.bfloat16),
        ],
        compiler_params=pltpu.CompilerParams(
            dimension_semantics=("parallel", "parallel"), vmem_limit_bytes=VMEM_LIMIT),
        name="inproj",
    )(x, wn, bn, wt, bt)


def _swa_kernel(sink_ref, q_ref, k_ref, vt_ref, bias_ref, o_ref, vaug_ref):
    S = q_ref.shape[1]
    nblk = S // SWA_BLOCK
    half = HEAD_DIM
    vt = vt_ref[0]
    rows = lax.broadcasted_iota(jnp.int32, vt.shape, 0)
    one = jnp.ones_like(vt)
    vaug_ref[0] = jnp.where(rows < half, vt, one)
    vaug_ref[1] = jnp.where(rows < half, one, vt)
    lane = lax.broadcasted_iota(jnp.int32, (SWA_BLOCK, LANES), 1)
    col = lax.broadcasted_iota(jnp.int32, (1, 2 * SWA_BLOCK), 1)
    npair = SWA_Q_HEADS // 2

    def blocks(specs):
        items = [(qs, ks, nk, boff, pair, (2 * pair) // SWA_GROUP) for qs, ks, nk, boff in specs for pair in range(npair)]
        ss = []
        for qs, ks, nk, boff, pair, kv in items:
            qblk = q_ref[0, pl.ds(qs, SWA_BLOCK), pair * LANES:(pair + 1) * LANES]
            zero = jnp.zeros_like(qblk)
            q2 = jnp.concatenate([jnp.where(lane < half, qblk, zero), jnp.where(lane >= half, qblk, zero)], axis=0)
            kblk = k_ref[0, pl.ds(ks, nk), kv * LANES:(kv + 1) * LANES]
            s = lax.dot_general(kblk, q2, _NT, preferred_element_type=jnp.float32)
            ss.append(s + bias_ref[pair, boff:boff + nk, :])
        stats = []
        for (qs, ks, nk, boff, pair, kv), s in zip(items, ss):
            sink = jnp.where(col < SWA_BLOCK, sink_ref[2 * pair], sink_ref[2 * pair + 1])
            m = jnp.maximum(_colmax(s), sink)
            stats.append((jnp.exp(s - m).astype(jnp.bfloat16), jnp.exp(sink - m)))
        rs = [jnp.dot(vaug_ref[kv, :, pl.ds(ks, nk)], p, preferred_element_type=jnp.float32)
              for (qs, ks, nk, boff, pair, kv), (p, _) in zip(items, stats)]
        outs = []
        for (qs, ks, nk, boff, pair, kv), (_, esink), r in zip(items, stats, rs):
            l = (r[half:half + 1, :] if kv == 0 else r[0:1, :]) + esink
            o = (r[0:half, :] if kv == 0 else r[half:, :]) / l
            outs.append(jnp.concatenate([o[:, :SWA_BLOCK], o[:, SWA_BLOCK:]], axis=0))
        for i, (qs, _, _, _) in enumerate(specs):
            ot = jnp.concatenate(outs[i * npair:(i + 1) * npair], axis=0)
            o_ref[0, pl.ds(qs, SWA_BLOCK), :] = ot.T.astype(jnp.bfloat16)

    blocks([(0, 0, SWA_BLOCK, SWA_BLOCK), (SWA_BLOCK, 0, 2 * SWA_BLOCK, 0)])

    def body(t, carry):
        specs = []
        for n in (2 * t, 2 * t + 1):
            qs = pl.multiple_of(n * SWA_BLOCK, SWA_BLOCK)
            specs.append((qs, pl.multiple_of(qs - SWA_BLOCK, SWA_BLOCK), 2 * SWA_BLOCK, 0))
        blocks(specs)
        return carry

    assert nblk % 2 == 0
    lax.fori_loop(1, nblk // 2, body, 0)


def _swa(sinks, qk, vt, bias):
    B, S, _ = qk.shape
    nq = SWA_Q_W // LANES
    return pl.pallas_call(
        _swa_kernel,
        grid_spec=pltpu.PrefetchScalarGridSpec(
            num_scalar_prefetch=1,
            grid=(B,),
            in_specs=[
                pl.BlockSpec((1, S, SWA_Q_W), lambda b, s: (b, 0, QK_SWA_Q // SWA_Q_W)),
                pl.BlockSpec((1, S, 2 * LANES), lambda b, s: (b, 0, QK_SWA_K // (2 * LANES))),
                pl.BlockSpec((1, SWA_KV_W, S), lambda b, s: (b, VT_SWA // SWA_KV_W, 0)),
                pl.BlockSpec((SWA_Q_HEADS // 2, 2 * SWA_BLOCK, 2 * SWA_BLOCK), lambda b, s: (0, 0, 0)),
            ],
            out_specs=pl.BlockSpec((1, S, SWA_Q_W), lambda b, s: (b, 0, 0)),
            scratch_shapes=[pltpu.VMEM((SWA_KV_HEADS, SWA_KV_W, S), jnp.bfloat16)],
        ),
        out_shape=jax.ShapeDtypeStruct((B, S, SWA_Q_W), jnp.bfloat16),
        compiler_params=pltpu.CompilerParams(
            dimension_semantics=("parallel",), vmem_limit_bytes=VMEM_LIMIT),
        name="swa",
    )(sinks, qk, qk, vt, bias)


def _colmax(s):
    while s.shape[0] > 8:
        h = s.shape[0] // 2
        s = jnp.maximum(s[:h], s[h:])
    return jnp.max(s, axis=0, keepdims=True)


def _moba_kernel(fqb_ref, fc_ref, q_ref, k_ref, vt_ref, bias_ref, o_ref,
                 vaug_ref, gate_ref, rank_ref, sel_ref, far_ref, m_scr, acc_scr, sa_scr, sb_scr):
    S = q_ref.shape[1]
    nb = S // MOBA_BLOCK
    half = HEAD_DIM
    BLK = MOBA_BLOCK

    vt = vt_ref[0]
    rows = lax.broadcasted_iota(jnp.int32, vt.shape, 0)
    one = jnp.ones_like(vt)
    vaug_ref[0] = jnp.where(rows < half, vt, one)
    vaug_ref[1] = jnp.where(rows < half, one, vt)

    kf = k_ref[0].astype(jnp.float32).reshape(nb, BLK, LANES)
    kmean = jnp.sum(kf, axis=1) * (1.0 / BLK)
    k_hi = kmean.astype(jnp.bfloat16)
    k_lo = (kmean - k_hi.astype(jnp.float32)).astype(jnp.bfloat16)
    kcat = jnp.concatenate([k_hi, k_lo], axis=0)

    lane_q = lax.broadcasted_iota(jnp.int32, (S, LANES), 1)
    brow = lax.broadcasted_iota(jnp.int32, (nb, S), 0)
    qblk_of = lax.broadcasted_iota(jnp.int32, (nb, S), 1) // BLK
    qall = q_ref[0]
    for h in range(2):
        qh = jnp.where((lane_q < half) if h == 0 else (lane_q >= half), qall, jnp.zeros_like(qall))
        g2 = lax.dot_general(kcat, qh, _NT, preferred_element_type=jnp.float32)
        gate_ref[...] = g2[0:nb] + g2[nb:2 * nb]
        rank_ref[...] = jnp.zeros_like(rank_ref)
        for m in range(nb - 1):
            lo = (m + 1) * BLK
            G = gate_ref[:, lo:]
            gm = gate_ref[m:m + 1, lo:]
            ge = jnp.where(gm >= G, 1.0, 0.0)
            gt = jnp.where(gm > G, 1.0, 0.0)
            brow_m = lax.broadcasted_iota(jnp.int32, (nb, S - lo), 0)
            rank_ref[:, lo:] += jnp.where(brow_m > m, ge, gt)
        top = rank_ref[...] < float(MOBA_TOPK)
        sel_ref[h] = jnp.where((brow < qblk_of) & top, 0.0, NEG_INF)
        far_ref[h] = jnp.where((brow < qblk_of - 1) & top, 0.0, NEG_INF)

    lane = lax.broadcasted_iota(jnp.int32, (BLK, LANES), 1)
    orow = lax.broadcasted_iota(jnp.int32, (LANES, BLK), 0)

    def head_q(qs, h):
        qblk = q_ref[0, pl.ds(qs, BLK), :]
        return jnp.where((lane < half) if h == 0 else (lane >= half), qblk, jnp.zeros_like(qblk))

    m0 = jnp.full((1, BLK), NEG_INF, jnp.float32)
    acc0 = jnp.zeros((LANES, BLK), jnp.float32)

    def item(h, qb, qs, ks, nk, bias, sel_a, sel_b, m, acc):
        return dict(h=h, qb=qb, qs=qs, ks=ks, nk=nk, bias=bias, sel_a=sel_a, sel_b=sel_b, m=m, acc=acc)

    def near_item(qb, h):
        qs = pl.multiple_of(qb * BLK, BLK)
        ps = pl.multiple_of(qs - BLK, BLK)
        sel_prev = sel_ref[h, pl.ds(qb - 1, 1), pl.ds(qs, BLK)]
        return item(h, qb, qs, ps, 2 * BLK, bias_ref[h], sel_prev, None, m0, acc0)

    def far_item(i, h):
        qb = fqb_ref[i]
        c = fc_ref[i]
        qs = pl.multiple_of(qb * BLK, BLK)
        ks = pl.multiple_of(c * (2 * BLK), 2 * BLK)
        sel_a = far_ref[h, pl.ds(2 * c, 1), pl.ds(qs, BLK)]
        sel_b = far_ref[h, pl.ds(2 * c + 1, 1), pl.ds(qs, BLK)]
        return item(h, qb, qs, ks, 2 * BLK, None, sel_a, sel_b, m_scr[h, qb], acc_scr[h, qb])

    def scores(it):
        kslab = k_ref[0, pl.ds(it["ks"], it["nk"]), :]
        s = lax.dot_general(kslab, head_q(it["qs"], it["h"]), _NT,
                            preferred_element_type=jnp.float32)
        return s if it["bias"] is None else s + it["bias"]

    def probs(it, s):
        m, sel_a, sel_b = it["m"], it["sel_a"], it["sel_b"]
        if it["nk"] == BLK:
            m_new = jnp.maximum(m, _colmax(s))
            p = jnp.exp(s - m_new)
        else:
            cm_a = _colmax(s[:BLK]) + sel_a
            cm_b = _colmax(s[BLK:])
            if sel_b is not None:
                cm_b = cm_b + sel_b
            m_new = jnp.maximum(m, jnp.maximum(cm_a, cm_b))
            p_a = jnp.exp(s[:BLK] - (m_new - sel_a))
            p_b = jnp.exp(s[BLK:] - (m_new if sel_b is None else m_new - sel_b))
            p = jnp.concatenate([p_a, p_b], axis=0)
        return m_new, jnp.exp(m - m_new), p.astype(jnp.bfloat16)

    def run(items):
        ss = [scores(it) for it in items]
        ps = [probs(it, s) for it, s in zip(items, ss)]
        pvs = [jnp.dot(vaug_ref[it["h"], :, pl.ds(it["ks"], it["nk"])], p, preferred_element_type=jnp.float32)
               for it, (_, _, p) in zip(items, ps)]
        for it, (m_new, alpha, _), pv in zip(items, ps, pvs):
            m_scr[it["h"], it["qb"]] = m_new
            acc_scr[it["h"], it["qb"]] = it["acc"] * alpha + pv

    run([item(h, 0, 0, 0, BLK, bias_ref[h, BLK:, :], None, None, m0, acc0) for h in range(2)]
        + [near_item(nb - 1, h) for h in range(2)])

    near_pairs = nb // 2 - 1

    def near_body(i, carry):
        run([near_item(qb, h) for qb in (i, i + near_pairs) for h in range(2)])
        return carry

    lax.fori_loop(1, near_pairs + 1, near_body, 0)

    n_far = fqb_ref.shape[0]
    n_groups = n_far // 2

    def group(g):
        return [(j, h) for j in (g, g + n_groups) for h in range(2)]

    def score_stage(g, s_buf):
        for k, (j, h) in enumerate(group(g)):
            qs = pl.multiple_of(fqb_ref[j] * BLK, BLK)
            ks = pl.multiple_of(fc_ref[j] * (2 * BLK), 2 * BLK)
            s_buf[k] = lax.dot_general(k_ref[0, pl.ds(ks, 2 * BLK), :], head_q(qs, h), _NT,
                                       preferred_element_type=jnp.float32)

    def finish_stage(g, s_buf):
        items = [far_item(j, h) for j, h in group(g)]
        ps = [probs(it, s_buf[k]) for k, it in enumerate(items)]
        pvs = [jnp.dot(vaug_ref[it["h"], :, pl.ds(it["ks"], it["nk"])], p, preferred_element_type=jnp.float32)
               for it, (_, _, p) in zip(items, ps)]
        for it, (m_new, alpha, _), pv in zip(items, ps, pvs):
            m_scr[it["h"], it["qb"]] = m_new
            acc_scr[it["h"], it["qb"]] = it["acc"] * alpha + pv

    score_stage(0, sa_scr)

    def far_body(t, carry):
        g = 2 * t
        score_stage(g + 1, sb_scr)
        finish_stage(g, sa_scr)
        score_stage(jnp.minimum(g + 2, n_groups - 1), sa_scr)
        finish_stage(g + 1, sb_scr)
        return carry

    assert n_groups % 2 == 0
    lax.fori_loop(0, n_groups // 2, far_body, 0)

    def out_body(qb, carry):
        qs = pl.multiple_of(qb * BLK, BLK)
        a0 = acc_scr[0, qb]
        a1 = acc_scr[1, qb]
        ot = jnp.where(orow < half, a0 / a0[half:half + 1, :], a1 / a1[0:1, :])
        o_ref[0, pl.ds(qs, BLK), :] = ot.T.astype(jnp.bfloat16)
        return carry

    lax.fori_loop(0, nb, out_body, 0)


def _moba_far_items(nb):
    items = [(qb, c) for qb in range(2, nb) for c in range(qb // 2)]
    n = len(items)
    assert n % 2 == 0 and all(items[i][0] != items[i + n // 2][0] for i in range(n // 2))
    return np.array([it[0] for it in items], np.int32), np.array([it[1] for it in items], np.int32)


def _moba(qk, vt, bias):
    B, S, _ = qk.shape
    nb = S // MOBA_BLOCK
    npair = MOBA_W // LANES
    far_qb, far_c = _moba_far_items(nb)
    return pl.pallas_call(
        _moba_kernel,
        grid_spec=pltpu.PrefetchScalarGridSpec(
            num_scalar_prefetch=2,
            grid=(B, npair),
            in_specs=[
                pl.BlockSpec((1, S, LANES), lambda b, p, fq, fc: (b, 0, QK_MOBA_Q // LANES + p)),
                pl.BlockSpec((1, S, LANES), lambda b, p, fq, fc: (b, 0, QK_MOBA_K // LANES + p)),
                pl.BlockSpec((1, LANES, S), lambda b, p, fq, fc: (b, VT_MOBA // LANES + p, 0)),
                pl.BlockSpec((2, 2 * MOBA_BLOCK, MOBA_BLOCK), lambda b, p, fq, fc: (p, 0, 0)),
            ],
            out_specs=pl.BlockSpec((1, S, LANES), lambda b, p, fq, fc: (b, 0, p)),
            scratch_shapes=[
                pltpu.VMEM((2, LANES, S), jnp.bfloat16),
                pltpu.VMEM((nb, S), jnp.float32),
                pltpu.VMEM((nb, S), jnp.float32),
                pltpu.VMEM((2, nb, S), jnp.float32),
                pltpu.VMEM((2, nb, S), jnp.float32),
                pltpu.VMEM((2, nb, 1, MOBA_BLOCK), jnp.float32),
                pltpu.VMEM((2, nb, LANES, MOBA_BLOCK), jnp.float32),
                pltpu.VMEM((4, 2 * MOBA_BLOCK, MOBA_BLOCK), jnp.float32),
                pltpu.VMEM((4, 2 * MOBA_BLOCK, MOBA_BLOCK), jnp.float32),
            ],
        ),
        out_shape=jax.ShapeDtypeStruct((B, S, MOBA_W), jnp.bfloat16),
        compiler_params=pltpu.CompilerParams(
            dimension_semantics=("parallel", "parallel"), vmem_limit_bytes=VMEM_LIMIT),
        name="moba",
    )(jnp.asarray(far_qb), jnp.asarray(far_c), qk, qk, vt, bias)


def _layer_norm(h, gain, bias):
    mu = jnp.mean(h, axis=-1, keepdims=True)
    c = h - mu
    var = jnp.mean(c * c, axis=-1, keepdims=True)
    return c * lax.rsqrt(var + LN_EPS) * gain + bias


def _pack_bf16_pair(a, b):
    ia = lax.bitcast_convert_type(a.astype(jnp.bfloat16).astype(jnp.float32), jnp.int32)
    ib = lax.bitcast_convert_type(b.astype(jnp.bfloat16).astype(jnp.float32), jnp.int32)
    return lax.shift_right_logical(ia, 16) | ib


def _unpack_bf16_pair(w):
    lo = lax.bitcast_convert_type(lax.shift_left(w, 16), jnp.float32)
    hi = lax.bitcast_convert_type(w & jnp.int32(-65536), jnp.float32)
    return lo.astype(jnp.bfloat16), hi.astype(jnp.bfloat16)


def _merge_kernel(x_ref, ya_ref, yb_ref, wg_ref, bg_ref, wa_ref, wb_ref, wo_ref, g1_ref, b1_ref,
                  wr_ref, br_ref, tri_ref, x1_ref, x1p_ref, r_ref, cnt_ref):
    x = x_ref[...]
    xb = x.astype(jnp.bfloat16)
    z = jnp.dot(xb, wg_ref[...], preferred_element_type=jnp.float32) + bg_ref[...]
    gates = 1.0 / (1.0 + jnp.exp(-z))
    pa = jnp.dot(ya_ref[...], wa_ref[...], preferred_element_type=jnp.float32)
    pb = jnp.dot(yb_ref[...], wb_ref[...], preferred_element_type=jnp.float32)
    merged = gates[:, :D_MODEL] * pa + gates[:, D_MODEL:] * pb
    mixed = jnp.dot(merged.astype(jnp.bfloat16), wo_ref[...], preferred_element_type=jnp.float32)
    x1 = _layer_norm(DEEPNORM_ALPHA * x + mixed, g1_ref[...], b1_ref[...])
    x1_ref[...] = x1
    x1_hi = x1.astype(jnp.bfloat16)
    x1p_ref[...] = _pack_bf16_pair(x1[:, :D_MODEL // 2], x1[:, D_MODEL // 2:])
    x1_lo = (x1 - x1_hi.astype(jnp.float32)).astype(jnp.bfloat16)

    R = ROUTER_ROWS
    l1 = lax.dot_general(wr_ref[...], x1_hi, _NT, preferred_element_type=jnp.float32)
    l2 = lax.dot_general(wr_ref[0:R, :], x1_lo, _NT, preferred_element_type=jnp.float32)
    L = l1[0:R] + l1[R:2 * R] + l2 + br_ref[...]
    tm = x.shape[0]
    row = lax.broadcasted_iota(jnp.int32, (8, tm), 0)
    big = jnp.float32(-3e38)
    gl = jnp.where(row < N_GROUPS, L[0:8], big)
    gmax = jnp.max(gl, axis=0, keepdims=True)
    g_idx = jnp.min(jnp.where(gl == gmax, row, 8), axis=0, keepdims=True)
    gsum = jnp.sum(jnp.where(row < N_GROUPS, jnp.exp(gl - gmax), 0.0), axis=0, keepdims=True)
    g_prob = 1.0 / gsum
    E = L[8 + 8 * (N_GROUPS - 1):8 + 8 * N_GROUPS]
    for g in range(N_GROUPS - 2, -1, -1):
        E = jnp.where(g_idx == g, L[8 + 8 * g:16 + 8 * g], E)
    t0 = jnp.max(E, axis=0, keepdims=True)
    loc0 = jnp.min(jnp.where(E == t0, row, 8), axis=0, keepdims=True)
    E2 = jnp.where(row == loc0, big, E)
    t1 = jnp.max(E2, axis=0, keepdims=True)
    loc1 = jnp.min(jnp.where(E2 == t1, row, 8), axis=0, keepdims=True)
    ex = jnp.exp(t1 - t0)
    w0 = g_prob / (1.0 + ex)
    w1 = g_prob * ex / (1.0 + ex)
    e0i = g_idx * EXPERTS_PER_GROUP + loc0
    e1i = g_idx * EXPERTS_PER_GROUP + loc1

    erow = lax.broadcasted_iota(jnp.int32, (N_EXPERTS, tm), 0)
    oh0 = jnp.where(erow == e0i, 1.0, 0.0)
    oh1 = jnp.where(erow == e1i, 1.0, 0.0)
    both = oh0 + oh1

    @pl.when(pl.program_id(0) == 0)
    def _():
        cnt_ref[...] = jnp.zeros_like(cnt_ref)

    prefix = jnp.dot(both.astype(jnp.bfloat16), tri_ref[...], preferred_element_type=jnp.float32)
    prefix = prefix + cnt_ref[:, 0:1]
    rank0 = jnp.sum(oh0 * prefix, axis=0, keepdims=True)
    rank1 = jnp.sum(oh1 * prefix, axis=0, keepdims=True)
    cnt_ref[...] = cnt_ref[...] + jnp.sum(both, axis=1, keepdims=True)

    vals = (e0i.astype(jnp.float32), e1i.astype(jnp.float32), w0, w1, rank0, rank1)
    out = jnp.zeros((8, tm), jnp.float32)
    for k, v in enumerate(vals):
        out = jnp.where(row == k, v, out)
    r_ref[...] = out


def _merge(x2, ya, yb, wg, bg, wa, wb, wo, g1, b1, wr, br, part, n_parts, tm=512):
    D = x2.shape[1]
    T = x2.shape[0] // n_parts
    off = part * (T // tm)
    const = lambda i: (0, 0)
    tri = jnp.triu(jnp.ones((tm, tm), jnp.bfloat16), k=1)
    return pl.pallas_call(
        _merge_kernel,
        grid=(T // tm,),
        in_specs=[
            pl.BlockSpec((tm, D), lambda i: (i + off, 0)),
            pl.BlockSpec((tm, SWA_Q_W), lambda i: (i + off, 0)),
            pl.BlockSpec((tm, MOBA_W), lambda i: (i + off, 0)),
            pl.BlockSpec((D, 2 * D), const),
            pl.BlockSpec((1, 2 * D), const),
            pl.BlockSpec((SWA_Q_W, D), const),
            pl.BlockSpec((MOBA_W, D), const),
            pl.BlockSpec((D, D), const),
            pl.BlockSpec((1, D), const),
            pl.BlockSpec((1, D), const),
            pl.BlockSpec((2 * ROUTER_ROWS, D), const),
            pl.BlockSpec((ROUTER_ROWS, 1), const),
            pl.BlockSpec((tm, tm), const),
        ],
        out_specs=[
            pl.BlockSpec((tm, D), lambda i: (i, 0)),
            pl.BlockSpec((tm, D // 2), lambda i: (i, 0)),
            pl.BlockSpec((8, tm), lambda i: (0, i)),
            pl.BlockSpec((N_EXPERTS, LANES), const),
        ],
        out_shape=[
            jax.ShapeDtypeStruct((T, D), jnp.float32),
            jax.ShapeDtypeStruct((T, D // 2), jnp.int32),
            jax.ShapeDtypeStruct((8, T), jnp.float32),
            jax.ShapeDtypeStruct((N_EXPERTS, LANES), jnp.float32),
        ],
        compiler_params=pltpu.CompilerParams(
            dimension_semantics=("arbitrary",), vmem_limit_bytes=VMEM_LIMIT),
        name="merge",
    )(x2, ya, yb, wg, bg, wa, wb, wo, g1, b1, wr, br, tri)


def _dest_kernel(r_ref, ps_ref, d_ref):
    tt = r_ref.shape[1]
    erow = lax.broadcasted_iota(jnp.int32, (N_EXPERTS, tt), 0)
    row = lax.broadcasted_iota(jnp.int32, (8, tt), 0)
    ps = ps_ref[...]
    out = jnp.zeros((8, tt), jnp.float32)
    for k in range(EXPERT_TOPK):
        e = r_ref[k:k + 1, :].astype(jnp.int32)
        start = jnp.sum(jnp.where(erow == e, ps, 0.0), axis=0, keepdims=True)
        out = jnp.where(row == k, start + r_ref[4 + k:5 + k, :], out)
    d_ref[...] = out.astype(jnp.int32)


def _dest(rinfo, padded_start, tt=8192):
    T = rinfo.shape[1]
    tt = min(tt, T)
    return pl.pallas_call(
        _dest_kernel,
        grid=(T // tt,),
        in_specs=[pl.BlockSpec((8, tt), lambda i: (0, i)), pl.BlockSpec((N_EXPERTS, 1), lambda i: (0, 0))],
        out_specs=pl.BlockSpec((8, tt), lambda i: (0, i)),
        out_shape=jax.ShapeDtypeStruct((8, T), jnp.int32),
        compiler_params=pltpu.CompilerParams(dimension_semantics=("parallel",)),
        name="dest",
    )(rinfo, padded_start.astype(jnp.float32)[:, None])


def _sc_workers():
    info = plsc.get_sparse_core_info()
    return info.num_cores, info.num_subcores


def _sc_scatter_rows(src, dest0, dest1, cap):
    T, W = src.shape
    nc, ns = _sc_workers()
    per_w = T // (nc * ns)
    assert per_w * nc * ns == T and per_w % SC_CHUNK == 0
    mesh = plsc.VectorSubcoreMesh(core_axis_name="c", subcore_axis_name="s")

    @functools.partial(
        pl.kernel, mesh=mesh,
        out_type=jax.ShapeDtypeStruct((cap, W), src.dtype),
        scratch_types=[pltpu.VMEM((SC_CHUNK,), jnp.int32), pltpu.VMEM((SC_CHUNK, W), src.dtype)],
    )
    def scatter(src_hbm, d0_hbm, d1_hbm, out_hbm, idx_v, rows_v):
        wid = lax.axis_index("s") * nc + lax.axis_index("c")
        base = wid * per_w

        @pl.loop(0, per_w // SC_CHUNK)
        def _(c):
            off = pl.multiple_of(base + c * SC_CHUNK, SC_CHUNK)
            pltpu.sync_copy(src_hbm.at[pl.ds(off, SC_CHUNK)], rows_v)
            for d_hbm in (d0_hbm, d1_hbm):
                pltpu.sync_copy(d_hbm.at[pl.ds(off, SC_CHUNK)], idx_v)
                pltpu.sync_copy(rows_v, out_hbm.at[idx_v])

    return scatter(src, dest0, dest1)


def _sc_gather_rows(table, idx):
    N = idx.shape[0]
    W = table.shape[1]
    nc, ns = _sc_workers()
    per_w = N // (nc * ns)
    assert per_w * nc * ns == N and per_w % SC_CHUNK == 0
    mesh = plsc.VectorSubcoreMesh(core_axis_name="c", subcore_axis_name="s")

    @functools.partial(
        pl.kernel, mesh=mesh,
        out_type=jax.ShapeDtypeStruct((N, W), table.dtype),
        scratch_types=[pltpu.VMEM((SC_CHUNK,), jnp.int32), pltpu.VMEM((SC_CHUNK, W), table.dtype)],
    )
    def gather(table_hbm, idx_hbm, out_hbm, idx_v, rows_v):
        wid = lax.axis_index("s") * nc + lax.axis_index("c")
        base = wid * per_w

        @pl.loop(0, per_w // SC_CHUNK)
        def _(c):
            off = pl.multiple_of(base + c * SC_CHUNK, SC_CHUNK)
            pltpu.sync_copy(idx_hbm.at[pl.ds(off, SC_CHUNK)], idx_v)
            pltpu.sync_copy(table_hbm.at[idx_v], rows_v)
            pltpu.sync_copy(rows_v, out_hbm.at[pl.ds(off, SC_CHUNK)])

    return gather(table, idx)


def _expert_kernel(be_ref, nv_ref, x_ref, wg_ref, wu_ref, wd_ref, y_ref, wgu_scr, wd_scr):
    i = pl.program_id(0)
    nv = nv_ref[i]

    @pl.when((i == 0) | (be_ref[i] != be_ref[jnp.maximum(i - 1, 0)]))
    def _():
        wgu_scr[:, :D_EXPERT] = wg_ref[0].astype(jnp.bfloat16)
        wgu_scr[:, D_EXPERT:] = wu_ref[0].astype(jnp.bfloat16)
        wd_scr[...] = wd_ref[0].astype(jnp.bfloat16)

    @pl.when(nv > 0)
    def _():
        lo, hi = _unpack_bf16_pair(x_ref[...])
        xb = jnp.concatenate([lo, hi], axis=1)
        rows = lax.broadcasted_iota(jnp.int32, xb.shape, 0)
        xb = jnp.where(rows < nv, xb, jnp.zeros_like(xb))
        gu = jnp.dot(xb, wgu_scr[...], preferred_element_type=jnp.float32)
        g = gu[:, :D_EXPERT]
        u = gu[:, D_EXPERT:]
        act = (g / (1.0 + jnp.exp(-g))) * u
        y = jnp.dot(act.astype(jnp.bfloat16), wd_scr[...], preferred_element_type=jnp.float32)
        y_ref[...] = _pack_bf16_pair(y[:, :D_MODEL // 2], y[:, D_MODEL // 2:])

    @pl.when(nv <= 0)
    def _():
        y_ref[...] = jnp.zeros_like(y_ref)


def _experts(blk_expert, blk_valid, xs, w_gate, w_up, w_down):
    cap, DW = xs.shape
    D = 2 * DW
    n_blocks = cap // MOE_TM
    return pl.pallas_call(
        _expert_kernel,
        grid_spec=pltpu.PrefetchScalarGridSpec(
            num_scalar_prefetch=2,
            grid=(n_blocks,),
            in_specs=[
                pl.BlockSpec((MOE_TM, DW), lambda i, be, nv: (i, 0)),
                pl.BlockSpec((1, D, D_EXPERT), lambda i, be, nv: (be[i], 0, 0)),
                pl.BlockSpec((1, D, D_EXPERT), lambda i, be, nv: (be[i], 0, 0)),
                pl.BlockSpec((1, D_EXPERT, D), lambda i, be, nv: (be[i], 0, 0)),
            ],
            out_specs=pl.BlockSpec((MOE_TM, DW), lambda i, be, nv: (i, 0)),
            scratch_shapes=[pltpu.VMEM((D, 2 * D_EXPERT), jnp.bfloat16), pltpu.VMEM((D_EXPERT, D), jnp.bfloat16)],
        ),
        out_shape=jax.ShapeDtypeStruct((cap, DW), jnp.int32),
        compiler_params=pltpu.CompilerParams(
            dimension_semantics=("arbitrary",), vmem_limit_bytes=VMEM_LIMIT),
        name="experts",
    )(blk_expert, blk_valid, xs, w_gate, w_up, w_down)


def _final_kernel(x1_ref, y0_ref, y1_ref, w_ref, g2_ref, b2_ref, *rest):
    o_ref = rest[-1]
    w = w_ref[...].T
    halves = []
    for part in range(2):
        y0 = _unpack_bf16_pair(y0_ref[...])[part].astype(jnp.float32)
        y1 = _unpack_bf16_pair(y1_ref[...])[part].astype(jnp.float32)
        halves.append(y0 * w[:, 2:3] + y1 * w[:, 3:4])
    moe = jnp.concatenate(halves, axis=1)
    o_ref[...] = _layer_norm(DEEPNORM_ALPHA * x1_ref[...] + moe, g2_ref[...], b2_ref[...])


def _final(x1, ypair, rinfo, g2, b2, out_prev, part, n_parts, tm=1024):
    T, D = x1.shape
    const = lambda i: (0, 0)
    nt = T // tm
    off = part * nt
    in_specs = [
        pl.BlockSpec((tm, D), lambda i: (i, 0)),
        pl.BlockSpec((tm, D // 2), lambda i: (i, 0)),
        pl.BlockSpec((tm, D // 2), lambda i: (i + nt, 0)),
        pl.BlockSpec((8, tm), lambda i: (0, i)),
        pl.BlockSpec((1, D), const),
        pl.BlockSpec((1, D), const),
    ]
    args = [x1, ypair, ypair, rinfo, g2, b2]
    aliases = {}
    if out_prev is not None:
        in_specs.append(pl.BlockSpec(memory_space=pl.ANY))
        args.append(out_prev)
        aliases = {len(args) - 1: 0}
    return pl.pallas_call(
        _final_kernel,
        grid=(nt,),
        in_specs=in_specs,
        out_specs=pl.BlockSpec((tm, D), lambda i: (i + off, 0)),
        out_shape=jax.ShapeDtypeStruct((n_parts * T, D), jnp.float32),
        input_output_aliases=aliases,
        compiler_params=pltpu.CompilerParams(
            dimension_semantics=("parallel",), vmem_limit_bytes=VMEM_LIMIT),
        name="final",
    )(*args)


def _bias_kernel(tab_ref, sub_ref, idx_ref, o_ref):
    h = pl.program_id(0)
    idx = idx_ref[...]
    acc = jnp.full(idx.shape, NEG_INF, jnp.float32)
    for b in range(REL_BUCKETS):
        acc = jnp.where(idx == b, tab_ref[h, b] - sub_ref[h], acc)
    o_ref[0] = acc


def _bias_expand(table_hb, sub_h, idx):
    H = table_hb.shape[0]
    K, Q = idx.shape
    return pl.pallas_call(
        _bias_kernel,
        grid_spec=pltpu.PrefetchScalarGridSpec(
            num_scalar_prefetch=2,
            grid=(H,),
            in_specs=[pl.BlockSpec((K, Q), lambda h, t, s: (0, 0))],
            out_specs=pl.BlockSpec((1, K, Q), lambda h, t, s: (h, 0, 0)),
        ),
        out_shape=jax.ShapeDtypeStruct((H, K, Q), jnp.float32),
        compiler_params=pltpu.CompilerParams(dimension_semantics=("parallel",)),
        name="bias",
    )(table_hb, sub_h, jnp.asarray(idx))


def _attention_bias_tables(rel_bias_table):
    rel = rel_bias_table.astype(jnp.float32)
    tab_a = rel[:, :SWA_Q_HEADS].T
    tab_b = rel[:, SWA_Q_HEADS:].T
    kj = np.arange(2 * SWA_BLOCK)[:, None]
    qi = np.arange(SWA_BLOCK)[None, :]
    dist = SWA_BLOCK + qi - kj
    idx_a = np.where((dist >= 0) & (dist < SWA_WINDOW), _rel_bucket_np(dist), -1).astype(np.int32)
    bias_a = _bias_expand(tab_a, jnp.zeros((SWA_Q_HEADS,), jnp.float32), idx_a)
    bias_a = bias_a.reshape(SWA_Q_HEADS // 2, 2, 2 * SWA_BLOCK, SWA_BLOCK).transpose(0, 2, 1, 3)
    bias_a = bias_a.reshape(SWA_Q_HEADS // 2, 2 * SWA_BLOCK, 2 * SWA_BLOCK)
    j = np.arange(MOBA_BLOCK)[:, None]
    i = np.arange(MOBA_BLOCK)[None, :]
    d_own = i - j
    idx_own = np.where(d_own >= 0, _rel_bucket_np(d_own), -1)
    idx_prev = _rel_bucket_np(MOBA_BLOCK + i - j)
    idx_b = np.concatenate([idx_prev, idx_own], axis=0).astype(np.int32)
    bias_b = _bias_expand(tab_b, tab_b[:, REL_BUCKETS - 1], idx_b)
    return bias_a, bias_b


def _block_plan(sizes, n_tok):
    n_assign = n_tok * EXPERT_TOPK
    padded = ((sizes + MOE_TM - 1) // MOE_TM) * MOE_TM
    padded_end = jnp.cumsum(padded)
    padded_start = padded_end - padded
    cap = -(-n_assign // MOE_TM) * MOE_TM + N_EXPERTS * MOE_TM
    blk_start = jnp.arange(cap // MOE_TM, dtype=jnp.int32) * MOE_TM
    blk_expert = jnp.minimum(
        jnp.sum(padded_end[None, :] <= blk_start[:, None], axis=1), N_EXPERTS - 1).astype(jnp.int32)
    used = blk_start < padded_end[-1]
    blk_valid = jnp.clip(sizes[blk_expert] - (blk_start - padded_start[blk_expert]), 0, MOE_TM)
    blk_valid = jnp.where(used, blk_valid, 0).astype(jnp.int32)
    return padded_start, blk_expert, blk_valid, cap


def kernel(x, w_in, b_in, attn_sinks, rel_bias_table, w_branch_swa, w_branch_moba, w_out, ln1_gain, ln1_bias,
           w_group_router, b_group_router, w_expert_router, b_expert_router, w_expert_gate, w_expert_up,
           w_expert_down, ln2_gain, ln2_bias):
    assert w_in.shape[0] == DEPTH == 1
    B, S, D = x.shape
    T = B * S
    bf16 = jnp.bfloat16
    f32 = jnp.float32
    w = w_in[0]
    b = b_in[0]

    def cols(off, width):
        return w[:, off:off + width], b[off:off + width]

    wq_a, bq_a = cols(OFF_SWA_Q, SWA_Q_W)
    wk_a, bk_a = cols(OFF_SWA_K, SWA_KV_W)
    wv_a, bv_a = cols(OFF_SWA_V, SWA_KV_W)
    wq_b, bq_b = cols(OFF_MOBA_Q, MOBA_W)
    wk_b, bk_b = cols(OFF_MOBA_K, MOBA_W)
    wv_b, bv_b = cols(OFF_MOBA_V, MOBA_W)

    def dup_kv(t):
        parts = [t[..., i * HEAD_DIM:(i + 1) * HEAD_DIM] for i in range(SWA_KV_HEADS)]
        return jnp.concatenate([p for p in parts for _ in range(2)], axis=-1)

    wn = jnp.concatenate([wq_a * ATTN_SCALE, dup_kv(wk_a), wq_b * ATTN_SCALE, wk_b], axis=1).astype(bf16)
    bn = jnp.concatenate([bq_a * ATTN_SCALE, dup_kv(bk_a), bq_b * ATTN_SCALE, bk_b])[None, :].astype(f32)
    wt = jnp.concatenate([wv_a, wv_b], axis=1).T.astype(bf16)
    bt = jnp.concatenate([bv_a, bv_b])[:, None].astype(f32)

    qk, vt = _inproj(x, wn, bn, wt, bt)

    bias_a, bias_b = _attention_bias_tables(rel_bias_table)
    y_a = _swa(attn_sinks[0].astype(f32), qk, vt, bias_a)
    y_b = _moba(qk, vt, bias_b)

    wg, bg = cols(OFF_GATE, 2 * D_MODEL)
    wr = jnp.zeros((ROUTER_ROWS, D), f32)
    wr = wr.at[0:N_GROUPS].set(w_group_router[0].T).at[8:8 + N_EXPERTS].set(w_expert_router[0].T)
    wr_hi = wr.astype(bf16)
    wr_lo = (wr - wr_hi.astype(f32)).astype(bf16)
    br = jnp.zeros((ROUTER_ROWS,), f32)
    br = br.at[0:N_GROUPS].set(b_group_router[0]).at[8:8 + N_EXPERTS].set(b_expert_router[0])[:, None]
    merge_args = (
        x.reshape(T, D), y_a.reshape(T, SWA_Q_W), y_b.reshape(T, MOBA_W),
        wg.astype(bf16), bg[None, :].astype(f32), w_branch_swa[0].astype(bf16), w_branch_moba[0].astype(bf16),
        w_out[0].astype(bf16), ln1_gain[0][None, :].astype(f32), ln1_bias[0][None, :].astype(f32),
        jnp.concatenate([wr_hi, wr_lo], axis=0), br)
    g2 = ln2_gain[0][None, :].astype(f32)
    b2 = ln2_bias[0][None, :].astype(f32)

    Tp = T // MOE_PARTS
    out = None
    for part in range(MOE_PARTS):
        x1, x1p, rinfo, counts = _merge(*merge_args, part, MOE_PARTS)
        sizes = counts[:, 0].astype(jnp.int32)
        padded_start, blk_expert, blk_valid, cap = _block_plan(sizes, Tp)
        dest = _dest(rinfo, padded_start)
        xs = _sc_scatter_rows(x1p, dest[0], dest[1], cap)
        y_buf = _experts(blk_expert, blk_valid, xs, w_expert_gate[0], w_expert_up[0], w_expert_down[0])
        ypair = _sc_gather_rows(y_buf, dest[0:EXPERT_TOPK].reshape(-1))
        out = _final(x1, ypair, rinfo, g2, b2, out, part, MOE_PARTS)
    return out.reshape(B, S, D)
```

```python
import functools
import math

import numpy as np
import jax
import jax.numpy as jnp
from jax import lax
from jax.experimental import pallas as pl
from jax.experimental.pallas import tpu as pltpu
from jax.experimental.pallas import tpu_sc as plsc

D_MODEL = 1024
HEAD_DIM = 64
SWA_Q_HEADS = 8
SWA_KV_HEADS = 2
SWA_GROUP = SWA_Q_HEADS // SWA_KV_HEADS
SWA_WINDOW = 128
SWA_BLOCK = 128
MOBA_HEADS = 8
MOBA_BLOCK = 256
MOBA_TOPK = 3
REL_BUCKETS = 32
REL_MAX_DIST = 128
N_GROUPS = 4
EXPERTS_PER_GROUP = 8
N_EXPERTS = N_GROUPS * EXPERTS_PER_GROUP
EXPERT_TOPK = 2
D_EXPERT = 512
LN_EPS = 1e-5
DEPTH = 1
DEEPNORM_ALPHA = (2.0 * DEPTH) ** 0.25
NEG_INF = -1e30
ATTN_SCALE = HEAD_DIM ** -0.5

SWA_Q_W = SWA_Q_HEADS * HEAD_DIM
SWA_KV_W = SWA_KV_HEADS * HEAD_DIM
MOBA_W = MOBA_HEADS * HEAD_DIM
OFF_SWA_Q = 0
OFF_SWA_K = OFF_SWA_Q + SWA_Q_W
OFF_SWA_V = OFF_SWA_K + SWA_KV_W
OFF_MOBA_Q = OFF_SWA_V + SWA_KV_W
OFF_MOBA_K = OFF_MOBA_Q + MOBA_W
OFF_MOBA_V = OFF_MOBA_K + MOBA_W
OFF_GATE = OFF_MOBA_V + MOBA_W

LANES = 128
QK_SWA_Q = 0
QK_SWA_K = QK_SWA_Q + SWA_Q_W
QK_MOBA_Q = QK_SWA_K + SWA_KV_HEADS * LANES
QK_MOBA_K = QK_MOBA_Q + MOBA_W
QK_W = QK_MOBA_K + MOBA_W
VT_SWA = 0
VT_MOBA = VT_SWA + SWA_KV_W
VT_W = VT_MOBA + MOBA_W

MOE_TM = 512
MOE_PARTS = 2
SC_CHUNK = 128
ROUTER_ROWS = 8 + N_EXPERTS
VMEM_LIMIT = 56 * 1024 * 1024

_NT = (((1,), (1,)), ((), ()))


def _rel_bucket_np(dist):
    n = np.maximum(dist, 0)
    max_exact = REL_BUCKETS // 2
    nf = np.maximum(n, 1).astype(np.float32)
    large = max_exact + (np.log(nf / np.float32(max_exact)) / np.float32(math.log(REL_MAX_DIST / max_exact))
                         * np.float32(REL_BUCKETS - max_exact)).astype(np.int32)
    large = np.minimum(large, REL_BUCKETS - 1)
    return np.where(n < max_exact, n, large).astype(np.int32)


def _inproj_kernel(x_ref, wn_ref, bn_ref, wt_ref, bt_ref, qk_ref, vt_ref):
    xb = x_ref[0].astype(jnp.bfloat16)
    qk = jnp.dot(xb, wn_ref[...], preferred_element_type=jnp.float32) + bn_ref[...]
    qk_ref[0] = qk.astype(jnp.bfloat16)
    vt = lax.dot_general(wt_ref[...], xb, _NT, preferred_element_type=jnp.float32) + bt_ref[...]
    vt_ref[0] = vt.astype(jnp.bfloat16)


def _inproj(x, wn, bn, wt, bt, tm=512):
    B, S, D = x.shape
    return pl.pallas_call(
        _inproj_kernel,
        grid=(B, S // tm),
        in_specs=[
            pl.BlockSpec((1, tm, D), lambda b, i: (b, i, 0)),
            pl.BlockSpec((D, QK_W), lambda b, i: (0, 0)),
            pl.BlockSpec((1, QK_W), lambda b, i: (0, 0)),
            pl.BlockSpec((VT_W, D), lambda b, i: (0, 0)),
            pl.BlockSpec((VT_W, 1), lambda b, i: (0, 0)),
        ],
        out_specs=[
            pl.BlockSpec((1, tm, QK_W), lambda b, i: (b, i, 0)),
            pl.BlockSpec((1, VT_W, tm), lambda b, i: (b, 0, i)),
        ],
        out_shape=[
            jax.ShapeDtypeStruct((B, S, QK_W), jnp.bfloat16),
            jax.ShapeDtypeStruct((B, VT_W, S), jnp.bfloat16),
        ],
        compiler_params=pltpu.CompilerParams(
            dimension_semantics=("parallel", "parallel"), vmem_limit_bytes=VMEM_LIMIT),
        name="inproj",
    )(x, wn, bn, wt, bt)


def _swa_kernel(sink_ref, q_ref, k_ref, vt_ref, bias_ref, o_ref, vaug_ref):
    S = q_ref.shape[1]
    nblk = S // SWA_BLOCK
    half = HEAD_DIM
    vt = vt_ref[0]
    rows = lax.broadcasted_iota(jnp.int32, vt.shape, 0)
    one = jnp.ones_like(vt)
    vaug_ref[0] = jnp.where(rows < half, vt, one)
    vaug_ref[1] = jnp.where(rows < half, one, vt)
    lane = lax.broadcasted_iota(jnp.int32, (SWA_BLOCK, LANES), 1)
    col = lax.broadcasted_iota(jnp.int32, (1, 2 * SWA_BLOCK), 1)
    npair = SWA_Q_HEADS // 2

    def blocks(specs):
        items = [(qs, ks, nk, boff, pair, (2 * pair) // SWA_GROUP) for qs, ks, nk, boff in specs for pair in range(npair)]
        ss = []
        for qs, ks, nk, boff, pair, kv in items:
            qblk = q_ref[0, pl.ds(qs, SWA_BLOCK), pair * LANES:(pair + 1) * LANES]
            zero = jnp.zeros_like(qblk)
            q2 = jnp.concatenate([jnp.where(lane < half, qblk, zero), jnp.where(lane >= half, qblk, zero)], axis=0)
            kblk = k_ref[0, pl.ds(ks, nk), kv * LANES:(kv + 1) * LANES]
            s = lax.dot_general(kblk, q2, _NT, preferred_element_type=jnp.float32)
            ss.append(s + bias_ref[pair, boff:boff + nk, :])
        stats = []
        for (qs, ks, nk, boff, pair, kv), s in zip(items, ss):
            sink = jnp.where(col < SWA_BLOCK, sink_ref[2 * pair], sink_ref[2 * pair + 1])
            m = jnp.maximum(_colmax(s), sink)
            stats.append((jnp.exp(s - m).astype(jnp.bfloat16), jnp.exp(sink - m)))
        rs = [jnp.dot(vaug_ref[kv, :, pl.ds(ks, nk)], p, preferred_element_type=jnp.float32)
              for (qs, ks, nk, boff, pair, kv), (p, _) in zip(items, stats)]
        outs = []
        for (qs, ks, nk, boff, pair, kv), (_, esink), r in zip(items, stats, rs):
            l = (r[half:half + 1, :] if kv == 0 else r[0:1, :]) + esink
            o = (r[0:half, :] if kv == 0 else r[half:, :]) / l
            outs.append(jnp.concatenate([o[:, :SWA_BLOCK], o[:, SWA_BLOCK:]], axis=0))
        for i, (qs, _, _, _) in enumerate(specs):
            ot = jnp.concatenate(outs[i * npair:(i + 1) * npair], axis=0)
            o_ref[0, pl.ds(qs, SWA_BLOCK), :] = ot.T.astype(jnp.bfloat16)

    blocks([(0, 0, SWA_BLOCK, SWA_BLOCK), (SWA_BLOCK, 0, 2 * SWA_BLOCK, 0)])

    def body(t, carry):
        specs = []
        for n in (2 * t, 2 * t + 1):
            qs = pl.multiple_of(n * SWA_BLOCK, SWA_BLOCK)
            specs.append((qs, pl.multiple_of(qs - SWA_BLOCK, SWA_BLOCK), 2 * SWA_BLOCK, 0))
        blocks(specs)
        return carry

    assert nblk % 2 == 0
    lax.fori_loop(1, nblk // 2, body, 0)


def _swa(sinks, qk, vt, bias):
    B, S, _ = qk.shape
    nq = SWA_Q_W // LANES
    return pl.pallas_call(
        _swa_kernel,
        grid_spec=pltpu.PrefetchScalarGridSpec(
            num_scalar_prefetch=1,
            grid=(B,),
            in_specs=[
                pl.BlockSpec((1, S, SWA_Q_W), lambda b, s: (b, 0, QK_SWA_Q // SWA_Q_W)),
                pl.BlockSpec((1, S, 2 * LANES), lambda b, s: (b, 0, QK_SWA_K // (2 * LANES))),
                pl.BlockSpec((1, SWA_KV_W, S), lambda b, s: (b, VT_SWA // SWA_KV_W, 0)),
                pl.BlockSpec((SWA_Q_HEADS // 2, 2 * SWA_BLOCK, 2 * SWA_BLOCK), lambda b, s: (0, 0, 0)),
            ],
            out_specs=pl.BlockSpec((1, S, SWA_Q_W), lambda b, s: (b, 0, 0)),
            scratch_shapes=[pltpu.VMEM((SWA_KV_HEADS, SWA_KV_W, S), jnp.bfloat16)],
        ),
        out_shape=jax.ShapeDtypeStruct((B, S, SWA_Q_W), jnp.bfloat16),
        compiler_params=pltpu.CompilerParams(
            dimension_semantics=("parallel",), vmem_limit_bytes=VMEM_LIMIT),
        name="swa",
    )(sinks, qk, qk, vt, bias)


def _colmax(s):
    while s.shape[0] > 8:
        h = s.shape[0] // 2
        s = jnp.maximum(s[:h], s[h:])
    return jnp.max(s, axis=0, keepdims=True)


def _moba_kernel(fqb_ref, fc_ref, q_ref, k_ref, vt_ref, bias_ref, o_ref,
                 vaug_ref, gate_ref, rank_ref, sel_ref, far_ref, m_scr, acc_scr, sa_scr, sb_scr, sc_scr, sd_scr):
    S = q_ref.shape[1]
    nb = S // MOBA_BLOCK
    half = HEAD_DIM
    BLK = MOBA_BLOCK

    vt = vt_ref[0]
    rows = lax.broadcasted_iota(jnp.int32, vt.shape, 0)
    one = jnp.ones_like(vt)
    vaug_ref[0] = jnp.where(rows < half, vt, one)
    vaug_ref[1] = jnp.where(rows < half, one, vt)

    kf = k_ref[0].astype(jnp.float32).reshape(nb, BLK, LANES)
    kmean = jnp.sum(kf, axis=1) * (1.0 / BLK)
    k_hi = kmean.astype(jnp.bfloat16)
    k_lo = (kmean - k_hi.astype(jnp.float32)).astype(jnp.bfloat16)
    kcat = jnp.concatenate([k_hi, k_lo], axis=0)

    lane_q = lax.broadcasted_iota(jnp.int32, (S, LANES), 1)
    brow = lax.broadcasted_iota(jnp.int32, (nb, S), 0)
    qblk_of = lax.broadcasted_iota(jnp.int32, (nb, S), 1) // BLK
    qall = q_ref[0]
    for h in range(2):
        qh = jnp.where((lane_q < half) if h == 0 else (lane_q >= half), qall, jnp.zeros_like(qall))
        g2 = lax.dot_general(kcat, qh, _NT, preferred_element_type=jnp.float32)
        gate_ref[...] = g2[0:nb] + g2[nb:2 * nb]
        rank_ref[...] = jnp.zeros_like(rank_ref)
        for m in range(nb - 1):
            lo = (m + 1) * BLK
            G = gate_ref[:, lo:]
            gm = gate_ref[m:m + 1, lo:]
            ge = jnp.where(gm >= G, 1.0, 0.0)
            gt = jnp.where(gm > G, 1.0, 0.0)
            brow_m = lax.broadcasted_iota(jnp.int32, (nb, S - lo), 0)
            rank_ref[:, lo:] += jnp.where(brow_m > m, ge, gt)
        top = rank_ref[...] < float(MOBA_TOPK)
        sel_ref[h] = jnp.where((brow < qblk_of) & top, 0.0, NEG_INF)
        far_ref[h] = jnp.where((brow < qblk_of - 1) & top, 0.0, NEG_INF)

    lane = lax.broadcasted_iota(jnp.int32, (BLK, LANES), 1)
    orow = lax.broadcasted_iota(jnp.int32, (LANES, BLK), 0)

    def head_q(qs, h):
        qblk = q_ref[0, pl.ds(qs, BLK), :]
        return jnp.where((lane < half) if h == 0 else (lane >= half), qblk, jnp.zeros_like(qblk))

    m0 = jnp.full((1, BLK), NEG_INF, jnp.float32)
    acc0 = jnp.zeros((LANES, BLK), jnp.float32)

    def item(h, qb, qs, ks, nk, bias, sel_a, sel_b, m, acc):
        return dict(h=h, qb=qb, qs=qs, ks=ks, nk=nk, bias=bias, sel_a=sel_a, sel_b=sel_b, m=m, acc=acc)

    def near_item(qb, h):
        qs = pl.multiple_of(qb * BLK, BLK)
        ps = pl.multiple_of(qs - BLK, BLK)
        sel_prev = sel_ref[h, pl.ds(qb - 1, 1), pl.ds(qs, BLK)]
        return item(h, qb, qs, ps, 2 * BLK, bias_ref[h], sel_prev, None, m0, acc0)

    def far_item(i, h):
        qb = fqb_ref[i]
        c = fc_ref[i]
        qs = pl.multiple_of(qb * BLK, BLK)
        ks = pl.multiple_of(c * (2 * BLK), 2 * BLK)
        sel_a = far_ref[h, pl.ds(2 * c, 1), pl.ds(qs, BLK)]
        sel_b = far_ref[h, pl.ds(2 * c + 1, 1), pl.ds(qs, BLK)]
        return item(h, qb, qs, ks, 2 * BLK, None, sel_a, sel_b, m_scr[h, qb], acc_scr[h, qb])

    def scores(it):
        kslab = k_ref[0, pl.ds(it["ks"], it["nk"]), :]
        s = lax.dot_general(kslab, head_q(it["qs"], it["h"]), _NT,
                            preferred_element_type=jnp.float32)
        return s if it["bias"] is None else s + it["bias"]

    def probs(it, s):
        m, sel_a, sel_b = it["m"], it["sel_a"], it["sel_b"]
        if it["nk"] == BLK:
            m_new = jnp.maximum(m, _colmax(s))
            p = jnp.exp(s - m_new)
        else:
            cm_a = _colmax(s[:BLK]) + sel_a
            cm_b = _colmax(s[BLK:])
            if sel_b is not None:
                cm_b = cm_b + sel_b
            m_new = jnp.maximum(m, jnp.maximum(cm_a, cm_b))
            p_a = jnp.exp(s[:BLK] - (m_new - sel_a))
            p_b = jnp.exp(s[BLK:] - (m_new if sel_b is None else m_new - sel_b))
            p = jnp.concatenate([p_a, p_b], axis=0)
        return m_new, jnp.exp(m - m_new), p.astype(jnp.bfloat16)

    def run(items):
        ss = [scores(it) for it in items]
        ps = [probs(it, s) for it, s in zip(items, ss)]
        pvs = [jnp.dot(vaug_ref[it["h"], :, pl.ds(it["ks"], it["nk"])], p, preferred_element_type=jnp.float32)
               for it, (_, _, p) in zip(items, ps)]
        for it, (m_new, alpha, _), pv in zip(items, ps, pvs):
            m_scr[it["h"], it["qb"]] = m_new
            acc_scr[it["h"], it["qb"]] = it["acc"] * alpha + pv

    run([item(h, 0, 0, 0, BLK, bias_ref[h, BLK:, :], None, None, m0, acc0) for h in range(2)]
        + [near_item(nb - 1, h) for h in range(2)])

    bufs = (sa_scr, sb_scr, sc_scr, sd_scr)
    nbuf = len(bufs)

    def finish(items, s_buf):
        ps = [probs(it, s_buf[k]) for k, it in enumerate(items)]
        pvs = [jnp.dot(vaug_ref[it["h"], :, pl.ds(it["ks"], it["nk"])], p, preferred_element_type=jnp.float32)
               for it, (_, _, p) in zip(items, ps)]
        for it, (m_new, alpha, _), pv in zip(items, ps, pvs):
            m_scr[it["h"], it["qb"]] = m_new
            acc_scr[it["h"], it["qb"]] = it["acc"] * alpha + pv

    def pipelined(n_groups, score, items_of):
        for g in range(min(2, n_groups)):
            score(g, bufs[g])

        def body(t, carry):
            for j in range(nbuf):
                score(nbuf * t + 2 + j, bufs[(2 + j) % nbuf])
                finish(items_of(nbuf * t + j), bufs[j])
            return carry

        n_trips = max(n_groups - 2, 0) // nbuf
        lax.fori_loop(0, n_trips, body, 0)
        for g in range(nbuf * n_trips, n_groups):
            if g + 2 < n_groups:
                score(g + 2, bufs[(g + 2) % nbuf])
            finish(items_of(g), bufs[g % nbuf])

    near_pairs = nb // 2 - 1

    def near_group(g):
        return [(qb, h) for qb in (g + 1, g + 1 + near_pairs) for h in range(2)]

    def near_score(g, s_buf):
        for k, (qb, h) in enumerate(near_group(g)):
            qs = pl.multiple_of(qb * BLK, BLK)
            ps = pl.multiple_of(qs - BLK, BLK)
            s_buf[k] = lax.dot_general(k_ref[0, pl.ds(ps, 2 * BLK), :], head_q(qs, h), _NT,
                                       preferred_element_type=jnp.float32) + bias_ref[h]

    pipelined(near_pairs, near_score, lambda g: [near_item(qb, h) for qb, h in near_group(g)])

    n_far_groups = fqb_ref.shape[0] // 2

    def far_group(g):
        return [(j, h) for j in (g, g + n_far_groups) for h in range(2)]

    def far_score(g, s_buf):
        for k, (j, h) in enumerate(far_group(g)):
            qs = pl.multiple_of(fqb_ref[j] * BLK, BLK)
            ks = pl.multiple_of(fc_ref[j] * (2 * BLK), 2 * BLK)
            s_buf[k] = lax.dot_general(k_ref[0, pl.ds(ks, 2 * BLK), :], head_q(qs, h), _NT,
                                       preferred_element_type=jnp.float32)

    pipelined(n_far_groups, far_score, lambda g: [far_item(j, h) for j, h in far_group(g)])

    def out_body(t, carry):
        for qb in (2 * t, 2 * t + 1):
            qs = pl.multiple_of(qb * BLK, BLK)
            a0 = acc_scr[0, qb]
            a1 = acc_scr[1, qb]
            ot = jnp.where(orow < half, a0 / a0[half:half + 1, :], a1 / a1[0:1, :])
            o_ref[0, pl.ds(qs, BLK), :] = ot.T.astype(jnp.bfloat16)
        return carry

    lax.fori_loop(0, nb // 2, out_body, 0)


def _moba_far_items(nb):
    items = [(qb, c) for qb in range(2, nb) for c in range(qb // 2)]
    n = len(items)
    assert n % 2 == 0 and all(items[i][0] != items[i + n // 2][0] for i in range(n // 2))
    return np.array([it[0] for it in items], np.int32), np.array([it[1] for it in items], np.int32)


def _moba(qk, vt, bias):
    B, S, _ = qk.shape
    nb = S // MOBA_BLOCK
    npair = MOBA_W // LANES
    far_qb, far_c = _moba_far_items(nb)
    return pl.pallas_call(
        _moba_kernel,
        grid_spec=pltpu.PrefetchScalarGridSpec(
            num_scalar_prefetch=2,
            grid=(B, npair),
            in_specs=[
                pl.BlockSpec((1, S, LANES), lambda b, p, fq, fc: (b, 0, QK_MOBA_Q // LANES + p)),
                pl.BlockSpec((1, S, LANES), lambda b, p, fq, fc: (b, 0, QK_MOBA_K // LANES + p)),
                pl.BlockSpec((1, LANES, S), lambda b, p, fq, fc: (b, VT_MOBA // LANES + p, 0)),
                pl.BlockSpec((2, 2 * MOBA_BLOCK, MOBA_BLOCK), lambda b, p, fq, fc: (p, 0, 0)),
            ],
            out_specs=pl.BlockSpec((1, S, LANES), lambda b, p, fq, fc: (b, 0, p)),
            scratch_shapes=[
                pltpu.VMEM((2, LANES, S), jnp.bfloat16),
                pltpu.VMEM((nb, S), jnp.float32),
                pltpu.VMEM((nb, S), jnp.float32),
                pltpu.VMEM((2, nb, S), jnp.float32),
                pltpu.VMEM((2, nb, S), jnp.float32),
                pltpu.VMEM((2, nb, 1, MOBA_BLOCK), jnp.float32),
                pltpu.VMEM((2, nb, LANES, MOBA_BLOCK), jnp.float32),
            ] + [pltpu.VMEM((4, 2 * MOBA_BLOCK, MOBA_BLOCK), jnp.float32)] * 4,
        ),
        out_shape=jax.ShapeDtypeStruct((B, S, MOBA_W), jnp.bfloat16),
        compiler_params=pltpu.CompilerParams(
            dimension_semantics=("parallel", "parallel"), vmem_limit_bytes=VMEM_LIMIT),
        name="moba",
    )(jnp.asarray(far_qb), jnp.asarray(far_c), qk, qk, vt, bias)


def _layer_norm(h, gain, bias):
    mu = jnp.mean(h, axis=-1, keepdims=True)
    c = h - mu
    var = jnp.mean(c * c, axis=-1, keepdims=True)
    return c * lax.rsqrt(var + LN_EPS) * gain + bias


def _pack_bf16_pair(a, b):
    ia = lax.bitcast_convert_type(a.astype(jnp.bfloat16).astype(jnp.float32), jnp.int32)
    ib = lax.bitcast_convert_type(b.astype(jnp.bfloat16).astype(jnp.float32), jnp.int32)
    return lax.shift_right_logical(ia, 16) | ib


def _unpack_bf16_pair(w):
    lo = lax.bitcast_convert_type(lax.shift_left(w, 16), jnp.float32)
    hi = lax.bitcast_convert_type(w & jnp.int32(-65536), jnp.float32)
    return lo.astype(jnp.bfloat16), hi.astype(jnp.bfloat16)


def _merge_kernel(x_ref, ya_ref, yb_ref, wg_ref, bg_ref, wa_ref, wb_ref, wo_ref, g1_ref, b1_ref,
                  wr_ref, br_ref, tri_ref, x1_ref, x1p_ref, r_ref, cnt_ref):
    x = x_ref[...]
    xb = x.astype(jnp.bfloat16)
    z = jnp.dot(xb, wg_ref[...], preferred_element_type=jnp.float32) + bg_ref[...]
    gates = 1.0 / (1.0 + jnp.exp(-z))
    pa = jnp.dot(ya_ref[...], wa_ref[...], preferred_element_type=jnp.float32)
    pb = jnp.dot(yb_ref[...], wb_ref[...], preferred_element_type=jnp.float32)
    merged = gates[:, :D_MODEL] * pa + gates[:, D_MODEL:] * pb
    mixed = jnp.dot(merged.astype(jnp.bfloat16), wo_ref[...], preferred_element_type=jnp.float32)
    x1 = _layer_norm(DEEPNORM_ALPHA * x + mixed, g1_ref[...], b1_ref[...])
    x1_ref[...] = x1
    x1_hi = x1.astype(jnp.bfloat16)
    x1p_ref[...] = _pack_bf16_pair(x1[:, :D_MODEL // 2], x1[:, D_MODEL // 2:])
    x1_lo = (x1 - x1_hi.astype(jnp.float32)).astype(jnp.bfloat16)

    R = ROUTER_ROWS
    l1 = lax.dot_general(wr_ref[...], x1_hi, _NT, preferred_element_type=jnp.float32)
    l2 = lax.dot_general(wr_ref[0:R, :], x1_lo, _NT, preferred_element_type=jnp.float32)
    L = l1[0:R] + l1[R:2 * R] + l2 + br_ref[...]
    tm = x.shape[0]
    row = lax.broadcasted_iota(jnp.int32, (8, tm), 0)
    big = jnp.float32(-3e38)
    gl = jnp.where(row < N_GROUPS, L[0:8], big)
    gmax = jnp.max(gl, axis=0, keepdims=True)
    g_idx = jnp.min(jnp.where(gl == gmax, row, 8), axis=0, keepdims=True)
    gsum = jnp.sum(jnp.where(row < N_GROUPS, jnp.exp(gl - gmax), 0.0), axis=0, keepdims=True)
    g_prob = 1.0 / gsum
    E = L[8 + 8 * (N_GROUPS - 1):8 + 8 * N_GROUPS]
    for g in range(N_GROUPS - 2, -1, -1):
        E = jnp.where(g_idx == g, L[8 + 8 * g:16 + 8 * g], E)
    t0 = jnp.max(E, axis=0, keepdims=True)
    loc0 = jnp.min(jnp.where(E == t0, row, 8), axis=0, keepdims=True)
    E2 = jnp.where(row == loc0, big, E)
    t1 = jnp.max(E2, axis=0, keepdims=True)
    loc1 = jnp.min(jnp.where(E2 == t1, row, 8), axis=0, keepdims=True)
    ex = jnp.exp(t1 - t0)
    w0 = g_prob / (1.0 + ex)
    w1 = g_prob * ex / (1.0 + ex)
    e0i = g_idx * EXPERTS_PER_GROUP + loc0
    e1i = g_idx * EXPERTS_PER_GROUP + loc1

    erow = lax.broadcasted_iota(jnp.int32, (N_EXPERTS, tm), 0)
    oh0 = jnp.where(erow == e0i, 1.0, 0.0)
    oh1 = jnp.where(erow == e1i, 1.0, 0.0)
    both = oh0 + oh1

    @pl.when(pl.program_id(0) == 0)
    def _():
        cnt_ref[...] = jnp.zeros_like(cnt_ref)

    prefix = jnp.dot(both.astype(jnp.bfloat16), tri_ref[...], preferred_element_type=jnp.float32)
    prefix = prefix + cnt_ref[:, 0:1]
    rank0 = jnp.sum(oh0 * prefix, axis=0, keepdims=True)
    rank1 = jnp.sum(oh1 * prefix, axis=0, keepdims=True)
    cnt_ref[...] = cnt_ref[...] + jnp.sum(both, axis=1, keepdims=True)

    vals = (e0i.astype(jnp.float32), e1i.astype(jnp.float32), w0, w1, rank0, rank1)
    out = jnp.zeros((8, tm), jnp.float32)
    for k, v in enumerate(vals):
        out = jnp.where(row == k, v, out)
    r_ref[...] = out


def _merge(x2, ya, yb, wg, bg, wa, wb, wo, g1, b1, wr, br, part, n_parts, tm=512):
    D = x2.shape[1]
    T = x2.shape[0] // n_parts
    off = part * (T // tm)
    const = lambda i: (0, 0)
    tri = jnp.triu(jnp.ones((tm, tm), jnp.bfloat16), k=1)
    return pl.pallas_call(
        _merge_kernel,
        grid=(T // tm,),
        in_specs=[
            pl.BlockSpec((tm, D), lambda i: (i + off, 0)),
            pl.BlockSpec((tm, SWA_Q_W), lambda i: (i + off, 0)),
            pl.BlockSpec((tm, MOBA_W), lambda i: (i + off, 0)),
            pl.BlockSpec((D, 2 * D), const),
            pl.BlockSpec((1, 2 * D), const),
            pl.BlockSpec((SWA_Q_W, D), const),
            pl.BlockSpec((MOBA_W, D), const),
            pl.BlockSpec((D, D), const),
            pl.BlockSpec((1, D), const),
            pl.BlockSpec((1, D), const),
            pl.BlockSpec((2 * ROUTER_ROWS, D), const),
            pl.BlockSpec((ROUTER_ROWS, 1), const),
            pl.BlockSpec((tm, tm), const),
        ],
        out_specs=[
            pl.BlockSpec((tm, D), lambda i: (i, 0)),
            pl.BlockSpec((tm, D // 2), lambda i: (i, 0)),
            pl.BlockSpec((8, tm), lambda i: (0, i)),
            pl.BlockSpec((N_EXPERTS, LANES), const),
        ],
        out_shape=[
            jax.ShapeDtypeStruct((T, D), jnp.float32),
            jax.ShapeDtypeStruct((T, D // 2), jnp.int32),
            jax.ShapeDtypeStruct((8, T), jnp.float32),
            jax.ShapeDtypeStruct((N_EXPERTS, LANES), jnp.float32),
        ],
        compiler_params=pltpu.CompilerParams(
            dimension_semantics=("arbitrary",), vmem_limit_bytes=VMEM_LIMIT),
        name="merge",
    )(x2, ya, yb, wg, bg, wa, wb, wo, g1, b1, wr, br, tri)


def _dest_kernel(r_ref, ps_ref, d_ref):
    tt = r_ref.shape[1]
    erow = lax.broadcasted_iota(jnp.int32, (N_EXPERTS, tt), 0)
    row = lax.broadcasted_iota(jnp.int32, (8, tt), 0)
    ps = ps_ref[...]
    out = jnp.zeros((8, tt), jnp.float32)
    for k in range(EXPERT_TOPK):
        e = r_ref[k:k + 1, :].astype(jnp.int32)
        start = jnp.sum(jnp.where(erow == e, ps, 0.0), axis=0, keepdims=True)
        out = jnp.where(row == k, start + r_ref[4 + k:5 + k, :], out)
    d_ref[...] = out.astype(jnp.int32)


def _dest(rinfo, padded_start, tt=8192):
    T = rinfo.shape[1]
    tt = min(tt, T)
    return pl.pallas_call(
        _dest_kernel,
        grid=(T // tt,),
        in_specs=[pl.BlockSpec((8, tt), lambda i: (0, i)), pl.BlockSpec((N_EXPERTS, 1), lambda i: (0, 0))],
        out_specs=pl.BlockSpec((8, tt), lambda i: (0, i)),
        out_shape=jax.ShapeDtypeStruct((8, T), jnp.int32),
        compiler_params=pltpu.CompilerParams(dimension_semantics=("parallel",)),
        name="dest",
    )(rinfo, padded_start.astype(jnp.float32)[:, None])


def _sc_workers():
    info = plsc.get_sparse_core_info()
    return info.num_cores, info.num_subcores


def _sc_scatter_rows(src, dest0, dest1, cap):
    T, W = src.shape
    nc, ns = _sc_workers()
    per_w = T // (nc * ns)
    assert per_w * nc * ns == T and per_w % SC_CHUNK == 0
    mesh = plsc.VectorSubcoreMesh(core_axis_name="c", subcore_axis_name="s")

    @functools.partial(
        pl.kernel, mesh=mesh,
        out_type=jax.ShapeDtypeStruct((cap, W), src.dtype),
        scratch_types=[pltpu.VMEM((SC_CHUNK,), jnp.int32), pltpu.VMEM((SC_CHUNK, W), src.dtype)],
    )
    def scatter(src_hbm, d0_hbm, d1_hbm, out_hbm, idx_v, rows_v):
        wid = lax.axis_index("s") * nc + lax.axis_index("c")
        base = wid * per_w

        @pl.loop(0, per_w // SC_CHUNK)
        def _(c):
            off = pl.multiple_of(base + c * SC_CHUNK, SC_CHUNK)
            pltpu.sync_copy(src_hbm.at[pl.ds(off, SC_CHUNK)], rows_v)
            for d_hbm in (d0_hbm, d1_hbm):
                pltpu.sync_copy(d_hbm.at[pl.ds(off, SC_CHUNK)], idx_v)
                pltpu.sync_copy(rows_v, out_hbm.at[idx_v])

    return scatter(src, dest0, dest1)


def _sc_gather_rows(table, idx):
    N = idx.shape[0]
    W = table.shape[1]
    nc, ns = _sc_workers()
    per_w = N // (nc * ns)
    assert per_w * nc * ns == N and per_w % SC_CHUNK == 0
    mesh = plsc.VectorSubcoreMesh(core_axis_name="c", subcore_axis_name="s")

    @functools.partial(
        pl.kernel, mesh=mesh,
        out_type=jax.ShapeDtypeStruct((N, W), table.dtype),
        scratch_types=[pltpu.VMEM((SC_CHUNK,), jnp.int32), pltpu.VMEM((SC_CHUNK, W), table.dtype)],
    )
    def gather(table_hbm, idx_hbm, out_hbm, idx_v, rows_v):
        wid = lax.axis_index("s") * nc + lax.axis_index("c")
        base = wid * per_w

        @pl.loop(0, per_w // SC_CHUNK)
        def _(c):
            off = pl.multiple_of(base + c * SC_CHUNK, SC_CHUNK)
            pltpu.sync_copy(idx_hbm.at[pl.ds(off, SC_CHUNK)], idx_v)
            pltpu.sync_copy(table_hbm.at[idx_v], rows_v)
            pltpu.sync_copy(rows_v, out_hbm.at[pl.ds(off, SC_CHUNK)])

    return gather(table, idx)


def _expert_kernel(be_ref, nv_ref, x_ref, wg_ref, wu_ref, wd_ref, y_ref, wgu_scr, wd_scr):
    i = pl.program_id(0)
    nv = nv_ref[i]

    @pl.when((i == 0) | (be_ref[i] != be_ref[jnp.maximum(i - 1, 0)]))
    def _():
        wgu_scr[:, :D_EXPERT] = wg_ref[0].astype(jnp.bfloat16)
        wgu_scr[:, D_EXPERT:] = wu_ref[0].astype(jnp.bfloat16)
        wd_scr[...] = wd_ref[0].astype(jnp.bfloat16)

    @pl.when(nv > 0)
    def _():
        lo, hi = _unpack_bf16_pair(x_ref[...])
        xb = jnp.concatenate([lo, hi], axis=1)
        rows = lax.broadcasted_iota(jnp.int32, xb.shape, 0)
        xb = jnp.where(rows < nv, xb, jnp.zeros_like(xb))
        gu = jnp.dot(xb, wgu_scr[...], preferred_element_type=jnp.float32)
        g = gu[:, :D_EXPERT]
        u = gu[:, D_EXPERT:]
        act = (g / (1.0 + jnp.exp(-g))) * u
        y = jnp.dot(act.astype(jnp.bfloat16), wd_scr[...], preferred_element_type=jnp.float32)
        y_ref[...] = _pack_bf16_pair(y[:, :D_MODEL // 2], y[:, D_MODEL // 2:])

    @pl.when(nv <= 0)
    def _():
        y_ref[...] = jnp.zeros_like(y_ref)


def _experts(blk_expert, blk_valid, xs, w_gate, w_up, w_down):
    cap, DW = xs.shape
    D = 2 * DW
    n_blocks = cap // MOE_TM
    return pl.pallas_call(
        _expert_kernel,
        grid_spec=pltpu.PrefetchScalarGridSpec(
            num_scalar_prefetch=2,
            grid=(n_blocks,),
            in_specs=[
                pl.BlockSpec((MOE_TM, DW), lambda i, be, nv: (i, 0)),
                pl.BlockSpec((1, D, D_EXPERT), lambda i, be, nv: (be[i], 0, 0)),
                pl.BlockSpec((1, D, D_EXPERT), lambda i, be, nv: (be[i], 0, 0)),
                pl.BlockSpec((1, D_EXPERT, D), lambda i, be, nv: (be[i], 0, 0)),
            ],
            out_specs=pl.BlockSpec((MOE_TM, DW), lambda i, be, nv: (i, 0)),
            scratch_shapes=[pltpu.VMEM((D, 2 * D_EXPERT), jnp.bfloat16), pltpu.VMEM((D_EXPERT, D), jnp.bfloat16)],
        ),
        out_shape=jax.ShapeDtypeStruct((cap, DW), jnp.int32),
        compiler_params=pltpu.CompilerParams(
            dimension_semantics=("arbitrary",), vmem_limit_bytes=VMEM_LIMIT),
        name="experts",
    )(blk_expert, blk_valid, xs, w_gate, w_up, w_down)


def _final_kernel(x1_ref, y0_ref, y1_ref, w_ref, g2_ref, b2_ref, *rest):
    o_ref = rest[-1]
    w = w_ref[...].T
    halves = []
    for part in range(2):
        y0 = _unpack_bf16_pair(y0_ref[...])[part].astype(jnp.float32)
        y1 = _unpack_bf16_pair(y1_ref[...])[part].astype(jnp.float32)
        halves.append(y0 * w[:, 2:3] + y1 * w[:, 3:4])
    moe = jnp.concatenate(halves, axis=1)
    o_ref[...] = _layer_norm(DEEPNORM_ALPHA * x1_ref[...] + moe, g2_ref[...], b2_ref[...])


def _final(x1, ypair, rinfo, g2, b2, out_prev, part, n_parts, tm=1024):
    T, D = x1.shape
    const = lambda i: (0, 0)
    nt = T // tm
    off = part * nt
    in_specs = [
        pl.BlockSpec((tm, D), lambda i: (i, 0)),
        pl.BlockSpec((tm, D // 2), lambda i: (i, 0)),
        pl.BlockSpec((tm, D // 2), lambda i: (i + nt, 0)),
        pl.BlockSpec((8, tm), lambda i: (0, i)),
        pl.BlockSpec((1, D), const),
        pl.BlockSpec((1, D), const),
    ]
    args = [x1, ypair, ypair, rinfo, g2, b2]
    aliases = {}
    if out_prev is not None:
        in_specs.append(pl.BlockSpec(memory_space=pl.ANY))
        args.append(out_prev)
        aliases = {len(args) - 1: 0}
    return pl.pallas_call(
        _final_kernel,
        grid=(nt,),
        in_specs=in_specs,
        out_specs=pl.BlockSpec((tm, D), lambda i: (i + off, 0)),
        out_shape=jax.ShapeDtypeStruct((n_parts * T, D), jnp.float32),
        input_output_aliases=aliases,
        compiler_params=pltpu.CompilerParams(
            dimension_semantics=("parallel",), vmem_limit_bytes=VMEM_LIMIT),
        name="final",
    )(*args)


def _bias_kernel(tab_ref, sub_ref, idx_ref, o_ref):
    h = pl.program_id(0)
    idx = idx_ref[...]
    acc = jnp.full(idx.shape, NEG_INF, jnp.float32)
    for b in range(REL_BUCKETS):
        acc = jnp.where(idx == b, tab_ref[h, b] - sub_ref[h], acc)
    o_ref[0] = acc


def _bias_expand(table_hb, sub_h, idx):
    H = table_hb.shape[0]
    K, Q = idx.shape
    return pl.pallas_call(
        _bias_kernel,
        grid_spec=pltpu.PrefetchScalarGridSpec(
            num_scalar_prefetch=2,
            grid=(H,),
            in_specs=[pl.BlockSpec((K, Q), lambda h, t, s: (0, 0))],
            out_specs=pl.BlockSpec((1, K, Q), lambda h, t, s: (h, 0, 0)),
        ),
        out_shape=jax.ShapeDtypeStruct((H, K, Q), jnp.float32),
        compiler_params=pltpu.CompilerParams(dimension_semantics=("parallel",)),
        name="bias",
    )(table_hb, sub_h, jnp.asarray(idx))


def _attention_bias_tables(rel_bias_table):
    rel = rel_bias_table.astype(jnp.float32)
    tab_a = rel[:, :SWA_Q_HEADS].T
    tab_b = rel[:, SWA_Q_HEADS:].T
    kj = np.arange(2 * SWA_BLOCK)[:, None]
    qi = np.arange(SWA_BLOCK)[None, :]
    dist = SWA_BLOCK + qi - kj
    idx_a = np.where((dist >= 0) & (dist < SWA_WINDOW), _rel_bucket_np(dist), -1).astype(np.int32)
    bias_a = _bias_expand(tab_a, jnp.zeros((SWA_Q_HEADS,), jnp.float32), idx_a)
    bias_a = bias_a.reshape(SWA_Q_HEADS // 2, 2, 2 * SWA_BLOCK, SWA_BLOCK).transpose(0, 2, 1, 3)
    bias_a = bias_a.reshape(SWA_Q_HEADS // 2, 2 * SWA_BLOCK, 2 * SWA_BLOCK)
    j = np.arange(MOBA_BLOCK)[:, None]
    i = np.arange(MOBA_BLOCK)[None, :]
    d_own = i - j
    idx_own = np.where(d_own >= 0, _rel_bucket_np(d_own), -1)
    idx_prev = _rel_bucket_np(MOBA_BLOCK + i - j)
    idx_b = np.concatenate([idx_prev, idx_own], axis=0).astype(np.int32)
    bias_b = _bias_expand(tab_b, tab_b[:, REL_BUCKETS - 1], idx_b)
    return bias_a, bias_b


def _block_plan(sizes, n_tok):
    n_assign = n_tok * EXPERT_TOPK
    padded = ((sizes + MOE_TM - 1) // MOE_TM) * MOE_TM
    eid = jnp.arange(N_EXPERTS, dtype=jnp.int32)
    padded_end = jnp.sum(jnp.where(eid[None, :] <= eid[:, None], padded[None, :], 0), axis=1)
    padded_start = padded_end - padded
    cap = -(-n_assign // MOE_TM) * MOE_TM + N_EXPERTS * MOE_TM
    blk_start = jnp.arange(cap // MOE_TM, dtype=jnp.int32) * MOE_TM
    blk_expert = jnp.minimum(
        jnp.sum(padded_end[None, :] <= blk_start[:, None], axis=1), N_EXPERTS - 1).astype(jnp.int32)
    mine = blk_expert[:, None] == eid[None, :]
    size_b = jnp.sum(jnp.where(mine, sizes[None, :], 0), axis=1)
    start_b = jnp.sum(jnp.where(mine, padded_start[None, :], 0), axis=1)
    blk_valid = jnp.clip(size_b - (blk_start - start_b), 0, MOE_TM)
    blk_valid = jnp.where(blk_start < padded_end[-1], blk_valid, 0).astype(jnp.int32)
    return padded_start, blk_expert, blk_valid, cap


def kernel(x, w_in, b_in, attn_sinks, rel_bias_table, w_branch_swa, w_branch_moba, w_out, ln1_gain, ln1_bias,
           w_group_router, b_group_router, w_expert_router, b_expert_router, w_expert_gate, w_expert_up,
           w_expert_down, ln2_gain, ln2_bias):
    assert w_in.shape[0] == DEPTH == 1
    B, S, D = x.shape
    T = B * S
    bf16 = jnp.bfloat16
    f32 = jnp.float32
    w = w_in[0]
    b = b_in[0]

    def cols(off, width):
        return w[:, off:off + width], b[off:off + width]

    wq_a, bq_a = cols(OFF_SWA_Q, SWA_Q_W)
    wk_a, bk_a = cols(OFF_SWA_K, SWA_KV_W)
    wv_a, bv_a = cols(OFF_SWA_V, SWA_KV_W)
    wq_b, bq_b = cols(OFF_MOBA_Q, MOBA_W)
    wk_b, bk_b = cols(OFF_MOBA_K, MOBA_W)
    wv_b, bv_b = cols(OFF_MOBA_V, MOBA_W)

    def dup_kv(t):
        parts = [t[..., i * HEAD_DIM:(i + 1) * HEAD_DIM] for i in range(SWA_KV_HEADS)]
        return jnp.concatenate([p for p in parts for _ in range(2)], axis=-1)

    wn = jnp.concatenate([wq_a * ATTN_SCALE, dup_kv(wk_a), wq_b * ATTN_SCALE, wk_b], axis=1).astype(bf16)
    bn = jnp.concatenate([bq_a * ATTN_SCALE, dup_kv(bk_a), bq_b * ATTN_SCALE, bk_b])[None, :].astype(f32)
    wt = jnp.concatenate([wv_a, wv_b], axis=1).T.astype(bf16)
    bt = jnp.concatenate([bv_a, bv_b])[:, None].astype(f32)

    qk, vt = _inproj(x, wn, bn, wt, bt)

    bias_a, bias_b = _attention_bias_tables(rel_bias_table)
    y_a = _swa(attn_sinks[0].astype(f32), qk, vt, bias_a)
    y_b = _moba(qk, vt, bias_b)

    wg, bg = cols(OFF_GATE, 2 * D_MODEL)
    wr = jnp.zeros((ROUTER_ROWS, D), f32)
    wr = wr.at[0:N_GROUPS].set(w_group_router[0].T).at[8:8 + N_EXPERTS].set(w_expert_router[0].T)
    wr_hi = wr.astype(bf16)
    wr_lo = (wr - wr_hi.astype(f32)).astype(bf16)
    br = jnp.zeros((ROUTER_ROWS,), f32)
    br = br.at[0:N_GROUPS].set(b_group_router[0]).at[8:8 + N_EXPERTS].set(b_expert_router[0])[:, None]
    merge_args = (
        x.reshape(T, D), y_a.reshape(T, SWA_Q_W), y_b.reshape(T, MOBA_W),
        wg.astype(bf16), bg[None, :].astype(f32), w_branch_swa[0].astype(bf16), w_branch_moba[0].astype(bf16),
        w_out[0].astype(bf16), ln1_gain[0][None, :].astype(f32), ln1_bias[0][None, :].astype(f32),
        jnp.concatenate([wr_hi, wr_lo], axis=0), br)
    g2 = ln2_gain[0][None, :].astype(f32)
    b2 = ln2_bias[0][None, :].astype(f32)

    Tp = T // MOE_PARTS
    out = None
    for part in range(MOE_PARTS):
        x1, x1p, rinfo, counts = _merge(*merge_args, part, MOE_PARTS)
        sizes = counts[:, 0].astype(jnp.int32)
        padded_start, blk_expert, blk_valid, cap = _block_plan(sizes, Tp)
        dest = _dest(rinfo, padded_start)
        xs = _sc_scatter_rows(x1p, dest[0], dest[1], cap)
        y_buf = _experts(blk_expert, blk_valid, xs, w_expert_gate[0], w_expert_up[0], w_expert_down[0])
        ypair = _sc_gather_rows(y_buf, dest[0:EXPERT_TOPK].reshape(-1))
        out = _final(x1, ypair, rinfo, g2, b2, out, part, MOE_PARTS)
    return out.reshape(B, S, D)
```

```python
import functools
import math

import numpy as np
import jax
import jax.numpy as jnp
from jax import lax
from jax.experimental import pallas as pl
from jax.experimental.pallas import tpu as pltpu
from jax.experimental.pallas import tpu_sc as plsc

D_MODEL = 1024
HEAD_DIM = 64
SWA_Q_HEADS = 8
SWA_KV_HEADS = 2
SWA_GROUP = SWA_Q_HEADS // SWA_KV_HEADS
SWA_WINDOW = 128
SWA_BLOCK = 128
MOBA_HEADS = 8
MOBA_BLOCK = 256
MOBA_TOPK = 3
MOBA_VROWS = HEAD_DIM + 16
REL_BUCKETS = 32
REL_MAX_DIST = 128
N_GROUPS = 4
EXPERTS_PER_GROUP = 8
N_EXPERTS = N_GROUPS * EXPERTS_PER_GROUP
EXPERT_TOPK = 2
D_EXPERT = 512
LN_EPS = 1e-5
DEPTH = 1
DEEPNORM_ALPHA = (2.0 * DEPTH) ** 0.25
NEG_INF = -1e30
ATTN_SCALE = HEAD_DIM ** -0.5
LOG2E = math.log2(math.e)

SWA_Q_W = SWA_Q_HEADS * HEAD_DIM
SWA_KV_W = SWA_KV_HEADS * HEAD_DIM
MOBA_W = MOBA_HEADS * HEAD_DIM
OFF_SWA_Q = 0
OFF_SWA_K = OFF_SWA_Q + SWA_Q_W
OFF_SWA_V = OFF_SWA_K + SWA_KV_W
OFF_MOBA_Q = OFF_SWA_V + SWA_KV_W
OFF_MOBA_K = OFF_MOBA_Q + MOBA_W
OFF_MOBA_V = OFF_MOBA_K + MOBA_W
OFF_GATE = OFF_MOBA_V + MOBA_W

LANES = 128
QK_SWA_Q = 0
QK_SWA_K = QK_SWA_Q + SWA_Q_W
QK_MOBA_Q = QK_SWA_K + SWA_KV_HEADS * LANES
QK_MOBA_K = QK_MOBA_Q + MOBA_W
QK_W = QK_MOBA_K + MOBA_W
VT_SWA = 0
VT_MOBA = VT_SWA + SWA_KV_W
VT_W = VT_MOBA + MOBA_W

MOE_TM = 512
MOE_PARTS = 2
SC_CHUNK = 128
ROUTER_ROWS = 8 + N_EXPERTS
VMEM_LIMIT = 56 * 1024 * 1024

_NT = (((1,), (1,)), ((), ()))


def _rel_bucket_np(dist):
    n = np.maximum(dist, 0)
    max_exact = REL_BUCKETS // 2
    nf = np.maximum(n, 1).astype(np.float32)
    large = max_exact + (np.log(nf / np.float32(max_exact)) / np.float32(math.log(REL_MAX_DIST / max_exact))
                         * np.float32(REL_BUCKETS - max_exact)).astype(np.int32)
    large = np.minimum(large, REL_BUCKETS - 1)
    return np.where(n < max_exact, n, large).astype(np.int32)


def _inproj_kernel(x_ref, wn_ref, bn_ref, wt_ref, bt_ref, qk_ref, vt_ref):
    xb = x_ref[0].astype(jnp.bfloat16)
    qk = jnp.dot(xb, wn_ref[...], preferred_element_type=jnp.float32) + bn_ref[...]
    qk_ref[0] = qk.astype(jnp.bfloat16)
    vt = lax.dot_general(wt_ref[...], xb, _NT, preferred_element_type=jnp.float32) + bt_ref[...]
    vt_ref[0] = vt.astype(jnp.bfloat16)


def _inproj(x, wn, bn, wt, bt, tm=512):
    B, S, D = x.shape
    return pl.pallas_call(
        _inproj_kernel,
        grid=(B, S // tm),
        in_specs=[
            pl.BlockSpec((1, tm, D), lambda b, i: (b, i, 0)),
            pl.BlockSpec((D, QK_W), lambda b, i: (0, 0)),
            pl.BlockSpec((1, QK_W), lambda b, i: (0, 0)),
            pl.BlockSpec((VT_W, D), lambda b, i: (0, 0)),
            pl.BlockSpec((VT_W, 1), lambda b, i: (0, 0)),
        ],
        out_specs=[
            pl.BlockSpec((1, tm, QK_W), lambda b, i: (b, i, 0)),
            pl.BlockSpec((1, VT_W, tm), lambda b, i: (b, 0, i)),
        ],
        out_shape=[
            jax.ShapeDtypeStruct((B, S, QK_W), jnp.bfloat16),
            jax.ShapeDtypeStruct((B, VT_W, S), jnp.bfloat16),
        ],
        compiler_params=pltpu.CompilerParams(
            dimension_semantics=("parallel", "parallel"), vmem_limit_bytes=VMEM_LIMIT),
        name="inproj",
    )(x, wn, bn, wt, bt)


def _swa_kernel(sink_ref, q_ref, k_ref, vt_ref, bias_ref, o_ref, vaug_ref):
    S = q_ref.shape[1]
    nblk = S // SWA_BLOCK
    half = HEAD_DIM
    vt = vt_ref[0]
    rows = lax.broadcasted_iota(jnp.int32, vt.shape, 0)
    one = jnp.ones_like(vt)
    vaug_ref[0] = jnp.where(rows < half, vt, one)
    vaug_ref[1] = jnp.where(rows < half, one, vt)
    lane = lax.broadcasted_iota(jnp.int32, (SWA_BLOCK, LANES), 1)
    col = lax.broadcasted_iota(jnp.int32, (1, 2 * SWA_BLOCK), 1)
    npair = SWA_Q_HEADS // 2

    def blocks(specs):
        items = [(qs, ks, nk, boff, pair, (2 * pair) // SWA_GROUP) for qs, ks, nk, boff in specs for pair in range(npair)]
        ss = []
        for qs, ks, nk, boff, pair, kv in items:
            qblk = q_ref[0, pl.ds(qs, SWA_BLOCK), pair * LANES:(pair + 1) * LANES]
            zero = jnp.zeros_like(qblk)
            q2 = jnp.concatenate([jnp.where(lane < half, qblk, zero), jnp.where(lane >= half, qblk, zero)], axis=0)
            kblk = k_ref[0, pl.ds(ks, nk), kv * LANES:(kv + 1) * LANES]
            s = lax.dot_general(kblk, q2, _NT, preferred_element_type=jnp.float32)
            ss.append(s + bias_ref[pair, boff:boff + nk, :])
        stats = []
        for (qs, ks, nk, boff, pair, kv), s in zip(items, ss):
            sink = jnp.where(col < SWA_BLOCK, sink_ref[2 * pair], sink_ref[2 * pair + 1]) * LOG2E
            m = jnp.maximum(_colmax(s), sink)
            stats.append((jnp.exp2(s - m).astype(jnp.bfloat16), jnp.exp2(sink - m)))
        rs = [jnp.dot(vaug_ref[kv, :, pl.ds(ks, nk)], p, preferred_element_type=jnp.float32)
              for (qs, ks, nk, boff, pair, kv), (p, _) in zip(items, stats)]
        outs = []
        for (qs, ks, nk, boff, pair, kv), (_, esink), r in zip(items, stats, rs):
            l = (r[half:half + 1, :] if kv == 0 else r[0:1, :]) + esink
            o = (r[0:half, :] if kv == 0 else r[half:, :]) / l
            outs.append(jnp.concatenate([o[:, :SWA_BLOCK], o[:, SWA_BLOCK:]], axis=0))
        for i, (qs, _, _, _) in enumerate(specs):
            ot = jnp.concatenate(outs[i * npair:(i + 1) * npair], axis=0)
            o_ref[0, pl.ds(qs, SWA_BLOCK), :] = ot.T.astype(jnp.bfloat16)

    blocks([(0, 0, SWA_BLOCK, SWA_BLOCK), (SWA_BLOCK, 0, 2 * SWA_BLOCK, 0)])

    def body(t, carry):
        specs = []
        for n in (2 * t, 2 * t + 1):
            qs = pl.multiple_of(n * SWA_BLOCK, SWA_BLOCK)
            specs.append((qs, pl.multiple_of(qs - SWA_BLOCK, SWA_BLOCK), 2 * SWA_BLOCK, 0))
        blocks(specs)
        return carry

    assert nblk % 2 == 0
    lax.fori_loop(1, nblk // 2, body, 0)


def _swa(sinks, qk, vt, bias):
    B, S, _ = qk.shape
    nq = SWA_Q_W // LANES
    return pl.pallas_call(
        _swa_kernel,
        grid_spec=pltpu.PrefetchScalarGridSpec(
            num_scalar_prefetch=1,
            grid=(B,),
            in_specs=[
                pl.BlockSpec((1, S, SWA_Q_W), lambda b, s: (b, 0, QK_SWA_Q // SWA_Q_W)),
                pl.BlockSpec((1, S, 2 * LANES), lambda b, s: (b, 0, QK_SWA_K // (2 * LANES))),
                pl.BlockSpec((1, SWA_KV_W, S), lambda b, s: (b, VT_SWA // SWA_KV_W, 0)),
                pl.BlockSpec((SWA_Q_HEADS // 2, 2 * SWA_BLOCK, 2 * SWA_BLOCK), lambda b, s: (0, 0, 0)),
            ],
            out_specs=pl.BlockSpec((1, S, SWA_Q_W), lambda b, s: (b, 0, 0)),
            scratch_shapes=[pltpu.VMEM((SWA_KV_HEADS, SWA_KV_W, S), jnp.bfloat16)],
        ),
        out_shape=jax.ShapeDtypeStruct((B, S, SWA_Q_W), jnp.bfloat16),
        compiler_params=pltpu.CompilerParams(
            dimension_semantics=("parallel",), vmem_limit_bytes=VMEM_LIMIT),
        name="swa",
    )(sinks, qk, qk, vt, bias)


def _colmax(s):
    while s.shape[0] > 8:
        h = s.shape[0] // 2
        s = jnp.maximum(s[:h], s[h:])
    return jnp.max(s, axis=0, keepdims=True)


def _moba_kernel(fqb_ref, fc_ref, q_ref, k_ref, vt_ref, bias_ref, o_ref,
                 vaug_ref, gate_ref, rank_ref, sel_ref, far_ref, m_scr, acc_scr, sa_scr, sb_scr, sc_scr, sd_scr):
    S = q_ref.shape[1]
    nb = S // MOBA_BLOCK
    half = HEAD_DIM
    BLK = MOBA_BLOCK

    ones_rows = jnp.ones((MOBA_VROWS - half, S), jnp.bfloat16)
    for h in range(2):
        vaug_ref[h, 0:half, :] = vt_ref[0, h * half:(h + 1) * half, :]
        vaug_ref[h, half:, :] = ones_rows

    kf = k_ref[0].astype(jnp.float32).reshape(nb, BLK, LANES)
    kmean = jnp.sum(kf, axis=1) * (1.0 / BLK)
    k_hi = kmean.astype(jnp.bfloat16)
    k_lo = (kmean - k_hi.astype(jnp.float32)).astype(jnp.bfloat16)
    kcat = jnp.concatenate([k_hi, k_lo], axis=0)

    lane_q = lax.broadcasted_iota(jnp.int32, (S, LANES), 1)
    brow = lax.broadcasted_iota(jnp.int32, (nb, S), 0)
    qblk_of = lax.broadcasted_iota(jnp.int32, (nb, S), 1) // BLK
    qall = q_ref[0]
    for h in range(2):
        qh = jnp.where((lane_q < half) if h == 0 else (lane_q >= half), qall, jnp.zeros_like(qall))
        g2 = lax.dot_general(kcat, qh, _NT, preferred_element_type=jnp.float32)
        gate_ref[...] = g2[0:nb] + g2[nb:2 * nb]
        rank_ref[...] = jnp.zeros_like(rank_ref)
        for m in range(nb - 1):
            lo = (m + 1) * BLK
            G = gate_ref[:, lo:]
            gm = gate_ref[m:m + 1, lo:]
            ge = jnp.where(gm >= G, 1.0, 0.0)
            gt = jnp.where(gm > G, 1.0, 0.0)
            brow_m = lax.broadcasted_iota(jnp.int32, (nb, S - lo), 0)
            rank_ref[:, lo:] += jnp.where(brow_m > m, ge, gt)
        top = rank_ref[...] < float(MOBA_TOPK)
        sel_ref[h] = jnp.where((brow < qblk_of) & top, 0.0, NEG_INF)
        far_ref[h] = jnp.where((brow < qblk_of - 1) & top, 0.0, NEG_INF)

    lane = lax.broadcasted_iota(jnp.int32, (BLK, LANES), 1)

    def head_q(qs, h):
        qblk = q_ref[0, pl.ds(qs, BLK), :]
        return jnp.where((lane < half) if h == 0 else (lane >= half), qblk, jnp.zeros_like(qblk))

    m0 = jnp.full((1, BLK), NEG_INF, jnp.float32)
    acc0 = jnp.zeros((MOBA_VROWS, BLK), jnp.float32)

    def item(h, qb, qs, ks, nk, bias, sel_a, sel_b, m, acc):
        return dict(h=h, qb=qb, qs=qs, ks=ks, nk=nk, bias=bias, sel_a=sel_a, sel_b=sel_b, m=m, acc=acc)

    def near_item(qb, h):
        qs = pl.multiple_of(qb * BLK, BLK)
        ps = pl.multiple_of(qs - BLK, BLK)
        sel_prev = sel_ref[h, pl.ds(qb - 1, 1), pl.ds(qs, BLK)]
        return item(h, qb, qs, ps, 2 * BLK, bias_ref[h], sel_prev, None, m0, acc0)

    def far_item(i, h):
        qb = fqb_ref[i]
        c = fc_ref[i]
        qs = pl.multiple_of(qb * BLK, BLK)
        ks = pl.multiple_of(c * (2 * BLK), 2 * BLK)
        sel_a = far_ref[h, pl.ds(2 * c, 1), pl.ds(qs, BLK)]
        sel_b = far_ref[h, pl.ds(2 * c + 1, 1), pl.ds(qs, BLK)]
        return item(h, qb, qs, ks, 2 * BLK, None, sel_a, sel_b, m_scr[h, qb], acc_scr[h, qb])

    def scores(it):
        kslab = k_ref[0, pl.ds(it["ks"], it["nk"]), :]
        s = lax.dot_general(kslab, head_q(it["qs"], it["h"]), _NT,
                            preferred_element_type=jnp.float32)
        return s if it["bias"] is None else s + it["bias"]

    def probs(it, s):
        m, sel_a, sel_b = it["m"], it["sel_a"], it["sel_b"]
        if it["nk"] == BLK:
            m_new = jnp.maximum(m, _colmax(s))
            p = jnp.exp2(s - m_new)
        else:
            cm_a = _colmax(s[:BLK]) + sel_a
            cm_b = _colmax(s[BLK:])
            if sel_b is not None:
                cm_b = cm_b + sel_b
            m_new = jnp.maximum(m, jnp.maximum(cm_a, cm_b))
            p_a = jnp.exp2(s[:BLK] - (m_new - sel_a))
            p_b = jnp.exp2(s[BLK:] - (m_new if sel_b is None else m_new - sel_b))
            p = jnp.concatenate([p_a, p_b], axis=0)
        return m_new, jnp.exp2(m - m_new), p.astype(jnp.bfloat16)

    def run(items):
        ss = [scores(it) for it in items]
        ps = [probs(it, s) for it, s in zip(items, ss)]
        pvs = [jnp.dot(vaug_ref[it["h"], :, pl.ds(it["ks"], it["nk"])], p, preferred_element_type=jnp.float32)
               for it, (_, _, p) in zip(items, ps)]
        for it, (m_new, alpha, _), pv in zip(items, ps, pvs):
            m_scr[it["h"], it["qb"]] = m_new
            acc_scr[it["h"], it["qb"]] = it["acc"] * alpha + pv

    run([item(h, 0, 0, 0, BLK, bias_ref[h, BLK:, :], None, None, m0, acc0) for h in range(2)]
        + [near_item(nb - 1, h) for h in range(2)])

    bufs = (sa_scr, sb_scr, sc_scr, sd_scr)
    nbuf = len(bufs)

    def finish(items, s_buf):
        ps = [probs(it, s_buf[k]) for k, it in enumerate(items)]
        pvs = [jnp.dot(vaug_ref[it["h"], :, pl.ds(it["ks"], it["nk"])], p, preferred_element_type=jnp.float32)
               for it, (_, _, p) in zip(items, ps)]
        for it, (m_new, alpha, _), pv in zip(items, ps, pvs):
            m_scr[it["h"], it["qb"]] = m_new
            acc_scr[it["h"], it["qb"]] = it["acc"] * alpha + pv

    def pipelined(n_groups, score, items_of):
        for g in range(min(2, n_groups)):
            score(g, bufs[g])

        def body(t, carry):
            for j in range(nbuf):
                score(nbuf * t + 2 + j, bufs[(2 + j) % nbuf])
                finish(items_of(nbuf * t + j), bufs[j])
            return carry

        n_trips = max(n_groups - 2, 0) // nbuf
        lax.fori_loop(0, n_trips, body, 0)
        for g in range(nbuf * n_trips, n_groups):
            if g + 2 < n_groups:
                score(g + 2, bufs[(g + 2) % nbuf])
            finish(items_of(g), bufs[g % nbuf])

    near_pairs = nb // 2 - 1

    def near_group(g):
        return [(qb, h) for qb in (g + 1, g + 1 + near_pairs) for h in range(2)]

    def near_score(g, s_buf):
        for k, (qb, h) in enumerate(near_group(g)):
            qs = pl.multiple_of(qb * BLK, BLK)
            ps = pl.multiple_of(qs - BLK, BLK)
            s_buf[k] = lax.dot_general(k_ref[0, pl.ds(ps, 2 * BLK), :], head_q(qs, h), _NT,
                                       preferred_element_type=jnp.float32) + bias_ref[h]

    pipelined(near_pairs, near_score, lambda g: [near_item(qb, h) for qb, h in near_group(g)])

    n_far_groups = fqb_ref.shape[0] // 2

    def far_group(g):
        return [(j, h) for j in (g, g + n_far_groups) for h in range(2)]

    def far_score(g, s_buf):
        for k, (j, h) in enumerate(far_group(g)):
            qs = pl.multiple_of(fqb_ref[j] * BLK, BLK)
            ks = pl.multiple_of(fc_ref[j] * (2 * BLK), 2 * BLK)
            s_buf[k] = lax.dot_general(k_ref[0, pl.ds(ks, 2 * BLK), :], head_q(qs, h), _NT,
                                       preferred_element_type=jnp.float32)

    pipelined(n_far_groups, far_score, lambda g: [far_item(j, h) for j, h in far_group(g)])

    def out_body(t, carry):
        for qb in (2 * t, 2 * t + 1):
            qs = pl.multiple_of(qb * BLK, BLK)
            a0 = acc_scr[0, qb]
            a1 = acc_scr[1, qb]
            ot = jnp.concatenate([a0[0:half] / a0[half:half + 1, :], a1[0:half] / a1[half:half + 1, :]],
                                 axis=0)
            o_ref[0, pl.ds(qs, BLK), :] = ot.T.astype(jnp.bfloat16)
        return carry

    lax.fori_loop(0, nb // 2, out_body, 0)


def _moba_far_items(nb):
    items = [(qb, c) for qb in range(2, nb) for c in range(qb // 2)]
    n = len(items)
    assert n % 2 == 0 and all(items[i][0] != items[i + n // 2][0] for i in range(n // 2))
    return np.array([it[0] for it in items], np.int32), np.array([it[1] for it in items], np.int32)


def _moba(qk, vt, bias):
    B, S, _ = qk.shape
    nb = S // MOBA_BLOCK
    npair = MOBA_W // LANES
    far_qb, far_c = _moba_far_items(nb)
    return pl.pallas_call(
        _moba_kernel,
        grid_spec=pltpu.PrefetchScalarGridSpec(
            num_scalar_prefetch=2,
            grid=(B, npair),
            in_specs=[
                pl.BlockSpec((1, S, LANES), lambda b, p, fq, fc: (b, 0, QK_MOBA_Q // LANES + p)),
                pl.BlockSpec((1, S, LANES), lambda b, p, fq, fc: (b, 0, QK_MOBA_K // LANES + p)),
                pl.BlockSpec((1, LANES, S), lambda b, p, fq, fc: (b, VT_MOBA // LANES + p, 0)),
                pl.BlockSpec((2, 2 * MOBA_BLOCK, MOBA_BLOCK), lambda b, p, fq, fc: (p, 0, 0)),
            ],
            out_specs=pl.BlockSpec((1, S, LANES), lambda b, p, fq, fc: (b, 0, p)),
            scratch_shapes=[
                pltpu.VMEM((2, MOBA_VROWS, S), jnp.bfloat16),
                pltpu.VMEM((nb, S), jnp.float32),
                pltpu.VMEM((nb, S), jnp.float32),
                pltpu.VMEM((2, nb, S), jnp.float32),
                pltpu.VMEM((2, nb, S), jnp.float32),
                pltpu.VMEM((2, nb, 1, MOBA_BLOCK), jnp.float32),
                pltpu.VMEM((2, nb, MOBA_VROWS, MOBA_BLOCK), jnp.float32),
            ] + [pltpu.VMEM((4, 2 * MOBA_BLOCK, MOBA_BLOCK), jnp.float32)] * 4,
        ),
        out_shape=jax.ShapeDtypeStruct((B, S, MOBA_W), jnp.bfloat16),
        compiler_params=pltpu.CompilerParams(
            dimension_semantics=("parallel", "parallel"), vmem_limit_bytes=VMEM_LIMIT),
        name="moba",
    )(jnp.asarray(far_qb), jnp.asarray(far_c), qk, qk, vt, bias)


def _layer_norm(h, gain, bias):
    mu = jnp.mean(h, axis=-1, keepdims=True)
    c = h - mu
    var = jnp.mean(c * c, axis=-1, keepdims=True)
    return c * lax.rsqrt(var + LN_EPS) * gain + bias


def _pack_bf16_pair(a, b):
    ia = lax.bitcast_convert_type(a.astype(jnp.bfloat16).astype(jnp.float32), jnp.int32)
    ib = lax.bitcast_convert_type(b.astype(jnp.bfloat16).astype(jnp.float32), jnp.int32)
    return lax.shift_right_logical(ia, 16) | ib


def _unpack_bf16_pair(w):
    lo = lax.bitcast_convert_type(lax.shift_left(w, 16), jnp.float32)
    hi = lax.bitcast_convert_type(w & jnp.int32(-65536), jnp.float32)
    return lo.astype(jnp.bfloat16), hi.astype(jnp.bfloat16)


def _merge_kernel(x_ref, ya_ref, yb_ref, wg_ref, bg_ref, wa_ref, wb_ref, wo_ref, g1_ref, b1_ref,
                  wr_ref, br_ref, tri_ref, x1_ref, x1p_ref, r_ref, cnt_ref):
    @pl.when(pl.program_id(0) == 0)
    def _():
        cnt_ref[...] = jnp.zeros_like(cnt_ref)

    ts = tri_ref.shape[0]
    subs = [pl.ds(r0, ts) for r0 in range(0, x_ref.shape[0], ts)]
    pre = [_merge_matmuls(x_ref[rows, :], ya_ref[rows, :], yb_ref[rows, :], wg_ref, bg_ref, wa_ref, wb_ref, wo_ref)
           for rows in subs]
    for rows, h in zip(subs, pre):
        _merge_route(h, rows, g1_ref, b1_ref, wr_ref, br_ref, tri_ref, x1_ref, x1p_ref, r_ref, cnt_ref)


def _merge_matmuls(x, ya, yb, wg_ref, bg_ref, wa_ref, wb_ref, wo_ref):
    xb = x.astype(jnp.bfloat16)
    z = jnp.dot(xb, wg_ref[...], preferred_element_type=jnp.float32) + bg_ref[...]
    gates = 1.0 / (1.0 + jnp.exp(-z))
    pa = jnp.dot(ya, wa_ref[...], preferred_element_type=jnp.float32)
    pb = jnp.dot(yb, wb_ref[...], preferred_element_type=jnp.float32)
    merged = gates[:, :D_MODEL] * pa + gates[:, D_MODEL:] * pb
    mixed = jnp.dot(merged.astype(jnp.bfloat16), wo_ref[...], preferred_element_type=jnp.float32)
    return DEEPNORM_ALPHA * x + mixed


def _merge_route(h, rows, g1_ref, b1_ref, wr_ref, br_ref, tri_ref, x1_ref, x1p_ref, r_ref, cnt_ref):
    x1 = _layer_norm(h, g1_ref[...], b1_ref[...])
    x1_ref[rows, :] = x1
    x1_hi = x1.astype(jnp.bfloat16)
    x1p_ref[rows, :] = _pack_bf16_pair(x1[:, :D_MODEL // 2], x1[:, D_MODEL // 2:])
    x1_lo = (x1 - x1_hi.astype(jnp.float32)).astype(jnp.bfloat16)

    R = ROUTER_ROWS
    l1 = lax.dot_general(wr_ref[...], x1_hi, _NT, preferred_element_type=jnp.float32)
    l2 = lax.dot_general(wr_ref[0:R, :], x1_lo, _NT, preferred_element_type=jnp.float32)
    L = l1[0:R] + l1[R:2 * R] + l2 + br_ref[...]
    tm = x1.shape[0]
    row = lax.broadcasted_iota(jnp.int32, (8, tm), 0)
    big = jnp.float32(-3e38)
    gl = jnp.where(row < N_GROUPS, L[0:8], big)
    gmax = jnp.max(gl, axis=0, keepdims=True)
    g_idx = jnp.min(jnp.where(gl == gmax, row, 8), axis=0, keepdims=True)
    gsum = jnp.sum(jnp.where(row < N_GROUPS, jnp.exp(gl - gmax), 0.0), axis=0, keepdims=True)
    g_prob = 1.0 / gsum
    E = L[8 + 8 * (N_GROUPS - 1):8 + 8 * N_GROUPS]
    for g in range(N_GROUPS - 2, -1, -1):
        E = jnp.where(g_idx == g, L[8 + 8 * g:16 + 8 * g], E)
    t0 = jnp.max(E, axis=0, keepdims=True)
    loc0 = jnp.min(jnp.where(E == t0, row, 8), axis=0, keepdims=True)
    E2 = jnp.where(row == loc0, big, E)
    t1 = jnp.max(E2, axis=0, keepdims=True)
    loc1 = jnp.min(jnp.where(E2 == t1, row, 8), axis=0, keepdims=True)
    ex = jnp.exp(t1 - t0)
    w0 = g_prob / (1.0 + ex)
    w1 = g_prob * ex / (1.0 + ex)
    e0i = g_idx * EXPERTS_PER_GROUP + loc0
    e1i = g_idx * EXPERTS_PER_GROUP + loc1

    erow = lax.broadcasted_iota(jnp.int32, (N_EXPERTS, tm), 0)
    oh0 = jnp.where(erow == e0i, 1.0, 0.0)
    oh1 = jnp.where(erow == e1i, 1.0, 0.0)
    both = oh0 + oh1
    prefix = jnp.dot(both.astype(jnp.bfloat16), tri_ref[...], preferred_element_type=jnp.float32)
    prefix = prefix + cnt_ref[:, 0:1]
    rank0 = jnp.sum(oh0 * prefix, axis=0, keepdims=True)
    rank1 = jnp.sum(oh1 * prefix, axis=0, keepdims=True)
    cnt_ref[...] = cnt_ref[...] + jnp.sum(both, axis=1, keepdims=True)

    vals = (e0i.astype(jnp.float32), e1i.astype(jnp.float32), w0, w1, rank0, rank1)
    out = jnp.zeros((8, tm), jnp.float32)
    for k, v in enumerate(vals):
        out = jnp.where(row == k, v, out)
    r_ref[:, rows] = out


def _merge(x2, ya, yb, wg, bg, wa, wb, wo, g1, b1, wr, br, part, n_parts, tm=512, ts=256):
    D = x2.shape[1]
    T = x2.shape[0] // n_parts
    off = part * (T // tm)
    const = lambda i: (0, 0)
    tri = jnp.triu(jnp.ones((ts, ts), jnp.bfloat16), k=1)
    return pl.pallas_call(
        _merge_kernel,
        grid=(T // tm,),
        in_specs=[
            pl.BlockSpec((tm, D), lambda i: (i + off, 0)),
            pl.BlockSpec((tm, SWA_Q_W), lambda i: (i + off, 0)),
            pl.BlockSpec((tm, MOBA_W), lambda i: (i + off, 0)),
            pl.BlockSpec((D, 2 * D), const),
            pl.BlockSpec((1, 2 * D), const),
            pl.BlockSpec((SWA_Q_W, D), const),
            pl.BlockSpec((MOBA_W, D), const),
            pl.BlockSpec((D, D), const),
            pl.BlockSpec((1, D), const),
            pl.BlockSpec((1, D), const),
            pl.BlockSpec((2 * ROUTER_ROWS, D), const),
            pl.BlockSpec((ROUTER_ROWS, 1), const),
            pl.BlockSpec((ts, ts), const),
        ],
        out_specs=[
            pl.BlockSpec((tm, D), lambda i: (i, 0)),
            pl.BlockSpec((tm, D // 2), lambda i: (i, 0)),
            pl.BlockSpec((8, tm), lambda i: (0, i)),
            pl.BlockSpec((N_EXPERTS, LANES), const),
        ],
        out_shape=[
            jax.ShapeDtypeStruct((T, D), jnp.float32),
            jax.ShapeDtypeStruct((T, D // 2), jnp.int32),
            jax.ShapeDtypeStruct((8, T), jnp.float32),
            jax.ShapeDtypeStruct((N_EXPERTS, LANES), jnp.float32),
        ],
        compiler_params=pltpu.CompilerParams(
            dimension_semantics=("arbitrary",), vmem_limit_bytes=VMEM_LIMIT),
        name="merge",
    )(x2, ya, yb, wg, bg, wa, wb, wo, g1, b1, wr, br, tri)


def _dest_kernel(r_ref, ps_ref, d_ref):
    tt = r_ref.shape[1]
    erow = lax.broadcasted_iota(jnp.int32, (N_EXPERTS, tt), 0)
    row = lax.broadcasted_iota(jnp.int32, (8, tt), 0)
    ps = ps_ref[...]
    out = jnp.zeros((8, tt), jnp.float32)
    for k in range(EXPERT_TOPK):
        e = r_ref[k:k + 1, :].astype(jnp.int32)
        start = jnp.sum(jnp.where(erow == e, ps, 0.0), axis=0, keepdims=True)
        out = jnp.where(row == k, start + r_ref[4 + k:5 + k, :], out)
    d_ref[...] = out.astype(jnp.int32)


def _dest(rinfo, padded_start, tt=8192):
    T = rinfo.shape[1]
    tt = min(tt, T)
    return pl.pallas_call(
        _dest_kernel,
        grid=(T // tt,),
        in_specs=[pl.BlockSpec((8, tt), lambda i: (0, i)), pl.BlockSpec((N_EXPERTS, 1), lambda i: (0, 0))],
        out_specs=pl.BlockSpec((8, tt), lambda i: (0, i)),
        out_shape=jax.ShapeDtypeStruct((8, T), jnp.int32),
        compiler_params=pltpu.CompilerParams(dimension_semantics=("parallel",)),
        name="dest",
    )(rinfo, padded_start.astype(jnp.float32)[:, None])


def _sc_workers():
    info = plsc.get_sparse_core_info()
    return info.num_cores, info.num_subcores


def _sc_scatter_rows(src, dest0, dest1, cap):
    T, W = src.shape
    nc, ns = _sc_workers()
    per_w = T // (nc * ns)
    assert per_w * nc * ns == T and per_w % SC_CHUNK == 0
    mesh = plsc.VectorSubcoreMesh(core_axis_name="c", subcore_axis_name="s")

    @functools.partial(
        pl.kernel, mesh=mesh,
        out_type=jax.ShapeDtypeStruct((cap, W), src.dtype),
        scratch_types=[pltpu.VMEM((SC_CHUNK,), jnp.int32), pltpu.VMEM((SC_CHUNK, W), src.dtype)],
    )
    def scatter(src_hbm, d0_hbm, d1_hbm, out_hbm, idx_v, rows_v):
        wid = lax.axis_index("s") * nc + lax.axis_index("c")
        base = wid * per_w

        @pl.loop(0, per_w // SC_CHUNK)
        def _(c):
            off = pl.multiple_of(base + c * SC_CHUNK, SC_CHUNK)
            pltpu.sync_copy(src_hbm.at[pl.ds(off, SC_CHUNK)], rows_v)
            for d_hbm in (d0_hbm, d1_hbm):
                pltpu.sync_copy(d_hbm.at[pl.ds(off, SC_CHUNK)], idx_v)
                pltpu.sync_copy(rows_v, out_hbm.at[idx_v])

    return scatter(src, dest0, dest1)


def _sc_gather_rows(table, idx):
    N = idx.shape[0]
    W = table.shape[1]
    nc, ns = _sc_workers()
    per_w = N // (nc * ns)
    assert per_w * nc * ns == N and per_w % SC_CHUNK == 0
    mesh = plsc.VectorSubcoreMesh(core_axis_name="c", subcore_axis_name="s")

    @functools.partial(
        pl.kernel, mesh=mesh,
        out_type=jax.ShapeDtypeStruct((N, W), table.dtype),
        scratch_types=[pltpu.VMEM((SC_CHUNK,), jnp.int32), pltpu.VMEM((SC_CHUNK, W), table.dtype)],
    )
    def gather(table_hbm, idx_hbm, out_hbm, idx_v, rows_v):
        wid = lax.axis_index("s") * nc + lax.axis_index("c")
        base = wid * per_w

        @pl.loop(0, per_w // SC_CHUNK)
        def _(c):
            off = pl.multiple_of(base + c * SC_CHUNK, SC_CHUNK)
            pltpu.sync_copy(idx_hbm.at[pl.ds(off, SC_CHUNK)], idx_v)
            pltpu.sync_copy(table_hbm.at[idx_v], rows_v)
            pltpu.sync_copy(rows_v, out_hbm.at[pl.ds(off, SC_CHUNK)])

    return gather(table, idx)


def _expert_kernel(be_ref, nv_ref, x_ref, wgu_ref, wd_ref, y_ref):
    i = pl.program_id(0)
    nv = nv_ref[i]

    @pl.when(nv > 0)
    def _():
        lo, hi = _unpack_bf16_pair(x_ref[...])
        xb = jnp.concatenate([lo, hi], axis=1)
        rows = lax.broadcasted_iota(jnp.int32, xb.shape, 0)
        xb = jnp.where(rows < nv, xb, jnp.zeros_like(xb))
        gu = jnp.dot(xb, wgu_ref[0], preferred_element_type=jnp.float32)
        g = gu[:, :D_EXPERT]
        u = gu[:, D_EXPERT:]
        act = (g / (1.0 + jnp.exp(-g))) * u
        y = jnp.dot(act.astype(jnp.bfloat16), wd_ref[0], preferred_element_type=jnp.float32)
        y_ref[...] = _pack_bf16_pair(y[:, :D_MODEL // 2], y[:, D_MODEL // 2:])

    @pl.when(nv <= 0)
    def _():
        y_ref[...] = jnp.zeros_like(y_ref)


def _experts(blk_expert, blk_valid, xs, wgu, wd):
    cap, DW = xs.shape
    D = 2 * DW
    n_blocks = cap // MOE_TM
    return pl.pallas_call(
        _expert_kernel,
        grid_spec=pltpu.PrefetchScalarGridSpec(
            num_scalar_prefetch=2,
            grid=(n_blocks,),
            in_specs=[
                pl.BlockSpec((MOE_TM, DW), lambda i, be, nv: (i, 0)),
                pl.BlockSpec((1, D, 2 * D_EXPERT), lambda i, be, nv: (be[i], 0, 0)),
                pl.BlockSpec((1, D_EXPERT, D), lambda i, be, nv: (be[i], 0, 0)),
            ],
            out_specs=pl.BlockSpec((MOE_TM, DW), lambda i, be, nv: (i, 0)),
        ),
        out_shape=jax.ShapeDtypeStruct((cap, DW), jnp.int32),
        compiler_params=pltpu.CompilerParams(
            dimension_semantics=("arbitrary",), vmem_limit_bytes=VMEM_LIMIT),
        name="experts",
    )(blk_expert, blk_valid, xs, wgu, wd)


def _final_kernel(x1_ref, y0_ref, y1_ref, w_ref, g2_ref, b2_ref, *rest):
    o_ref = rest[-1]
    w = w_ref[...].T
    halves = []
    for part in range(2):
        y0 = _unpack_bf16_pair(y0_ref[...])[part].astype(jnp.float32)
        y1 = _unpack_bf16_pair(y1_ref[...])[part].astype(jnp.float32)
        halves.append(y0 * w[:, 2:3] + y1 * w[:, 3:4])
    moe = jnp.concatenate(halves, axis=1)
    o_ref[...] = _layer_norm(DEEPNORM_ALPHA * x1_ref[...] + moe, g2_ref[...], b2_ref[...])


def _final(x1, ypair, rinfo, g2, b2, out_prev, part, n_parts, tm=1024):
    T, D = x1.shape
    const = lambda i: (0, 0)
    nt = T // tm
    off = part * nt
    in_specs = [
        pl.BlockSpec((tm, D), lambda i: (i, 0)),
        pl.BlockSpec((tm, D // 2), lambda i: (i, 0)),
        pl.BlockSpec((tm, D // 2), lambda i: (i + nt, 0)),
        pl.BlockSpec((8, tm), lambda i: (0, i)),
        pl.BlockSpec((1, D), const),
        pl.BlockSpec((1, D), const),
    ]
    args = [x1, ypair, ypair, rinfo, g2, b2]
    aliases = {}
    if out_prev is not None:
        in_specs.append(pl.BlockSpec(memory_space=pl.ANY))
        args.append(out_prev)
        aliases = {len(args) - 1: 0}
    return pl.pallas_call(
        _final_kernel,
        grid=(nt,),
        in_specs=in_specs,
        out_specs=pl.BlockSpec((tm, D), lambda i: (i + off, 0)),
        out_shape=jax.ShapeDtypeStruct((n_parts * T, D), jnp.float32),
        input_output_aliases=aliases,
        compiler_params=pltpu.CompilerParams(
            dimension_semantics=("parallel",), vmem_limit_bytes=VMEM_LIMIT),
        name="final",
    )(*args)


def _bias_kernel(tab_ref, sub_ref, idx_ref, o_ref):
    h = pl.program_id(0)
    idx = idx_ref[...]
    acc = jnp.full(idx.shape, NEG_INF, jnp.float32)
    for b in range(REL_BUCKETS):
        acc = jnp.where(idx == b, (tab_ref[h, b] - sub_ref[h]) * LOG2E, acc)
    o_ref[0] = acc


def _bias_expand(table_hb, sub_h, idx):
    H = table_hb.shape[0]
    K, Q = idx.shape
    return pl.pallas_call(
        _bias_kernel,
        grid_spec=pltpu.PrefetchScalarGridSpec(
            num_scalar_prefetch=2,
            grid=(H,),
            in_specs=[pl.BlockSpec((K, Q), lambda h, t, s: (0, 0))],
            out_specs=pl.BlockSpec((1, K, Q), lambda h, t, s: (h, 0, 0)),
        ),
        out_shape=jax.ShapeDtypeStruct((H, K, Q), jnp.float32),
        compiler_params=pltpu.CompilerParams(dimension_semantics=("parallel",)),
        name="bias",
    )(table_hb, sub_h, jnp.asarray(idx))


def _attention_bias_tables(rel_bias_table):
    rel = rel_bias_table.astype(jnp.float32)
    tab_a = rel[:, :SWA_Q_HEADS].T
    tab_b = rel[:, SWA_Q_HEADS:].T
    kj = np.arange(2 * SWA_BLOCK)[:, None]
    qi = np.arange(SWA_BLOCK)[None, :]
    dist = SWA_BLOCK + qi - kj
    idx_a = np.where((dist >= 0) & (dist < SWA_WINDOW), _rel_bucket_np(dist), -1).astype(np.int32)
    bias_a = _bias_expand(tab_a, jnp.zeros((SWA_Q_HEADS,), jnp.float32), idx_a)
    bias_a = bias_a.reshape(SWA_Q_HEADS // 2, 2, 2 * SWA_BLOCK, SWA_BLOCK).transpose(0, 2, 1, 3)
    bias_a = bias_a.reshape(SWA_Q_HEADS // 2, 2 * SWA_BLOCK, 2 * SWA_BLOCK)
    j = np.arange(MOBA_BLOCK)[:, None]
    i = np.arange(MOBA_BLOCK)[None, :]
    d_own = i - j
    idx_own = np.where(d_own >= 0, _rel_bucket_np(d_own), -1)
    idx_prev = _rel_bucket_np(MOBA_BLOCK + i - j)
    idx_b = np.concatenate([idx_prev, idx_own], axis=0).astype(np.int32)
    bias_b = _bias_expand(tab_b, tab_b[:, REL_BUCKETS - 1], idx_b)
    return bias_a, bias_b


def _block_plan(sizes, n_tok):
    n_assign = n_tok * EXPERT_TOPK
    padded = ((sizes + MOE_TM - 1) // MOE_TM) * MOE_TM
    eid = jnp.arange(N_EXPERTS, dtype=jnp.int32)
    padded_end = jnp.sum(jnp.where(eid[None, :] <= eid[:, None], padded[None, :], 0), axis=1)
    padded_start = padded_end - padded
    cap = -(-n_assign // MOE_TM) * MOE_TM + N_EXPERTS * MOE_TM
    blk_start = jnp.arange(cap // MOE_TM, dtype=jnp.int32) * MOE_TM
    blk_expert = jnp.minimum(
        jnp.sum(padded_end[None, :] <= blk_start[:, None], axis=1), N_EXPERTS - 1).astype(jnp.int32)
    mine = blk_expert[:, None] == eid[None, :]
    size_b = jnp.sum(jnp.where(mine, sizes[None, :], 0), axis=1)
    start_b = jnp.sum(jnp.where(mine, padded_start[None, :], 0), axis=1)
    blk_valid = jnp.clip(size_b - (blk_start - start_b), 0, MOE_TM)
    blk_valid = jnp.where(blk_start < padded_end[-1], blk_valid, 0).astype(jnp.int32)
    return padded_start, blk_expert, blk_valid, cap


def kernel(x, w_in, b_in, attn_sinks, rel_bias_table, w_branch_swa, w_branch_moba, w_out, ln1_gain, ln1_bias,
           w_group_router, b_group_router, w_expert_router, b_expert_router, w_expert_gate, w_expert_up,
           w_expert_down, ln2_gain, ln2_bias):
    assert w_in.shape[0] == DEPTH == 1
    B, S, D = x.shape
    T = B * S
    bf16 = jnp.bfloat16
    f32 = jnp.float32
    w = w_in[0]
    b = b_in[0]

    def cols(off, width):
        return w[:, off:off + width], b[off:off + width]

    wq_a, bq_a = cols(OFF_SWA_Q, SWA_Q_W)
    wk_a, bk_a = cols(OFF_SWA_K, SWA_KV_W)
    wv_a, bv_a = cols(OFF_SWA_V, SWA_KV_W)
    wq_b, bq_b = cols(OFF_MOBA_Q, MOBA_W)
    wk_b, bk_b = cols(OFF_MOBA_K, MOBA_W)
    wv_b, bv_b = cols(OFF_MOBA_V, MOBA_W)

    def dup_kv(t):
        parts = [t[..., i * HEAD_DIM:(i + 1) * HEAD_DIM] for i in range(SWA_KV_HEADS)]
        return jnp.concatenate([p for p in parts for _ in range(2)], axis=-1)

    qs = ATTN_SCALE * LOG2E
    wn = jnp.concatenate([wq_a * qs, dup_kv(wk_a), wq_b * qs, wk_b], axis=1).astype(bf16)
    bn = jnp.concatenate([bq_a * qs, dup_kv(bk_a), bq_b * qs, bk_b])[None, :].astype(f32)
    wt = jnp.concatenate([wv_a, wv_b], axis=1).T.astype(bf16)
    bt = jnp.concatenate([bv_a, bv_b])[:, None].astype(f32)

    qk, vt = _inproj(x, wn, bn, wt, bt)

    bias_a, bias_b = _attention_bias_tables(rel_bias_table)
    y_a = _swa(attn_sinks[0].astype(f32), qk, vt, bias_a)
    y_b = _moba(qk, vt, bias_b)

    wg, bg = cols(OFF_GATE, 2 * D_MODEL)
    wr = jnp.zeros((ROUTER_ROWS, D), f32)
    wr = wr.at[0:N_GROUPS].set(w_group_router[0].T).at[8:8 + N_EXPERTS].set(w_expert_router[0].T)
    wr_hi = wr.astype(bf16)
    wr_lo = (wr - wr_hi.astype(f32)).astype(bf16)
    br = jnp.zeros((ROUTER_ROWS,), f32)
    br = br.at[0:N_GROUPS].set(b_group_router[0]).at[8:8 + N_EXPERTS].set(b_expert_router[0])[:, None]
    merge_args = (
        x.reshape(T, D), y_a.reshape(T, SWA_Q_W), y_b.reshape(T, MOBA_W),
        wg.astype(bf16), bg[None, :].astype(f32), w_branch_swa[0].astype(bf16), w_branch_moba[0].astype(bf16),
        w_out[0].astype(bf16), ln1_gain[0][None, :].astype(f32), ln1_bias[0][None, :].astype(f32),
        jnp.concatenate([wr_hi, wr_lo], axis=0), br)
    wgu = jnp.concatenate([w_expert_gate[0], w_expert_up[0]], axis=-1).astype(bf16)
    wd = w_expert_down[0].astype(bf16)
    g2 = ln2_gain[0][None, :].astype(f32)
    b2 = ln2_bias[0][None, :].astype(f32)

    Tp = T // MOE_PARTS
    out = None
    for part in range(MOE_PARTS):
        x1, x1p, rinfo, counts = _merge(*merge_args, part, MOE_PARTS)
        sizes = counts[:, 0].astype(jnp.int32)
        padded_start, blk_expert, blk_valid, cap = _block_plan(sizes, Tp)
        dest = _dest(rinfo, padded_start)
        xs = _sc_scatter_rows(x1p, dest[0], dest[1], cap)
        y_buf = _experts(blk_expert, blk_valid, xs, wgu, wd)
        ypair = _sc_gather_rows(y_buf, dest[0:EXPERT_TOPK].reshape(-1))
        out = _final(x1, ypair, rinfo, g2, b2, out, part, MOE_PARTS)
    return out.reshape(B, S, D)
```

```python
import functools
import math

import numpy as np
import jax
import jax.numpy as jnp
from jax import lax
from jax.experimental import pallas as pl
from jax.experimental.pallas import tpu as pltpu
from jax.experimental.pallas import tpu_sc as plsc

D_MODEL = 1024
HEAD_DIM = 64
SWA_Q_HEADS = 8
SWA_KV_HEADS = 2
SWA_GROUP = SWA_Q_HEADS // SWA_KV_HEADS
SWA_WINDOW = 128
SWA_BLOCK = 128
MOBA_HEADS = 8
MOBA_BLOCK = 256
MOBA_TOPK = 3
MOBA_VROWS = HEAD_DIM + 16
REL_BUCKETS = 32
REL_MAX_DIST = 128
N_GROUPS = 4
EXPERTS_PER_GROUP = 8
N_EXPERTS = N_GROUPS * EXPERTS_PER_GROUP
EXPERT_TOPK = 2
D_EXPERT = 512
LN_EPS = 1e-5
DEPTH = 1
DEEPNORM_ALPHA = (2.0 * DEPTH) ** 0.25
NEG_INF = -1e30
ATTN_SCALE = HEAD_DIM ** -0.5
LOG2E = math.log2(math.e)

SWA_Q_W = SWA_Q_HEADS * HEAD_DIM
SWA_KV_W = SWA_KV_HEADS * HEAD_DIM
MOBA_W = MOBA_HEADS * HEAD_DIM
OFF_SWA_Q = 0
OFF_SWA_K = OFF_SWA_Q + SWA_Q_W
OFF_SWA_V = OFF_SWA_K + SWA_KV_W
OFF_MOBA_Q = OFF_SWA_V + SWA_KV_W
OFF_MOBA_K = OFF_MOBA_Q + MOBA_W
OFF_MOBA_V = OFF_MOBA_K + MOBA_W
OFF_GATE = OFF_MOBA_V + MOBA_W

LANES = 128
QK_SWA_Q = 0
QK_SWA_K = QK_SWA_Q + SWA_Q_W
QK_MOBA_Q = QK_SWA_K + SWA_KV_HEADS * LANES
QK_MOBA_K = QK_MOBA_Q + MOBA_W
QK_W = QK_MOBA_K + MOBA_W
VT_SWA = 0
VT_MOBA = VT_SWA + SWA_KV_W
VT_W = VT_MOBA + MOBA_W

MOE_TM = 512
MOE_PARTS = 2
SC_CHUNK = 128
ROUTER_ROWS = 8 + N_EXPERTS
VMEM_LIMIT = 56 * 1024 * 1024

_NT = (((1,), (1,)), ((), ()))


def _rel_bucket_np(dist):
    n = np.maximum(dist, 0)
    max_exact = REL_BUCKETS // 2
    nf = np.maximum(n, 1).astype(np.float32)
    large = max_exact + (np.log(nf / np.float32(max_exact)) / np.float32(math.log(REL_MAX_DIST / max_exact))
                         * np.float32(REL_BUCKETS - max_exact)).astype(np.int32)
    large = np.minimum(large, REL_BUCKETS - 1)
    return np.where(n < max_exact, n, large).astype(np.int32)


def _inproj_kernel(x_ref, wn_ref, bn_ref, wt_ref, bt_ref, eg_ref, eu_ref, ed_ref,
                   qk_ref, vt_ref, egb_ref, eub_ref, edb_ref):
    xb = x_ref[0].astype(jnp.bfloat16)
    qk = jnp.dot(xb, wn_ref[...], preferred_element_type=jnp.float32) + bn_ref[...]
    qk_ref[0] = qk.astype(jnp.bfloat16)
    vt = lax.dot_general(wt_ref[...], xb, _NT, preferred_element_type=jnp.float32) + bt_ref[...]
    vt_ref[0] = vt.astype(jnp.bfloat16)
    egb_ref[...] = eg_ref[...].astype(jnp.bfloat16)
    eub_ref[...] = eu_ref[...].astype(jnp.bfloat16)
    edb_ref[...] = ed_ref[...].astype(jnp.bfloat16)


def _inproj(x, wn, bn, wt, bt, w_gate, w_up, w_down, tm=512):
    B, S, D = x.shape
    nt = S // tm
    steps = B * nt

    def sliced(w):
        E, R, C = w.shape
        assert (E * R) % (steps * 16) == 0
        return w.reshape(steps, E * R // steps, C)

    eg, eu, ed = sliced(w_gate), sliced(w_up), sliced(w_down)
    espec = lambda w: pl.BlockSpec((1,) + w.shape[1:], lambda b, i: (b * nt + i, 0, 0))
    qk, vt, egb, eub, edb = pl.pallas_call(
        _inproj_kernel,
        grid=(B, nt),
        in_specs=[
            pl.BlockSpec((1, tm, D), lambda b, i: (b, i, 0)),
            pl.BlockSpec((D, QK_W), lambda b, i: (0, 0)),
            pl.BlockSpec((1, QK_W), lambda b, i: (0, 0)),
            pl.BlockSpec((VT_W, D), lambda b, i: (0, 0)),
            pl.BlockSpec((VT_W, 1), lambda b, i: (0, 0)),
            espec(eg), espec(eu), espec(ed),
        ],
        out_specs=[
            pl.BlockSpec((1, tm, QK_W), lambda b, i: (b, i, 0)),
            pl.BlockSpec((1, VT_W, tm), lambda b, i: (b, 0, i)),
            espec(eg), espec(eu), espec(ed),
        ],
        out_shape=[
            jax.ShapeDtypeStruct((B, S, QK_W), jnp.bfloat16),
            jax.ShapeDtypeStruct((B, VT_W, S), jnp.bfloat16),
            jax.ShapeDtypeStruct(eg.shape, jnp.bfloat16),
            jax.ShapeDtypeStruct(eu.shape, jnp.bfloat16),
            jax.ShapeDtypeStruct(ed.shape, jnp.bfloat16),
        ],
        compiler_params=pltpu.CompilerParams(
            dimension_semantics=("parallel", "parallel"), vmem_limit_bytes=VMEM_LIMIT),
        name="inproj",
    )(x, wn, bn, wt, bt, eg, eu, ed)
    return qk, vt, egb.reshape(w_gate.shape), eub.reshape(w_up.shape), edb.reshape(w_down.shape)


def _swa_kernel(sink_ref, q_ref, k_ref, vt_ref, bias_ref, o_ref, vaug_ref, *s_bufs):
    S = q_ref.shape[1]
    nblk = S // SWA_BLOCK
    half = HEAD_DIM
    vt = vt_ref[0]
    rows = lax.broadcasted_iota(jnp.int32, vt.shape, 0)
    one = jnp.ones_like(vt)
    vaug_ref[0] = jnp.where(rows < half, vt, one)
    vaug_ref[1] = jnp.where(rows < half, one, vt)
    lane = lax.broadcasted_iota(jnp.int32, (SWA_BLOCK, LANES), 1)
    col = lax.broadcasted_iota(jnp.int32, (1, 2 * SWA_BLOCK), 1)
    npair = SWA_Q_HEADS // 2

    def expand(specs):
        return [(qs, ks, nk, boff, pair, (2 * pair) // SWA_GROUP) for qs, ks, nk, boff in specs for pair in range(npair)]

    def score(specs):
        ss = []
        for qs, ks, nk, boff, pair, kv in expand(specs):
            qblk = q_ref[0, pl.ds(qs, SWA_BLOCK), pair * LANES:(pair + 1) * LANES]
            zero = jnp.zeros_like(qblk)
            q2 = jnp.concatenate([jnp.where(lane < half, qblk, zero), jnp.where(lane >= half, qblk, zero)], axis=0)
            kblk = k_ref[0, pl.ds(ks, nk), kv * LANES:(kv + 1) * LANES]
            s = lax.dot_general(kblk, q2, _NT, preferred_element_type=jnp.float32)
            ss.append(s + bias_ref[pair, boff:boff + nk, :])
        return ss

    def finish(specs, ss):
        items = expand(specs)
        stats = []
        for (qs, ks, nk, boff, pair, kv), s in zip(items, ss):
            sink = jnp.where(col < SWA_BLOCK, sink_ref[2 * pair], sink_ref[2 * pair + 1]) * LOG2E
            m = jnp.maximum(_colmax(s), sink)
            stats.append((jnp.exp2(s - m).astype(jnp.bfloat16), jnp.exp2(sink - m)))
        rs = [jnp.dot(vaug_ref[kv, :, pl.ds(ks, nk)], p, preferred_element_type=jnp.float32)
              for (qs, ks, nk, boff, pair, kv), (p, _) in zip(items, stats)]
        outs = []
        for (qs, ks, nk, boff, pair, kv), (_, esink), r in zip(items, stats, rs):
            l = (r[half:half + 1, :] if kv == 0 else r[0:1, :]) + esink
            o = (r[0:half, :] if kv == 0 else r[half:, :]) / l
            outs.append(jnp.concatenate([o[:, :SWA_BLOCK], o[:, SWA_BLOCK:]], axis=0))
        for i, (qs, _, _, _) in enumerate(specs):
            ot = jnp.concatenate(outs[i * npair:(i + 1) * npair], axis=0)
            o_ref[0, pl.ds(qs, SWA_BLOCK), :] = ot.T.astype(jnp.bfloat16)

    first = [(0, 0, SWA_BLOCK, SWA_BLOCK), (SWA_BLOCK, 0, 2 * SWA_BLOCK, 0)]
    finish(first, score(first))

    assert nblk % 2 == 0
    n_groups = nblk // 2 - 1
    nbuf = len(s_bufs)

    def group_specs(g):
        specs = []
        for n in (2 * g + 2, 2 * g + 3):
            qs = pl.multiple_of(n * SWA_BLOCK, SWA_BLOCK)
            specs.append((qs, pl.multiple_of(qs - SWA_BLOCK, SWA_BLOCK), 2 * SWA_BLOCK, 0))
        return specs

    def score_to(g, buf):
        for k, s in enumerate(score(group_specs(g))):
            buf[k] = s

    def finish_from(g, buf):
        finish(group_specs(g), [buf[k] for k in range(2 * npair)])

    for g in range(min(2, n_groups)):
        score_to(g, s_bufs[g])

    def body(t, carry):
        for j in range(nbuf):
            score_to(nbuf * t + 2 + j, s_bufs[(2 + j) % nbuf])
            finish_from(nbuf * t + j, s_bufs[j])
        return carry

    n_trips = max(n_groups - 2, 0) // nbuf
    lax.fori_loop(0, n_trips, body, 0)
    for g in range(nbuf * n_trips, n_groups):
        if g + 2 < n_groups:
            score_to(g + 2, s_bufs[(g + 2) % nbuf])
        finish_from(g, s_bufs[g % nbuf])


def _swa(sinks, qk, vt, bias):
    B, S, _ = qk.shape
    nq = SWA_Q_W // LANES
    return pl.pallas_call(
        _swa_kernel,
        grid_spec=pltpu.PrefetchScalarGridSpec(
            num_scalar_prefetch=1,
            grid=(B,),
            in_specs=[
                pl.BlockSpec((1, S, SWA_Q_W), lambda b, s: (b, 0, QK_SWA_Q // SWA_Q_W)),
                pl.BlockSpec((1, S, 2 * LANES), lambda b, s: (b, 0, QK_SWA_K // (2 * LANES))),
                pl.BlockSpec((1, SWA_KV_W, S), lambda b, s: (b, VT_SWA // SWA_KV_W, 0)),
                pl.BlockSpec((SWA_Q_HEADS // 2, 2 * SWA_BLOCK, 2 * SWA_BLOCK), lambda b, s: (0, 0, 0)),
            ],
            out_specs=pl.BlockSpec((1, S, SWA_Q_W), lambda b, s: (b, 0, 0)),
            scratch_shapes=[pltpu.VMEM((SWA_KV_HEADS, SWA_KV_W, S), jnp.bfloat16)]
            + [pltpu.VMEM((SWA_Q_HEADS, 2 * SWA_BLOCK, 2 * SWA_BLOCK), jnp.float32)] * 4,
        ),
        out_shape=jax.ShapeDtypeStruct((B, S, SWA_Q_W), jnp.bfloat16),
        compiler_params=pltpu.CompilerParams(
            dimension_semantics=("parallel",), vmem_limit_bytes=VMEM_LIMIT),
        name="swa",
    )(sinks, qk, qk, vt, bias)


def _colmax(s):
    while s.shape[0] > 8:
        h = s.shape[0] // 2
        s = jnp.maximum(s[:h], s[h:])
    return jnp.max(s, axis=0, keepdims=True)


def _moba_kernel(fqb_ref, fc_ref, q_ref, k_ref, vt_ref, bias_ref, o_ref,
                 vaug_ref, gate_ref, rank_ref, sel_ref, far_ref, m_scr, acc_scr, sa_scr, sb_scr, sc_scr, sd_scr):
    S = q_ref.shape[1]
    nb = S // MOBA_BLOCK
    half = HEAD_DIM
    BLK = MOBA_BLOCK

    ones_rows = jnp.ones((MOBA_VROWS - half, S), jnp.bfloat16)
    for h in range(2):
        vaug_ref[h, 0:half, :] = vt_ref[0, h * half:(h + 1) * half, :]
        vaug_ref[h, half:, :] = ones_rows

    kf = k_ref[0].astype(jnp.float32).reshape(nb, BLK, LANES)
    kmean = jnp.sum(kf, axis=1) * (1.0 / BLK)
    k_hi = kmean.astype(jnp.bfloat16)
    k_lo = (kmean - k_hi.astype(jnp.float32)).astype(jnp.bfloat16)
    kcat = jnp.concatenate([k_hi, k_lo], axis=0)

    lane_q = lax.broadcasted_iota(jnp.int32, (S, LANES), 1)
    brow = lax.broadcasted_iota(jnp.int32, (nb, S), 0)
    qblk_of = lax.broadcasted_iota(jnp.int32, (nb, S), 1) // BLK
    qall = q_ref[0]
    for h in range(2):
        qh = jnp.where((lane_q < half) if h == 0 else (lane_q >= half), qall, jnp.zeros_like(qall))
        g2 = lax.dot_general(kcat, qh, _NT, preferred_element_type=jnp.float32)
        gate_ref[...] = g2[0:nb] + g2[nb:2 * nb]
        rank_ref[...] = jnp.zeros_like(rank_ref)
        for m in range(nb - 1):
            lo = (m + 1) * BLK
            G = gate_ref[:, lo:]
            gm = gate_ref[m:m + 1, lo:]
            ge = jnp.where(gm >= G, 1.0, 0.0)
            gt = jnp.where(gm > G, 1.0, 0.0)
            brow_m = lax.broadcasted_iota(jnp.int32, (nb, S - lo), 0)
            rank_ref[:, lo:] += jnp.where(brow_m > m, ge, gt)
        top = rank_ref[...] < float(MOBA_TOPK)
        sel_ref[h] = jnp.where((brow < qblk_of) & top, 0.0, NEG_INF)
        far_ref[h] = jnp.where((brow < qblk_of - 1) & top, 0.0, NEG_INF)

    lane = lax.broadcasted_iota(jnp.int32, (BLK, LANES), 1)

    def head_q(qs, h):
        qblk = q_ref[0, pl.ds(qs, BLK), :]
        return jnp.where((lane < half) if h == 0 else (lane >= half), qblk, jnp.zeros_like(qblk))

    m0 = jnp.full((1, BLK), NEG_INF, jnp.float32)
    acc0 = jnp.zeros((MOBA_VROWS, BLK), jnp.float32)

    def item(h, qb, qs, ks, nk, bias, sel_a, sel_b, m, acc):
        return dict(h=h, qb=qb, qs=qs, ks=ks, nk=nk, bias=bias, sel_a=sel_a, sel_b=sel_b, m=m, acc=acc)

    def near_item(qb, h):
        qs = pl.multiple_of(qb * BLK, BLK)
        ps = pl.multiple_of(qs - BLK, BLK)
        sel_prev = sel_ref[h, pl.ds(qb - 1, 1), pl.ds(qs, BLK)]
        return item(h, qb, qs, ps, 2 * BLK, bias_ref[h], sel_prev, None, m0, acc0)

    def far_item(i, h):
        qb = fqb_ref[i]
        c = fc_ref[i]
        qs = pl.multiple_of(qb * BLK, BLK)
        ks = pl.multiple_of(c * (2 * BLK), 2 * BLK)
        sel_a = far_ref[h, pl.ds(2 * c, 1), pl.ds(qs, BLK)]
        sel_b = far_ref[h, pl.ds(2 * c + 1, 1), pl.ds(qs, BLK)]
        return item(h, qb, qs, ks, 2 * BLK, None, sel_a, sel_b, m_scr[h, qb], acc_scr[h, qb])

    def scores(it):
        kslab = k_ref[0, pl.ds(it["ks"], it["nk"]), :]
        s = lax.dot_general(kslab, head_q(it["qs"], it["h"]), _NT,
                            preferred_element_type=jnp.float32)
        return s if it["bias"] is None else s + it["bias"]

    def probs(it, s):
        m, sel_a, sel_b = it["m"], it["sel_a"], it["sel_b"]
        if it["nk"] == BLK:
            m_new = jnp.maximum(m, _colmax(s))
            p = jnp.exp2(s - m_new)
        else:
            cm_a = _colmax(s[:BLK]) + sel_a
            cm_b = _colmax(s[BLK:])
            if sel_b is not None:
                cm_b = cm_b + sel_b
            m_new = jnp.maximum(m, jnp.maximum(cm_a, cm_b))
            p_a = jnp.exp2(s[:BLK] - (m_new - sel_a))
            p_b = jnp.exp2(s[BLK:] - (m_new if sel_b is None else m_new - sel_b))
            p = jnp.concatenate([p_a, p_b], axis=0)
        return m_new, jnp.exp2(m - m_new), p.astype(jnp.bfloat16)

    def run(items):
        ss = [scores(it) for it in items]
        ps = [probs(it, s) for it, s in zip(items, ss)]
        pvs = [jnp.dot(vaug_ref[it["h"], :, pl.ds(it["ks"], it["nk"])], p, preferred_element_type=jnp.float32)
               for it, (_, _, p) in zip(items, ps)]
        for it, (m_new, alpha, _), pv in zip(items, ps, pvs):
            m_scr[it["h"], it["qb"]] = m_new
            acc_scr[it["h"], it["qb"]] = it["acc"] * alpha + pv

    run([item(h, 0, 0, 0, BLK, bias_ref[h, BLK:, :], None, None, m0, acc0) for h in range(2)]
        + [near_item(nb - 1, h) for h in range(2)])

    bufs = (sa_scr, sb_scr, sc_scr, sd_scr)
    nbuf = len(bufs)

    def finish(items, s_buf):
        ps = [probs(it, s_buf[k]) for k, it in enumerate(items)]
        pvs = [jnp.dot(vaug_ref[it["h"], :, pl.ds(it["ks"], it["nk"])], p, preferred_element_type=jnp.float32)
               for it, (_, _, p) in zip(items, ps)]
        for it, (m_new, alpha, _), pv in zip(items, ps, pvs):
            m_scr[it["h"], it["qb"]] = m_new
            acc_scr[it["h"], it["qb"]] = it["acc"] * alpha + pv

    def pipelined(n_groups, score, items_of):
        for g in range(min(2, n_groups)):
            score(g, bufs[g])

        def body(t, carry):
            for j in range(nbuf):
                score(nbuf * t + 2 + j, bufs[(2 + j) % nbuf])
                finish(items_of(nbuf * t + j), bufs[j])
            return carry

        n_trips = max(n_groups - 2, 0) // nbuf
        lax.fori_loop(0, n_trips, body, 0)
        for g in range(nbuf * n_trips, n_groups):
            if g + 2 < n_groups:
                score(g + 2, bufs[(g + 2) % nbuf])
            finish(items_of(g), bufs[g % nbuf])

    near_pairs = nb // 2 - 1

    def near_group(g):
        return [(qb, h) for qb in (g + 1, g + 1 + near_pairs) for h in range(2)]

    def near_score(g, s_buf):
        for k, (qb, h) in enumerate(near_group(g)):
            qs = pl.multiple_of(qb * BLK, BLK)
            ps = pl.multiple_of(qs - BLK, BLK)
            s_buf[k] = lax.dot_general(k_ref[0, pl.ds(ps, 2 * BLK), :], head_q(qs, h), _NT,
                                       preferred_element_type=jnp.float32) + bias_ref[h]

    pipelined(near_pairs, near_score, lambda g: [near_item(qb, h) for qb, h in near_group(g)])

    n_far_groups = fqb_ref.shape[0] // 2

    def far_group(g):
        return [(j, h) for j in (g, g + n_far_groups) for h in range(2)]

    def far_score(g, s_buf):
        for k, (j, h) in enumerate(far_group(g)):
            qs = pl.multiple_of(fqb_ref[j] * BLK, BLK)
            ks = pl.multiple_of(fc_ref[j] * (2 * BLK), 2 * BLK)
            s_buf[k] = lax.dot_general(k_ref[0, pl.ds(ks, 2 * BLK), :], head_q(qs, h), _NT,
                                       preferred_element_type=jnp.float32)

    pipelined(n_far_groups, far_score, lambda g: [far_item(j, h) for j, h in far_group(g)])

    def out_body(t, carry):
        for qb in (2 * t, 2 * t + 1):
            qs = pl.multiple_of(qb * BLK, BLK)
            a0 = acc_scr[0, qb]
            a1 = acc_scr[1, qb]
            ot = jnp.concatenate([a0[0:half] / a0[half:half + 1, :], a1[0:half] / a1[half:half + 1, :]],
                                 axis=0)
            o_ref[0, pl.ds(qs, BLK), :] = ot.T.astype(jnp.bfloat16)
        return carry

    lax.fori_loop(0, nb // 2, out_body, 0)


def _moba_far_items(nb):
    items = [(qb, c) for qb in range(2, nb) for c in range(qb // 2)]
    n = len(items)
    assert n % 2 == 0 and all(items[i][0] != items[i + n // 2][0] for i in range(n // 2))
    return np.array([it[0] for it in items], np.int32), np.array([it[1] for it in items], np.int32)


def _moba(qk, vt, bias):
    B, S, _ = qk.shape
    nb = S // MOBA_BLOCK
    npair = MOBA_W // LANES
    far_qb, far_c = _moba_far_items(nb)
    return pl.pallas_call(
        _moba_kernel,
        grid_spec=pltpu.PrefetchScalarGridSpec(
            num_scalar_prefetch=2,
            grid=(B, npair),
            in_specs=[
                pl.BlockSpec((1, S, LANES), lambda b, p, fq, fc: (b, 0, QK_MOBA_Q // LANES + p)),
                pl.BlockSpec((1, S, LANES), lambda b, p, fq, fc: (b, 0, QK_MOBA_K // LANES + p)),
                pl.BlockSpec((1, LANES, S), lambda b, p, fq, fc: (b, VT_MOBA // LANES + p, 0)),
                pl.BlockSpec((2, 2 * MOBA_BLOCK, MOBA_BLOCK), lambda b, p, fq, fc: (p, 0, 0)),
            ],
            out_specs=pl.BlockSpec((1, S, LANES), lambda b, p, fq, fc: (b, 0, p)),
            scratch_shapes=[
                pltpu.VMEM((2, MOBA_VROWS, S), jnp.bfloat16),
                pltpu.VMEM((nb, S), jnp.float32),
                pltpu.VMEM((nb, S), jnp.float32),
                pltpu.VMEM((2, nb, S), jnp.float32),
                pltpu.VMEM((2, nb, S), jnp.float32),
                pltpu.VMEM((2, nb, 1, MOBA_BLOCK), jnp.float32),
                pltpu.VMEM((2, nb, MOBA_VROWS, MOBA_BLOCK), jnp.float32),
            ] + [pltpu.VMEM((4, 2 * MOBA_BLOCK, MOBA_BLOCK), jnp.float32)] * 4,
        ),
        out_shape=jax.ShapeDtypeStruct((B, S, MOBA_W), jnp.bfloat16),
        compiler_params=pltpu.CompilerParams(
            dimension_semantics=("parallel", "parallel"), vmem_limit_bytes=VMEM_LIMIT),
        name="moba",
    )(jnp.asarray(far_qb), jnp.asarray(far_c), qk, qk, vt, bias)


def _layer_norm(h, gain, bias):
    mu = jnp.mean(h, axis=-1, keepdims=True)
    c = h - mu
    var = jnp.mean(c * c, axis=-1, keepdims=True)
    return c * lax.rsqrt(var + LN_EPS) * gain + bias


def _pack_bf16_pair(a, b):
    ia = lax.bitcast_convert_type(a.astype(jnp.bfloat16).astype(jnp.float32), jnp.int32)
    ib = lax.bitcast_convert_type(b.astype(jnp.bfloat16).astype(jnp.float32), jnp.int32)
    return lax.shift_right_logical(ia, 16) | ib


def _unpack_bf16_pair(w):
    lo = lax.bitcast_convert_type(lax.shift_left(w, 16), jnp.float32)
    hi = lax.bitcast_convert_type(w & jnp.int32(-65536), jnp.float32)
    return lo.astype(jnp.bfloat16), hi.astype(jnp.bfloat16)


def _merge_kernel(x_ref, ya_ref, yb_ref, wg_ref, bg_ref, wa_ref, wb_ref, wo_ref, g1_ref, b1_ref,
                  wr_ref, br_ref, tri_ref, x1_ref, x1p_ref, r_ref, cnt_ref):
    @pl.when(pl.program_id(0) == 0)
    def _():
        cnt_ref[...] = jnp.zeros_like(cnt_ref)

    ts = tri_ref.shape[0]
    subs = [pl.ds(r0, ts) for r0 in range(0, x_ref.shape[0], ts)]
    pre = [_merge_matmuls(x_ref[rows, :], ya_ref[rows, :], yb_ref[rows, :], wg_ref, bg_ref, wa_ref, wb_ref, wo_ref)
           for rows in subs]
    for rows, h in zip(subs, pre):
        _merge_route(h, rows, g1_ref, b1_ref, wr_ref, br_ref, tri_ref, x1_ref, x1p_ref, r_ref, cnt_ref)


def _merge_matmuls(x, ya, yb, wg_ref, bg_ref, wa_ref, wb_ref, wo_ref):
    xb = x.astype(jnp.bfloat16)
    z = jnp.dot(xb, wg_ref[...], preferred_element_type=jnp.float32) + bg_ref[...]
    gates = 1.0 / (1.0 + jnp.exp(-z))
    pa = jnp.dot(ya, wa_ref[...], preferred_element_type=jnp.float32)
    pb = jnp.dot(yb, wb_ref[...], preferred_element_type=jnp.float32)
    merged = gates[:, :D_MODEL] * pa + gates[:, D_MODEL:] * pb
    mixed = jnp.dot(merged.astype(jnp.bfloat16), wo_ref[...], preferred_element_type=jnp.float32)
    return DEEPNORM_ALPHA * x + mixed


def _merge_route(h, rows, g1_ref, b1_ref, wr_ref, br_ref, tri_ref, x1_ref, x1p_ref, r_ref, cnt_ref):
    x1 = _layer_norm(h, g1_ref[...], b1_ref[...])
    x1_ref[rows, :] = x1
    x1_hi = x1.astype(jnp.bfloat16)
    x1p_ref[rows, :] = _pack_bf16_pair(x1[:, :D_MODEL // 2], x1[:, D_MODEL // 2:])
    x1_lo = (x1 - x1_hi.astype(jnp.float32)).astype(jnp.bfloat16)

    R = ROUTER_ROWS
    l1 = lax.dot_general(wr_ref[...], x1_hi, _NT, preferred_element_type=jnp.float32)
    l2 = lax.dot_general(wr_ref[0:R, :], x1_lo, _NT, preferred_element_type=jnp.float32)
    L = l1[0:R] + l1[R:2 * R] + l2 + br_ref[...]
    tm = x1.shape[0]
    row = lax.broadcasted_iota(jnp.int32, (8, tm), 0)
    big = jnp.float32(-3e38)
    gl = jnp.where(row < N_GROUPS, L[0:8], big)
    gmax = jnp.max(gl, axis=0, keepdims=True)
    g_idx = jnp.min(jnp.where(gl == gmax, row, 8), axis=0, keepdims=True)
    gsum = jnp.sum(jnp.where(row < N_GROUPS, jnp.exp(gl - gmax), 0.0), axis=0, keepdims=True)
    g_prob = 1.0 / gsum
    E = L[8 + 8 * (N_GROUPS - 1):8 + 8 * N_GROUPS]
    for g in range(N_GROUPS - 2, -1, -1):
        E = jnp.where(g_idx == g, L[8 + 8 * g:16 + 8 * g], E)
    t0 = jnp.max(E, axis=0, keepdims=True)
    loc0 = jnp.min(jnp.where(E == t0, row, 8), axis=0, keepdims=True)
    E2 = jnp.where(row == loc0, big, E)
    t1 = jnp.max(E2, axis=0, keepdims=True)
    loc1 = jnp.min(jnp.where(E2 == t1, row, 8), axis=0, keepdims=True)
    ex = jnp.exp(t1 - t0)
    w0 = g_prob / (1.0 + ex)
    w1 = g_prob * ex / (1.0 + ex)
    e0i = g_idx * EXPERTS_PER_GROUP + loc0
    e1i = g_idx * EXPERTS_PER_GROUP + loc1

    erow = lax.broadcasted_iota(jnp.int32, (N_EXPERTS, tm), 0)
    oh0 = jnp.where(erow == e0i, 1.0, 0.0)
    oh1 = jnp.where(erow == e1i, 1.0, 0.0)
    both = oh0 + oh1
    prefix = jnp.dot(both.astype(jnp.bfloat16), tri_ref[...], preferred_element_type=jnp.float32)
    prefix = prefix + cnt_ref[:, 0:1]
    rank0 = jnp.sum(oh0 * prefix, axis=0, keepdims=True)
    rank1 = jnp.sum(oh1 * prefix, axis=0, keepdims=True)
    cnt_ref[...] = cnt_ref[...] + jnp.sum(both, axis=1, keepdims=True)

    vals = (e0i.astype(jnp.float32), e1i.astype(jnp.float32), w0, w1, rank0, rank1)
    out = jnp.zeros((8, tm), jnp.float32)
    for k, v in enumerate(vals):
        out = jnp.where(row == k, v, out)
    r_ref[:, rows] = out


def _merge(x2, ya, yb, wg, bg, wa, wb, wo, g1, b1, wr, br, part, n_parts, tm=512, ts=256):
    D = x2.shape[1]
    T = x2.shape[0] // n_parts
    off = part * (T // tm)
    const = lambda i: (0, 0)
    tri = jnp.triu(jnp.ones((ts, ts), jnp.bfloat16), k=1)
    return pl.pallas_call(
        _merge_kernel,
        grid=(T // tm,),
        in_specs=[
            pl.BlockSpec((tm, D), lambda i: (i + off, 0)),
            pl.BlockSpec((tm, SWA_Q_W), lambda i: (i + off, 0)),
            pl.BlockSpec((tm, MOBA_W), lambda i: (i + off, 0)),
            pl.BlockSpec((D, 2 * D), const),
            pl.BlockSpec((1, 2 * D), const),
            pl.BlockSpec((SWA_Q_W, D), const),
            pl.BlockSpec((MOBA_W, D), const),
            pl.BlockSpec((D, D), const),
            pl.BlockSpec((1, D), const),
            pl.BlockSpec((1, D), const),
            pl.BlockSpec((2 * ROUTER_ROWS, D), const),
            pl.BlockSpec((ROUTER_ROWS, 1), const),
            pl.BlockSpec((ts, ts), const),
        ],
        out_specs=[
            pl.BlockSpec((tm, D), lambda i: (i, 0)),
            pl.BlockSpec((tm, D // 2), lambda i: (i, 0)),
            pl.BlockSpec((8, tm), lambda i: (0, i)),
            pl.BlockSpec((N_EXPERTS, LANES), const),
        ],
        out_shape=[
            jax.ShapeDtypeStruct((T, D), jnp.float32),
            jax.ShapeDtypeStruct((T, D // 2), jnp.int32),
            jax.ShapeDtypeStruct((8, T), jnp.float32),
            jax.ShapeDtypeStruct((N_EXPERTS, LANES), jnp.float32),
        ],
        compiler_params=pltpu.CompilerParams(
            dimension_semantics=("arbitrary",), vmem_limit_bytes=VMEM_LIMIT),
        name="merge",
    )(x2, ya, yb, wg, bg, wa, wb, wo, g1, b1, wr, br, tri)


def _dest_kernel(r_ref, ps_ref, d_ref):
    tt = r_ref.shape[1]
    erow = lax.broadcasted_iota(jnp.int32, (N_EXPERTS, tt), 0)
    row = lax.broadcasted_iota(jnp.int32, (8, tt), 0)
    ps = ps_ref[...]
    out = jnp.zeros((8, tt), jnp.float32)
    for k in range(EXPERT_TOPK):
        e = r_ref[k:k + 1, :].astype(jnp.int32)
        start = jnp.sum(jnp.where(erow == e, ps, 0.0), axis=0, keepdims=True)
        out = jnp.where(row == k, start + r_ref[4 + k:5 + k, :], out)
    d_ref[...] = out.astype(jnp.int32)


def _dest(rinfo, padded_start, tt=8192):
    T = rinfo.shape[1]
    tt = min(tt, T)
    return pl.pallas_call(
        _dest_kernel,
        grid=(T // tt,),
        in_specs=[pl.BlockSpec((8, tt), lambda i: (0, i)), pl.BlockSpec((N_EXPERTS, 1), lambda i: (0, 0))],
        out_specs=pl.BlockSpec((8, tt), lambda i: (0, i)),
        out_shape=jax.ShapeDtypeStruct((8, T), jnp.int32),
        compiler_params=pltpu.CompilerParams(dimension_semantics=("parallel",)),
        name="dest",
    )(rinfo, padded_start.astype(jnp.float32)[:, None])


def _sc_workers():
    info = plsc.get_sparse_core_info()
    return info.num_cores, info.num_subcores


def _sc_scatter_rows(src, dest0, dest1, cap):
    T, W = src.shape
    nc, ns = _sc_workers()
    per_w = T // (nc * ns)
    assert per_w * nc * ns == T and per_w % SC_CHUNK == 0
    mesh = plsc.VectorSubcoreMesh(core_axis_name="c", subcore_axis_name="s")

    @functools.partial(
        pl.kernel, mesh=mesh,
        out_type=jax.ShapeDtypeStruct((cap, W), src.dtype),
        scratch_types=[pltpu.VMEM((SC_CHUNK,), jnp.int32), pltpu.VMEM((SC_CHUNK, W), src.dtype)],
    )
    def scatter(src_hbm, d0_hbm, d1_hbm, out_hbm, idx_v, rows_v):
        wid = lax.axis_index("s") * nc + lax.axis_index("c")
        base = wid * per_w

        @pl.loop(0, per_w // SC_CHUNK)
        def _(c):
            off = pl.multiple_of(base + c * SC_CHUNK, SC_CHUNK)
            pltpu.sync_copy(src_hbm.at[pl.ds(off, SC_CHUNK)], rows_v)
            for d_hbm in (d0_hbm, d1_hbm):
                pltpu.sync_copy(d_hbm.at[pl.ds(off, SC_CHUNK)], idx_v)
                pltpu.sync_copy(rows_v, out_hbm.at[idx_v])

    return scatter(src, dest0, dest1)


def _sc_gather_rows(table, idx):
    N = idx.shape[0]
    W = table.shape[1]
    nc, ns = _sc_workers()
    per_w = N // (nc * ns)
    assert per_w * nc * ns == N and per_w % SC_CHUNK == 0
    mesh = plsc.VectorSubcoreMesh(core_axis_name="c", subcore_axis_name="s")

    @functools.partial(
        pl.kernel, mesh=mesh,
        out_type=jax.ShapeDtypeStruct((N, W), table.dtype),
        scratch_types=[pltpu.VMEM((SC_CHUNK,), jnp.int32), pltpu.VMEM((SC_CHUNK, W), table.dtype)],
    )
    def gather(table_hbm, idx_hbm, out_hbm, idx_v, rows_v):
        wid = lax.axis_index("s") * nc + lax.axis_index("c")
        base = wid * per_w

        @pl.loop(0, per_w // SC_CHUNK)
        def _(c):
            off = pl.multiple_of(base + c * SC_CHUNK, SC_CHUNK)
            pltpu.sync_copy(idx_hbm.at[pl.ds(off, SC_CHUNK)], idx_v)
            pltpu.sync_copy(table_hbm.at[idx_v], rows_v)
            pltpu.sync_copy(rows_v, out_hbm.at[pl.ds(off, SC_CHUNK)])

    return gather(table, idx)


def _expert_kernel(be_ref, nv_ref, x_ref, wg_ref, wu_ref, wd_ref, y_ref):
    i = pl.program_id(0)
    nv = nv_ref[i]

    @pl.when(nv > 0)
    def _():
        lo, hi = _unpack_bf16_pair(x_ref[...])
        xb = jnp.concatenate([lo, hi], axis=1)
        rows = lax.broadcasted_iota(jnp.int32, xb.shape, 0)
        xb = jnp.where(rows < nv, xb, jnp.zeros_like(xb))
        g = jnp.dot(xb, wg_ref[0], preferred_element_type=jnp.float32)
        u = jnp.dot(xb, wu_ref[0], preferred_element_type=jnp.float32)
        act = (g / (1.0 + jnp.exp(-g))) * u
        y = jnp.dot(act.astype(jnp.bfloat16), wd_ref[0], preferred_element_type=jnp.float32)
        y_ref[...] = _pack_bf16_pair(y[:, :D_MODEL // 2], y[:, D_MODEL // 2:])

    @pl.when(nv <= 0)
    def _():
        y_ref[...] = jnp.zeros_like(y_ref)


def _experts(blk_expert, blk_valid, xs, wg, wu, wd):
    cap, DW = xs.shape
    D = 2 * DW
    n_blocks = cap // MOE_TM
    return pl.pallas_call(
        _expert_kernel,
        grid_spec=pltpu.PrefetchScalarGridSpec(
            num_scalar_prefetch=2,
            grid=(n_blocks,),
            in_specs=[
                pl.BlockSpec((MOE_TM, DW), lambda i, be, nv: (i, 0)),
                pl.BlockSpec((1, D, D_EXPERT), lambda i, be, nv: (be[i], 0, 0)),
                pl.BlockSpec((1, D, D_EXPERT), lambda i, be, nv: (be[i], 0, 0)),
                pl.BlockSpec((1, D_EXPERT, D), lambda i, be, nv: (be[i], 0, 0)),
            ],
            out_specs=pl.BlockSpec((MOE_TM, DW), lambda i, be, nv: (i, 0)),
        ),
        out_shape=jax.ShapeDtypeStruct((cap, DW), jnp.int32),
        compiler_params=pltpu.CompilerParams(
            dimension_semantics=("arbitrary",), vmem_limit_bytes=VMEM_LIMIT),
        name="experts",
    )(blk_expert, blk_valid, xs, wg, wu, wd)


def _final_kernel(x1_ref, y0_ref, y1_ref, w_ref, g2_ref, b2_ref, *rest):
    o_ref = rest[-1]
    w = w_ref[...].T
    halves = []
    for part in range(2):
        y0 = _unpack_bf16_pair(y0_ref[...])[part].astype(jnp.float32)
        y1 = _unpack_bf16_pair(y1_ref[...])[part].astype(jnp.float32)
        halves.append(y0 * w[:, 2:3] + y1 * w[:, 3:4])
    moe = jnp.concatenate(halves, axis=1)
    o_ref[...] = _layer_norm(DEEPNORM_ALPHA * x1_ref[...] + moe, g2_ref[...], b2_ref[...])


def _final(x1, ypair, rinfo, g2, b2, out_prev, part, n_parts, tm=1024):
    T, D = x1.shape
    const = lambda i: (0, 0)
    nt = T // tm
    off = part * nt
    in_specs = [
        pl.BlockSpec((tm, D), lambda i: (i, 0)),
        pl.BlockSpec((tm, D // 2), lambda i: (i, 0)),
        pl.BlockSpec((tm, D // 2), lambda i: (i + nt, 0)),
        pl.BlockSpec((8, tm), lambda i: (0, i)),
        pl.BlockSpec((1, D), const),
        pl.BlockSpec((1, D), const),
    ]
    args = [x1, ypair, ypair, rinfo, g2, b2]
    aliases = {}
    if out_prev is not None:
        in_specs.append(pl.BlockSpec(memory_space=pl.ANY))
        args.append(out_prev)
        aliases = {len(args) - 1: 0}
    return pl.pallas_call(
        _final_kernel,
        grid=(nt,),
        in_specs=in_specs,
        out_specs=pl.BlockSpec((tm, D), lambda i: (i + off, 0)),
        out_shape=jax.ShapeDtypeStruct((n_parts * T, D), jnp.float32),
        input_output_aliases=aliases,
        compiler_params=pltpu.CompilerParams(
            dimension_semantics=("parallel",), vmem_limit_bytes=VMEM_LIMIT),
        name="final",
    )(*args)


def _bias_kernel(tab_ref, sub_ref, idx_ref, o_ref):
    h = pl.program_id(0)
    idx = idx_ref[...]
    acc = jnp.full(idx.shape, NEG_INF, jnp.float32)
    for b in range(REL_BUCKETS):
        acc = jnp.where(idx == b, (tab_ref[h, b] - sub_ref[h]) * LOG2E, acc)
    o_ref[0] = acc


def _bias_expand(table_hb, sub_h, idx):
    H = table_hb.shape[0]
    K, Q = idx.shape
    return pl.pallas_call(
        _bias_kernel,
        grid_spec=pltpu.PrefetchScalarGridSpec(
            num_scalar_prefetch=2,
            grid=(H,),
            in_specs=[pl.BlockSpec((K, Q), lambda h, t, s: (0, 0))],
            out_specs=pl.BlockSpec((1, K, Q), lambda h, t, s: (h, 0, 0)),
        ),
        out_shape=jax.ShapeDtypeStruct((H, K, Q), jnp.float32),
        compiler_params=pltpu.CompilerParams(dimension_semantics=("parallel",)),
        name="bias",
    )(table_hb, sub_h, jnp.asarray(idx))


def _attention_bias_tables(rel_bias_table):
    rel = rel_bias_table.astype(jnp.float32)
    tab_a = rel[:, :SWA_Q_HEADS].T
    tab_b = rel[:, SWA_Q_HEADS:].T
    kj = np.arange(2 * SWA_BLOCK)[:, None]
    qi = np.arange(SWA_BLOCK)[None, :]
    dist = SWA_BLOCK + qi - kj
    idx_a = np.where((dist >= 0) & (dist < SWA_WINDOW), _rel_bucket_np(dist), -1).astype(np.int32)
    bias_a = _bias_expand(tab_a, jnp.zeros((SWA_Q_HEADS,), jnp.float32), idx_a)
    bias_a = bias_a.reshape(SWA_Q_HEADS // 2, 2, 2 * SWA_BLOCK, SWA_BLOCK).transpose(0, 2, 1, 3)
    bias_a = bias_a.reshape(SWA_Q_HEADS // 2, 2 * SWA_BLOCK, 2 * SWA_BLOCK)
    j = np.arange(MOBA_BLOCK)[:, None]
    i = np.arange(MOBA_BLOCK)[None, :]
    d_own = i - j
    idx_own = np.where(d_own >= 0, _rel_bucket_np(d_own), -1)
    idx_prev = _rel_bucket_np(MOBA_BLOCK + i - j)
    idx_b = np.concatenate([idx_prev, idx_own], axis=0).astype(np.int32)
    bias_b = _bias_expand(tab_b, tab_b[:, REL_BUCKETS - 1], idx_b)
    return bias_a, bias_b


def _block_plan(sizes, n_tok):
    n_assign = n_tok * EXPERT_TOPK
    padded = ((sizes + MOE_TM - 1) // MOE_TM) * MOE_TM
    eid = jnp.arange(N_EXPERTS, dtype=jnp.int32)
    padded_end = jnp.sum(jnp.where(eid[None, :] <= eid[:, None], padded[None, :], 0), axis=1)
    padded_start = padded_end - padded
    cap = -(-n_assign // MOE_TM) * MOE_TM + N_EXPERTS * MOE_TM
    blk_start = jnp.arange(cap // MOE_TM, dtype=jnp.int32) * MOE_TM
    blk_expert = jnp.minimum(
        jnp.sum(padded_end[None, :] <= blk_start[:, None], axis=1), N_EXPERTS - 1).astype(jnp.int32)
    mine = blk_expert[:, None] == eid[None, :]
    size_b = jnp.sum(jnp.where(mine, sizes[None, :], 0), axis=1)
    start_b = jnp.sum(jnp.where(mine, padded_start[None, :], 0), axis=1)
    blk_valid = jnp.clip(size_b - (blk_start - start_b), 0, MOE_TM)
    blk_valid = jnp.where(blk_start < padded_end[-1], blk_valid, 0).astype(jnp.int32)
    return padded_start, blk_expert, blk_valid, cap


def kernel(x, w_in, b_in, attn_sinks, rel_bias_table, w_branch_swa, w_branch_moba, w_out, ln1_gain, ln1_bias,
           w_group_router, b_group_router, w_expert_router, b_expert_router, w_expert_gate, w_expert_up,
           w_expert_down, ln2_gain, ln2_bias):
    assert w_in.shape[0] == DEPTH == 1
    B, S, D = x.shape
    T = B * S
    bf16 = jnp.bfloat16
    f32 = jnp.float32
    w = w_in[0]
    b = b_in[0]

    def cols(off, width):
        return w[:, off:off + width], b[off:off + width]

    wq_a, bq_a = cols(OFF_SWA_Q, SWA_Q_W)
    wk_a, bk_a = cols(OFF_SWA_K, SWA_KV_W)
    wv_a, bv_a = cols(OFF_SWA_V, SWA_KV_W)
    wq_b, bq_b = cols(OFF_MOBA_Q, MOBA_W)
    wk_b, bk_b = cols(OFF_MOBA_K, MOBA_W)
    wv_b, bv_b = cols(OFF_MOBA_V, MOBA_W)

    def dup_kv(t):
        parts = [t[..., i * HEAD_DIM:(i + 1) * HEAD_DIM] for i in range(SWA_KV_HEADS)]
        return jnp.concatenate([p for p in parts for _ in range(2)], axis=-1)

    qs = ATTN_SCALE * LOG2E
    wn = jnp.concatenate([wq_a * qs, dup_kv(wk_a), wq_b * qs, wk_b], axis=1).astype(bf16)
    bn = jnp.concatenate([bq_a * qs, dup_kv(bk_a), bq_b * qs, bk_b])[None, :].astype(f32)
    wt = jnp.concatenate([wv_a, wv_b], axis=1).T.astype(bf16)
    bt = jnp.concatenate([bv_a, bv_b])[:, None].astype(f32)

    qk, vt, eg_b, eu_b, ed_b = _inproj(x, wn, bn, wt, bt, w_expert_gate[0], w_expert_up[0], w_expert_down[0])

    bias_a, bias_b = _attention_bias_tables(rel_bias_table)
    y_a = _swa(attn_sinks[0].astype(f32), qk, vt, bias_a)
    y_b = _moba(qk, vt, bias_b)

    wg, bg = cols(OFF_GATE, 2 * D_MODEL)
    wr = jnp.zeros((ROUTER_ROWS, D), f32)
    wr = wr.at[0:N_GROUPS].set(w_group_router[0].T).at[8:8 + N_EXPERTS].set(w_expert_router[0].T)
    wr_hi = wr.astype(bf16)
    wr_lo = (wr - wr_hi.astype(f32)).astype(bf16)
    br = jnp.zeros((ROUTER_ROWS,), f32)
    br = br.at[0:N_GROUPS].set(b_group_router[0]).at[8:8 + N_EXPERTS].set(b_expert_router[0])[:, None]
    merge_args = (
        x.reshape(T, D), y_a.reshape(T, SWA_Q_W), y_b.reshape(T, MOBA_W),
        wg.astype(bf16), bg[None, :].astype(f32), w_branch_swa[0].astype(bf16), w_branch_moba[0].astype(bf16),
        w_out[0].astype(bf16), ln1_gain[0][None, :].astype(f32), ln1_bias[0][None, :].astype(f32),
        jnp.concatenate([wr_hi, wr_lo], axis=0), br)
    g2 = ln2_gain[0][None, :].astype(f32)
    b2 = ln2_bias[0][None, :].astype(f32)

    Tp = T // MOE_PARTS
    out = None
    for part in range(MOE_PARTS):
        x1, x1p, rinfo, counts = _merge(*merge_args, part, MOE_PARTS)
        sizes = counts[:, 0].astype(jnp.int32)
        padded_start, blk_expert, blk_valid, cap = _block_plan(sizes, Tp)
        dest = _dest(rinfo, padded_start)
        xs = _sc_scatter_rows(x1p, dest[0], dest[1], cap)
        y_buf = _experts(blk_expert, blk_valid, xs, eg_b, eu_b, ed_b)
        ypair = _sc_gather_rows(y_buf, dest[0:EXPERT_TOPK].reshape(-1))
        out = _final(x1, ypair, rinfo, g2, b2, out, part, MOE_PARTS)
    return out.reshape(B, S, D)
```

```python
import functools
import math

import numpy as np
import jax
import jax.numpy as jnp
from jax import lax
from jax.experimental import pallas as pl
from jax.experimental.pallas import tpu as pltpu
from jax.experimental.pallas import tpu_sc as plsc

D_MODEL = 1024
HEAD_DIM = 64
SWA_Q_HEADS = 8
SWA_KV_HEADS = 2
SWA_GROUP = SWA_Q_HEADS // SWA_KV_HEADS
SWA_WINDOW = 128
SWA_BLOCK = 128
MOBA_HEADS = 8
MOBA_BLOCK = 256
MOBA_TOPK = 3
MOBA_LOOKAHEAD = 2
BF16_SUBLANES = 16
MOBA_VROWS = HEAD_DIM + BF16_SUBLANES
REL_BUCKETS = 32
REL_MAX_DIST = 128
N_GROUPS = 4
EXPERTS_PER_GROUP = 8
N_EXPERTS = N_GROUPS * EXPERTS_PER_GROUP
EXPERT_TOPK = 2
D_EXPERT = 512
LN_EPS = 1e-5
DEPTH = 1
DEEPNORM_ALPHA = (2.0 * DEPTH) ** 0.25
NEG_INF = -1e30
ATTN_SCALE = HEAD_DIM ** -0.5
LOG2E = math.log2(math.e)

SWA_Q_W = SWA_Q_HEADS * HEAD_DIM
SWA_KV_W = SWA_KV_HEADS * HEAD_DIM
MOBA_W = MOBA_HEADS * HEAD_DIM
OFF_SWA_Q = 0
OFF_SWA_K = OFF_SWA_Q + SWA_Q_W
OFF_SWA_V = OFF_SWA_K + SWA_KV_W
OFF_MOBA_Q = OFF_SWA_V + SWA_KV_W
OFF_MOBA_K = OFF_MOBA_Q + MOBA_W
OFF_MOBA_V = OFF_MOBA_K + MOBA_W
OFF_GATE = OFF_MOBA_V + MOBA_W

LANES = 128
QK_SWA_Q = 0
QK_SWA_K = QK_SWA_Q + SWA_Q_W
QK_MOBA_Q = QK_SWA_K + SWA_KV_HEADS * LANES
QK_MOBA_K = QK_MOBA_Q + MOBA_W
QK_W = QK_MOBA_K + MOBA_W
VT_SWA = 0
VT_MOBA = VT_SWA + SWA_KV_W
VT_W = VT_MOBA + MOBA_W

MOE_TM = 512
MOE_PARTS = 2
SC_CHUNK = 128
ROUTER_ROWS = 8 + N_EXPERTS
VMEM_LIMIT = 56 * 1024 * 1024

_NT = (((1,), (1,)), ((), ()))


def _rel_bucket_np(dist):
    n = np.maximum(dist, 0)
    max_exact = REL_BUCKETS // 2
    nf = np.maximum(n, 1).astype(np.float32)
    large = max_exact + (np.log(nf / np.float32(max_exact)) / np.float32(math.log(REL_MAX_DIST / max_exact))
                         * np.float32(REL_BUCKETS - max_exact)).astype(np.int32)
    large = np.minimum(large, REL_BUCKETS - 1)
    return np.where(n < max_exact, n, large).astype(np.int32)


def _inproj_kernel(x_ref, wn_ref, bn_ref, wt_ref, bt_ref, eg_ref, eu_ref, ed_ref,
                   qk_ref, vt_ref, egb_ref, eub_ref, edb_ref):
    xb = x_ref[0].astype(jnp.bfloat16)
    qk = jnp.dot(xb, wn_ref[...], preferred_element_type=jnp.float32) + bn_ref[...]
    qk_ref[0] = qk.astype(jnp.bfloat16)
    vt = lax.dot_general(wt_ref[...], xb, _NT, preferred_element_type=jnp.float32) + bt_ref[...]
    vt_ref[0] = vt.astype(jnp.bfloat16)
    egb_ref[...] = eg_ref[...].astype(jnp.bfloat16)
    eub_ref[...] = eu_ref[...].astype(jnp.bfloat16)
    edb_ref[...] = ed_ref[...].astype(jnp.bfloat16)


def _inproj(x, wn, bn, wt, bt, w_gate, w_up, w_down, tm=512):
    B, S, D = x.shape
    nt = S // tm
    steps = B * nt

    def sliced(w):
        E, R, C = w.shape
        assert (E * R) % (steps * BF16_SUBLANES) == 0
        return w.reshape(steps, E * R // steps, C)

    eg, eu, ed = sliced(w_gate), sliced(w_up), sliced(w_down)
    espec = lambda w: pl.BlockSpec((1,) + w.shape[1:], lambda b, i: (b * nt + i, 0, 0))
    qk, vt, egb, eub, edb = pl.pallas_call(
        _inproj_kernel,
        grid=(B, nt),
        in_specs=[
            pl.BlockSpec((1, tm, D), lambda b, i: (b, i, 0)),
            pl.BlockSpec((D, QK_W), lambda b, i: (0, 0)),
            pl.BlockSpec((1, QK_W), lambda b, i: (0, 0)),
            pl.BlockSpec((VT_W, D), lambda b, i: (0, 0)),
            pl.BlockSpec((VT_W, 1), lambda b, i: (0, 0)),
            espec(eg), espec(eu), espec(ed),
        ],
        out_specs=[
            pl.BlockSpec((1, tm, QK_W), lambda b, i: (b, i, 0)),
            pl.BlockSpec((1, VT_W, tm), lambda b, i: (b, 0, i)),
            espec(eg), espec(eu), espec(ed),
        ],
        out_shape=[
            jax.ShapeDtypeStruct((B, S, QK_W), jnp.bfloat16),
            jax.ShapeDtypeStruct((B, VT_W, S), jnp.bfloat16),
            jax.ShapeDtypeStruct(eg.shape, jnp.bfloat16),
            jax.ShapeDtypeStruct(eu.shape, jnp.bfloat16),
            jax.ShapeDtypeStruct(ed.shape, jnp.bfloat16),
        ],
        compiler_params=pltpu.CompilerParams(
            dimension_semantics=("parallel", "parallel"), vmem_limit_bytes=VMEM_LIMIT),
        name="inproj",
    )(x, wn, bn, wt, bt, eg, eu, ed)
    return qk, vt, egb.reshape(w_gate.shape), eub.reshape(w_up.shape), edb.reshape(w_down.shape)


def _swa_kernel(sink_ref, q_ref, k_ref, vt_ref, bias_ref, o_ref, vaug_ref, *s_bufs):
    S = q_ref.shape[1]
    nblk = S // SWA_BLOCK
    half = HEAD_DIM
    vt = vt_ref[0]
    rows = lax.broadcasted_iota(jnp.int32, vt.shape, 0)
    one = jnp.ones_like(vt)
    vaug_ref[0] = jnp.where(rows < half, vt, one)
    vaug_ref[1] = jnp.where(rows < half, one, vt)
    lane = lax.broadcasted_iota(jnp.int32, (SWA_BLOCK, LANES), 1)
    col = lax.broadcasted_iota(jnp.int32, (1, 2 * SWA_BLOCK), 1)
    npair = SWA_Q_HEADS // 2

    def expand(specs):
        return [(qs, ks, nk, boff, pair, (2 * pair) // SWA_GROUP) for qs, ks, nk, boff in specs for pair in range(npair)]

    def score(specs):
        ss = []
        for qs, ks, nk, boff, pair, kv in expand(specs):
            qblk = q_ref[0, pl.ds(qs, SWA_BLOCK), pair * LANES:(pair + 1) * LANES]
            zero = jnp.zeros_like(qblk)
            q2 = jnp.concatenate([jnp.where(lane < half, qblk, zero), jnp.where(lane >= half, qblk, zero)], axis=0)
            kblk = k_ref[0, pl.ds(ks, nk), kv * LANES:(kv + 1) * LANES]
            s = lax.dot_general(kblk, q2, _NT, preferred_element_type=jnp.float32)
            ss.append(s + bias_ref[pair, boff:boff + nk, :])
        return ss

    def finish(specs, ss):
        items = expand(specs)
        stats = []
        for (qs, ks, nk, boff, pair, kv), s in zip(items, ss):
            sink = jnp.where(col < SWA_BLOCK, sink_ref[2 * pair], sink_ref[2 * pair + 1]) * LOG2E
            m = jnp.maximum(_colmax(s), sink)
            stats.append((jnp.exp2(s - m).astype(jnp.bfloat16), jnp.exp2(sink - m)))
        rs = [jnp.dot(vaug_ref[kv, :, pl.ds(ks, nk)], p, preferred_element_type=jnp.float32)
              for (qs, ks, nk, boff, pair, kv), (p, _) in zip(items, stats)]
        outs = []
        for (qs, ks, nk, boff, pair, kv), (_, esink), r in zip(items, stats, rs):
            l = (r[half:half + 1, :] if kv == 0 else r[0:1, :]) + esink
            o = (r[0:half, :] if kv == 0 else r[half:, :]) / l
            outs.append(jnp.concatenate([o[:, :SWA_BLOCK], o[:, SWA_BLOCK:]], axis=0))
        for i, (qs, _, _, _) in enumerate(specs):
            ot = jnp.concatenate(outs[i * npair:(i + 1) * npair], axis=0)
            o_ref[0, pl.ds(qs, SWA_BLOCK), :] = ot.T.astype(jnp.bfloat16)

    first = [(0, 0, SWA_BLOCK, SWA_BLOCK), (SWA_BLOCK, 0, 2 * SWA_BLOCK, 0)]
    finish(first, score(first))

    assert nblk % 2 == 0
    n_groups = nblk // 2 - 1
    nbuf = len(s_bufs)

    def group_specs(g):
        specs = []
        for n in (2 * g + 2, 2 * g + 3):
            qs = pl.multiple_of(n * SWA_BLOCK, SWA_BLOCK)
            specs.append((qs, pl.multiple_of(qs - SWA_BLOCK, SWA_BLOCK), 2 * SWA_BLOCK, 0))
        return specs

    def score_to(g, buf):
        for k, s in enumerate(score(group_specs(g))):
            buf[k] = s

    def finish_from(g, buf):
        finish(group_specs(g), [buf[k] for k in range(2 * npair)])

    for g in range(min(2, n_groups)):
        score_to(g, s_bufs[g])

    def body(t, carry):
        for j in range(nbuf):
            score_to(nbuf * t + 2 + j, s_bufs[(2 + j) % nbuf])
            finish_from(nbuf * t + j, s_bufs[j])
        return carry

    n_trips = max(n_groups - 2, 0) // nbuf
    lax.fori_loop(0, n_trips, body, 0)
    for g in range(nbuf * n_trips, n_groups):
        if g + 2 < n_groups:
            score_to(g + 2, s_bufs[(g + 2) % nbuf])
        finish_from(g, s_bufs[g % nbuf])


def _swa(sinks, qk, vt, bias):
    B, S, _ = qk.shape
    return pl.pallas_call(
        _swa_kernel,
        grid_spec=pltpu.PrefetchScalarGridSpec(
            num_scalar_prefetch=1,
            grid=(B,),
            in_specs=[
                pl.BlockSpec((1, S, SWA_Q_W), lambda b, s: (b, 0, QK_SWA_Q // SWA_Q_W)),
                pl.BlockSpec((1, S, 2 * LANES), lambda b, s: (b, 0, QK_SWA_K // (2 * LANES))),
                pl.BlockSpec((1, SWA_KV_W, S), lambda b, s: (b, VT_SWA // SWA_KV_W, 0)),
                pl.BlockSpec((SWA_Q_HEADS // 2, 2 * SWA_BLOCK, 2 * SWA_BLOCK), lambda b, s: (0, 0, 0)),
            ],
            out_specs=pl.BlockSpec((1, S, SWA_Q_W), lambda b, s: (b, 0, 0)),
            scratch_shapes=[pltpu.VMEM((SWA_KV_HEADS, SWA_KV_W, S), jnp.bfloat16)]
            + [pltpu.VMEM((SWA_Q_HEADS, 2 * SWA_BLOCK, 2 * SWA_BLOCK), jnp.float32)] * 4,
        ),
        out_shape=jax.ShapeDtypeStruct((B, S, SWA_Q_W), jnp.bfloat16),
        compiler_params=pltpu.CompilerParams(
            dimension_semantics=("parallel",), vmem_limit_bytes=VMEM_LIMIT),
        name="swa",
    )(sinks, qk, qk, vt, bias)


def _colmax(s):
    while s.shape[0] > 8:
        h = s.shape[0] // 2
        s = jnp.maximum(s[:h], s[h:])
    return jnp.max(s, axis=0, keepdims=True)


def _moba_kernel(fqb_ref, fc_ref, q_ref, k_ref, vt_ref, bias_ref, o_ref,
                 vaug_ref, gate_ref, rank_ref, sel_ref, far_ref, m_scr, acc_scr, sa_scr, sb_scr, sc_scr, sd_scr):
    S = q_ref.shape[1]
    nb = S // MOBA_BLOCK
    half = HEAD_DIM
    BLK = MOBA_BLOCK

    ones_rows = jnp.ones((MOBA_VROWS - half, S), jnp.bfloat16)
    for h in range(2):
        vaug_ref[h, 0:half, :] = vt_ref[0, h * half:(h + 1) * half, :]
        vaug_ref[h, half:, :] = ones_rows

    kf = k_ref[0].astype(jnp.float32).reshape(nb, BLK, LANES)
    kmean = jnp.sum(kf, axis=1) * (1.0 / BLK)
    k_hi = kmean.astype(jnp.bfloat16)
    k_lo = (kmean - k_hi.astype(jnp.float32)).astype(jnp.bfloat16)
    kcat = jnp.concatenate([k_hi, k_lo], axis=0)

    lane_q = lax.broadcasted_iota(jnp.int32, (S, LANES), 1)
    brow = lax.broadcasted_iota(jnp.int32, (nb, S), 0)
    qblk_of = lax.broadcasted_iota(jnp.int32, (nb, S), 1) // BLK
    qall = q_ref[0]
    for h in range(2):
        qh = jnp.where((lane_q < half) if h == 0 else (lane_q >= half), qall, jnp.zeros_like(qall))
        g2 = lax.dot_general(kcat, qh, _NT, preferred_element_type=jnp.float32)
        gate_ref[...] = g2[0:nb] + g2[nb:2 * nb]
        rank_ref[...] = jnp.zeros_like(rank_ref)
        for m in range(nb - 1):
            lo = (m + 1) * BLK
            G = gate_ref[:, lo:]
            gm = gate_ref[m:m + 1, lo:]
            ge = jnp.where(gm >= G, 1.0, 0.0)
            gt = jnp.where(gm > G, 1.0, 0.0)
            brow_m = lax.broadcasted_iota(jnp.int32, (nb, S - lo), 0)
            rank_ref[:, lo:] += jnp.where(brow_m > m, ge, gt)
        top = rank_ref[...] < float(MOBA_TOPK)
        sel_ref[h] = jnp.where((brow < qblk_of) & top, 0.0, NEG_INF)
        far_ref[h] = jnp.where((brow < qblk_of - 1) & top, 0.0, NEG_INF)

    lane = lax.broadcasted_iota(jnp.int32, (BLK, LANES), 1)

    def head_q(qs, h):
        qblk = q_ref[0, pl.ds(qs, BLK), :]
        return jnp.where((lane < half) if h == 0 else (lane >= half), qblk, jnp.zeros_like(qblk))

    m0 = jnp.full((1, BLK), NEG_INF, jnp.float32)
    acc0 = jnp.zeros((MOBA_VROWS, BLK), jnp.float32)

    def item(h, qb, qs, ks, nk, bias, sel_a, sel_b, m, acc):
        return dict(h=h, qb=qb, qs=qs, ks=ks, nk=nk, bias=bias, sel_a=sel_a, sel_b=sel_b, m=m, acc=acc)

    def near_item(qb, h):
        qs = pl.multiple_of(qb * BLK, BLK)
        ps = pl.multiple_of(qs - BLK, BLK)
        sel_prev = sel_ref[h, pl.ds(qb - 1, 1), pl.ds(qs, BLK)]
        return item(h, qb, qs, ps, 2 * BLK, bias_ref[h], sel_prev, None, m0, acc0)

    def far_item(i, h):
        qb = fqb_ref[i]
        c = fc_ref[i]
        qs = pl.multiple_of(qb * BLK, BLK)
        ks = pl.multiple_of(c * (2 * BLK), 2 * BLK)
        sel_a = far_ref[h, pl.ds(2 * c, 1), pl.ds(qs, BLK)]
        sel_b = far_ref[h, pl.ds(2 * c + 1, 1), pl.ds(qs, BLK)]
        return item(h, qb, qs, ks, 2 * BLK, None, sel_a, sel_b, m_scr[h, qb], acc_scr[h, qb])

    def scores(it):
        kslab = k_ref[0, pl.ds(it["ks"], it["nk"]), :]
        s = lax.dot_general(kslab, head_q(it["qs"], it["h"]), _NT,
                            preferred_element_type=jnp.float32)
        return s if it["bias"] is None else s + it["bias"]

    def probs(it, s):
        m, sel_a, sel_b = it["m"], it["sel_a"], it["sel_b"]
        if it["nk"] == BLK:
            m_new = jnp.maximum(m, _colmax(s))
            p = jnp.exp2(s - m_new)
        else:
            cm_a = _colmax(s[:BLK]) + sel_a
            cm_b = _colmax(s[BLK:])
            if sel_b is not None:
                cm_b = cm_b + sel_b
            m_new = jnp.maximum(m, jnp.maximum(cm_a, cm_b))
            p_a = jnp.exp2(s[:BLK] - (m_new - sel_a))
            p_b = jnp.exp2(s[BLK:] - (m_new if sel_b is None else m_new - sel_b))
            p = jnp.concatenate([p_a, p_b], axis=0)
        return m_new, jnp.exp2(m - m_new), p.astype(jnp.bfloat16)

    def run(items):
        ss = [scores(it) for it in items]
        ps = [probs(it, s) for it, s in zip(items, ss)]
        pvs = [jnp.dot(vaug_ref[it["h"], :, pl.ds(it["ks"], it["nk"])], p, preferred_element_type=jnp.float32)
               for it, (_, _, p) in zip(items, ps)]
        for it, (m_new, alpha, _), pv in zip(items, ps, pvs):
            m_scr[it["h"], it["qb"]] = m_new
            acc_scr[it["h"], it["qb"]] = it["acc"] * alpha + pv

    run([item(h, 0, 0, 0, BLK, bias_ref[h, BLK:, :], None, None, m0, acc0) for h in range(2)]
        + [near_item(nb - 1, h) for h in range(2)])

    bufs = (sa_scr, sb_scr, sc_scr, sd_scr)
    nbuf = len(bufs)

    def finish(items, s_buf):
        ps = [probs(it, s_buf[k]) for k, it in enumerate(items)]
        pvs = [jnp.dot(vaug_ref[it["h"], :, pl.ds(it["ks"], it["nk"])], p, preferred_element_type=jnp.float32)
               for it, (_, _, p) in zip(items, ps)]
        for it, (m_new, alpha, _), pv in zip(items, ps, pvs):
            m_scr[it["h"], it["qb"]] = m_new
            acc_scr[it["h"], it["qb"]] = it["acc"] * alpha + pv

    def pipelined(n_groups, score, items_of):
        for g in range(min(MOBA_LOOKAHEAD, n_groups)):
            score(g, bufs[g])

        def body(t, carry):
            for j in range(nbuf):
                score(nbuf * t + MOBA_LOOKAHEAD + j, bufs[(MOBA_LOOKAHEAD + j) % nbuf])
                finish(items_of(nbuf * t + j), bufs[j])
            return carry

        n_trips = max(n_groups - MOBA_LOOKAHEAD, 0) // nbuf
        lax.fori_loop(0, n_trips, body, 0)
        for g in range(nbuf * n_trips, n_groups):
            if g + MOBA_LOOKAHEAD < n_groups:
                score(g + MOBA_LOOKAHEAD, bufs[(g + MOBA_LOOKAHEAD) % nbuf])
            finish(items_of(g), bufs[g % nbuf])

    near_pairs = nb // 2 - 1

    def near_group(g):
        return [(qb, h) for qb in (g + 1, g + 1 + near_pairs) for h in range(2)]

    def near_score(g, s_buf):
        for k, (qb, h) in enumerate(near_group(g)):
            qs = pl.multiple_of(qb * BLK, BLK)
            ps = pl.multiple_of(qs - BLK, BLK)
            s_buf[k] = lax.dot_general(k_ref[0, pl.ds(ps, 2 * BLK), :], head_q(qs, h), _NT,
                                       preferred_element_type=jnp.float32) + bias_ref[h]

    pipelined(near_pairs, near_score, lambda g: [near_item(qb, h) for qb, h in near_group(g)])

    n_far_groups = fqb_ref.shape[0] // 2

    def far_group(g):
        return [(j, h) for j in (g, g + n_far_groups) for h in range(2)]

    def far_score(g, s_buf):
        for k, (j, h) in enumerate(far_group(g)):
            qs = pl.multiple_of(fqb_ref[j] * BLK, BLK)
            ks = pl.multiple_of(fc_ref[j] * (2 * BLK), 2 * BLK)
            s_buf[k] = lax.dot_general(k_ref[0, pl.ds(ks, 2 * BLK), :], head_q(qs, h), _NT,
                                       preferred_element_type=jnp.float32)

    pipelined(n_far_groups, far_score, lambda g: [far_item(j, h) for j, h in far_group(g)])

    def out_body(t, carry):
        for qb in (2 * t, 2 * t + 1):
            qs = pl.multiple_of(qb * BLK, BLK)
            a0 = acc_scr[0, qb]
            a1 = acc_scr[1, qb]
            ot = jnp.concatenate([a0[0:half] / a0[half:half + 1, :], a1[0:half] / a1[half:half + 1, :]],
                                 axis=0)
            o_ref[0, pl.ds(qs, BLK), :] = ot.T.astype(jnp.bfloat16)
        return carry

    lax.fori_loop(0, nb // 2, out_body, 0)


def _moba_far_items(nb):
    items = [(qb, c) for qb in range(2, nb) for c in range(qb // 2)]
    n = len(items)
    assert n % 2 == 0 and all(items[i][0] != items[i + n // 2][0] for i in range(n // 2))
    return np.array([it[0] for it in items], np.int32), np.array([it[1] for it in items], np.int32)


def _moba(qk, vt, bias):
    B, S, _ = qk.shape
    nb = S // MOBA_BLOCK
    npair = MOBA_W // LANES
    far_qb, far_c = _moba_far_items(nb)
    return pl.pallas_call(
        _moba_kernel,
        grid_spec=pltpu.PrefetchScalarGridSpec(
            num_scalar_prefetch=2,
            grid=(B, npair),
            in_specs=[
                pl.BlockSpec((1, S, LANES), lambda b, p, fq, fc: (b, 0, QK_MOBA_Q // LANES + p)),
                pl.BlockSpec((1, S, LANES), lambda b, p, fq, fc: (b, 0, QK_MOBA_K // LANES + p)),
                pl.BlockSpec((1, LANES, S), lambda b, p, fq, fc: (b, VT_MOBA // LANES + p, 0)),
                pl.BlockSpec((2, 2 * MOBA_BLOCK, MOBA_BLOCK), lambda b, p, fq, fc: (p, 0, 0)),
            ],
            out_specs=pl.BlockSpec((1, S, LANES), lambda b, p, fq, fc: (b, 0, p)),
            scratch_shapes=[
                pltpu.VMEM((2, MOBA_VROWS, S), jnp.bfloat16),
                pltpu.VMEM((nb, S), jnp.float32),
                pltpu.VMEM((nb, S), jnp.float32),
                pltpu.VMEM((2, nb, S), jnp.float32),
                pltpu.VMEM((2, nb, S), jnp.float32),
                pltpu.VMEM((2, nb, 1, MOBA_BLOCK), jnp.float32),
                pltpu.VMEM((2, nb, MOBA_VROWS, MOBA_BLOCK), jnp.float32),
            ] + [pltpu.VMEM((4, 2 * MOBA_BLOCK, MOBA_BLOCK), jnp.float32)] * 4,
        ),
        out_shape=jax.ShapeDtypeStruct((B, S, MOBA_W), jnp.bfloat16),
        compiler_params=pltpu.CompilerParams(
            dimension_semantics=("parallel", "parallel"), vmem_limit_bytes=VMEM_LIMIT),
        name="moba",
    )(jnp.asarray(far_qb), jnp.asarray(far_c), qk, qk, vt, bias)


def _layer_norm(h, gain, bias):
    mu = jnp.mean(h, axis=-1, keepdims=True)
    c = h - mu
    var = jnp.mean(c * c, axis=-1, keepdims=True)
    return c * lax.rsqrt(var + LN_EPS) * gain + bias


def _pack_bf16_pair(a, b):
    ia = lax.bitcast_convert_type(a.astype(jnp.bfloat16).astype(jnp.float32), jnp.int32)
    ib = lax.bitcast_convert_type(b.astype(jnp.bfloat16).astype(jnp.float32), jnp.int32)
    return lax.shift_right_logical(ia, 16) | ib


def _unpack_bf16_pair(w):
    lo = lax.bitcast_convert_type(lax.shift_left(w, 16), jnp.float32)
    hi = lax.bitcast_convert_type(w & jnp.int32(-65536), jnp.float32)
    return lo.astype(jnp.bfloat16), hi.astype(jnp.bfloat16)


def _merge_kernel(x_ref, ya_ref, yb_ref, wg_ref, bg_ref, wa_ref, wb_ref, wo_ref, g1_ref, b1_ref,
                  wr_ref, br_ref, tri_ref, x1_ref, x1p_ref, r_ref, cnt_ref):
    @pl.when(pl.program_id(0) == 0)
    def _():
        cnt_ref[...] = jnp.zeros_like(cnt_ref)

    ts = tri_ref.shape[0]
    subs = [pl.ds(r0, ts) for r0 in range(0, x_ref.shape[0], ts)]
    pre = [_merge_matmuls(x_ref[rows, :], ya_ref[rows, :], yb_ref[rows, :], wg_ref, bg_ref, wa_ref, wb_ref, wo_ref)
           for rows in subs]
    for rows, h in zip(subs, pre):
        _merge_route(h, rows, g1_ref, b1_ref, wr_ref, br_ref, tri_ref, x1_ref, x1p_ref, r_ref, cnt_ref)


def _merge_matmuls(x, ya, yb, wg_ref, bg_ref, wa_ref, wb_ref, wo_ref):
    xb = x.astype(jnp.bfloat16)
    z = jnp.dot(xb, wg_ref[...], preferred_element_type=jnp.float32) + bg_ref[...]
    gates = 1.0 / (1.0 + jnp.exp(-z))
    pa = jnp.dot(ya, wa_ref[...], preferred_element_type=jnp.float32)
    pb = jnp.dot(yb, wb_ref[...], preferred_element_type=jnp.float32)
    merged = gates[:, :D_MODEL] * pa + gates[:, D_MODEL:] * pb
    mixed = jnp.dot(merged.astype(jnp.bfloat16), wo_ref[...], preferred_element_type=jnp.float32)
    return DEEPNORM_ALPHA * x + mixed


def _merge_route(h, rows, g1_ref, b1_ref, wr_ref, br_ref, tri_ref, x1_ref, x1p_ref, r_ref, cnt_ref):
    x1 = _layer_norm(h, g1_ref[...], b1_ref[...])
    x1_ref[rows, :] = x1
    x1_hi = x1.astype(jnp.bfloat16)
    x1p_ref[rows, :] = _pack_bf16_pair(x1[:, :D_MODEL // 2], x1[:, D_MODEL // 2:])
    x1_lo = (x1 - x1_hi.astype(jnp.float32)).astype(jnp.bfloat16)

    R = ROUTER_ROWS
    l1 = lax.dot_general(wr_ref[...], x1_hi, _NT, preferred_element_type=jnp.float32)
    l2 = lax.dot_general(wr_ref[0:R, :], x1_lo, _NT, preferred_element_type=jnp.float32)
    L = l1[0:R] + l1[R:2 * R] + l2 + br_ref[...]
    tm = x1.shape[0]
    row = lax.broadcasted_iota(jnp.int32, (8, tm), 0)
    big = jnp.float32(-3e38)
    gl = jnp.where(row < N_GROUPS, L[0:8], big)
    gmax = jnp.max(gl, axis=0, keepdims=True)
    g_idx = jnp.min(jnp.where(gl == gmax, row, 8), axis=0, keepdims=True)
    gsum = jnp.sum(jnp.where(row < N_GROUPS, jnp.exp(gl - gmax), 0.0), axis=0, keepdims=True)
    g_prob = 1.0 / gsum
    E = L[8 + 8 * (N_GROUPS - 1):8 + 8 * N_GROUPS]
    for g in range(N_GROUPS - 2, -1, -1):
        E = jnp.where(g_idx == g, L[8 + 8 * g:16 + 8 * g], E)
    t0 = jnp.max(E, axis=0, keepdims=True)
    loc0 = jnp.min(jnp.where(E == t0, row, 8), axis=0, keepdims=True)
    E2 = jnp.where(row == loc0, big, E)
    t1 = jnp.max(E2, axis=0, keepdims=True)
    loc1 = jnp.min(jnp.where(E2 == t1, row, 8), axis=0, keepdims=True)
    ex = jnp.exp(t1 - t0)
    w0 = g_prob / (1.0 + ex)
    w1 = g_prob * ex / (1.0 + ex)
    e0i = g_idx * EXPERTS_PER_GROUP + loc0
    e1i = g_idx * EXPERTS_PER_GROUP + loc1

    erow = lax.broadcasted_iota(jnp.int32, (N_EXPERTS, tm), 0)
    oh0 = jnp.where(erow == e0i, 1.0, 0.0)
    oh1 = jnp.where(erow == e1i, 1.0, 0.0)
    both = oh0 + oh1
    prefix = jnp.dot(both.astype(jnp.bfloat16), tri_ref[...], preferred_element_type=jnp.float32)
    prefix = prefix + cnt_ref[:, 0:1]
    rank0 = jnp.sum(oh0 * prefix, axis=0, keepdims=True)
    rank1 = jnp.sum(oh1 * prefix, axis=0, keepdims=True)
    cnt_ref[...] = cnt_ref[...] + jnp.sum(both, axis=1, keepdims=True)

    vals = (e0i.astype(jnp.float32), e1i.astype(jnp.float32), w0, w1, rank0, rank1)
    out = jnp.zeros((8, tm), jnp.float32)
    for k, v in enumerate(vals):
        out = jnp.where(row == k, v, out)
    r_ref[:, rows] = out


def _merge(x2, ya, yb, wg, bg, wa, wb, wo, g1, b1, wr, br, part, n_parts, tm=512, ts=256):
    D = x2.shape[1]
    T = x2.shape[0] // n_parts
    off = part * (T // tm)
    const = lambda i: (0, 0)
    tri = jnp.triu(jnp.ones((ts, ts), jnp.bfloat16), k=1)
    return pl.pallas_call(
        _merge_kernel,
        grid=(T // tm,),
        in_specs=[
            pl.BlockSpec((tm, D), lambda i: (i + off, 0)),
            pl.BlockSpec((tm, SWA_Q_W), lambda i: (i + off, 0)),
            pl.BlockSpec((tm, MOBA_W), lambda i: (i + off, 0)),
            pl.BlockSpec((D, 2 * D), const),
            pl.BlockSpec((1, 2 * D), const),
            pl.BlockSpec((SWA_Q_W, D), const),
            pl.BlockSpec((MOBA_W, D), const),
            pl.BlockSpec((D, D), const),
            pl.BlockSpec((1, D), const),
            pl.BlockSpec((1, D), const),
            pl.BlockSpec((2 * ROUTER_ROWS, D), const),
            pl.BlockSpec((ROUTER_ROWS, 1), const),
            pl.BlockSpec((ts, ts), const),
        ],
        out_specs=[
            pl.BlockSpec((tm, D), lambda i: (i, 0)),
            pl.BlockSpec((tm, D // 2), lambda i: (i, 0)),
            pl.BlockSpec((8, tm), lambda i: (0, i)),
            pl.BlockSpec((N_EXPERTS, LANES), const),
        ],
        out_shape=[
            jax.ShapeDtypeStruct((T, D), jnp.float32),
            jax.ShapeDtypeStruct((T, D // 2), jnp.int32),
            jax.ShapeDtypeStruct((8, T), jnp.float32),
            jax.ShapeDtypeStruct((N_EXPERTS, LANES), jnp.float32),
        ],
        compiler_params=pltpu.CompilerParams(
            dimension_semantics=("arbitrary",), vmem_limit_bytes=VMEM_LIMIT),
        name="merge",
    )(x2, ya, yb, wg, bg, wa, wb, wo, g1, b1, wr, br, tri)


def _dest_kernel(r_ref, ps_ref, d_ref):
    tt = r_ref.shape[1]
    erow = lax.broadcasted_iota(jnp.int32, (N_EXPERTS, tt), 0)
    row = lax.broadcasted_iota(jnp.int32, (8, tt), 0)
    ps = ps_ref[...]
    out = jnp.zeros((8, tt), jnp.float32)
    for k in range(EXPERT_TOPK):
        e = r_ref[k:k + 1, :].astype(jnp.int32)
        start = jnp.sum(jnp.where(erow == e, ps, 0.0), axis=0, keepdims=True)
        out = jnp.where(row == k, start + r_ref[4 + k:5 + k, :], out)
    d_ref[...] = out.astype(jnp.int32)


def _dest(rinfo, padded_start, tt=8192):
    T = rinfo.shape[1]
    tt = min(tt, T)
    return pl.pallas_call(
        _dest_kernel,
        grid=(T // tt,),
        in_specs=[pl.BlockSpec((8, tt), lambda i: (0, i)), pl.BlockSpec((N_EXPERTS, 1), lambda i: (0, 0))],
        out_specs=pl.BlockSpec((8, tt), lambda i: (0, i)),
        out_shape=jax.ShapeDtypeStruct((8, T), jnp.int32),
        compiler_params=pltpu.CompilerParams(dimension_semantics=("parallel",)),
        name="dest",
    )(rinfo, padded_start.astype(jnp.float32)[:, None])


def _sc_workers():
    info = plsc.get_sparse_core_info()
    return info.num_cores, info.num_subcores


def _sc_scatter_rows(src, dest0, dest1, cap):
    T, W = src.shape
    nc, ns = _sc_workers()
    per_w = T // (nc * ns)
    assert per_w * nc * ns == T and per_w % SC_CHUNK == 0
    mesh = plsc.VectorSubcoreMesh(core_axis_name="c", subcore_axis_name="s")

    @functools.partial(
        pl.kernel, mesh=mesh,
        out_type=jax.ShapeDtypeStruct((cap, W), src.dtype),
        scratch_types=[pltpu.VMEM((SC_CHUNK,), jnp.int32), pltpu.VMEM((SC_CHUNK, W), src.dtype)],
    )
    def scatter(src_hbm, d0_hbm, d1_hbm, out_hbm, idx_v, rows_v):
        wid = lax.axis_index("s") * nc + lax.axis_index("c")
        base = wid * per_w

        @pl.loop(0, per_w // SC_CHUNK)
        def _(c):
            off = pl.multiple_of(base + c * SC_CHUNK, SC_CHUNK)
            pltpu.sync_copy(src_hbm.at[pl.ds(off, SC_CHUNK)], rows_v)
            for d_hbm in (d0_hbm, d1_hbm):
                pltpu.sync_copy(d_hbm.at[pl.ds(off, SC_CHUNK)], idx_v)
                pltpu.sync_copy(rows_v, out_hbm.at[idx_v])

    return scatter(src, dest0, dest1)


def _sc_gather_rows(table, idx):
    N = idx.shape[0]
    W = table.shape[1]
    nc, ns = _sc_workers()
    per_w = N // (nc * ns)
    assert per_w * nc * ns == N and per_w % SC_CHUNK == 0
    mesh = plsc.VectorSubcoreMesh(core_axis_name="c", subcore_axis_name="s")

    @functools.partial(
        pl.kernel, mesh=mesh,
        out_type=jax.ShapeDtypeStruct((N, W), table.dtype),
        scratch_types=[pltpu.VMEM((SC_CHUNK,), jnp.int32), pltpu.VMEM((SC_CHUNK, W), table.dtype)],
    )
    def gather(table_hbm, idx_hbm, out_hbm, idx_v, rows_v):
        wid = lax.axis_index("s") * nc + lax.axis_index("c")
        base = wid * per_w

        @pl.loop(0, per_w // SC_CHUNK)
        def _(c):
            off = pl.multiple_of(base + c * SC_CHUNK, SC_CHUNK)
            pltpu.sync_copy(idx_hbm.at[pl.ds(off, SC_CHUNK)], idx_v)
            pltpu.sync_copy(table_hbm.at[idx_v], rows_v)
            pltpu.sync_copy(rows_v, out_hbm.at[pl.ds(off, SC_CHUNK)])

    return gather(table, idx)


def _expert_kernel(be_ref, nv_ref, x_ref, wg_ref, wu_ref, wd_ref, y_ref):
    i = pl.program_id(0)
    nv = nv_ref[i]

    def run(n):
        lo, hi = _unpack_bf16_pair(x_ref[0:n, :])
        xb = jnp.concatenate([lo, hi], axis=1)
        rows = lax.broadcasted_iota(jnp.int32, xb.shape, 0)
        xb = jnp.where(rows < nv, xb, jnp.zeros_like(xb))
        g = jnp.dot(xb, wg_ref[0], preferred_element_type=jnp.float32)
        u = jnp.dot(xb, wu_ref[0], preferred_element_type=jnp.float32)
        act = (g / (1.0 + jnp.exp(-g))) * u
        y = jnp.dot(act.astype(jnp.bfloat16), wd_ref[0], preferred_element_type=jnp.float32)
        y_ref[0:n, :] = _pack_bf16_pair(y[:, :D_MODEL // 2], y[:, D_MODEL // 2:])
        if n < MOE_TM:
            y_ref[n:, :] = jnp.zeros((MOE_TM - n, y_ref.shape[1]), y_ref.dtype)

    half = MOE_TM // 2
    pl.when(nv > half)(lambda: run(MOE_TM))
    pl.when((nv > 0) & (nv <= half))(lambda: run(half))

    @pl.when(nv <= 0)
    def _():
        y_ref[...] = jnp.zeros_like(y_ref)


def _experts(blk_expert, blk_valid, xs, wg, wu, wd):
    cap, DW = xs.shape
    D = 2 * DW
    n_blocks = cap // MOE_TM
    return pl.pallas_call(
        _expert_kernel,
        grid_spec=pltpu.PrefetchScalarGridSpec(
            num_scalar_prefetch=2,
            grid=(n_blocks,),
            in_specs=[
                pl.BlockSpec((MOE_TM, DW), lambda i, be, nv: (i, 0)),
                pl.BlockSpec((1, D, D_EXPERT), lambda i, be, nv: (be[i], 0, 0)),
                pl.BlockSpec((1, D, D_EXPERT), lambda i, be, nv: (be[i], 0, 0)),
                pl.BlockSpec((1, D_EXPERT, D), lambda i, be, nv: (be[i], 0, 0)),
            ],
            out_specs=pl.BlockSpec((MOE_TM, DW), lambda i, be, nv: (i, 0)),
        ),
        out_shape=jax.ShapeDtypeStruct((cap, DW), jnp.int32),
        compiler_params=pltpu.CompilerParams(
            dimension_semantics=("arbitrary",), vmem_limit_bytes=VMEM_LIMIT),
        name="experts",
    )(blk_expert, blk_valid, xs, wg, wu, wd)


def _final_kernel(x1_ref, y0_ref, y1_ref, w_ref, g2_ref, b2_ref, *rest):
    o_ref = rest[-1]
    w = w_ref[...].T
    halves = []
    for part in range(2):
        y0 = _unpack_bf16_pair(y0_ref[...])[part].astype(jnp.float32)
        y1 = _unpack_bf16_pair(y1_ref[...])[part].astype(jnp.float32)
        halves.append(y0 * w[:, 2:3] + y1 * w[:, 3:4])
    moe = jnp.concatenate(halves, axis=1)
    o_ref[...] = _layer_norm(DEEPNORM_ALPHA * x1_ref[...] + moe, g2_ref[...], b2_ref[...])


def _final(x1, ypair, rinfo, g2, b2, out_prev, part, n_parts, tm=1024):
    T, D = x1.shape
    const = lambda i: (0, 0)
    nt = T // tm
    off = part * nt
    in_specs = [
        pl.BlockSpec((tm, D), lambda i: (i, 0)),
        pl.BlockSpec((tm, D // 2), lambda i: (i, 0)),
        pl.BlockSpec((tm, D // 2), lambda i: (i + nt, 0)),
        pl.BlockSpec((8, tm), lambda i: (0, i)),
        pl.BlockSpec((1, D), const),
        pl.BlockSpec((1, D), const),
    ]
    args = [x1, ypair, ypair, rinfo, g2, b2]
    aliases = {}
    if out_prev is not None:
        in_specs.append(pl.BlockSpec(memory_space=pl.ANY))
        args.append(out_prev)
        aliases = {len(args) - 1: 0}
    return pl.pallas_call(
        _final_kernel,
        grid=(nt,),
        in_specs=in_specs,
        out_specs=pl.BlockSpec((tm, D), lambda i: (i + off, 0)),
        out_shape=jax.ShapeDtypeStruct((n_parts * T, D), jnp.float32),
        input_output_aliases=aliases,
        compiler_params=pltpu.CompilerParams(
            dimension_semantics=("parallel",), vmem_limit_bytes=VMEM_LIMIT),
        name="final",
    )(*args)


def _bias_kernel(tab_ref, sub_ref, idx_ref, o_ref):
    h = pl.program_id(0)
    idx = idx_ref[...]
    acc = jnp.full(idx.shape, NEG_INF, jnp.float32)
    for b in range(REL_BUCKETS):
        acc = jnp.where(idx == b, (tab_ref[h, b] - sub_ref[h]) * LOG2E, acc)
    o_ref[0] = acc


def _bias_expand(table_hb, sub_h, idx):
    H = table_hb.shape[0]
    K, Q = idx.shape
    return pl.pallas_call(
        _bias_kernel,
        grid_spec=pltpu.PrefetchScalarGridSpec(
            num_scalar_prefetch=2,
            grid=(H,),
            in_specs=[pl.BlockSpec((K, Q), lambda h, t, s: (0, 0))],
            out_specs=pl.BlockSpec((1, K, Q), lambda h, t, s: (h, 0, 0)),
        ),
        out_shape=jax.ShapeDtypeStruct((H, K, Q), jnp.float32),
        compiler_params=pltpu.CompilerParams(dimension_semantics=("parallel",)),
        name="bias",
    )(table_hb, sub_h, jnp.asarray(idx))


def _attention_bias_tables(rel_bias_table):
    rel = rel_bias_table.astype(jnp.float32)
    tab_a = rel[:, :SWA_Q_HEADS].T
    tab_b = rel[:, SWA_Q_HEADS:].T
    kj = np.arange(2 * SWA_BLOCK)[:, None]
    qi = np.arange(SWA_BLOCK)[None, :]
    dist = SWA_BLOCK + qi - kj
    idx_a = np.where((dist >= 0) & (dist < SWA_WINDOW), _rel_bucket_np(dist), -1).astype(np.int32)
    bias_a = _bias_expand(tab_a, jnp.zeros((SWA_Q_HEADS,), jnp.float32), idx_a)
    bias_a = bias_a.reshape(SWA_Q_HEADS // 2, 2, 2 * SWA_BLOCK, SWA_BLOCK).transpose(0, 2, 1, 3)
    bias_a = bias_a.reshape(SWA_Q_HEADS // 2, 2 * SWA_BLOCK, 2 * SWA_BLOCK)
    j = np.arange(MOBA_BLOCK)[:, None]
    i = np.arange(MOBA_BLOCK)[None, :]
    d_own = i - j
    idx_own = np.where(d_own >= 0, _rel_bucket_np(d_own), -1)
    idx_prev = _rel_bucket_np(MOBA_BLOCK + i - j)
    idx_b = np.concatenate([idx_prev, idx_own], axis=0).astype(np.int32)
    bias_b = _bias_expand(tab_b, tab_b[:, REL_BUCKETS - 1], idx_b)
    return bias_a, bias_b


def _block_plan(sizes, n_tok):
    n_assign = n_tok * EXPERT_TOPK
    padded = ((sizes + MOE_TM - 1) // MOE_TM) * MOE_TM
    eid = jnp.arange(N_EXPERTS, dtype=jnp.int32)
    padded_end = jnp.sum(jnp.where(eid[None, :] <= eid[:, None], padded[None, :], 0), axis=1)
    padded_start = padded_end - padded
    cap = -(-n_assign // MOE_TM) * MOE_TM + N_EXPERTS * MOE_TM
    blk_start = jnp.arange(cap // MOE_TM, dtype=jnp.int32) * MOE_TM
    blk_expert = jnp.minimum(
        jnp.sum(padded_end[None, :] <= blk_start[:, None], axis=1), N_EXPERTS - 1).astype(jnp.int32)
    mine = blk_expert[:, None] == eid[None, :]
    size_b = jnp.sum(jnp.where(mine, sizes[None, :], 0), axis=1)
    start_b = jnp.sum(jnp.where(mine, padded_start[None, :], 0), axis=1)
    blk_valid = jnp.clip(size_b - (blk_start - start_b), 0, MOE_TM)
    blk_valid = jnp.where(blk_start < padded_end[-1], blk_valid, 0).astype(jnp.int32)
    return padded_start, blk_expert, blk_valid, cap


def kernel(x, w_in, b_in, attn_sinks, rel_bias_table, w_branch_swa, w_branch_moba, w_out, ln1_gain, ln1_bias,
           w_group_router, b_group_router, w_expert_router, b_expert_router, w_expert_gate, w_expert_up,
           w_expert_down, ln2_gain, ln2_bias):
    assert w_in.shape[0] == DEPTH == 1
    B, S, D = x.shape
    T = B * S
    bf16 = jnp.bfloat16
    f32 = jnp.float32
    w = w_in[0]
    b = b_in[0]

    def cols(off, width):
        return w[:, off:off + width], b[off:off + width]

    wq_a, bq_a = cols(OFF_SWA_Q, SWA_Q_W)
    wk_a, bk_a = cols(OFF_SWA_K, SWA_KV_W)
    wv_a, bv_a = cols(OFF_SWA_V, SWA_KV_W)
    wq_b, bq_b = cols(OFF_MOBA_Q, MOBA_W)
    wk_b, bk_b = cols(OFF_MOBA_K, MOBA_W)
    wv_b, bv_b = cols(OFF_MOBA_V, MOBA_W)

    def dup_kv(t):
        parts = [t[..., i * HEAD_DIM:(i + 1) * HEAD_DIM] for i in range(SWA_KV_HEADS)]
        return jnp.concatenate([p for p in parts for _ in range(2)], axis=-1)

    qs = ATTN_SCALE * LOG2E
    wn = jnp.concatenate([wq_a * qs, dup_kv(wk_a), wq_b * qs, wk_b], axis=1).astype(bf16)
    bn = jnp.concatenate([bq_a * qs, dup_kv(bk_a), bq_b * qs, bk_b])[None, :].astype(f32)
    wt = jnp.concatenate([wv_a, wv_b], axis=1).T.astype(bf16)
    bt = jnp.concatenate([bv_a, bv_b])[:, None].astype(f32)

    qk, vt, eg_b, eu_b, ed_b = _inproj(x, wn, bn, wt, bt, w_expert_gate[0], w_expert_up[0], w_expert_down[0])

    bias_a, bias_b = _attention_bias_tables(rel_bias_table)
    y_a = _swa(attn_sinks[0].astype(f32), qk, vt, bias_a)
    y_b = _moba(qk, vt, bias_b)

    wg, bg = cols(OFF_GATE, 2 * D_MODEL)
    wr = jnp.zeros((ROUTER_ROWS, D), f32)
    wr = wr.at[0:N_GROUPS].set(w_group_router[0].T).at[8:8 + N_EXPERTS].set(w_expert_router[0].T)
    wr_hi = wr.astype(bf16)
    wr_lo = (wr - wr_hi.astype(f32)).astype(bf16)
    br = jnp.zeros((ROUTER_ROWS,), f32)
    br = br.at[0:N_GROUPS].set(b_group_router[0]).at[8:8 + N_EXPERTS].set(b_expert_router[0])[:, None]
    merge_args = (
        x.reshape(T, D), y_a.reshape(T, SWA_Q_W), y_b.reshape(T, MOBA_W),
        wg.astype(bf16), bg[None, :].astype(f32), w_branch_swa[0].astype(bf16), w_branch_moba[0].astype(bf16),
        w_out[0].astype(bf16), ln1_gain[0][None, :].astype(f32), ln1_bias[0][None, :].astype(f32),
        jnp.concatenate([wr_hi, wr_lo], axis=0), br)
    g2 = ln2_gain[0][None, :].astype(f32)
    b2 = ln2_bias[0][None, :].astype(f32)

    Tp = T // MOE_PARTS
    out = None
    for part in range(MOE_PARTS):
        x1, x1p, rinfo, counts = _merge(*merge_args, part, MOE_PARTS)
        sizes = counts[:, 0].astype(jnp.int32)
        padded_start, blk_expert, blk_valid, cap = _block_plan(sizes, Tp)
        dest = _dest(rinfo, padded_start)
        xs = _sc_scatter_rows(x1p, dest[0], dest[1], cap)
        y_buf = _experts(blk_expert, blk_valid, xs, eg_b, eu_b, ed_b)
        ypair = _sc_gather_rows(y_buf, dest[0:EXPERT_TOPK].reshape(-1))
        out = _final(x1, ypair, rinfo, g2, b2, out, part, MOE_PARTS)
    return out.reshape(B, S, D)
```

```python
import functools
import math

import numpy as np
import jax
import jax.numpy as jnp
from jax import lax
from jax.experimental import pallas as pl
from jax.experimental.pallas import tpu as pltpu
from jax.experimental.pallas import tpu_sc as plsc

D_MODEL = 1024
HEAD_DIM = 64
SWA_Q_HEADS = 8
SWA_KV_HEADS = 2
SWA_GROUP = SWA_Q_HEADS // SWA_KV_HEADS
SWA_WINDOW = 128
SWA_BLOCK = 128
MOBA_HEADS = 8
MOBA_BLOCK = 256
MOBA_TOPK = 3
MOBA_LOOKAHEAD = 2
BF16_SUBLANES = 16
MOBA_VROWS = HEAD_DIM + BF16_SUBLANES
REL_BUCKETS = 32
REL_MAX_DIST = 128
N_GROUPS = 4
EXPERTS_PER_GROUP = 8
N_EXPERTS = N_GROUPS * EXPERTS_PER_GROUP
EXPERT_TOPK = 2
D_EXPERT = 512
LN_EPS = 1e-5
DEPTH = 1
DEEPNORM_ALPHA = (2.0 * DEPTH) ** 0.25
NEG_INF = -1e30
ATTN_SCALE = HEAD_DIM ** -0.5
LOG2E = math.log2(math.e)

SWA_Q_W = SWA_Q_HEADS * HEAD_DIM
SWA_KV_W = SWA_KV_HEADS * HEAD_DIM
MOBA_W = MOBA_HEADS * HEAD_DIM
OFF_SWA_Q = 0
OFF_SWA_K = OFF_SWA_Q + SWA_Q_W
OFF_SWA_V = OFF_SWA_K + SWA_KV_W
OFF_MOBA_Q = OFF_SWA_V + SWA_KV_W
OFF_MOBA_K = OFF_MOBA_Q + MOBA_W
OFF_MOBA_V = OFF_MOBA_K + MOBA_W
OFF_GATE = OFF_MOBA_V + MOBA_W

LANES = 128
QK_SWA_Q = 0
QK_SWA_K = QK_SWA_Q + SWA_Q_W
QK_MOBA_Q = QK_SWA_K + SWA_KV_HEADS * LANES
QK_MOBA_K = QK_MOBA_Q + MOBA_W
QK_W = QK_MOBA_K + MOBA_W
VT_SWA = 0
VT_MOBA = VT_SWA + SWA_KV_W
VT_W = VT_MOBA + MOBA_W

MOE_TM = 512
MOE_PARTS = 2
SC_CHUNK = 128
ROUTER_ROWS = 8 + N_EXPERTS
VMEM_LIMIT = 56 * 1024 * 1024

_NT = (((1,), (1,)), ((), ()))
_TN = (((0,), (0,)), ((), ()))


def _rel_bucket_np(dist):
    n = np.maximum(dist, 0)
    max_exact = REL_BUCKETS // 2
    nf = np.maximum(n, 1).astype(np.float32)
    large = max_exact + (np.log(nf / np.float32(max_exact)) / np.float32(math.log(REL_MAX_DIST / max_exact))
                         * np.float32(REL_BUCKETS - max_exact)).astype(np.int32)
    large = np.minimum(large, REL_BUCKETS - 1)
    return np.where(n < max_exact, n, large).astype(np.int32)


def _inproj_kernel(x_ref, wn_ref, bn_ref, wt_ref, bt_ref, eg_ref, eu_ref, ed_ref,
                   qk_ref, vt_ref, egb_ref, eub_ref, edb_ref):
    xb = x_ref[0].astype(jnp.bfloat16)
    qk = jnp.dot(xb, wn_ref[...], preferred_element_type=jnp.float32) + bn_ref[...]
    qk_ref[0] = qk.astype(jnp.bfloat16)
    vt = lax.dot_general(wt_ref[...], xb, _NT, preferred_element_type=jnp.float32) + bt_ref[...]
    vt_ref[0] = vt.astype(jnp.bfloat16)
    egb_ref[...] = eg_ref[...].astype(jnp.bfloat16)
    eub_ref[...] = eu_ref[...].astype(jnp.bfloat16)
    edb_ref[...] = ed_ref[...].astype(jnp.bfloat16)


def _inproj(x, wn, bn, wt, bt, w_gate, w_up, w_down, tm=512):
    B, S, D = x.shape
    nt = S // tm
    steps = B * nt

    def sliced(w):
        E, R, C = w.shape
        assert (E * R) % (steps * BF16_SUBLANES) == 0
        return w.reshape(steps, E * R // steps, C)

    eg, eu, ed = sliced(w_gate), sliced(w_up), sliced(w_down)
    espec = lambda w: pl.BlockSpec((1,) + w.shape[1:], lambda b, i: (b * nt + i, 0, 0))
    qk, vt, egb, eub, edb = pl.pallas_call(
        _inproj_kernel,
        grid=(B, nt),
        in_specs=[
            pl.BlockSpec((1, tm, D), lambda b, i: (b, i, 0)),
            pl.BlockSpec((D, QK_W), lambda b, i: (0, 0)),
            pl.BlockSpec((1, QK_W), lambda b, i: (0, 0)),
            pl.BlockSpec((VT_W, D), lambda b, i: (0, 0)),
            pl.BlockSpec((VT_W, 1), lambda b, i: (0, 0)),
            espec(eg), espec(eu), espec(ed),
        ],
        out_specs=[
            pl.BlockSpec((1, tm, QK_W), lambda b, i: (b, i, 0)),
            pl.BlockSpec((1, VT_W, tm), lambda b, i: (b, 0, i)),
            espec(eg), espec(eu), espec(ed),
        ],
        out_shape=[
            jax.ShapeDtypeStruct((B, S, QK_W), jnp.bfloat16),
            jax.ShapeDtypeStruct((B, VT_W, S), jnp.bfloat16),
            jax.ShapeDtypeStruct(eg.shape, jnp.bfloat16),
            jax.ShapeDtypeStruct(eu.shape, jnp.bfloat16),
            jax.ShapeDtypeStruct(ed.shape, jnp.bfloat16),
        ],
        compiler_params=pltpu.CompilerParams(
            dimension_semantics=("parallel", "parallel"), vmem_limit_bytes=VMEM_LIMIT),
        name="inproj",
    )(x, wn, bn, wt, bt, eg, eu, ed)
    return qk, vt, egb.reshape(w_gate.shape), eub.reshape(w_up.shape), edb.reshape(w_down.shape)


def _swa_kernel(sink_ref, q_ref, k_ref, vt_ref, bias_ref, o_ref, vaug_ref, *s_bufs):
    S = q_ref.shape[1]
    nblk = S // SWA_BLOCK
    half = HEAD_DIM
    vt = vt_ref[0]
    rows = lax.broadcasted_iota(jnp.int32, vt.shape, 0)
    one = jnp.ones_like(vt)
    vaug_ref[0] = jnp.where(rows < half, vt, one)
    vaug_ref[1] = jnp.where(rows < half, one, vt)
    lane = lax.broadcasted_iota(jnp.int32, (SWA_BLOCK, LANES), 1)
    col = lax.broadcasted_iota(jnp.int32, (1, 2 * SWA_BLOCK), 1)
    npair = SWA_Q_HEADS // 2

    def expand(specs):
        return [(qs, ks, nk, boff, pair, (2 * pair) // SWA_GROUP) for qs, ks, nk, boff in specs for pair in range(npair)]

    def score(specs):
        ss = []
        for qs, ks, nk, boff, pair, kv in expand(specs):
            qblk = q_ref[0, pl.ds(qs, SWA_BLOCK), pair * LANES:(pair + 1) * LANES]
            zero = jnp.zeros_like(qblk)
            q2 = jnp.concatenate([jnp.where(lane < half, qblk, zero), jnp.where(lane >= half, qblk, zero)], axis=0)
            kblk = k_ref[0, pl.ds(ks, nk), kv * LANES:(kv + 1) * LANES]
            s = lax.dot_general(kblk, q2, _NT, preferred_element_type=jnp.float32)
            ss.append(s + bias_ref[pair, boff:boff + nk, :])
        return ss

    def finish(specs, ss):
        items = expand(specs)
        stats = []
        for (qs, ks, nk, boff, pair, kv), s in zip(items, ss):
            sink = jnp.where(col < SWA_BLOCK, sink_ref[2 * pair], sink_ref[2 * pair + 1]) * LOG2E
            m = jnp.maximum(_colmax(s), sink)
            stats.append((jnp.exp2(s - m).astype(jnp.bfloat16), jnp.exp2(sink - m)))
        rs = [jnp.dot(vaug_ref[kv, :, pl.ds(ks, nk)], p, preferred_element_type=jnp.float32)
              for (qs, ks, nk, boff, pair, kv), (p, _) in zip(items, stats)]
        outs = []
        for (qs, ks, nk, boff, pair, kv), (_, esink), r in zip(items, stats, rs):
            l = (r[half:half + 1, :] if kv == 0 else r[0:1, :]) + esink
            o = (r[0:half, :] if kv == 0 else r[half:, :]) / l
            outs.append(jnp.concatenate([o[:, :SWA_BLOCK], o[:, SWA_BLOCK:]], axis=0))
        for i, (qs, _, _, _) in enumerate(specs):
            ot = jnp.concatenate(outs[i * npair:(i + 1) * npair], axis=0)
            o_ref[0, :, pl.ds(qs, SWA_BLOCK)] = ot.astype(jnp.bfloat16)

    first = [(0, 0, SWA_BLOCK, SWA_BLOCK), (SWA_BLOCK, 0, 2 * SWA_BLOCK, 0)]
    finish(first, score(first))

    assert nblk % 2 == 0
    n_groups = nblk // 2 - 1
    nbuf = len(s_bufs)

    def group_specs(g):
        specs = []
        for n in (2 * g + 2, 2 * g + 3):
            qs = pl.multiple_of(n * SWA_BLOCK, SWA_BLOCK)
            specs.append((qs, pl.multiple_of(qs - SWA_BLOCK, SWA_BLOCK), 2 * SWA_BLOCK, 0))
        return specs

    def score_to(g, buf):
        for k, s in enumerate(score(group_specs(g))):
            buf[k] = s

    def finish_from(g, buf):
        finish(group_specs(g), [buf[k] for k in range(2 * npair)])

    for g in range(min(2, n_groups)):
        score_to(g, s_bufs[g])

    def body(t, carry):
        for j in range(nbuf):
            score_to(nbuf * t + 2 + j, s_bufs[(2 + j) % nbuf])
            finish_from(nbuf * t + j, s_bufs[j])
        return carry

    n_trips = max(n_groups - 2, 0) // nbuf
    lax.fori_loop(0, n_trips, body, 0)
    for g in range(nbuf * n_trips, n_groups):
        if g + 2 < n_groups:
            score_to(g + 2, s_bufs[(g + 2) % nbuf])
        finish_from(g, s_bufs[g % nbuf])


def _swa(sinks, qk, vt, bias):
    B, S, _ = qk.shape
    return pl.pallas_call(
        _swa_kernel,
        grid_spec=pltpu.PrefetchScalarGridSpec(
            num_scalar_prefetch=1,
            grid=(B,),
            in_specs=[
                pl.BlockSpec((1, S, SWA_Q_W), lambda b, s: (b, 0, QK_SWA_Q // SWA_Q_W)),
                pl.BlockSpec((1, S, 2 * LANES), lambda b, s: (b, 0, QK_SWA_K // (2 * LANES))),
                pl.BlockSpec((1, SWA_KV_W, S), lambda b, s: (b, VT_SWA // SWA_KV_W, 0)),
                pl.BlockSpec((SWA_Q_HEADS // 2, 2 * SWA_BLOCK, 2 * SWA_BLOCK), lambda b, s: (0, 0, 0)),
            ],
            out_specs=pl.BlockSpec((1, SWA_Q_W, S), lambda b, s: (b, 0, 0)),
            scratch_shapes=[pltpu.VMEM((SWA_KV_HEADS, SWA_KV_W, S), jnp.bfloat16)]
            + [pltpu.VMEM((SWA_Q_HEADS, 2 * SWA_BLOCK, 2 * SWA_BLOCK), jnp.float32)] * 4,
        ),
        out_shape=jax.ShapeDtypeStruct((B, SWA_Q_W, S), jnp.bfloat16),
        compiler_params=pltpu.CompilerParams(
            dimension_semantics=("parallel",), vmem_limit_bytes=VMEM_LIMIT),
        name="swa",
    )(sinks, qk, qk, vt, bias)


def _colmax(s):
    while s.shape[0] > 8:
        h = s.shape[0] // 2
        s = jnp.maximum(s[:h], s[h:])
    return jnp.max(s, axis=0, keepdims=True)


def _moba_kernel(fqb_ref, fc_ref, q_ref, k_ref, vt_ref, bias_ref, o_ref,
                 vaug_ref, gate_ref, rank_ref, sel_ref, far_ref, m_scr, acc_scr, sa_scr, sb_scr, sc_scr, sd_scr):
    S = q_ref.shape[1]
    nb = S // MOBA_BLOCK
    half = HEAD_DIM
    BLK = MOBA_BLOCK

    ones_rows = jnp.ones((MOBA_VROWS - half, S), jnp.bfloat16)
    for h in range(2):
        vaug_ref[h, 0:half, :] = vt_ref[0, h * half:(h + 1) * half, :]
        vaug_ref[h, half:, :] = ones_rows

    kf = k_ref[0].astype(jnp.float32).reshape(nb, BLK, LANES)
    kmean = jnp.sum(kf, axis=1) * (1.0 / BLK)
    k_hi = kmean.astype(jnp.bfloat16)
    k_lo = (kmean - k_hi.astype(jnp.float32)).astype(jnp.bfloat16)
    kcat = jnp.concatenate([k_hi, k_lo], axis=0)

    lane_q = lax.broadcasted_iota(jnp.int32, (S, LANES), 1)
    brow = lax.broadcasted_iota(jnp.int32, (nb, S), 0)
    qblk_of = lax.broadcasted_iota(jnp.int32, (nb, S), 1) // BLK
    qall = q_ref[0]
    for h in range(2):
        qh = jnp.where((lane_q < half) if h == 0 else (lane_q >= half), qall, jnp.zeros_like(qall))
        g2 = lax.dot_general(kcat, qh, _NT, preferred_element_type=jnp.float32)
        gate_ref[...] = g2[0:nb] + g2[nb:2 * nb]
        rank_ref[...] = jnp.zeros_like(rank_ref)
        for m in range(nb - 1):
            lo = (m + 1) * BLK
            G = gate_ref[:, lo:]
            gm = gate_ref[m:m + 1, lo:]
            ge = jnp.where(gm >= G, 1.0, 0.0)
            gt = jnp.where(gm > G, 1.0, 0.0)
            brow_m = lax.broadcasted_iota(jnp.int32, (nb, S - lo), 0)
            rank_ref[:, lo:] += jnp.where(brow_m > m, ge, gt)
        top = rank_ref[...] < float(MOBA_TOPK)
        sel_ref[h] = jnp.where((brow < qblk_of) & top, 0.0, NEG_INF)
        far_ref[h] = jnp.where((brow < qblk_of - 1) & top, 0.0, NEG_INF)

    lane = lax.broadcasted_iota(jnp.int32, (BLK, LANES), 1)

    def head_q(qs, h):
        qblk = q_ref[0, pl.ds(qs, BLK), :]
        return jnp.where((lane < half) if h == 0 else (lane >= half), qblk, jnp.zeros_like(qblk))

    m0 = jnp.full((1, BLK), NEG_INF, jnp.float32)
    acc0 = jnp.zeros((MOBA_VROWS, BLK), jnp.float32)

    def item(h, qb, qs, ks, nk, bias, sel_a, sel_b, m, acc):
        return dict(h=h, qb=qb, qs=qs, ks=ks, nk=nk, bias=bias, sel_a=sel_a, sel_b=sel_b, m=m, acc=acc)

    def near_item(qb, h):
        qs = pl.multiple_of(qb * BLK, BLK)
        ps = pl.multiple_of(qs - BLK, BLK)
        sel_prev = sel_ref[h, pl.ds(qb - 1, 1), pl.ds(qs, BLK)]
        return item(h, qb, qs, ps, 2 * BLK, bias_ref[h], sel_prev, None, m0, acc0)

    def far_item(i, h):
        qb = fqb_ref[i]
        c = fc_ref[i]
        qs = pl.multiple_of(qb * BLK, BLK)
        ks = pl.multiple_of(c * (2 * BLK), 2 * BLK)
        sel_a = far_ref[h, pl.ds(2 * c, 1), pl.ds(qs, BLK)]
        sel_b = far_ref[h, pl.ds(2 * c + 1, 1), pl.ds(qs, BLK)]
        return item(h, qb, qs, ks, 2 * BLK, None, sel_a, sel_b, m_scr[h, qb], acc_scr[h, qb])

    def scores(it):
        kslab = k_ref[0, pl.ds(it["ks"], it["nk"]), :]
        s = lax.dot_general(kslab, head_q(it["qs"], it["h"]), _NT,
                            preferred_element_type=jnp.float32)
        return s if it["bias"] is None else s + it["bias"]

    def probs(it, s):
        m, sel_a, sel_b = it["m"], it["sel_a"], it["sel_b"]
        if it["nk"] == BLK:
            m_new = jnp.maximum(m, _colmax(s))
            p = jnp.exp2(s - m_new)
        else:
            cm_a = _colmax(s[:BLK]) + sel_a
            cm_b = _colmax(s[BLK:])
            if sel_b is not None:
                cm_b = cm_b + sel_b
            m_new = jnp.maximum(m, jnp.maximum(cm_a, cm_b))
            p_a = jnp.exp2(s[:BLK] - (m_new - sel_a))
            p_b = jnp.exp2(s[BLK:] - (m_new if sel_b is None else m_new - sel_b))
            p = jnp.concatenate([p_a, p_b], axis=0)
        return m_new, jnp.exp2(m - m_new), p.astype(jnp.bfloat16)

    def run(items):
        ss = [scores(it) for it in items]
        ps = [probs(it, s) for it, s in zip(items, ss)]
        pvs = [jnp.dot(vaug_ref[it["h"], :, pl.ds(it["ks"], it["nk"])], p, preferred_element_type=jnp.float32)
               for it, (_, _, p) in zip(items, ps)]
        for it, (m_new, alpha, _), pv in zip(items, ps, pvs):
            m_scr[it["h"], it["qb"]] = m_new
            acc_scr[it["h"], it["qb"]] = it["acc"] * alpha + pv

    run([item(h, 0, 0, 0, BLK, bias_ref[h, BLK:, :], None, None, m0, acc0) for h in range(2)]
        + [near_item(nb - 1, h) for h in range(2)])

    bufs = (sa_scr, sb_scr, sc_scr, sd_scr)
    nbuf = len(bufs)

    def finish(items, s_buf):
        ps = [probs(it, s_buf[k]) for k, it in enumerate(items)]
        pvs = [jnp.dot(vaug_ref[it["h"], :, pl.ds(it["ks"], it["nk"])], p, preferred_element_type=jnp.float32)
               for it, (_, _, p) in zip(items, ps)]
        for it, (m_new, alpha, _), pv in zip(items, ps, pvs):
            m_scr[it["h"], it["qb"]] = m_new
            acc_scr[it["h"], it["qb"]] = it["acc"] * alpha + pv

    def pipelined(n_groups, score, items_of):
        for g in range(min(MOBA_LOOKAHEAD, n_groups)):
            score(g, bufs[g])

        def body(t, carry):
            for j in range(nbuf):
                score(nbuf * t + MOBA_LOOKAHEAD + j, bufs[(MOBA_LOOKAHEAD + j) % nbuf])
                finish(items_of(nbuf * t + j), bufs[j])
            return carry

        n_trips = max(n_groups - MOBA_LOOKAHEAD, 0) // nbuf
        lax.fori_loop(0, n_trips, body, 0)
        for g in range(nbuf * n_trips, n_groups):
            if g + MOBA_LOOKAHEAD < n_groups:
                score(g + MOBA_LOOKAHEAD, bufs[(g + MOBA_LOOKAHEAD) % nbuf])
            finish(items_of(g), bufs[g % nbuf])

    near_pairs = nb // 2 - 1

    def near_group(g):
        return [(qb, h) for qb in (g + 1, g + 1 + near_pairs) for h in range(2)]

    def near_score(g, s_buf):
        for k, (qb, h) in enumerate(near_group(g)):
            qs = pl.multiple_of(qb * BLK, BLK)
            ps = pl.multiple_of(qs - BLK, BLK)
            s_buf[k] = lax.dot_general(k_ref[0, pl.ds(ps, 2 * BLK), :], head_q(qs, h), _NT,
                                       preferred_element_type=jnp.float32) + bias_ref[h]

    pipelined(near_pairs, near_score, lambda g: [near_item(qb, h) for qb, h in near_group(g)])

    n_far_groups = fqb_ref.shape[0] // 2

    def far_group(g):
        return [(j, h) for j in (g, g + n_far_groups) for h in range(2)]

    def far_score(g, s_buf):
        for k, (j, h) in enumerate(far_group(g)):
            qs = pl.multiple_of(fqb_ref[j] * BLK, BLK)
            ks = pl.multiple_of(fc_ref[j] * (2 * BLK), 2 * BLK)
            s_buf[k] = lax.dot_general(k_ref[0, pl.ds(ks, 2 * BLK), :], head_q(qs, h), _NT,
                                       preferred_element_type=jnp.float32)

    pipelined(n_far_groups, far_score, lambda g: [far_item(j, h) for j, h in far_group(g)])

    def out_body(t, carry):
        for qb in (2 * t, 2 * t + 1):
            qs = pl.multiple_of(qb * BLK, BLK)
            a0 = acc_scr[0, qb]
            a1 = acc_scr[1, qb]
            ot = jnp.concatenate([a0[0:half] / a0[half:half + 1, :], a1[0:half] / a1[half:half + 1, :]],
                                 axis=0)
            o_ref[0, :, pl.ds(qs, BLK)] = ot.astype(jnp.bfloat16)
        return carry

    lax.fori_loop(0, nb // 2, out_body, 0)


def _moba_far_items(nb):
    items = [(qb, c) for qb in range(2, nb) for c in range(qb // 2)]
    n = len(items)
    assert n % 2 == 0 and all(items[i][0] != items[i + n // 2][0] for i in range(n // 2))
    return np.array([it[0] for it in items], np.int32), np.array([it[1] for it in items], np.int32)


def _moba(qk, vt, bias):
    B, S, _ = qk.shape
    nb = S // MOBA_BLOCK
    npair = MOBA_W // LANES
    far_qb, far_c = _moba_far_items(nb)
    return pl.pallas_call(
        _moba_kernel,
        grid_spec=pltpu.PrefetchScalarGridSpec(
            num_scalar_prefetch=2,
            grid=(B, npair),
            in_specs=[
                pl.BlockSpec((1, S, LANES), lambda b, p, fq, fc: (b, 0, QK_MOBA_Q // LANES + p)),
                pl.BlockSpec((1, S, LANES), lambda b, p, fq, fc: (b, 0, QK_MOBA_K // LANES + p)),
                pl.BlockSpec((1, LANES, S), lambda b, p, fq, fc: (b, VT_MOBA // LANES + p, 0)),
                pl.BlockSpec((2, 2 * MOBA_BLOCK, MOBA_BLOCK), lambda b, p, fq, fc: (p, 0, 0)),
            ],
            out_specs=pl.BlockSpec((1, LANES, S), lambda b, p, fq, fc: (b, p, 0)),
            scratch_shapes=[
                pltpu.VMEM((2, MOBA_VROWS, S), jnp.bfloat16),
                pltpu.VMEM((nb, S), jnp.float32),
                pltpu.VMEM((nb, S), jnp.float32),
                pltpu.VMEM((2, nb, S), jnp.float32),
                pltpu.VMEM((2, nb, S), jnp.float32),
                pltpu.VMEM((2, nb, 1, MOBA_BLOCK), jnp.float32),
                pltpu.VMEM((2, nb, MOBA_VROWS, MOBA_BLOCK), jnp.float32),
            ] + [pltpu.VMEM((4, 2 * MOBA_BLOCK, MOBA_BLOCK), jnp.float32)] * 4,
        ),
        out_shape=jax.ShapeDtypeStruct((B, MOBA_W, S), jnp.bfloat16),
        compiler_params=pltpu.CompilerParams(
            dimension_semantics=("parallel", "parallel"), vmem_limit_bytes=VMEM_LIMIT),
        name="moba",
    )(jnp.asarray(far_qb), jnp.asarray(far_c), qk, qk, vt, bias)


def _layer_norm(h, gain, bias):
    mu = jnp.mean(h, axis=-1, keepdims=True)
    c = h - mu
    var = jnp.mean(c * c, axis=-1, keepdims=True)
    return c * lax.rsqrt(var + LN_EPS) * gain + bias


def _pack_bf16_pair(a, b):
    ia = lax.bitcast_convert_type(a.astype(jnp.bfloat16).astype(jnp.float32), jnp.int32)
    ib = lax.bitcast_convert_type(b.astype(jnp.bfloat16).astype(jnp.float32), jnp.int32)
    return lax.shift_right_logical(ia, 16) | ib


def _unpack_bf16_pair(w):
    lo = lax.bitcast_convert_type(lax.shift_left(w, 16), jnp.float32)
    hi = lax.bitcast_convert_type(w & jnp.int32(-65536), jnp.float32)
    return lo.astype(jnp.bfloat16), hi.astype(jnp.bfloat16)


def _merge_kernel(x_ref, ya_ref, yb_ref, wg_ref, bg_ref, wa_ref, wb_ref, wo_ref, g1_ref, b1_ref,
                  wr_ref, br_ref, tri_ref, x1_ref, x1p_ref, r_ref, cnt_ref):
    @pl.when(pl.program_id(0) == 0)
    def _():
        cnt_ref[...] = jnp.zeros_like(cnt_ref)

    ts = tri_ref.shape[0]
    subs = [pl.ds(r0, ts) for r0 in range(0, x_ref.shape[0], ts)]
    pre = [_merge_matmuls(x_ref[rows, :], ya_ref[0, :, rows], yb_ref[0, :, rows], wg_ref, bg_ref, wa_ref, wb_ref, wo_ref)
           for rows in subs]
    for rows, h in zip(subs, pre):
        _merge_route(h, rows, g1_ref, b1_ref, wr_ref, br_ref, tri_ref, x1_ref, x1p_ref, r_ref, cnt_ref)


def _merge_matmuls(x, ya_t, yb_t, wg_ref, bg_ref, wa_ref, wb_ref, wo_ref):
    xb = x.astype(jnp.bfloat16)
    z = jnp.dot(xb, wg_ref[...], preferred_element_type=jnp.float32) + bg_ref[...]
    gates = 1.0 / (1.0 + jnp.exp(-z))
    pa = lax.dot_general(ya_t, wa_ref[...], _TN, preferred_element_type=jnp.float32)
    pb = lax.dot_general(yb_t, wb_ref[...], _TN, preferred_element_type=jnp.float32)
    merged = gates[:, :D_MODEL] * pa + gates[:, D_MODEL:] * pb
    mixed = jnp.dot(merged.astype(jnp.bfloat16), wo_ref[...], preferred_element_type=jnp.float32)
    return DEEPNORM_ALPHA * x + mixed


def _merge_route(h, rows, g1_ref, b1_ref, wr_ref, br_ref, tri_ref, x1_ref, x1p_ref, r_ref, cnt_ref):
    x1 = _layer_norm(h, g1_ref[...], b1_ref[...])
    x1_ref[rows, :] = x1
    x1_hi = x1.astype(jnp.bfloat16)
    x1p_ref[rows, :] = _pack_bf16_pair(x1[:, :D_MODEL // 2], x1[:, D_MODEL // 2:])
    x1_lo = (x1 - x1_hi.astype(jnp.float32)).astype(jnp.bfloat16)

    R = ROUTER_ROWS
    l1 = lax.dot_general(wr_ref[...], x1_hi, _NT, preferred_element_type=jnp.float32)
    l2 = lax.dot_general(wr_ref[0:R, :], x1_lo, _NT, preferred_element_type=jnp.float32)
    L = l1[0:R] + l1[R:2 * R] + l2 + br_ref[...]
    tm = x1.shape[0]
    row = lax.broadcasted_iota(jnp.int32, (8, tm), 0)
    big = jnp.float32(-3e38)
    gl = jnp.where(row < N_GROUPS, L[0:8], big)
    gmax = jnp.max(gl, axis=0, keepdims=True)
    g_idx = jnp.min(jnp.where(gl == gmax, row, 8), axis=0, keepdims=True)
    gsum = jnp.sum(jnp.where(row < N_GROUPS, jnp.exp(gl - gmax), 0.0), axis=0, keepdims=True)
    g_prob = 1.0 / gsum
    E = L[8 + 8 * (N_GROUPS - 1):8 + 8 * N_GROUPS]
    for g in range(N_GROUPS - 2, -1, -1):
        E = jnp.where(g_idx == g, L[8 + 8 * g:16 + 8 * g], E)
    t0 = jnp.max(E, axis=0, keepdims=True)
    loc0 = jnp.min(jnp.where(E == t0, row, 8), axis=0, keepdims=True)
    E2 = jnp.where(row == loc0, big, E)
    t1 = jnp.max(E2, axis=0, keepdims=True)
    loc1 = jnp.min(jnp.where(E2 == t1, row, 8), axis=0, keepdims=True)
    ex = jnp.exp(t1 - t0)
    w0 = g_prob / (1.0 + ex)
    w1 = g_prob * ex / (1.0 + ex)
    e0i = g_idx * EXPERTS_PER_GROUP + loc0
    e1i = g_idx * EXPERTS_PER_GROUP + loc1

    erow = lax.broadcasted_iota(jnp.int32, (N_EXPERTS, tm), 0)
    oh0 = jnp.where(erow == e0i, 1.0, 0.0)
    oh1 = jnp.where(erow == e1i, 1.0, 0.0)
    both = oh0 + oh1
    prefix = jnp.dot(both.astype(jnp.bfloat16), tri_ref[...], preferred_element_type=jnp.float32)
    prefix = prefix + cnt_ref[:, 0:1]
    rank0 = jnp.sum(oh0 * prefix, axis=0, keepdims=True)
    rank1 = jnp.sum(oh1 * prefix, axis=0, keepdims=True)
    cnt_ref[...] = cnt_ref[...] + jnp.sum(both, axis=1, keepdims=True)

    vals = (e0i.astype(jnp.float32), e1i.astype(jnp.float32), w0, w1, rank0, rank1)
    out = jnp.zeros((8, tm), jnp.float32)
    for k, v in enumerate(vals):
        out = jnp.where(row == k, v, out)
    r_ref[:, rows] = out


def _merge(x2, ya, yb, wg, bg, wa, wb, wo, g1, b1, wr, br, part, n_parts, tm=512, ts=256):
    D = x2.shape[1]
    T = x2.shape[0] // n_parts
    S = ya.shape[2]
    per_seq = S // tm
    assert per_seq * tm == S
    off = part * (T // tm)
    const = lambda i: (0, 0)
    seq_tile = lambda i: ((i + off) // per_seq, 0, (i + off) % per_seq)
    tri = jnp.triu(jnp.ones((ts, ts), jnp.bfloat16), k=1)
    return pl.pallas_call(
        _merge_kernel,
        grid=(T // tm,),
        in_specs=[
            pl.BlockSpec((tm, D), lambda i: (i + off, 0)),
            pl.BlockSpec((1, SWA_Q_W, tm), seq_tile),
            pl.BlockSpec((1, MOBA_W, tm), seq_tile),
            pl.BlockSpec((D, 2 * D), const),
            pl.BlockSpec((1, 2 * D), const),
            pl.BlockSpec((SWA_Q_W, D), const),
            pl.BlockSpec((MOBA_W, D), const),
            pl.BlockSpec((D, D), const),
            pl.BlockSpec((1, D), const),
            pl.BlockSpec((1, D), const),
            pl.BlockSpec((2 * ROUTER_ROWS, D), const),
            pl.BlockSpec((ROUTER_ROWS, 1), const),
            pl.BlockSpec((ts, ts), const),
        ],
        out_specs=[
            pl.BlockSpec((tm, D), lambda i: (i, 0)),
            pl.BlockSpec((tm, D // 2), lambda i: (i, 0)),
            pl.BlockSpec((8, tm), lambda i: (0, i)),
            pl.BlockSpec((N_EXPERTS, LANES), const),
        ],
        out_shape=[
            jax.ShapeDtypeStruct((T, D), jnp.float32),
            jax.ShapeDtypeStruct((T, D // 2), jnp.int32),
            jax.ShapeDtypeStruct((8, T), jnp.float32),
            jax.ShapeDtypeStruct((N_EXPERTS, LANES), jnp.float32),
        ],
        compiler_params=pltpu.CompilerParams(
            dimension_semantics=("arbitrary",), vmem_limit_bytes=VMEM_LIMIT),
        name="merge",
    )(x2, ya, yb, wg, bg, wa, wb, wo, g1, b1, wr, br, tri)


def _dest_kernel(r_ref, ps_ref, d_ref):
    tt = r_ref.shape[1]
    erow = lax.broadcasted_iota(jnp.int32, (N_EXPERTS, tt), 0)
    row = lax.broadcasted_iota(jnp.int32, (8, tt), 0)
    ps = ps_ref[...]
    out = jnp.zeros((8, tt), jnp.float32)
    for k in range(EXPERT_TOPK):
        e = r_ref[k:k + 1, :].astype(jnp.int32)
        start = jnp.sum(jnp.where(erow == e, ps, 0.0), axis=0, keepdims=True)
        out = jnp.where(row == k, start + r_ref[4 + k:5 + k, :], out)
    d_ref[...] = out.astype(jnp.int32)


def _dest(rinfo, padded_start, tt=8192):
    T = rinfo.shape[1]
    tt = min(tt, T)
    return pl.pallas_call(
        _dest_kernel,
        grid=(T // tt,),
        in_specs=[pl.BlockSpec((8, tt), lambda i: (0, i)), pl.BlockSpec((N_EXPERTS, 1), lambda i: (0, 0))],
        out_specs=pl.BlockSpec((8, tt), lambda i: (0, i)),
        out_shape=jax.ShapeDtypeStruct((8, T), jnp.int32),
        compiler_params=pltpu.CompilerParams(dimension_semantics=("parallel",)),
        name="dest",
    )(rinfo, padded_start.astype(jnp.float32)[:, None])


def _sc_workers():
    info = plsc.get_sparse_core_info()
    return info.num_cores, info.num_subcores


def _sc_scatter_rows(src, dest0, dest1, cap):
    T, W = src.shape
    nc, ns = _sc_workers()
    per_w = T // (nc * ns)
    assert per_w * nc * ns == T and per_w % SC_CHUNK == 0
    mesh = plsc.VectorSubcoreMesh(core_axis_name="c", subcore_axis_name="s")

    @functools.partial(
        pl.kernel, mesh=mesh,
        out_type=jax.ShapeDtypeStruct((cap, W), src.dtype),
        scratch_types=[pltpu.VMEM((SC_CHUNK,), jnp.int32), pltpu.VMEM((SC_CHUNK, W), src.dtype)],
    )
    def scatter(src_hbm, d0_hbm, d1_hbm, out_hbm, idx_v, rows_v):
        wid = lax.axis_index("s") * nc + lax.axis_index("c")
        base = wid * per_w

        @pl.loop(0, per_w // SC_CHUNK)
        def _(c):
            off = pl.multiple_of(base + c * SC_CHUNK, SC_CHUNK)
            pltpu.sync_copy(src_hbm.at[pl.ds(off, SC_CHUNK)], rows_v)
            for d_hbm in (d0_hbm, d1_hbm):
                pltpu.sync_copy(d_hbm.at[pl.ds(off, SC_CHUNK)], idx_v)
                pltpu.sync_copy(rows_v, out_hbm.at[idx_v])

    return scatter(src, dest0, dest1)


def _sc_gather_rows(table, idx):
    N = idx.shape[0]
    W = table.shape[1]
    nc, ns = _sc_workers()
    per_w = N // (nc * ns)
    assert per_w * nc * ns == N and per_w % SC_CHUNK == 0
    mesh = plsc.VectorSubcoreMesh(core_axis_name="c", subcore_axis_name="s")

    @functools.partial(
        pl.kernel, mesh=mesh,
        out_type=jax.ShapeDtypeStruct((N, W), table.dtype),
        scratch_types=[pltpu.VMEM((SC_CHUNK,), jnp.int32), pltpu.VMEM((SC_CHUNK, W), table.dtype)],
    )
    def gather(table_hbm, idx_hbm, out_hbm, idx_v, rows_v):
        wid = lax.axis_index("s") * nc + lax.axis_index("c")
        base = wid * per_w

        @pl.loop(0, per_w // SC_CHUNK)
        def _(c):
            off = pl.multiple_of(base + c * SC_CHUNK, SC_CHUNK)
            pltpu.sync_copy(idx_hbm.at[pl.ds(off, SC_CHUNK)], idx_v)
            pltpu.sync_copy(table_hbm.at[idx_v], rows_v)
            pltpu.sync_copy(rows_v, out_hbm.at[pl.ds(off, SC_CHUNK)])

    return gather(table, idx)


def _expert_kernel(be_ref, nv_ref, x_ref, wg_ref, wu_ref, wd_ref, y_ref):
    i = pl.program_id(0)
    nv = nv_ref[i]

    @pl.when(nv > 0)
    def _():
        lo, hi = _unpack_bf16_pair(x_ref[...])
        xb = jnp.concatenate([lo, hi], axis=1)
        rows = lax.broadcasted_iota(jnp.int32, xb.shape, 0)
        xb = jnp.where(rows < nv, xb, jnp.zeros_like(xb))
        g = jnp.dot(xb, wg_ref[0], preferred_element_type=jnp.float32)
        u = jnp.dot(xb, wu_ref[0], preferred_element_type=jnp.float32)
        act = (g / (1.0 + jnp.exp(-g))) * u
        y = jnp.dot(act.astype(jnp.bfloat16), wd_ref[0], preferred_element_type=jnp.float32)
        y_ref[...] = _pack_bf16_pair(y[:, :D_MODEL // 2], y[:, D_MODEL // 2:])

    @pl.when(nv <= 0)
    def _():
        y_ref[...] = jnp.zeros_like(y_ref)


def _experts(blk_expert, blk_valid, xs, wg, wu, wd):
    cap, DW = xs.shape
    D = 2 * DW
    n_blocks = cap // MOE_TM
    return pl.pallas_call(
        _expert_kernel,
        grid_spec=pltpu.PrefetchScalarGridSpec(
            num_scalar_prefetch=2,
            grid=(n_blocks,),
            in_specs=[
                pl.BlockSpec((MOE_TM, DW), lambda i, be, nv: (i, 0)),
                pl.BlockSpec((1, D, D_EXPERT), lambda i, be, nv: (be[i], 0, 0)),
                pl.BlockSpec((1, D, D_EXPERT), lambda i, be, nv: (be[i], 0, 0)),
                pl.BlockSpec((1, D_EXPERT, D), lambda i, be, nv: (be[i], 0, 0)),
            ],
            out_specs=pl.BlockSpec((MOE_TM, DW), lambda i, be, nv: (i, 0)),
        ),
        out_shape=jax.ShapeDtypeStruct((cap, DW), jnp.int32),
        compiler_params=pltpu.CompilerParams(
            dimension_semantics=("arbitrary",), vmem_limit_bytes=VMEM_LIMIT),
        name="experts",
    )(blk_expert, blk_valid, xs, wg, wu, wd)


def _final_kernel(x1_ref, y0_ref, y1_ref, w_ref, g2_ref, b2_ref, *rest):
    o_ref = rest[-1]
    w = w_ref[...].T
    halves = []
    for part in range(2):
        y0 = _unpack_bf16_pair(y0_ref[...])[part].astype(jnp.float32)
        y1 = _unpack_bf16_pair(y1_ref[...])[part].astype(jnp.float32)
        halves.append(y0 * w[:, 2:3] + y1 * w[:, 3:4])
    moe = jnp.concatenate(halves, axis=1)
    o_ref[...] = _layer_norm(DEEPNORM_ALPHA * x1_ref[...] + moe, g2_ref[...], b2_ref[...])


def _final(x1, ypair, rinfo, g2, b2, out_prev, part, n_parts, tm=1024):
    T, D = x1.shape
    const = lambda i: (0, 0)
    nt = T // tm
    off = part * nt
    in_specs = [
        pl.BlockSpec((tm, D), lambda i: (i, 0)),
        pl.BlockSpec((tm, D // 2), lambda i: (i, 0)),
        pl.BlockSpec((tm, D // 2), lambda i: (i + nt, 0)),
        pl.BlockSpec((8, tm), lambda i: (0, i)),
        pl.BlockSpec((1, D), const),
        pl.BlockSpec((1, D), const),
    ]
    args = [x1, ypair, ypair, rinfo, g2, b2]
    aliases = {}
    if out_prev is not None:
        in_specs.append(pl.BlockSpec(memory_space=pl.ANY))
        args.append(out_prev)
        aliases = {len(args) - 1: 0}
    return pl.pallas_call(
        _final_kernel,
        grid=(nt,),
        in_specs=in_specs,
        out_specs=pl.BlockSpec((tm, D), lambda i: (i + off, 0)),
        out_shape=jax.ShapeDtypeStruct((n_parts * T, D), jnp.float32),
        input_output_aliases=aliases,
        compiler_params=pltpu.CompilerParams(
            dimension_semantics=("parallel",), vmem_limit_bytes=VMEM_LIMIT),
        name="final",
    )(*args)


def _bias_kernel(tab_ref, sub_ref, idx_ref, o_ref):
    h = pl.program_id(0)
    idx = idx_ref[...]
    acc = jnp.full(idx.shape, NEG_INF, jnp.float32)
    for b in range(REL_BUCKETS):
        acc = jnp.where(idx == b, (tab_ref[h, b] - sub_ref[h]) * LOG2E, acc)
    o_ref[0] = acc


def _bias_expand(table_hb, sub_h, idx):
    H = table_hb.shape[0]
    K, Q = idx.shape
    return pl.pallas_call(
        _bias_kernel,
        grid_spec=pltpu.PrefetchScalarGridSpec(
            num_scalar_prefetch=2,
            grid=(H,),
            in_specs=[pl.BlockSpec((K, Q), lambda h, t, s: (0, 0))],
            out_specs=pl.BlockSpec((1, K, Q), lambda h, t, s: (h, 0, 0)),
        ),
        out_shape=jax.ShapeDtypeStruct((H, K, Q), jnp.float32),
        compiler_params=pltpu.CompilerParams(dimension_semantics=("parallel",)),
        name="bias",
    )(table_hb, sub_h, jnp.asarray(idx))


def _attention_bias_tables(rel_bias_table):
    rel = rel_bias_table.astype(jnp.float32)
    tab_a = rel[:, :SWA_Q_HEADS].T
    tab_b = rel[:, SWA_Q_HEADS:].T
    kj = np.arange(2 * SWA_BLOCK)[:, None]
    qi = np.arange(SWA_BLOCK)[None, :]
    dist = SWA_BLOCK + qi - kj
    idx_a = np.where((dist >= 0) & (dist < SWA_WINDOW), _rel_bucket_np(dist), -1).astype(np.int32)
    bias_a = _bias_expand(tab_a, jnp.zeros((SWA_Q_HEADS,), jnp.float32), idx_a)
    bias_a = bias_a.reshape(SWA_Q_HEADS // 2, 2, 2 * SWA_BLOCK, SWA_BLOCK).transpose(0, 2, 1, 3)
    bias_a = bias_a.reshape(SWA_Q_HEADS // 2, 2 * SWA_BLOCK, 2 * SWA_BLOCK)
    j = np.arange(MOBA_BLOCK)[:, None]
    i = np.arange(MOBA_BLOCK)[None, :]
    d_own = i - j
    idx_own = np.where(d_own >= 0, _rel_bucket_np(d_own), -1)
    idx_prev = _rel_bucket_np(MOBA_BLOCK + i - j)
    idx_b = np.concatenate([idx_prev, idx_own], axis=0).astype(np.int32)
    bias_b = _bias_expand(tab_b, tab_b[:, REL_BUCKETS - 1], idx_b)
    return bias_a, bias_b


def _block_plan(sizes, n_tok):
    n_assign = n_tok * EXPERT_TOPK
    padded = ((sizes + MOE_TM - 1) // MOE_TM) * MOE_TM
    eid = jnp.arange(N_EXPERTS, dtype=jnp.int32)
    padded_end = jnp.sum(jnp.where(eid[None, :] <= eid[:, None], padded[None, :], 0), axis=1)
    padded_start = padded_end - padded
    cap = -(-n_assign // MOE_TM) * MOE_TM + N_EXPERTS * MOE_TM
    blk_start = jnp.arange(cap // MOE_TM, dtype=jnp.int32) * MOE_TM
    blk_expert = jnp.minimum(
        jnp.sum(padded_end[None, :] <= blk_start[:, None], axis=1), N_EXPERTS - 1).astype(jnp.int32)
    mine = blk_expert[:, None] == eid[None, :]
    size_b = jnp.sum(jnp.where(mine, sizes[None, :], 0), axis=1)
    start_b = jnp.sum(jnp.where(mine, padded_start[None, :], 0), axis=1)
    blk_valid = jnp.clip(size_b - (blk_start - start_b), 0, MOE_TM)
    blk_valid = jnp.where(blk_start < padded_end[-1], blk_valid, 0).astype(jnp.int32)
    return padded_start, blk_expert, blk_valid, cap


def kernel(x, w_in, b_in, attn_sinks, rel_bias_table, w_branch_swa, w_branch_moba, w_out, ln1_gain, ln1_bias,
           w_group_router, b_group_router, w_expert_router, b_expert_router, w_expert_gate, w_expert_up,
           w_expert_down, ln2_gain, ln2_bias):
    assert w_in.shape[0] == DEPTH == 1
    B, S, D = x.shape
    T = B * S
    bf16 = jnp.bfloat16
    f32 = jnp.float32
    w = w_in[0]
    b = b_in[0]

    def cols(off, width):
        return w[:, off:off + width], b[off:off + width]

    wq_a, bq_a = cols(OFF_SWA_Q, SWA_Q_W)
    wk_a, bk_a = cols(OFF_SWA_K, SWA_KV_W)
    wv_a, bv_a = cols(OFF_SWA_V, SWA_KV_W)
    wq_b, bq_b = cols(OFF_MOBA_Q, MOBA_W)
    wk_b, bk_b = cols(OFF_MOBA_K, MOBA_W)
    wv_b, bv_b = cols(OFF_MOBA_V, MOBA_W)

    def dup_kv(t):
        parts = [t[..., i * HEAD_DIM:(i + 1) * HEAD_DIM] for i in range(SWA_KV_HEADS)]
        return jnp.concatenate([p for p in parts for _ in range(2)], axis=-1)

    qs = ATTN_SCALE * LOG2E
    wn = jnp.concatenate([wq_a * qs, dup_kv(wk_a), wq_b * qs, wk_b], axis=1).astype(bf16)
    bn = jnp.concatenate([bq_a * qs, dup_kv(bk_a), bq_b * qs, bk_b])[None, :].astype(f32)
    wt = jnp.concatenate([wv_a, wv_b], axis=1).T.astype(bf16)
    bt = jnp.concatenate([bv_a, bv_b])[:, None].astype(f32)

    qk, vt, eg_b, eu_b, ed_b = _inproj(x, wn, bn, wt, bt, w_expert_gate[0], w_expert_up[0], w_expert_down[0])

    bias_a, bias_b = _attention_bias_tables(rel_bias_table)
    y_a = _swa(attn_sinks[0].astype(f32), qk, vt, bias_a)
    y_b = _moba(qk, vt, bias_b)

    wg, bg = cols(OFF_GATE, 2 * D_MODEL)
    wr = jnp.zeros((ROUTER_ROWS, D), f32)
    wr = wr.at[0:N_GROUPS].set(w_group_router[0].T).at[8:8 + N_EXPERTS].set(w_expert_router[0].T)
    wr_hi = wr.astype(bf16)
    wr_lo = (wr - wr_hi.astype(f32)).astype(bf16)
    br = jnp.zeros((ROUTER_ROWS,), f32)
    br = br.at[0:N_GROUPS].set(b_group_router[0]).at[8:8 + N_EXPERTS].set(b_expert_router[0])[:, None]
    merge_args = (
        x.reshape(T, D), y_a, y_b,
        wg.astype(bf16), bg[None, :].astype(f32), w_branch_swa[0].astype(bf16), w_branch_moba[0].astype(bf16),
        w_out[0].astype(bf16), ln1_gain[0][None, :].astype(f32), ln1_bias[0][None, :].astype(f32),
        jnp.concatenate([wr_hi, wr_lo], axis=0), br)
    g2 = ln2_gain[0][None, :].astype(f32)
    b2 = ln2_bias[0][None, :].astype(f32)

    Tp = T // MOE_PARTS
    out = None
    for part in range(MOE_PARTS):
        x1, x1p, rinfo, counts = _merge(*merge_args, part, MOE_PARTS)
        sizes = counts[:, 0].astype(jnp.int32)
        padded_start, blk_expert, blk_valid, cap = _block_plan(sizes, Tp)
        dest = _dest(rinfo, padded_start)
        xs = _sc_scatter_rows(x1p, dest[0], dest[1], cap)
        y_buf = _experts(blk_expert, blk_valid, xs, eg_b, eu_b, ed_b)
        ypair = _sc_gather_rows(y_buf, dest[0:EXPERT_TOPK].reshape(-1))
        out = _final(x1, ypair, rinfo, g2, b2, out, part, MOE_PARTS)
    return out.reshape(B, S, D)
```

```python
import functools
import math

import numpy as np
import jax
import jax.numpy as jnp
from jax import lax
from jax.experimental import pallas as pl
from jax.experimental.pallas import tpu as pltpu
from jax.experimental.pallas import tpu_sc as plsc

D_MODEL = 1024
HEAD_DIM = 64
SWA_Q_HEADS = 8
SWA_KV_HEADS = 2
SWA_GROUP = SWA_Q_HEADS // SWA_KV_HEADS
SWA_WINDOW = 128
SWA_BLOCK = 128
MOBA_HEADS = 8
MOBA_BLOCK = 256
MOBA_TOPK = 3
MOBA_LOOKAHEAD = 2
BF16_SUBLANES = 16
MOBA_VROWS = HEAD_DIM + BF16_SUBLANES
REL_BUCKETS = 32
REL_MAX_DIST = 128
N_GROUPS = 4
EXPERTS_PER_GROUP = 8
N_EXPERTS = N_GROUPS * EXPERTS_PER_GROUP
EXPERT_TOPK = 2
D_EXPERT = 512
LN_EPS = 1e-5
DEPTH = 1
DEEPNORM_ALPHA = (2.0 * DEPTH) ** 0.25
NEG_INF = -1e30
ATTN_SCALE = HEAD_DIM ** -0.5
LOG2E = math.log2(math.e)

SWA_Q_W = SWA_Q_HEADS * HEAD_DIM
SWA_KV_W = SWA_KV_HEADS * HEAD_DIM
MOBA_W = MOBA_HEADS * HEAD_DIM
OFF_SWA_Q = 0
OFF_SWA_K = OFF_SWA_Q + SWA_Q_W
OFF_SWA_V = OFF_SWA_K + SWA_KV_W
OFF_MOBA_Q = OFF_SWA_V + SWA_KV_W
OFF_MOBA_K = OFF_MOBA_Q + MOBA_W
OFF_MOBA_V = OFF_MOBA_K + MOBA_W
OFF_GATE = OFF_MOBA_V + MOBA_W

LANES = 128
QK_SWA_Q = 0
QK_SWA_K = QK_SWA_Q + SWA_Q_W
QK_MOBA_Q = QK_SWA_K + SWA_KV_HEADS * LANES
QK_MOBA_K = QK_MOBA_Q + MOBA_W
QK_W = QK_MOBA_K + MOBA_W
VT_SWA = 0
VT_MOBA = VT_SWA + SWA_KV_W
VT_W = VT_MOBA + MOBA_W

MOE_TM = 512
MOE_PART_SHARES = (5, 3)
SC_CHUNK = 128
ROUTER_ROWS = 8 + N_EXPERTS
VMEM_LIMIT = 56 * 1024 * 1024

_NT = (((1,), (1,)), ((), ()))
_TN = (((0,), (0,)), ((), ()))


def _rel_bucket_np(dist):
    n = np.maximum(dist, 0)
    max_exact = REL_BUCKETS // 2
    nf = np.maximum(n, 1).astype(np.float32)
    large = max_exact + (np.log(nf / np.float32(max_exact)) / np.float32(math.log(REL_MAX_DIST / max_exact))
                         * np.float32(REL_BUCKETS - max_exact)).astype(np.int32)
    large = np.minimum(large, REL_BUCKETS - 1)
    return np.where(n < max_exact, n, large).astype(np.int32)


def _inproj_kernel(x_ref, wn_ref, bn_ref, wt_ref, bt_ref, eg_ref, eu_ref, ed_ref,
                   qk_ref, vt_ref, egb_ref, eub_ref, edb_ref):
    xb = x_ref[0].astype(jnp.bfloat16)
    qk = jnp.dot(xb, wn_ref[...], preferred_element_type=jnp.float32) + bn_ref[...]
    qk_ref[0] = qk.astype(jnp.bfloat16)
    vt = lax.dot_general(wt_ref[...], xb, _NT, preferred_element_type=jnp.float32) + bt_ref[...]
    vt_ref[0] = vt.astype(jnp.bfloat16)
    egb_ref[...] = eg_ref[...].astype(jnp.bfloat16)
    eub_ref[...] = eu_ref[...].astype(jnp.bfloat16)
    edb_ref[...] = ed_ref[...].astype(jnp.bfloat16)


def _inproj(x, wn, bn, wt, bt, w_gate, w_up, w_down, tm=512):
    B, S, D = x.shape
    nt = S // tm
    steps = B * nt

    def sliced(w):
        E, R, C = w.shape
        assert (E * R) % (steps * BF16_SUBLANES) == 0
        return w.reshape(steps, E * R // steps, C)

    eg, eu, ed = sliced(w_gate), sliced(w_up), sliced(w_down)
    espec = lambda w: pl.BlockSpec((1,) + w.shape[1:], lambda b, i: (b * nt + i, 0, 0))
    qk, vt, egb, eub, edb = pl.pallas_call(
        _inproj_kernel,
        grid=(B, nt),
        in_specs=[
            pl.BlockSpec((1, tm, D), lambda b, i: (b, i, 0)),
            pl.BlockSpec((D, QK_W), lambda b, i: (0, 0)),
            pl.BlockSpec((1, QK_W), lambda b, i: (0, 0)),
            pl.BlockSpec((VT_W, D), lambda b, i: (0, 0)),
            pl.BlockSpec((VT_W, 1), lambda b, i: (0, 0)),
            espec(eg), espec(eu), espec(ed),
        ],
        out_specs=[
            pl.BlockSpec((1, tm, QK_W), lambda b, i: (b, i, 0)),
            pl.BlockSpec((1, VT_W, tm), lambda b, i: (b, 0, i)),
            espec(eg), espec(eu), espec(ed),
        ],
        out_shape=[
            jax.ShapeDtypeStruct((B, S, QK_W), jnp.bfloat16),
            jax.ShapeDtypeStruct((B, VT_W, S), jnp.bfloat16),
            jax.ShapeDtypeStruct(eg.shape, jnp.bfloat16),
            jax.ShapeDtypeStruct(eu.shape, jnp.bfloat16),
            jax.ShapeDtypeStruct(ed.shape, jnp.bfloat16),
        ],
        compiler_params=pltpu.CompilerParams(
            dimension_semantics=("parallel", "parallel"), vmem_limit_bytes=VMEM_LIMIT),
        name="inproj",
    )(x, wn, bn, wt, bt, eg, eu, ed)
    return qk, vt, egb.reshape(w_gate.shape), eub.reshape(w_up.shape), edb.reshape(w_down.shape)


def _swa_kernel(sink_ref, q_ref, k_ref, vt_ref, bias_ref, o_ref, vaug_ref, *s_bufs):
    S = q_ref.shape[1]
    nblk = S // SWA_BLOCK
    half = HEAD_DIM
    vt = vt_ref[0]
    rows = lax.broadcasted_iota(jnp.int32, vt.shape, 0)
    one = jnp.ones_like(vt)
    vaug_ref[0] = jnp.where(rows < half, vt, one)
    vaug_ref[1] = jnp.where(rows < half, one, vt)
    lane = lax.broadcasted_iota(jnp.int32, (SWA_BLOCK, LANES), 1)
    col = lax.broadcasted_iota(jnp.int32, (1, 2 * SWA_BLOCK), 1)
    npair = SWA_Q_HEADS // 2

    def expand(specs):
        return [(qs, ks, nk, boff, pair, (2 * pair) // SWA_GROUP) for qs, ks, nk, boff in specs for pair in range(npair)]

    def score(specs):
        ss = []
        for qs, ks, nk, boff, pair, kv in expand(specs):
            qblk = q_ref[0, pl.ds(qs, SWA_BLOCK), pair * LANES:(pair + 1) * LANES]
            zero = jnp.zeros_like(qblk)
            q2 = jnp.concatenate([jnp.where(lane < half, qblk, zero), jnp.where(lane >= half, qblk, zero)], axis=0)
            kblk = k_ref[0, pl.ds(ks, nk), kv * LANES:(kv + 1) * LANES]
            s = lax.dot_general(kblk, q2, _NT, preferred_element_type=jnp.float32)
            ss.append(s + bias_ref[pair, boff:boff + nk, :])
        return ss

    def finish(specs, ss):
        items = expand(specs)
        stats = []
        for (qs, ks, nk, boff, pair, kv), s in zip(items, ss):
            sink = jnp.where(col < SWA_BLOCK, sink_ref[2 * pair], sink_ref[2 * pair + 1]) * LOG2E
            m = jnp.maximum(_colmax(s), sink)
            stats.append((jnp.exp2(s - m).astype(jnp.bfloat16), jnp.exp2(sink - m)))
        rs = [jnp.dot(vaug_ref[kv, :, pl.ds(ks, nk)], p, preferred_element_type=jnp.float32)
              for (qs, ks, nk, boff, pair, kv), (p, _) in zip(items, stats)]
        outs = []
        for (qs, ks, nk, boff, pair, kv), (_, esink), r in zip(items, stats, rs):
            l = (r[half:half + 1, :] if kv == 0 else r[0:1, :]) + esink
            o = (r[0:half, :] if kv == 0 else r[half:, :]) / l
            outs.append(jnp.concatenate([o[:, :SWA_BLOCK], o[:, SWA_BLOCK:]], axis=0))
        for i, (qs, _, _, _) in enumerate(specs):
            ot = jnp.concatenate(outs[i * npair:(i + 1) * npair], axis=0)
            o_ref[0, :, pl.ds(qs, SWA_BLOCK)] = ot.astype(jnp.bfloat16)

    first = [(0, 0, SWA_BLOCK, SWA_BLOCK), (SWA_BLOCK, 0, 2 * SWA_BLOCK, 0)]
    finish(first, score(first))

    assert nblk % 2 == 0
    n_groups = nblk // 2 - 1
    nbuf = len(s_bufs)

    def group_specs(g):
        specs = []
        for n in (2 * g + 2, 2 * g + 3):
            qs = pl.multiple_of(n * SWA_BLOCK, SWA_BLOCK)
            specs.append((qs, pl.multiple_of(qs - SWA_BLOCK, SWA_BLOCK), 2 * SWA_BLOCK, 0))
        return specs

    def score_to(g, buf):
        for k, s in enumerate(score(group_specs(g))):
            buf[k] = s

    def finish_from(g, buf):
        finish(group_specs(g), [buf[k] for k in range(2 * npair)])

    for g in range(min(2, n_groups)):
        score_to(g, s_bufs[g])

    def body(t, carry):
        for j in range(nbuf):
            score_to(nbuf * t + 2 + j, s_bufs[(2 + j) % nbuf])
            finish_from(nbuf * t + j, s_bufs[j])
        return carry

    n_trips = max(n_groups - 2, 0) // nbuf
    lax.fori_loop(0, n_trips, body, 0)
    for g in range(nbuf * n_trips, n_groups):
        if g + 2 < n_groups:
            score_to(g + 2, s_bufs[(g + 2) % nbuf])
        finish_from(g, s_bufs[g % nbuf])


def _swa(sinks, qk, vt, bias):
    B, S, _ = qk.shape
    return pl.pallas_call(
        _swa_kernel,
        grid_spec=pltpu.PrefetchScalarGridSpec(
            num_scalar_prefetch=1,
            grid=(B,),
            in_specs=[
                pl.BlockSpec((1, S, SWA_Q_W), lambda b, s: (b, 0, QK_SWA_Q // SWA_Q_W)),
                pl.BlockSpec((1, S, 2 * LANES), lambda b, s: (b, 0, QK_SWA_K // (2 * LANES))),
                pl.BlockSpec((1, SWA_KV_W, S), lambda b, s: (b, VT_SWA // SWA_KV_W, 0)),
                pl.BlockSpec((SWA_Q_HEADS // 2, 2 * SWA_BLOCK, 2 * SWA_BLOCK), lambda b, s: (0, 0, 0)),
            ],
            out_specs=pl.BlockSpec((1, SWA_Q_W, S), lambda b, s: (b, 0, 0)),
            scratch_shapes=[pltpu.VMEM((SWA_KV_HEADS, SWA_KV_W, S), jnp.bfloat16)]
            + [pltpu.VMEM((SWA_Q_HEADS, 2 * SWA_BLOCK, 2 * SWA_BLOCK), jnp.float32)] * 4,
        ),
        out_shape=jax.ShapeDtypeStruct((B, SWA_Q_W, S), jnp.bfloat16),
        compiler_params=pltpu.CompilerParams(
            dimension_semantics=("parallel",), vmem_limit_bytes=VMEM_LIMIT),
        name="swa",
    )(sinks, qk, qk, vt, bias)


def _colmax(s):
    while s.shape[0] > 8:
        h = s.shape[0] // 2
        s = jnp.maximum(s[:h], s[h:])
    return jnp.max(s, axis=0, keepdims=True)


def _moba_kernel(fqb_ref, fc_ref, q_ref, k_ref, vt_ref, bias_ref, o_ref,
                 vaug_ref, gate_ref, rank_ref, sel_ref, far_ref, m_scr, acc_scr, sa_scr, sb_scr, sc_scr, sd_scr):
    S = q_ref.shape[1]
    nb = S // MOBA_BLOCK
    half = HEAD_DIM
    BLK = MOBA_BLOCK

    ones_rows = jnp.ones((MOBA_VROWS - half, S), jnp.bfloat16)
    for h in range(2):
        vaug_ref[h, 0:half, :] = vt_ref[0, h * half:(h + 1) * half, :]
        vaug_ref[h, half:, :] = ones_rows

    kf = k_ref[0].astype(jnp.float32).reshape(nb, BLK, LANES)
    kmean = jnp.sum(kf, axis=1) * (1.0 / BLK)
    k_hi = kmean.astype(jnp.bfloat16)
    k_lo = (kmean - k_hi.astype(jnp.float32)).astype(jnp.bfloat16)
    kcat = jnp.concatenate([k_hi, k_lo], axis=0)

    lane_q = lax.broadcasted_iota(jnp.int32, (S, LANES), 1)
    brow = lax.broadcasted_iota(jnp.int32, (nb, S), 0)
    qblk_of = lax.broadcasted_iota(jnp.int32, (nb, S), 1) // BLK
    qall = q_ref[0]
    for h in range(2):
        qh = jnp.where((lane_q < half) if h == 0 else (lane_q >= half), qall, jnp.zeros_like(qall))
        g2 = lax.dot_general(kcat, qh, _NT, preferred_element_type=jnp.float32)
        gate_ref[...] = g2[0:nb] + g2[nb:2 * nb]
        rank_ref[...] = jnp.zeros_like(rank_ref)
        for m in range(nb - 1):
            lo = (m + 1) * BLK
            G = gate_ref[:, lo:]
            gm = gate_ref[m:m + 1, lo:]
            ge = jnp.where(gm >= G, 1.0, 0.0)
            gt = jnp.where(gm > G, 1.0, 0.0)
            brow_m = lax.broadcasted_iota(jnp.int32, (nb, S - lo), 0)
            rank_ref[:, lo:] += jnp.where(brow_m > m, ge, gt)
        top = rank_ref[...] < float(MOBA_TOPK)
        sel_ref[h] = jnp.where((brow < qblk_of) & top, 0.0, NEG_INF)
        far_ref[h] = jnp.where((brow < qblk_of - 1) & top, 0.0, NEG_INF)

    lane = lax.broadcasted_iota(jnp.int32, (BLK, LANES), 1)

    def head_q(qs, h):
        qblk = q_ref[0, pl.ds(qs, BLK), :]
        return jnp.where((lane < half) if h == 0 else (lane >= half), qblk, jnp.zeros_like(qblk))

    m0 = jnp.full((1, BLK), NEG_INF, jnp.float32)
    acc0 = jnp.zeros((MOBA_VROWS, BLK), jnp.float32)

    def item(h, qb, qs, ks, nk, bias, sel_a, sel_b, m, acc):
        return dict(h=h, qb=qb, qs=qs, ks=ks, nk=nk, bias=bias, sel_a=sel_a, sel_b=sel_b, m=m, acc=acc)

    def near_item(qb, h):
        qs = pl.multiple_of(qb * BLK, BLK)
        ps = pl.multiple_of(qs - BLK, BLK)
        sel_prev = sel_ref[h, pl.ds(qb - 1, 1), pl.ds(qs, BLK)]
        return item(h, qb, qs, ps, 2 * BLK, bias_ref[h], sel_prev, None, m0, acc0)

    def far_item(i, h):
        qb = fqb_ref[i]
        c = fc_ref[i]
        qs = pl.multiple_of(qb * BLK, BLK)
        ks = pl.multiple_of(c * (2 * BLK), 2 * BLK)
        sel_a = far_ref[h, pl.ds(2 * c, 1), pl.ds(qs, BLK)]
        sel_b = far_ref[h, pl.ds(2 * c + 1, 1), pl.ds(qs, BLK)]
        return item(h, qb, qs, ks, 2 * BLK, None, sel_a, sel_b, m_scr[h, qb], acc_scr[h, qb])

    def scores(it):
        kslab = k_ref[0, pl.ds(it["ks"], it["nk"]), :]
        s = lax.dot_general(kslab, head_q(it["qs"], it["h"]), _NT,
                            preferred_element_type=jnp.float32)
        return s if it["bias"] is None else s + it["bias"]

    def probs(it, s):
        m, sel_a, sel_b = it["m"], it["sel_a"], it["sel_b"]
        if it["nk"] == BLK:
            m_new = jnp.maximum(m, _colmax(s))
            p = jnp.exp2(s - m_new)
        else:
            cm_a = _colmax(s[:BLK]) + sel_a
            cm_b = _colmax(s[BLK:])
            if sel_b is not None:
                cm_b = cm_b + sel_b
            m_new = jnp.maximum(m, jnp.maximum(cm_a, cm_b))
            p_a = jnp.exp2(s[:BLK] - (m_new - sel_a))
            p_b = jnp.exp2(s[BLK:] - (m_new if sel_b is None else m_new - sel_b))
            p = jnp.concatenate([p_a, p_b], axis=0)
        return m_new, jnp.exp2(m - m_new), p.astype(jnp.bfloat16)

    def run(items):
        ss = [scores(it) for it in items]
        ps = [probs(it, s) for it, s in zip(items, ss)]
        pvs = [jnp.dot(vaug_ref[it["h"], :, pl.ds(it["ks"], it["nk"])], p, preferred_element_type=jnp.float32)
               for it, (_, _, p) in zip(items, ps)]
        for it, (m_new, alpha, _), pv in zip(items, ps, pvs):
            m_scr[it["h"], it["qb"]] = m_new
            acc_scr[it["h"], it["qb"]] = it["acc"] * alpha + pv

    run([item(h, 0, 0, 0, BLK, bias_ref[h, BLK:, :], None, None, m0, acc0) for h in range(2)]
        + [near_item(nb - 1, h) for h in range(2)])

    bufs = (sa_scr, sb_scr, sc_scr, sd_scr)
    nbuf = len(bufs)

    def finish(items, s_buf):
        ps = [probs(it, s_buf[k]) for k, it in enumerate(items)]
        pvs = [jnp.dot(vaug_ref[it["h"], :, pl.ds(it["ks"], it["nk"])], p, preferred_element_type=jnp.float32)
               for it, (_, _, p) in zip(items, ps)]
        for it, (m_new, alpha, _), pv in zip(items, ps, pvs):
            m_scr[it["h"], it["qb"]] = m_new
            acc_scr[it["h"], it["qb"]] = it["acc"] * alpha + pv

    def pipelined(n_groups, score, items_of):
        for g in range(min(MOBA_LOOKAHEAD, n_groups)):
            score(g, bufs[g])

        def body(t, carry):
            for j in range(nbuf):
                score(nbuf * t + MOBA_LOOKAHEAD + j, bufs[(MOBA_LOOKAHEAD + j) % nbuf])
                finish(items_of(nbuf * t + j), bufs[j])
            return carry

        n_trips = max(n_groups - MOBA_LOOKAHEAD, 0) // nbuf
        lax.fori_loop(0, n_trips, body, 0)
        for g in range(nbuf * n_trips, n_groups):
            if g + MOBA_LOOKAHEAD < n_groups:
                score(g + MOBA_LOOKAHEAD, bufs[(g + MOBA_LOOKAHEAD) % nbuf])
            finish(items_of(g), bufs[g % nbuf])

    near_pairs = nb // 2 - 1

    def near_group(g):
        return [(qb, h) for qb in (g + 1, g + 1 + near_pairs) for h in range(2)]

    def near_score(g, s_buf):
        for k, (qb, h) in enumerate(near_group(g)):
            qs = pl.multiple_of(qb * BLK, BLK)
            ps = pl.multiple_of(qs - BLK, BLK)
            s_buf[k] = lax.dot_general(k_ref[0, pl.ds(ps, 2 * BLK), :], head_q(qs, h), _NT,
                                       preferred_element_type=jnp.float32) + bias_ref[h]

    pipelined(near_pairs, near_score, lambda g: [near_item(qb, h) for qb, h in near_group(g)])

    n_far_groups = fqb_ref.shape[0] // 2

    def far_group(g):
        return [(j, h) for j in (g, g + n_far_groups) for h in range(2)]

    def far_score(g, s_buf):
        for k, (j, h) in enumerate(far_group(g)):
            qs = pl.multiple_of(fqb_ref[j] * BLK, BLK)
            ks = pl.multiple_of(fc_ref[j] * (2 * BLK), 2 * BLK)
            s_buf[k] = lax.dot_general(k_ref[0, pl.ds(ks, 2 * BLK), :], head_q(qs, h), _NT,
                                       preferred_element_type=jnp.float32)

    pipelined(n_far_groups, far_score, lambda g: [far_item(j, h) for j, h in far_group(g)])

    def out_body(t, carry):
        for qb in (2 * t, 2 * t + 1):
            qs = pl.multiple_of(qb * BLK, BLK)
            a0 = acc_scr[0, qb]
            a1 = acc_scr[1, qb]
            ot = jnp.concatenate([a0[0:half] / a0[half:half + 1, :], a1[0:half] / a1[half:half + 1, :]],
                                 axis=0)
            o_ref[0, :, pl.ds(qs, BLK)] = ot.astype(jnp.bfloat16)
        return carry

    lax.fori_loop(0, nb // 2, out_body, 0)


def _moba_far_items(nb):
    items = [(qb, c) for qb in range(2, nb) for c in range(qb // 2)]
    n = len(items)
    assert n % 2 == 0 and all(items[i][0] != items[i + n // 2][0] for i in range(n // 2))
    return np.array([it[0] for it in items], np.int32), np.array([it[1] for it in items], np.int32)


def _moba(qk, vt, bias):
    B, S, _ = qk.shape
    nb = S // MOBA_BLOCK
    npair = MOBA_W // LANES
    far_qb, far_c = _moba_far_items(nb)
    return pl.pallas_call(
        _moba_kernel,
        grid_spec=pltpu.PrefetchScalarGridSpec(
            num_scalar_prefetch=2,
            grid=(B, npair),
            in_specs=[
                pl.BlockSpec((1, S, LANES), lambda b, p, fq, fc: (b, 0, QK_MOBA_Q // LANES + p)),
                pl.BlockSpec((1, S, LANES), lambda b, p, fq, fc: (b, 0, QK_MOBA_K // LANES + p)),
                pl.BlockSpec((1, LANES, S), lambda b, p, fq, fc: (b, VT_MOBA // LANES + p, 0)),
                pl.BlockSpec((2, 2 * MOBA_BLOCK, MOBA_BLOCK), lambda b, p, fq, fc: (p, 0, 0)),
            ],
            out_specs=pl.BlockSpec((1, LANES, S), lambda b, p, fq, fc: (b, p, 0)),
            scratch_shapes=[
                pltpu.VMEM((2, MOBA_VROWS, S), jnp.bfloat16),
                pltpu.VMEM((nb, S), jnp.float32),
                pltpu.VMEM((nb, S), jnp.float32),
                pltpu.VMEM((2, nb, S), jnp.float32),
                pltpu.VMEM((2, nb, S), jnp.float32),
                pltpu.VMEM((2, nb, 1, MOBA_BLOCK), jnp.float32),
                pltpu.VMEM((2, nb, MOBA_VROWS, MOBA_BLOCK), jnp.float32),
            ] + [pltpu.VMEM((4, 2 * MOBA_BLOCK, MOBA_BLOCK), jnp.float32)] * 4,
        ),
        out_shape=jax.ShapeDtypeStruct((B, MOBA_W, S), jnp.bfloat16),
        compiler_params=pltpu.CompilerParams(
            dimension_semantics=("parallel", "parallel"), vmem_limit_bytes=VMEM_LIMIT),
        name="moba",
    )(jnp.asarray(far_qb), jnp.asarray(far_c), qk, qk, vt, bias)


def _layer_norm(h, gain, bias):
    mu = jnp.mean(h, axis=-1, keepdims=True)
    c = h - mu
    var = jnp.mean(c * c, axis=-1, keepdims=True)
    return c * lax.rsqrt(var + LN_EPS) * gain + bias


def _pack_bf16_pair(a, b):
    ia = lax.bitcast_convert_type(a.astype(jnp.bfloat16).astype(jnp.float32), jnp.int32)
    ib = lax.bitcast_convert_type(b.astype(jnp.bfloat16).astype(jnp.float32), jnp.int32)
    return lax.shift_right_logical(ia, 16) | ib


def _unpack_bf16_pair(w):
    lo = lax.bitcast_convert_type(lax.shift_left(w, 16), jnp.float32)
    hi = lax.bitcast_convert_type(w & jnp.int32(-65536), jnp.float32)
    return lo.astype(jnp.bfloat16), hi.astype(jnp.bfloat16)


def _merge_kernel(x_ref, ya_ref, yb_ref, wg_ref, bg_ref, wa_ref, wb_ref, wo_ref, g1_ref, b1_ref,
                  wr_ref, br_ref, tri_ref, x1_ref, x1p_ref, r_ref, cnt_ref):
    @pl.when(pl.program_id(0) == 0)
    def _():
        cnt_ref[...] = jnp.zeros_like(cnt_ref)

    ts = tri_ref.shape[0]
    subs = [pl.ds(r0, ts) for r0 in range(0, x_ref.shape[0], ts)]
    pre = [_merge_matmuls(x_ref[rows, :], ya_ref[0, :, rows], yb_ref[0, :, rows], wg_ref, bg_ref, wa_ref, wb_ref, wo_ref)
           for rows in subs]
    for rows, h in zip(subs, pre):
        _merge_route(h, rows, g1_ref, b1_ref, wr_ref, br_ref, tri_ref, x1_ref, x1p_ref, r_ref, cnt_ref)


def _merge_matmuls(x, ya_t, yb_t, wg_ref, bg_ref, wa_ref, wb_ref, wo_ref):
    xb = x.astype(jnp.bfloat16)
    z = jnp.dot(xb, wg_ref[...], preferred_element_type=jnp.float32) + bg_ref[...]
    gates = 1.0 / (1.0 + jnp.exp(-z))
    pa = lax.dot_general(ya_t, wa_ref[...], _TN, preferred_element_type=jnp.float32)
    pb = lax.dot_general(yb_t, wb_ref[...], _TN, preferred_element_type=jnp.float32)
    merged = gates[:, :D_MODEL] * pa + gates[:, D_MODEL:] * pb
    mixed = jnp.dot(merged.astype(jnp.bfloat16), wo_ref[...], preferred_element_type=jnp.float32)
    return DEEPNORM_ALPHA * x + mixed


def _merge_route(h, rows, g1_ref, b1_ref, wr_ref, br_ref, tri_ref, x1_ref, x1p_ref, r_ref, cnt_ref):
    x1 = _layer_norm(h, g1_ref[...], b1_ref[...])
    x1_ref[rows, :] = x1
    x1_hi = x1.astype(jnp.bfloat16)
    x1p_ref[rows, :] = _pack_bf16_pair(x1[:, :D_MODEL // 2], x1[:, D_MODEL // 2:])
    x1_lo = (x1 - x1_hi.astype(jnp.float32)).astype(jnp.bfloat16)

    R = ROUTER_ROWS
    l1 = lax.dot_general(wr_ref[...], x1_hi, _NT, preferred_element_type=jnp.float32)
    l2 = lax.dot_general(wr_ref[0:R, :], x1_lo, _NT, preferred_element_type=jnp.float32)
    L = l1[0:R] + l1[R:2 * R] + l2 + br_ref[...]
    tm = x1.shape[0]
    row = lax.broadcasted_iota(jnp.int32, (8, tm), 0)
    big = jnp.float32(-3e38)
    gl = jnp.where(row < N_GROUPS, L[0:8], big)
    gmax = jnp.max(gl, axis=0, keepdims=True)
    g_idx = jnp.min(jnp.where(gl == gmax, row, 8), axis=0, keepdims=True)
    gsum = jnp.sum(jnp.where(row < N_GROUPS, jnp.exp(gl - gmax), 0.0), axis=0, keepdims=True)
    g_prob = 1.0 / gsum
    E = L[8 + 8 * (N_GROUPS - 1):8 + 8 * N_GROUPS]
    for g in range(N_GROUPS - 2, -1, -1):
        E = jnp.where(g_idx == g, L[8 + 8 * g:16 + 8 * g], E)
    t0 = jnp.max(E, axis=0, keepdims=True)
    loc0 = jnp.min(jnp.where(E == t0, row, 8), axis=0, keepdims=True)
    E2 = jnp.where(row == loc0, big, E)
    t1 = jnp.max(E2, axis=0, keepdims=True)
    loc1 = jnp.min(jnp.where(E2 == t1, row, 8), axis=0, keepdims=True)
    ex = jnp.exp(t1 - t0)
    w0 = g_prob / (1.0 + ex)
    w1 = g_prob * ex / (1.0 + ex)
    e0i = g_idx * EXPERTS_PER_GROUP + loc0
    e1i = g_idx * EXPERTS_PER_GROUP + loc1

    erow = lax.broadcasted_iota(jnp.int32, (N_EXPERTS, tm), 0)
    oh0 = jnp.where(erow == e0i, 1.0, 0.0)
    oh1 = jnp.where(erow == e1i, 1.0, 0.0)
    both = oh0 + oh1
    prefix = jnp.dot(both.astype(jnp.bfloat16), tri_ref[...], preferred_element_type=jnp.float32)
    prefix = prefix + cnt_ref[:, 0:1]
    rank0 = jnp.sum(oh0 * prefix, axis=0, keepdims=True)
    rank1 = jnp.sum(oh1 * prefix, axis=0, keepdims=True)
    cnt_ref[...] = cnt_ref[...] + jnp.sum(both, axis=1, keepdims=True)

    vals = (e0i.astype(jnp.float32), e1i.astype(jnp.float32), w0, w1, rank0, rank1)
    out = jnp.zeros((8, tm), jnp.float32)
    for k, v in enumerate(vals):
        out = jnp.where(row == k, v, out)
    r_ref[:, rows] = out


def _merge(x2, ya, yb, wg, bg, wa, wb, wo, g1, b1, wr, br, row0, T, tm=512, ts=256):
    D = x2.shape[1]
    S = ya.shape[2]
    per_seq = S // tm
    assert per_seq * tm == S and row0 % tm == 0 and T % tm == 0
    off = row0 // tm
    const = lambda i: (0, 0)
    seq_tile = lambda i: ((i + off) // per_seq, 0, (i + off) % per_seq)
    tri = jnp.triu(jnp.ones((ts, ts), jnp.bfloat16), k=1)
    return pl.pallas_call(
        _merge_kernel,
        grid=(T // tm,),
        in_specs=[
            pl.BlockSpec((tm, D), lambda i: (i + off, 0)),
            pl.BlockSpec((1, SWA_Q_W, tm), seq_tile),
            pl.BlockSpec((1, MOBA_W, tm), seq_tile),
            pl.BlockSpec((D, 2 * D), const),
            pl.BlockSpec((1, 2 * D), const),
            pl.BlockSpec((SWA_Q_W, D), const),
            pl.BlockSpec((MOBA_W, D), const),
            pl.BlockSpec((D, D), const),
            pl.BlockSpec((1, D), const),
            pl.BlockSpec((1, D), const),
            pl.BlockSpec((2 * ROUTER_ROWS, D), const),
            pl.BlockSpec((ROUTER_ROWS, 1), const),
            pl.BlockSpec((ts, ts), const),
        ],
        out_specs=[
            pl.BlockSpec((tm, D), lambda i: (i, 0)),
            pl.BlockSpec((tm, D // 2), lambda i: (i, 0)),
            pl.BlockSpec((8, tm), lambda i: (0, i)),
            pl.BlockSpec((N_EXPERTS, LANES), const),
        ],
        out_shape=[
            jax.ShapeDtypeStruct((T, D), jnp.float32),
            jax.ShapeDtypeStruct((T, D // 2), jnp.int32),
            jax.ShapeDtypeStruct((8, T), jnp.float32),
            jax.ShapeDtypeStruct((N_EXPERTS, LANES), jnp.float32),
        ],
        compiler_params=pltpu.CompilerParams(
            dimension_semantics=("arbitrary",), vmem_limit_bytes=VMEM_LIMIT),
        name="merge",
    )(x2, ya, yb, wg, bg, wa, wb, wo, g1, b1, wr, br, tri)


def _dest_kernel(r_ref, ps_ref, d_ref):
    tt = r_ref.shape[1]
    erow = lax.broadcasted_iota(jnp.int32, (N_EXPERTS, tt), 0)
    row = lax.broadcasted_iota(jnp.int32, (8, tt), 0)
    ps = ps_ref[...]
    out = jnp.zeros((8, tt), jnp.float32)
    for k in range(EXPERT_TOPK):
        e = r_ref[k:k + 1, :].astype(jnp.int32)
        start = jnp.sum(jnp.where(erow == e, ps, 0.0), axis=0, keepdims=True)
        out = jnp.where(row == k, start + r_ref[4 + k:5 + k, :], out)
    d_ref[...] = out.astype(jnp.int32)


def _dest(rinfo, padded_start, tt=8192):
    T = rinfo.shape[1]
    tt = min(tt, T)
    return pl.pallas_call(
        _dest_kernel,
        grid=(T // tt,),
        in_specs=[pl.BlockSpec((8, tt), lambda i: (0, i)), pl.BlockSpec((N_EXPERTS, 1), lambda i: (0, 0))],
        out_specs=pl.BlockSpec((8, tt), lambda i: (0, i)),
        out_shape=jax.ShapeDtypeStruct((8, T), jnp.int32),
        compiler_params=pltpu.CompilerParams(dimension_semantics=("parallel",)),
        name="dest",
    )(rinfo, padded_start.astype(jnp.float32)[:, None])


def _sc_workers():
    info = plsc.get_sparse_core_info()
    return info.num_cores, info.num_subcores


def _sc_scatter_rows(src, dest0, dest1, cap):
    T, W = src.shape
    nc, ns = _sc_workers()
    per_w = T // (nc * ns)
    assert per_w * nc * ns == T and per_w % SC_CHUNK == 0
    mesh = plsc.VectorSubcoreMesh(core_axis_name="c", subcore_axis_name="s")

    @functools.partial(
        pl.kernel, mesh=mesh,
        out_type=jax.ShapeDtypeStruct((cap, W), src.dtype),
        scratch_types=[pltpu.VMEM((SC_CHUNK,), jnp.int32), pltpu.VMEM((SC_CHUNK, W), src.dtype)],
    )
    def scatter(src_hbm, d0_hbm, d1_hbm, out_hbm, idx_v, rows_v):
        wid = lax.axis_index("s") * nc + lax.axis_index("c")
        base = wid * per_w

        @pl.loop(0, per_w // SC_CHUNK)
        def _(c):
            off = pl.multiple_of(base + c * SC_CHUNK, SC_CHUNK)
            pltpu.sync_copy(src_hbm.at[pl.ds(off, SC_CHUNK)], rows_v)
            for d_hbm in (d0_hbm, d1_hbm):
                pltpu.sync_copy(d_hbm.at[pl.ds(off, SC_CHUNK)], idx_v)
                pltpu.sync_copy(rows_v, out_hbm.at[idx_v])

    return scatter(src, dest0, dest1)


def _sc_gather_rows(table, idx):
    N = idx.shape[0]
    W = table.shape[1]
    nc, ns = _sc_workers()
    per_w = N // (nc * ns)
    assert per_w * nc * ns == N and per_w % SC_CHUNK == 0
    mesh = plsc.VectorSubcoreMesh(core_axis_name="c", subcore_axis_name="s")

    @functools.partial(
        pl.kernel, mesh=mesh,
        out_type=jax.ShapeDtypeStruct((N, W), table.dtype),
        scratch_types=[pltpu.VMEM((SC_CHUNK,), jnp.int32), pltpu.VMEM((SC_CHUNK, W), table.dtype)],
    )
    def gather(table_hbm, idx_hbm, out_hbm, idx_v, rows_v):
        wid = lax.axis_index("s") * nc + lax.axis_index("c")
        base = wid * per_w

        @pl.loop(0, per_w // SC_CHUNK)
        def _(c):
            off = pl.multiple_of(base + c * SC_CHUNK, SC_CHUNK)
            pltpu.sync_copy(idx_hbm.at[pl.ds(off, SC_CHUNK)], idx_v)
            pltpu.sync_copy(table_hbm.at[idx_v], rows_v)
            pltpu.sync_copy(rows_v, out_hbm.at[pl.ds(off, SC_CHUNK)])

    return gather(table, idx)


def _expert_kernel(be_ref, nv_ref, x_ref, wg_ref, wu_ref, wd_ref, y_ref):
    i = pl.program_id(0)
    nv = nv_ref[i]

    @pl.when(nv > 0)
    def _():
        lo, hi = _unpack_bf16_pair(x_ref[...])
        xb = jnp.concatenate([lo, hi], axis=1)
        rows = lax.broadcasted_iota(jnp.int32, xb.shape, 0)
        xb = jnp.where(rows < nv, xb, jnp.zeros_like(xb))
        g = jnp.dot(xb, wg_ref[0], preferred_element_type=jnp.float32)
        u = jnp.dot(xb, wu_ref[0], preferred_element_type=jnp.float32)
        act = (g / (1.0 + jnp.exp(-g))) * u
        y = jnp.dot(act.astype(jnp.bfloat16), wd_ref[0], preferred_element_type=jnp.float32)
        y_ref[...] = _pack_bf16_pair(y[:, :D_MODEL // 2], y[:, D_MODEL // 2:])

    @pl.when(nv <= 0)
    def _():
        y_ref[...] = jnp.zeros_like(y_ref)


def _experts(blk_expert, blk_valid, xs, wg, wu, wd):
    cap, DW = xs.shape
    D = 2 * DW
    n_blocks = cap // MOE_TM
    return pl.pallas_call(
        _expert_kernel,
        grid_spec=pltpu.PrefetchScalarGridSpec(
            num_scalar_prefetch=2,
            grid=(n_blocks,),
            in_specs=[
                pl.BlockSpec((MOE_TM, DW), lambda i, be, nv: (i, 0)),
                pl.BlockSpec((1, D, D_EXPERT), lambda i, be, nv: (be[i], 0, 0)),
                pl.BlockSpec((1, D, D_EXPERT), lambda i, be, nv: (be[i], 0, 0)),
                pl.BlockSpec((1, D_EXPERT, D), lambda i, be, nv: (be[i], 0, 0)),
            ],
            out_specs=pl.BlockSpec((MOE_TM, DW), lambda i, be, nv: (i, 0)),
        ),
        out_shape=jax.ShapeDtypeStruct((cap, DW), jnp.int32),
        compiler_params=pltpu.CompilerParams(
            dimension_semantics=("arbitrary",), vmem_limit_bytes=VMEM_LIMIT),
        name="experts",
    )(blk_expert, blk_valid, xs, wg, wu, wd)


def _final_kernel(x1_ref, y0_ref, y1_ref, w_ref, g2_ref, b2_ref, *rest):
    o_ref = rest[-1]
    w = w_ref[...].T
    halves = []
    for part in range(2):
        y0 = _unpack_bf16_pair(y0_ref[...])[part].astype(jnp.float32)
        y1 = _unpack_bf16_pair(y1_ref[...])[part].astype(jnp.float32)
        halves.append(y0 * w[:, 2:3] + y1 * w[:, 3:4])
    moe = jnp.concatenate(halves, axis=1)
    o_ref[...] = _layer_norm(DEEPNORM_ALPHA * x1_ref[...] + moe, g2_ref[...], b2_ref[...])


def _final(x1, ypair, rinfo, g2, b2, out_prev, row0, total, tm=1024):
    T, D = x1.shape
    const = lambda i: (0, 0)
    assert row0 % tm == 0 and T % tm == 0
    nt = T // tm
    off = row0 // tm
    in_specs = [
        pl.BlockSpec((tm, D), lambda i: (i, 0)),
        pl.BlockSpec((tm, D // 2), lambda i: (i, 0)),
        pl.BlockSpec((tm, D // 2), lambda i: (i + nt, 0)),
        pl.BlockSpec((8, tm), lambda i: (0, i)),
        pl.BlockSpec((1, D), const),
        pl.BlockSpec((1, D), const),
    ]
    args = [x1, ypair, ypair, rinfo, g2, b2]
    aliases = {}
    if out_prev is not None:
        in_specs.append(pl.BlockSpec(memory_space=pl.ANY))
        args.append(out_prev)
        aliases = {len(args) - 1: 0}
    return pl.pallas_call(
        _final_kernel,
        grid=(nt,),
        in_specs=in_specs,
        out_specs=pl.BlockSpec((tm, D), lambda i: (i + off, 0)),
        out_shape=jax.ShapeDtypeStruct((total, D), jnp.float32),
        input_output_aliases=aliases,
        compiler_params=pltpu.CompilerParams(
            dimension_semantics=("parallel",), vmem_limit_bytes=VMEM_LIMIT),
        name="final",
    )(*args)


def _bias_kernel(tab_ref, sub_ref, idx_ref, o_ref):
    h = pl.program_id(0)
    idx = idx_ref[...]
    acc = jnp.full(idx.shape, NEG_INF, jnp.float32)
    for b in range(REL_BUCKETS):
        acc = jnp.where(idx == b, (tab_ref[h, b] - sub_ref[h]) * LOG2E, acc)
    o_ref[0] = acc


def _bias_expand(table_hb, sub_h, idx):
    H = table_hb.shape[0]
    K, Q = idx.shape
    return pl.pallas_call(
        _bias_kernel,
        grid_spec=pltpu.PrefetchScalarGridSpec(
            num_scalar_prefetch=2,
            grid=(H,),
            in_specs=[pl.BlockSpec((K, Q), lambda h, t, s: (0, 0))],
            out_specs=pl.BlockSpec((1, K, Q), lambda h, t, s: (h, 0, 0)),
        ),
        out_shape=jax.ShapeDtypeStruct((H, K, Q), jnp.float32),
        compiler_params=pltpu.CompilerParams(dimension_semantics=("parallel",)),
        name="bias",
    )(table_hb, sub_h, jnp.asarray(idx))


def _attention_bias_tables(rel_bias_table):
    rel = rel_bias_table.astype(jnp.float32)
    tab_a = rel[:, :SWA_Q_HEADS].T
    tab_b = rel[:, SWA_Q_HEADS:].T
    kj = np.arange(2 * SWA_BLOCK)[:, None]
    qi = np.arange(SWA_BLOCK)[None, :]
    dist = SWA_BLOCK + qi - kj
    idx_a = np.where((dist >= 0) & (dist < SWA_WINDOW), _rel_bucket_np(dist), -1).astype(np.int32)
    bias_a = _bias_expand(tab_a, jnp.zeros((SWA_Q_HEADS,), jnp.float32), idx_a)
    bias_a = bias_a.reshape(SWA_Q_HEADS // 2, 2, 2 * SWA_BLOCK, SWA_BLOCK).transpose(0, 2, 1, 3)
    bias_a = bias_a.reshape(SWA_Q_HEADS // 2, 2 * SWA_BLOCK, 2 * SWA_BLOCK)
    j = np.arange(MOBA_BLOCK)[:, None]
    i = np.arange(MOBA_BLOCK)[None, :]
    d_own = i - j
    idx_own = np.where(d_own >= 0, _rel_bucket_np(d_own), -1)
    idx_prev = _rel_bucket_np(MOBA_BLOCK + i - j)
    idx_b = np.concatenate([idx_prev, idx_own], axis=0).astype(np.int32)
    bias_b = _bias_expand(tab_b, tab_b[:, REL_BUCKETS - 1], idx_b)
    return bias_a, bias_b


def _block_plan(sizes, n_tok):
    n_assign = n_tok * EXPERT_TOPK
    padded = ((sizes + MOE_TM - 1) // MOE_TM) * MOE_TM
    eid = jnp.arange(N_EXPERTS, dtype=jnp.int32)
    padded_end = jnp.sum(jnp.where(eid[None, :] <= eid[:, None], padded[None, :], 0), axis=1)
    padded_start = padded_end - padded
    cap = -(-n_assign // MOE_TM) * MOE_TM + N_EXPERTS * MOE_TM
    blk_start = jnp.arange(cap // MOE_TM, dtype=jnp.int32) * MOE_TM
    blk_expert = jnp.minimum(
        jnp.sum(padded_end[None, :] <= blk_start[:, None], axis=1), N_EXPERTS - 1).astype(jnp.int32)
    mine = blk_expert[:, None] == eid[None, :]
    size_b = jnp.sum(jnp.where(mine, sizes[None, :], 0), axis=1)
    start_b = jnp.sum(jnp.where(mine, padded_start[None, :], 0), axis=1)
    blk_valid = jnp.clip(size_b - (blk_start - start_b), 0, MOE_TM)
    blk_valid = jnp.where(blk_start < padded_end[-1], blk_valid, 0).astype(jnp.int32)
    return padded_start, blk_expert, blk_valid, cap


def kernel(x, w_in, b_in, attn_sinks, rel_bias_table, w_branch_swa, w_branch_moba, w_out, ln1_gain, ln1_bias,
           w_group_router, b_group_router, w_expert_router, b_expert_router, w_expert_gate, w_expert_up,
           w_expert_down, ln2_gain, ln2_bias):
    assert w_in.shape[0] == DEPTH == 1
    B, S, D = x.shape
    T = B * S
    bf16 = jnp.bfloat16
    f32 = jnp.float32
    w = w_in[0]
    b = b_in[0]

    def cols(off, width):
        return w[:, off:off + width], b[off:off + width]

    wq_a, bq_a = cols(OFF_SWA_Q, SWA_Q_W)
    wk_a, bk_a = cols(OFF_SWA_K, SWA_KV_W)
    wv_a, bv_a = cols(OFF_SWA_V, SWA_KV_W)
    wq_b, bq_b = cols(OFF_MOBA_Q, MOBA_W)
    wk_b, bk_b = cols(OFF_MOBA_K, MOBA_W)
    wv_b, bv_b = cols(OFF_MOBA_V, MOBA_W)

    def dup_kv(t):
        parts = [t[..., i * HEAD_DIM:(i + 1) * HEAD_DIM] for i in range(SWA_KV_HEADS)]
        return jnp.concatenate([p for p in parts for _ in range(2)], axis=-1)

    qs = ATTN_SCALE * LOG2E
    wn = jnp.concatenate([wq_a * qs, dup_kv(wk_a), wq_b * qs, wk_b], axis=1).astype(bf16)
    bn = jnp.concatenate([bq_a * qs, dup_kv(bk_a), bq_b * qs, bk_b])[None, :].astype(f32)
    wt = jnp.concatenate([wv_a, wv_b], axis=1).T.astype(bf16)
    bt = jnp.concatenate([bv_a, bv_b])[:, None].astype(f32)

    qk, vt, eg_b, eu_b, ed_b = _inproj(x, wn, bn, wt, bt, w_expert_gate[0], w_expert_up[0], w_expert_down[0])

    bias_a, bias_b = _attention_bias_tables(rel_bias_table)
    y_a = _swa(attn_sinks[0].astype(f32), qk, vt, bias_a)
    y_b = _moba(qk, vt, bias_b)

    wg, bg = cols(OFF_GATE, 2 * D_MODEL)
    wr = jnp.zeros((ROUTER_ROWS, D), f32)
    wr = wr.at[0:N_GROUPS].set(w_group_router[0].T).at[8:8 + N_EXPERTS].set(w_expert_router[0].T)
    wr_hi = wr.astype(bf16)
    wr_lo = (wr - wr_hi.astype(f32)).astype(bf16)
    br = jnp.zeros((ROUTER_ROWS,), f32)
    br = br.at[0:N_GROUPS].set(b_group_router[0]).at[8:8 + N_EXPERTS].set(b_expert_router[0])[:, None]
    merge_args = (
        x.reshape(T, D), y_a, y_b,
        wg.astype(bf16), bg[None, :].astype(f32), w_branch_swa[0].astype(bf16), w_branch_moba[0].astype(bf16),
        w_out[0].astype(bf16), ln1_gain[0][None, :].astype(f32), ln1_bias[0][None, :].astype(f32),
        jnp.concatenate([wr_hi, wr_lo], axis=0), br)
    g2 = ln2_gain[0][None, :].astype(f32)
    b2 = ln2_bias[0][None, :].astype(f32)

    out = None
    row0 = 0
    for share in MOE_PART_SHARES:
        Tp = T * share // sum(MOE_PART_SHARES)
        x1, x1p, rinfo, counts = _merge(*merge_args, row0, Tp)
        sizes = counts[:, 0].astype(jnp.int32)
        padded_start, blk_expert, blk_valid, cap = _block_plan(sizes, Tp)
        dest = _dest(rinfo, padded_start)
        xs = _sc_scatter_rows(x1p, dest[0], dest[1], cap)
        y_buf = _experts(blk_expert, blk_valid, xs, eg_b, eu_b, ed_b)
        ypair = _sc_gather_rows(y_buf, dest[0:EXPERT_TOPK].reshape(-1))
        out = _final(x1, ypair, rinfo, g2, b2, out, row0, T)
        row0 += Tp
    return out.reshape(B, S, D)
```

```python
import functools
import math

import numpy as np
import jax
import jax.numpy as jnp
from jax import lax
from jax.experimental import pallas as pl
from jax.experimental.pallas import tpu as pltpu
from jax.experimental.pallas import tpu_sc as plsc

D_MODEL = 1024
HEAD_DIM = 64
SWA_Q_HEADS = 8
SWA_KV_HEADS = 2
SWA_GROUP = SWA_Q_HEADS // SWA_KV_HEADS
SWA_WINDOW = 128
SWA_BLOCK = 128
MOBA_HEADS = 8
MOBA_BLOCK = 256
MOBA_TOPK = 3
MOBA_LOOKAHEAD = 2
MOBA_UNROLL = 8
BF16_SUBLANES = 16
MOBA_VROWS = HEAD_DIM + BF16_SUBLANES
REL_BUCKETS = 32
REL_MAX_DIST = 128
N_GROUPS = 4
EXPERTS_PER_GROUP = 8
N_EXPERTS = N_GROUPS * EXPERTS_PER_GROUP
EXPERT_TOPK = 2
D_EXPERT = 512
LN_EPS = 1e-5
DEPTH = 1
DEEPNORM_ALPHA = (2.0 * DEPTH) ** 0.25
NEG_INF = -1e30
ATTN_SCALE = HEAD_DIM ** -0.5
LOG2E = math.log2(math.e)

SWA_Q_W = SWA_Q_HEADS * HEAD_DIM
SWA_KV_W = SWA_KV_HEADS * HEAD_DIM
MOBA_W = MOBA_HEADS * HEAD_DIM
OFF_SWA_Q = 0
OFF_SWA_K = OFF_SWA_Q + SWA_Q_W
OFF_SWA_V = OFF_SWA_K + SWA_KV_W
OFF_MOBA_Q = OFF_SWA_V + SWA_KV_W
OFF_MOBA_K = OFF_MOBA_Q + MOBA_W
OFF_MOBA_V = OFF_MOBA_K + MOBA_W
OFF_GATE = OFF_MOBA_V + MOBA_W

LANES = 128
QK_SWA_Q = 0
QK_SWA_K = QK_SWA_Q + SWA_Q_W
QK_MOBA_Q = QK_SWA_K + SWA_KV_HEADS * LANES
QK_MOBA_K = QK_MOBA_Q + MOBA_W
QK_W = QK_MOBA_K + MOBA_W
VT_SWA = 0
VT_MOBA = VT_SWA + SWA_KV_W
VT_W = VT_MOBA + MOBA_W

MOE_TM = 512
MOE_PARTS = 2
SC_CHUNK = 128
ROUTER_ROWS = 8 + N_EXPERTS
VMEM_LIMIT = 56 * 1024 * 1024

_NT = (((1,), (1,)), ((), ()))
_TN = (((0,), (0,)), ((), ()))


def _rel_bucket_np(dist):
    n = np.maximum(dist, 0)
    max_exact = REL_BUCKETS // 2
    nf = np.maximum(n, 1).astype(np.float32)
    large = max_exact + (np.log(nf / np.float32(max_exact)) / np.float32(math.log(REL_MAX_DIST / max_exact))
                         * np.float32(REL_BUCKETS - max_exact)).astype(np.int32)
    large = np.minimum(large, REL_BUCKETS - 1)
    return np.where(n < max_exact, n, large).astype(np.int32)


def _inproj_kernel(x_ref, wn_ref, bn_ref, wt_ref, bt_ref, eg_ref, eu_ref, ed_ref,
                   qk_ref, vt_ref, egb_ref, eub_ref, edb_ref):
    xb = x_ref[0].astype(jnp.bfloat16)
    qk = jnp.dot(xb, wn_ref[...], preferred_element_type=jnp.float32) + bn_ref[...]
    qk_ref[0] = qk.astype(jnp.bfloat16)
    vt = lax.dot_general(wt_ref[...], xb, _NT, preferred_element_type=jnp.float32) + bt_ref[...]
    vt_ref[0] = vt.astype(jnp.bfloat16)
    egb_ref[...] = eg_ref[...].astype(jnp.bfloat16)
    eub_ref[...] = eu_ref[...].astype(jnp.bfloat16)
    edb_ref[...] = ed_ref[...].astype(jnp.bfloat16)


def _inproj(x, wn, bn, wt, bt, w_gate, w_up, w_down, tm=512):
    B, S, D = x.shape
    nt = S // tm
    steps = B * nt

    def sliced(w):
        E, R, C = w.shape
        assert (E * R) % (steps * BF16_SUBLANES) == 0
        return w.reshape(steps, E * R // steps, C)

    eg, eu, ed = sliced(w_gate), sliced(w_up), sliced(w_down)
    espec = lambda w: pl.BlockSpec((1,) + w.shape[1:], lambda b, i: (b * nt + i, 0, 0))
    qk, vt, egb, eub, edb = pl.pallas_call(
        _inproj_kernel,
        grid=(B, nt),
        in_specs=[
            pl.BlockSpec((1, tm, D), lambda b, i: (b, i, 0)),
            pl.BlockSpec((D, QK_W), lambda b, i: (0, 0)),
            pl.BlockSpec((1, QK_W), lambda b, i: (0, 0)),
            pl.BlockSpec((VT_W, D), lambda b, i: (0, 0)),
            pl.BlockSpec((VT_W, 1), lambda b, i: (0, 0)),
            espec(eg), espec(eu), espec(ed),
        ],
        out_specs=[
            pl.BlockSpec((1, tm, QK_W), lambda b, i: (b, i, 0)),
            pl.BlockSpec((1, VT_W, tm), lambda b, i: (b, 0, i)),
            espec(eg), espec(eu), espec(ed),
        ],
        out_shape=[
            jax.ShapeDtypeStruct((B, S, QK_W), jnp.bfloat16),
            jax.ShapeDtypeStruct((B, VT_W, S), jnp.bfloat16),
            jax.ShapeDtypeStruct(eg.shape, jnp.bfloat16),
            jax.ShapeDtypeStruct(eu.shape, jnp.bfloat16),
            jax.ShapeDtypeStruct(ed.shape, jnp.bfloat16),
        ],
        compiler_params=pltpu.CompilerParams(
            dimension_semantics=("parallel", "parallel"), vmem_limit_bytes=VMEM_LIMIT),
        name="inproj",
    )(x, wn, bn, wt, bt, eg, eu, ed)
    return qk, vt, egb.reshape(w_gate.shape), eub.reshape(w_up.shape), edb.reshape(w_down.shape)


def _swa_kernel(sink_ref, q_ref, k_ref, vt_ref, bias_ref, o_ref, vaug_ref, *s_bufs):
    S = q_ref.shape[1]
    nblk = S // SWA_BLOCK
    half = HEAD_DIM
    vt = vt_ref[0]
    rows = lax.broadcasted_iota(jnp.int32, vt.shape, 0)
    one = jnp.ones_like(vt)
    vaug_ref[0] = jnp.where(rows < half, vt, one)
    vaug_ref[1] = jnp.where(rows < half, one, vt)
    lane = lax.broadcasted_iota(jnp.int32, (SWA_BLOCK, LANES), 1)
    col = lax.broadcasted_iota(jnp.int32, (1, 2 * SWA_BLOCK), 1)
    npair = SWA_Q_HEADS // 2

    def expand(specs):
        return [(qs, ks, nk, boff, pair, (2 * pair) // SWA_GROUP) for qs, ks, nk, boff in specs for pair in range(npair)]

    def score(specs):
        ss = []
        for qs, ks, nk, boff, pair, kv in expand(specs):
            qblk = q_ref[0, pl.ds(qs, SWA_BLOCK), pair * LANES:(pair + 1) * LANES]
            zero = jnp.zeros_like(qblk)
            q2 = jnp.concatenate([jnp.where(lane < half, qblk, zero), jnp.where(lane >= half, qblk, zero)], axis=0)
            kblk = k_ref[0, pl.ds(ks, nk), kv * LANES:(kv + 1) * LANES]
            s = lax.dot_general(kblk, q2, _NT, preferred_element_type=jnp.float32)
            ss.append(s + bias_ref[pair, boff:boff + nk, :])
        return ss

    def finish(specs, ss):
        items = expand(specs)
        stats = []
        for (qs, ks, nk, boff, pair, kv), s in zip(items, ss):
            sink = jnp.where(col < SWA_BLOCK, sink_ref[2 * pair], sink_ref[2 * pair + 1]) * LOG2E
            m = jnp.maximum(_colmax(s), sink)
            stats.append((jnp.exp2(s - m).astype(jnp.bfloat16), jnp.exp2(sink - m)))
        rs = [jnp.dot(vaug_ref[kv, :, pl.ds(ks, nk)], p, preferred_element_type=jnp.float32)
              for (qs, ks, nk, boff, pair, kv), (p, _) in zip(items, stats)]
        outs = []
        for (qs, ks, nk, boff, pair, kv), (_, esink), r in zip(items, stats, rs):
            l = (r[half:half + 1, :] if kv == 0 else r[0:1, :]) + esink
            o = (r[0:half, :] if kv == 0 else r[half:, :]) / l
            outs.append(jnp.concatenate([o[:, :SWA_BLOCK], o[:, SWA_BLOCK:]], axis=0))
        for i, (qs, _, _, _) in enumerate(specs):
            ot = jnp.concatenate(outs[i * npair:(i + 1) * npair], axis=0)
            o_ref[0, :, pl.ds(qs, SWA_BLOCK)] = ot.astype(jnp.bfloat16)

    first = [(0, 0, SWA_BLOCK, SWA_BLOCK), (SWA_BLOCK, 0, 2 * SWA_BLOCK, 0)]
    finish(first, score(first))

    assert nblk % 2 == 0
    n_groups = nblk // 2 - 1
    nbuf = len(s_bufs)

    def group_specs(g):
        specs = []
        for n in (2 * g + 2, 2 * g + 3):
            qs = pl.multiple_of(n * SWA_BLOCK, SWA_BLOCK)
            specs.append((qs, pl.multiple_of(qs - SWA_BLOCK, SWA_BLOCK), 2 * SWA_BLOCK, 0))
        return specs

    def score_to(g, buf):
        for k, s in enumerate(score(group_specs(g))):
            buf[k] = s

    def finish_from(g, buf):
        finish(group_specs(g), [buf[k] for k in range(2 * npair)])

    for g in range(min(2, n_groups)):
        score_to(g, s_bufs[g])

    def body(t, carry):
        for j in range(nbuf):
            score_to(nbuf * t + 2 + j, s_bufs[(2 + j) % nbuf])
            finish_from(nbuf * t + j, s_bufs[j])
        return carry

    n_trips = max(n_groups - 2, 0) // nbuf
    lax.fori_loop(0, n_trips, body, 0)
    for g in range(nbuf * n_trips, n_groups):
        if g + 2 < n_groups:
            score_to(g + 2, s_bufs[(g + 2) % nbuf])
        finish_from(g, s_bufs[g % nbuf])


def _swa(sinks, qk, vt, bias):
    B, S, _ = qk.shape
    return pl.pallas_call(
        _swa_kernel,
        grid_spec=pltpu.PrefetchScalarGridSpec(
            num_scalar_prefetch=1,
            grid=(B,),
            in_specs=[
                pl.BlockSpec((1, S, SWA_Q_W), lambda b, s: (b, 0, QK_SWA_Q // SWA_Q_W)),
                pl.BlockSpec((1, S, 2 * LANES), lambda b, s: (b, 0, QK_SWA_K // (2 * LANES))),
                pl.BlockSpec((1, SWA_KV_W, S), lambda b, s: (b, VT_SWA // SWA_KV_W, 0)),
                pl.BlockSpec((SWA_Q_HEADS // 2, 2 * SWA_BLOCK, 2 * SWA_BLOCK), lambda b, s: (0, 0, 0)),
            ],
            out_specs=pl.BlockSpec((1, SWA_Q_W, S), lambda b, s: (b, 0, 0)),
            scratch_shapes=[pltpu.VMEM((SWA_KV_HEADS, SWA_KV_W, S), jnp.bfloat16)]
            + [pltpu.VMEM((SWA_Q_HEADS, 2 * SWA_BLOCK, 2 * SWA_BLOCK), jnp.float32)] * 4,
        ),
        out_shape=jax.ShapeDtypeStruct((B, SWA_Q_W, S), jnp.bfloat16),
        compiler_params=pltpu.CompilerParams(
            dimension_semantics=("parallel",), vmem_limit_bytes=VMEM_LIMIT),
        name="swa",
    )(sinks, qk, qk, vt, bias)


def _colmax(s):
    while s.shape[0] > 8:
        h = s.shape[0] // 2
        s = jnp.maximum(s[:h], s[h:])
    return jnp.max(s, axis=0, keepdims=True)


def _moba_kernel(fqb_ref, fc_ref, q_ref, k_ref, vt_ref, bias_ref, o_ref,
                 vaug_ref, gate_ref, rank_ref, sel_ref, far_ref, m_scr, acc_scr, sa_scr, sb_scr, sc_scr, sd_scr):
    S = q_ref.shape[1]
    nb = S // MOBA_BLOCK
    half = HEAD_DIM
    BLK = MOBA_BLOCK

    ones_rows = jnp.ones((MOBA_VROWS - half, S), jnp.bfloat16)
    for h in range(2):
        vaug_ref[h, 0:half, :] = vt_ref[0, h * half:(h + 1) * half, :]
        vaug_ref[h, half:, :] = ones_rows

    kf = k_ref[0].astype(jnp.float32).reshape(nb, BLK, LANES)
    kmean = jnp.sum(kf, axis=1) * (1.0 / BLK)
    k_hi = kmean.astype(jnp.bfloat16)
    k_lo = (kmean - k_hi.astype(jnp.float32)).astype(jnp.bfloat16)
    kcat = jnp.concatenate([k_hi, k_lo], axis=0)

    lane_q = lax.broadcasted_iota(jnp.int32, (S, LANES), 1)
    brow = lax.broadcasted_iota(jnp.int32, (nb, S), 0)
    qblk_of = lax.broadcasted_iota(jnp.int32, (nb, S), 1) // BLK
    qall = q_ref[0]
    for h in range(2):
        qh = jnp.where((lane_q < half) if h == 0 else (lane_q >= half), qall, jnp.zeros_like(qall))
        g2 = lax.dot_general(kcat, qh, _NT, preferred_element_type=jnp.float32)
        gate_ref[...] = g2[0:nb] + g2[nb:2 * nb]
        rank_ref[...] = jnp.zeros_like(rank_ref)
        for m in range(nb - 1):
            lo = (m + 1) * BLK
            G = gate_ref[:, lo:]
            gm = gate_ref[m:m + 1, lo:]
            ge = jnp.where(gm >= G, 1.0, 0.0)
            gt = jnp.where(gm > G, 1.0, 0.0)
            brow_m = lax.broadcasted_iota(jnp.int32, (nb, S - lo), 0)
            rank_ref[:, lo:] += jnp.where(brow_m > m, ge, gt)
        top = rank_ref[...] < float(MOBA_TOPK)
        sel_ref[h] = jnp.where((brow < qblk_of) & top, 0.0, NEG_INF)
        far_ref[h] = jnp.where((brow < qblk_of - 1) & top, 0.0, NEG_INF)

    lane = lax.broadcasted_iota(jnp.int32, (BLK, LANES), 1)

    def head_q(qs, h):
        qblk = q_ref[0, pl.ds(qs, BLK), :]
        return jnp.where((lane < half) if h == 0 else (lane >= half), qblk, jnp.zeros_like(qblk))

    m0 = jnp.full((1, BLK), NEG_INF, jnp.float32)
    acc0 = jnp.zeros((MOBA_VROWS, BLK), jnp.float32)

    def item(h, qb, qs, ks, nk, bias, sel_a, sel_b, m, acc):
        return dict(h=h, qb=qb, qs=qs, ks=ks, nk=nk, bias=bias, sel_a=sel_a, sel_b=sel_b, m=m, acc=acc)

    def near_item(qb, h):
        qs = pl.multiple_of(qb * BLK, BLK)
        ps = pl.multiple_of(qs - BLK, BLK)
        sel_prev = sel_ref[h, pl.ds(qb - 1, 1), pl.ds(qs, BLK)]
        return item(h, qb, qs, ps, 2 * BLK, bias_ref[h], sel_prev, None, m0, acc0)

    def far_item(i, h):
        qb = fqb_ref[i]
        c = fc_ref[i]
        qs = pl.multiple_of(qb * BLK, BLK)
        ks = pl.multiple_of(c * (2 * BLK), 2 * BLK)
        sel_a = far_ref[h, pl.ds(2 * c, 1), pl.ds(qs, BLK)]
        sel_b = far_ref[h, pl.ds(2 * c + 1, 1), pl.ds(qs, BLK)]
        return item(h, qb, qs, ks, 2 * BLK, None, sel_a, sel_b, m_scr[h, qb], acc_scr[h, qb])

    def scores(it):
        kslab = k_ref[0, pl.ds(it["ks"], it["nk"]), :]
        s = lax.dot_general(kslab, head_q(it["qs"], it["h"]), _NT,
                            preferred_element_type=jnp.float32)
        return s if it["bias"] is None else s + it["bias"]

    def probs(it, s):
        m, sel_a, sel_b = it["m"], it["sel_a"], it["sel_b"]
        if it["nk"] == BLK:
            m_new = jnp.maximum(m, _colmax(s))
            p = jnp.exp2(s - m_new)
        else:
            cm_a = _colmax(s[:BLK]) + sel_a
            cm_b = _colmax(s[BLK:])
            if sel_b is not None:
                cm_b = cm_b + sel_b
            m_new = jnp.maximum(m, jnp.maximum(cm_a, cm_b))
            p_a = jnp.exp2(s[:BLK] - (m_new - sel_a))
            p_b = jnp.exp2(s[BLK:] - (m_new if sel_b is None else m_new - sel_b))
            p = jnp.concatenate([p_a, p_b], axis=0)
        return m_new, jnp.exp2(m - m_new), p.astype(jnp.bfloat16)

    def run(items):
        ss = [scores(it) for it in items]
        ps = [probs(it, s) for it, s in zip(items, ss)]
        pvs = [jnp.dot(vaug_ref[it["h"], :, pl.ds(it["ks"], it["nk"])], p, preferred_element_type=jnp.float32)
               for it, (_, _, p) in zip(items, ps)]
        for it, (m_new, alpha, _), pv in zip(items, ps, pvs):
            m_scr[it["h"], it["qb"]] = m_new
            acc_scr[it["h"], it["qb"]] = it["acc"] * alpha + pv

    run([item(h, 0, 0, 0, BLK, bias_ref[h, BLK:, :], None, None, m0, acc0) for h in range(2)]
        + [near_item(nb - 1, h) for h in range(2)])

    bufs = (sa_scr, sb_scr, sc_scr, sd_scr)
    nbuf = len(bufs)

    def finish(items, s_buf):
        ps = [probs(it, s_buf[k]) for k, it in enumerate(items)]
        pvs = [jnp.dot(vaug_ref[it["h"], :, pl.ds(it["ks"], it["nk"])], p, preferred_element_type=jnp.float32)
               for it, (_, _, p) in zip(items, ps)]
        for it, (m_new, alpha, _), pv in zip(items, ps, pvs):
            m_scr[it["h"], it["qb"]] = m_new
            acc_scr[it["h"], it["qb"]] = it["acc"] * alpha + pv

    def pipelined(n_groups, score, items_of):
        for g in range(min(MOBA_LOOKAHEAD, n_groups)):
            score(g, bufs[g])

        def body(t, carry):
            for j in range(MOBA_UNROLL):
                score(MOBA_UNROLL * t + MOBA_LOOKAHEAD + j, bufs[(MOBA_LOOKAHEAD + j) % nbuf])
                finish(items_of(MOBA_UNROLL * t + j), bufs[j % nbuf])
            return carry

        n_trips = max(n_groups - MOBA_LOOKAHEAD, 0) // MOBA_UNROLL
        lax.fori_loop(0, n_trips, body, 0)
        for g in range(MOBA_UNROLL * n_trips, n_groups):
            if g + MOBA_LOOKAHEAD < n_groups:
                score(g + MOBA_LOOKAHEAD, bufs[(g + MOBA_LOOKAHEAD) % nbuf])
            finish(items_of(g), bufs[g % nbuf])

    near_pairs = nb // 2 - 1

    def near_group(g):
        return [(qb, h) for qb in (g + 1, g + 1 + near_pairs) for h in range(2)]

    def near_score(g, s_buf):
        for k, (qb, h) in enumerate(near_group(g)):
            qs = pl.multiple_of(qb * BLK, BLK)
            ps = pl.multiple_of(qs - BLK, BLK)
            s_buf[k] = lax.dot_general(k_ref[0, pl.ds(ps, 2 * BLK), :], head_q(qs, h), _NT,
                                       preferred_element_type=jnp.float32) + bias_ref[h]

    pipelined(near_pairs, near_score, lambda g: [near_item(qb, h) for qb, h in near_group(g)])

    n_far_groups = fqb_ref.shape[0] // 2

    def far_group(g):
        return [(j, h) for j in (g, g + n_far_groups) for h in range(2)]

    def far_score(g, s_buf):
        for k, (j, h) in enumerate(far_group(g)):
            qs = pl.multiple_of(fqb_ref[j] * BLK, BLK)
            ks = pl.multiple_of(fc_ref[j] * (2 * BLK), 2 * BLK)
            s_buf[k] = lax.dot_general(k_ref[0, pl.ds(ks, 2 * BLK), :], head_q(qs, h), _NT,
                                       preferred_element_type=jnp.float32)

    pipelined(n_far_groups, far_score, lambda g: [far_item(j, h) for j, h in far_group(g)])

    def out_body(t, carry):
        for qb in (2 * t, 2 * t + 1):
            qs = pl.multiple_of(qb * BLK, BLK)
            a0 = acc_scr[0, qb]
            a1 = acc_scr[1, qb]
            ot = jnp.concatenate([a0[0:half] / a0[half:half + 1, :], a1[0:half] / a1[half:half + 1, :]],
                                 axis=0)
            o_ref[0, :, pl.ds(qs, BLK)] = ot.astype(jnp.bfloat16)
        return carry

    lax.fori_loop(0, nb // 2, out_body, 0)


def _moba_far_items(nb):
    items = [(qb, c) for qb in range(2, nb) for c in range(qb // 2)]
    n = len(items)
    assert n % 2 == 0 and all(items[i][0] != items[i + n // 2][0] for i in range(n // 2))
    return np.array([it[0] for it in items], np.int32), np.array([it[1] for it in items], np.int32)


def _moba(qk, vt, bias):
    B, S, _ = qk.shape
    nb = S // MOBA_BLOCK
    npair = MOBA_W // LANES
    far_qb, far_c = _moba_far_items(nb)
    return pl.pallas_call(
        _moba_kernel,
        grid_spec=pltpu.PrefetchScalarGridSpec(
            num_scalar_prefetch=2,
            grid=(B, npair),
            in_specs=[
                pl.BlockSpec((1, S, LANES), lambda b, p, fq, fc: (b, 0, QK_MOBA_Q // LANES + p)),
                pl.BlockSpec((1, S, LANES), lambda b, p, fq, fc: (b, 0, QK_MOBA_K // LANES + p)),
                pl.BlockSpec((1, LANES, S), lambda b, p, fq, fc: (b, VT_MOBA // LANES + p, 0)),
                pl.BlockSpec((2, 2 * MOBA_BLOCK, MOBA_BLOCK), lambda b, p, fq, fc: (p, 0, 0)),
            ],
            out_specs=pl.BlockSpec((1, LANES, S), lambda b, p, fq, fc: (b, p, 0)),
            scratch_shapes=[
                pltpu.VMEM((2, MOBA_VROWS, S), jnp.bfloat16),
                pltpu.VMEM((nb, S), jnp.float32),
                pltpu.VMEM((nb, S), jnp.float32),
                pltpu.VMEM((2, nb, S), jnp.float32),
                pltpu.VMEM((2, nb, S), jnp.float32),
                pltpu.VMEM((2, nb, 1, MOBA_BLOCK), jnp.float32),
                pltpu.VMEM((2, nb, MOBA_VROWS, MOBA_BLOCK), jnp.float32),
            ] + [pltpu.VMEM((4, 2 * MOBA_BLOCK, MOBA_BLOCK), jnp.float32)] * 4,
        ),
        out_shape=jax.ShapeDtypeStruct((B, MOBA_W, S), jnp.bfloat16),
        compiler_params=pltpu.CompilerParams(
            dimension_semantics=("parallel", "parallel"), vmem_limit_bytes=VMEM_LIMIT),
        name="moba",
    )(jnp.asarray(far_qb), jnp.asarray(far_c), qk, qk, vt, bias)


def _layer_norm(h, gain, bias):
    mu = jnp.mean(h, axis=-1, keepdims=True)
    c = h - mu
    var = jnp.mean(c * c, axis=-1, keepdims=True)
    return c * lax.rsqrt(var + LN_EPS) * gain + bias


def _pack_bf16_pair(a, b):
    ia = lax.bitcast_convert_type(a.astype(jnp.bfloat16).astype(jnp.float32), jnp.int32)
    ib = lax.bitcast_convert_type(b.astype(jnp.bfloat16).astype(jnp.float32), jnp.int32)
    return lax.shift_right_logical(ia, 16) | ib


def _unpack_bf16_pair(w):
    lo = lax.bitcast_convert_type(lax.shift_left(w, 16), jnp.float32)
    hi = lax.bitcast_convert_type(w & jnp.int32(-65536), jnp.float32)
    return lo.astype(jnp.bfloat16), hi.astype(jnp.bfloat16)


def _merge_kernel(x_ref, ya_ref, yb_ref, wg_ref, bg_ref, wa_ref, wb_ref, wo_ref, g1_ref, b1_ref,
                  wr_ref, br_ref, tri_ref, x1_ref, x1p_ref, r_ref, cnt_ref):
    @pl.when(pl.program_id(0) == 0)
    def _():
        cnt_ref[...] = jnp.zeros_like(cnt_ref)

    ts = tri_ref.shape[0]
    subs = [pl.ds(r0, ts) for r0 in range(0, x_ref.shape[0], ts)]
    pre = [_merge_matmuls(x_ref[rows, :], ya_ref[0, :, rows], yb_ref[0, :, rows], wg_ref, bg_ref, wa_ref, wb_ref, wo_ref)
           for rows in subs]
    for rows, h in zip(subs, pre):
        _merge_route(h, rows, g1_ref, b1_ref, wr_ref, br_ref, tri_ref, x1_ref, x1p_ref, r_ref, cnt_ref)


def _merge_matmuls(x, ya_t, yb_t, wg_ref, bg_ref, wa_ref, wb_ref, wo_ref):
    xb = x.astype(jnp.bfloat16)
    z = jnp.dot(xb, wg_ref[...], preferred_element_type=jnp.float32) + bg_ref[...]
    gates = 1.0 / (1.0 + jnp.exp(-z))
    pa = lax.dot_general(ya_t, wa_ref[...], _TN, preferred_element_type=jnp.float32)
    pb = lax.dot_general(yb_t, wb_ref[...], _TN, preferred_element_type=jnp.float32)
    merged = gates[:, :D_MODEL] * pa + gates[:, D_MODEL:] * pb
    mixed = jnp.dot(merged.astype(jnp.bfloat16), wo_ref[...], preferred_element_type=jnp.float32)
    return DEEPNORM_ALPHA * x + mixed


def _merge_route(h, rows, g1_ref, b1_ref, wr_ref, br_ref, tri_ref, x1_ref, x1p_ref, r_ref, cnt_ref):
    x1 = _layer_norm(h, g1_ref[...], b1_ref[...])
    x1_ref[rows, :] = x1
    x1_hi = x1.astype(jnp.bfloat16)
    x1p_ref[rows, :] = _pack_bf16_pair(x1[:, :D_MODEL // 2], x1[:, D_MODEL // 2:])
    x1_lo = (x1 - x1_hi.astype(jnp.float32)).astype(jnp.bfloat16)

    R = ROUTER_ROWS
    l1 = lax.dot_general(wr_ref[...], x1_hi, _NT, preferred_element_type=jnp.float32)
    l2 = lax.dot_general(wr_ref[0:R, :], x1_lo, _NT, preferred_element_type=jnp.float32)
    L = l1[0:R] + l1[R:2 * R] + l2 + br_ref[...]
    tm = x1.shape[0]
    row = lax.broadcasted_iota(jnp.int32, (8, tm), 0)
    big = jnp.float32(-3e38)
    gl = jnp.where(row < N_GROUPS, L[0:8], big)
    gmax = jnp.max(gl, axis=0, keepdims=True)
    g_idx = jnp.min(jnp.where(gl == gmax, row, 8), axis=0, keepdims=True)
    gsum = jnp.sum(jnp.where(row < N_GROUPS, jnp.exp(gl - gmax), 0.0), axis=0, keepdims=True)
    g_prob = 1.0 / gsum
    E = L[8 + 8 * (N_GROUPS - 1):8 + 8 * N_GROUPS]
    for g in range(N_GROUPS - 2, -1, -1):
        E = jnp.where(g_idx == g, L[8 + 8 * g:16 + 8 * g], E)
    t0 = jnp.max(E, axis=0, keepdims=True)
    loc0 = jnp.min(jnp.where(E == t0, row, 8), axis=0, keepdims=True)
    E2 = jnp.where(row == loc0, big, E)
    t1 = jnp.max(E2, axis=0, keepdims=True)
    loc1 = jnp.min(jnp.where(E2 == t1, row, 8), axis=0, keepdims=True)
    ex = jnp.exp(t1 - t0)
    w0 = g_prob / (1.0 + ex)
    w1 = g_prob * ex / (1.0 + ex)
    e0i = g_idx * EXPERTS_PER_GROUP + loc0
    e1i = g_idx * EXPERTS_PER_GROUP + loc1

    erow = lax.broadcasted_iota(jnp.int32, (N_EXPERTS, tm), 0)
    oh0 = jnp.where(erow == e0i, 1.0, 0.0)
    oh1 = jnp.where(erow == e1i, 1.0, 0.0)
    both = oh0 + oh1
    prefix = jnp.dot(both.astype(jnp.bfloat16), tri_ref[...], preferred_element_type=jnp.float32)
    prefix = prefix + cnt_ref[:, 0:1]
    rank0 = jnp.sum(oh0 * prefix, axis=0, keepdims=True)
    rank1 = jnp.sum(oh1 * prefix, axis=0, keepdims=True)
    cnt_ref[...] = cnt_ref[...] + jnp.sum(both, axis=1, keepdims=True)

    vals = (e0i.astype(jnp.float32), e1i.astype(jnp.float32), w0, w1, rank0, rank1)
    out = jnp.zeros((8, tm), jnp.float32)
    for k, v in enumerate(vals):
        out = jnp.where(row == k, v, out)
    r_ref[:, rows] = out


def _merge(x2, ya, yb, wg, bg, wa, wb, wo, g1, b1, wr, br, part, n_parts, tm=512, ts=256):
    D = x2.shape[1]
    T = x2.shape[0] // n_parts
    S = ya.shape[2]
    per_seq = S // tm
    assert per_seq * tm == S
    off = part * (T // tm)
    const = lambda i: (0, 0)
    seq_tile = lambda i: ((i + off) // per_seq, 0, (i + off) % per_seq)
    tri = jnp.triu(jnp.ones((ts, ts), jnp.bfloat16), k=1)
    return pl.pallas_call(
        _merge_kernel,
        grid=(T // tm,),
        in_specs=[
            pl.BlockSpec((tm, D), lambda i: (i + off, 0)),
            pl.BlockSpec((1, SWA_Q_W, tm), seq_tile),
            pl.BlockSpec((1, MOBA_W, tm), seq_tile),
            pl.BlockSpec((D, 2 * D), const),
            pl.BlockSpec((1, 2 * D), const),
            pl.BlockSpec((SWA_Q_W, D), const),
            pl.BlockSpec((MOBA_W, D), const),
            pl.BlockSpec((D, D), const),
            pl.BlockSpec((1, D), const),
            pl.BlockSpec((1, D), const),
            pl.BlockSpec((2 * ROUTER_ROWS, D), const),
            pl.BlockSpec((ROUTER_ROWS, 1), const),
            pl.BlockSpec((ts, ts), const),
        ],
        out_specs=[
            pl.BlockSpec((tm, D), lambda i: (i, 0)),
            pl.BlockSpec((tm, D // 2), lambda i: (i, 0)),
            pl.BlockSpec((8, tm), lambda i: (0, i)),
            pl.BlockSpec((N_EXPERTS, LANES), const),
        ],
        out_shape=[
            jax.ShapeDtypeStruct((T, D), jnp.float32),
            jax.ShapeDtypeStruct((T, D // 2), jnp.int32),
            jax.ShapeDtypeStruct((8, T), jnp.float32),
            jax.ShapeDtypeStruct((N_EXPERTS, LANES), jnp.float32),
        ],
        compiler_params=pltpu.CompilerParams(
            dimension_semantics=("arbitrary",), vmem_limit_bytes=VMEM_LIMIT),
        name="merge",
    )(x2, ya, yb, wg, bg, wa, wb, wo, g1, b1, wr, br, tri)


def _dest_kernel(r_ref, ps_ref, d_ref):
    tt = r_ref.shape[1]
    erow = lax.broadcasted_iota(jnp.int32, (N_EXPERTS, tt), 0)
    row = lax.broadcasted_iota(jnp.int32, (8, tt), 0)
    ps = ps_ref[...]
    out = jnp.zeros((8, tt), jnp.float32)
    for k in range(EXPERT_TOPK):
        e = r_ref[k:k + 1, :].astype(jnp.int32)
        start = jnp.sum(jnp.where(erow == e, ps, 0.0), axis=0, keepdims=True)
        out = jnp.where(row == k, start + r_ref[4 + k:5 + k, :], out)
    d_ref[...] = out.astype(jnp.int32)


def _dest(rinfo, padded_start, tt=8192):
    T = rinfo.shape[1]
    tt = min(tt, T)
    return pl.pallas_call(
        _dest_kernel,
        grid=(T // tt,),
        in_specs=[pl.BlockSpec((8, tt), lambda i: (0, i)), pl.BlockSpec((N_EXPERTS, 1), lambda i: (0, 0))],
        out_specs=pl.BlockSpec((8, tt), lambda i: (0, i)),
        out_shape=jax.ShapeDtypeStruct((8, T), jnp.int32),
        compiler_params=pltpu.CompilerParams(dimension_semantics=("parallel",)),
        name="dest",
    )(rinfo, padded_start.astype(jnp.float32)[:, None])


def _sc_workers():
    info = plsc.get_sparse_core_info()
    return info.num_cores, info.num_subcores


def _sc_scatter_rows(src, dest0, dest1, cap):
    T, W = src.shape
    nc, ns = _sc_workers()
    per_w = T // (nc * ns)
    assert per_w * nc * ns == T and per_w % SC_CHUNK == 0
    mesh = plsc.VectorSubcoreMesh(core_axis_name="c", subcore_axis_name="s")

    @functools.partial(
        pl.kernel, mesh=mesh,
        out_type=jax.ShapeDtypeStruct((cap, W), src.dtype),
        scratch_types=[pltpu.VMEM((SC_CHUNK,), jnp.int32), pltpu.VMEM((SC_CHUNK, W), src.dtype)],
    )
    def scatter(src_hbm, d0_hbm, d1_hbm, out_hbm, idx_v, rows_v):
        wid = lax.axis_index("s") * nc + lax.axis_index("c")
        base = wid * per_w

        @pl.loop(0, per_w // SC_CHUNK)
        def _(c):
            off = pl.multiple_of(base + c * SC_CHUNK, SC_CHUNK)
            pltpu.sync_copy(src_hbm.at[pl.ds(off, SC_CHUNK)], rows_v)
            for d_hbm in (d0_hbm, d1_hbm):
                pltpu.sync_copy(d_hbm.at[pl.ds(off, SC_CHUNK)], idx_v)
                pltpu.sync_copy(rows_v, out_hbm.at[idx_v])

    return scatter(src, dest0, dest1)


def _sc_gather_rows(table, idx):
    N = idx.shape[0]
    W = table.shape[1]
    nc, ns = _sc_workers()
    per_w = N // (nc * ns)
    assert per_w * nc * ns == N and per_w % SC_CHUNK == 0
    mesh = plsc.VectorSubcoreMesh(core_axis_name="c", subcore_axis_name="s")

    @functools.partial(
        pl.kernel, mesh=mesh,
        out_type=jax.ShapeDtypeStruct((N, W), table.dtype),
        scratch_types=[pltpu.VMEM((SC_CHUNK,), jnp.int32), pltpu.VMEM((SC_CHUNK, W), table.dtype)],
    )
    def gather(table_hbm, idx_hbm, out_hbm, idx_v, rows_v):
        wid = lax.axis_index("s") * nc + lax.axis_index("c")
        base = wid * per_w

        @pl.loop(0, per_w // SC_CHUNK)
        def _(c):
            off = pl.multiple_of(base + c * SC_CHUNK, SC_CHUNK)
            pltpu.sync_copy(idx_hbm.at[pl.ds(off, SC_CHUNK)], idx_v)
            pltpu.sync_copy(table_hbm.at[idx_v], rows_v)
            pltpu.sync_copy(rows_v, out_hbm.at[pl.ds(off, SC_CHUNK)])

    return gather(table, idx)


def _expert_kernel(be_ref, nv_ref, x_ref, wg_ref, wu_ref, wd_ref, y_ref):
    i = pl.program_id(0)
    nv = nv_ref[i]

    @pl.when(nv > 0)
    def _():
        lo, hi = _unpack_bf16_pair(x_ref[...])
        xb = jnp.concatenate([lo, hi], axis=1)
        rows = lax.broadcasted_iota(jnp.int32, xb.shape, 0)
        xb = jnp.where(rows < nv, xb, jnp.zeros_like(xb))
        g = jnp.dot(xb, wg_ref[0], preferred_element_type=jnp.float32)
        u = jnp.dot(xb, wu_ref[0], preferred_element_type=jnp.float32)
        act = (g / (1.0 + jnp.exp(-g))) * u
        y = jnp.dot(act.astype(jnp.bfloat16), wd_ref[0], preferred_element_type=jnp.float32)
        y_ref[...] = _pack_bf16_pair(y[:, :D_MODEL // 2], y[:, D_MODEL // 2:])

    @pl.when(nv <= 0)
    def _():
        y_ref[...] = jnp.zeros_like(y_ref)


def _experts(blk_expert, blk_valid, xs, wg, wu, wd):
    cap, DW = xs.shape
    D = 2 * DW
    n_blocks = cap // MOE_TM
    return pl.pallas_call(
        _expert_kernel,
        grid_spec=pltpu.PrefetchScalarGridSpec(
            num_scalar_prefetch=2,
            grid=(n_blocks,),
            in_specs=[
                pl.BlockSpec((MOE_TM, DW), lambda i, be, nv: (i, 0)),
                pl.BlockSpec((1, D, D_EXPERT), lambda i, be, nv: (be[i], 0, 0)),
                pl.BlockSpec((1, D, D_EXPERT), lambda i, be, nv: (be[i], 0, 0)),
                pl.BlockSpec((1, D_EXPERT, D), lambda i, be, nv: (be[i], 0, 0)),
            ],
            out_specs=pl.BlockSpec((MOE_TM, DW), lambda i, be, nv: (i, 0)),
        ),
        out_shape=jax.ShapeDtypeStruct((cap, DW), jnp.int32),
        compiler_params=pltpu.CompilerParams(
            dimension_semantics=("arbitrary",), vmem_limit_bytes=VMEM_LIMIT),
        name="experts",
    )(blk_expert, blk_valid, xs, wg, wu, wd)


def _final_kernel(x1_ref, y0_ref, y1_ref, w_ref, g2_ref, b2_ref, *rest):
    o_ref = rest[-1]
    w = w_ref[...].T
    halves = []
    for part in range(2):
        y0 = _unpack_bf16_pair(y0_ref[...])[part].astype(jnp.float32)
        y1 = _unpack_bf16_pair(y1_ref[...])[part].astype(jnp.float32)
        halves.append(y0 * w[:, 2:3] + y1 * w[:, 3:4])
    moe = jnp.concatenate(halves, axis=1)
    o_ref[...] = _layer_norm(DEEPNORM_ALPHA * x1_ref[...] + moe, g2_ref[...], b2_ref[...])


def _final(x1, ypair, rinfo, g2, b2, out_prev, part, n_parts, tm=1024):
    T, D = x1.shape
    const = lambda i: (0, 0)
    nt = T // tm
    off = part * nt
    in_specs = [
        pl.BlockSpec((tm, D), lambda i: (i, 0)),
        pl.BlockSpec((tm, D // 2), lambda i: (i, 0)),
        pl.BlockSpec((tm, D // 2), lambda i: (i + nt, 0)),
        pl.BlockSpec((8, tm), lambda i: (0, i)),
        pl.BlockSpec((1, D), const),
        pl.BlockSpec((1, D), const),
    ]
    args = [x1, ypair, ypair, rinfo, g2, b2]
    aliases = {}
    if out_prev is not None:
        in_specs.append(pl.BlockSpec(memory_space=pl.ANY))
        args.append(out_prev)
        aliases = {len(args) - 1: 0}
    return pl.pallas_call(
        _final_kernel,
        grid=(nt,),
        in_specs=in_specs,
        out_specs=pl.BlockSpec((tm, D), lambda i: (i + off, 0)),
        out_shape=jax.ShapeDtypeStruct((n_parts * T, D), jnp.float32),
        input_output_aliases=aliases,
        compiler_params=pltpu.CompilerParams(
            dimension_semantics=("parallel",), vmem_limit_bytes=VMEM_LIMIT),
        name="final",
    )(*args)


def _bias_kernel(tab_ref, sub_ref, idx_ref, o_ref):
    h = pl.program_id(0)
    idx = idx_ref[...]
    acc = jnp.full(idx.shape, NEG_INF, jnp.float32)
    for b in range(REL_BUCKETS):
        acc = jnp.where(idx == b, (tab_ref[h, b] - sub_ref[h]) * LOG2E, acc)
    o_ref[0] = acc


def _bias_expand(table_hb, sub_h, idx):
    H = table_hb.shape[0]
    K, Q = idx.shape
    return pl.pallas_call(
        _bias_kernel,
        grid_spec=pltpu.PrefetchScalarGridSpec(
            num_scalar_prefetch=2,
            grid=(H,),
            in_specs=[pl.BlockSpec((K, Q), lambda h, t, s: (0, 0))],
            out_specs=pl.BlockSpec((1, K, Q), lambda h, t, s: (h, 0, 0)),
        ),
        out_shape=jax.ShapeDtypeStruct((H, K, Q), jnp.float32),
        compiler_params=pltpu.CompilerParams(dimension_semantics=("parallel",)),
        name="bias",
    )(table_hb, sub_h, jnp.asarray(idx))


def _attention_bias_tables(rel_bias_table):
    rel = rel_bias_table.astype(jnp.float32)
    tab_a = rel[:, :SWA_Q_HEADS].T
    tab_b = rel[:, SWA_Q_HEADS:].T
    kj = np.arange(2 * SWA_BLOCK)[:, None]
    qi = np.arange(SWA_BLOCK)[None, :]
    dist = SWA_BLOCK + qi - kj
    idx_a = np.where((dist >= 0) & (dist < SWA_WINDOW), _rel_bucket_np(dist), -1).astype(np.int32)
    bias_a = _bias_expand(tab_a, jnp.zeros((SWA_Q_HEADS,), jnp.float32), idx_a)
    bias_a = bias_a.reshape(SWA_Q_HEADS // 2, 2, 2 * SWA_BLOCK, SWA_BLOCK).transpose(0, 2, 1, 3)
    bias_a = bias_a.reshape(SWA_Q_HEADS // 2, 2 * SWA_BLOCK, 2 * SWA_BLOCK)
    j = np.arange(MOBA_BLOCK)[:, None]
    i = np.arange(MOBA_BLOCK)[None, :]
    d_own = i - j
    idx_own = np.where(d_own >= 0, _rel_bucket_np(d_own), -1)
    idx_prev = _rel_bucket_np(MOBA_BLOCK + i - j)
    idx_b = np.concatenate([idx_prev, idx_own], axis=0).astype(np.int32)
    bias_b = _bias_expand(tab_b, tab_b[:, REL_BUCKETS - 1], idx_b)
    return bias_a, bias_b


def _block_plan(sizes, n_tok):
    n_assign = n_tok * EXPERT_TOPK
    padded = ((sizes + MOE_TM - 1) // MOE_TM) * MOE_TM
    eid = jnp.arange(N_EXPERTS, dtype=jnp.int32)
    padded_end = jnp.sum(jnp.where(eid[None, :] <= eid[:, None], padded[None, :], 0), axis=1)
    padded_start = padded_end - padded
    cap = -(-n_assign // MOE_TM) * MOE_TM + N_EXPERTS * MOE_TM
    blk_start = jnp.arange(cap // MOE_TM, dtype=jnp.int32) * MOE_TM
    blk_expert = jnp.minimum(
        jnp.sum(padded_end[None, :] <= blk_start[:, None], axis=1), N_EXPERTS - 1).astype(jnp.int32)
    mine = blk_expert[:, None] == eid[None, :]
    size_b = jnp.sum(jnp.where(mine, sizes[None, :], 0), axis=1)
    start_b = jnp.sum(jnp.where(mine, padded_start[None, :], 0), axis=1)
    blk_valid = jnp.clip(size_b - (blk_start - start_b), 0, MOE_TM)
    blk_valid = jnp.where(blk_start < padded_end[-1], blk_valid, 0).astype(jnp.int32)
    return padded_start, blk_expert, blk_valid, cap


def kernel(x, w_in, b_in, attn_sinks, rel_bias_table, w_branch_swa, w_branch_moba, w_out, ln1_gain, ln1_bias,
           w_group_router, b_group_router, w_expert_router, b_expert_router, w_expert_gate, w_expert_up,
           w_expert_down, ln2_gain, ln2_bias):
    assert w_in.shape[0] == DEPTH == 1
    B, S, D = x.shape
    T = B * S
    bf16 = jnp.bfloat16
    f32 = jnp.float32
    w = w_in[0]
    b = b_in[0]

    def cols(off, width):
        return w[:, off:off + width], b[off:off + width]

    wq_a, bq_a = cols(OFF_SWA_Q, SWA_Q_W)
    wk_a, bk_a = cols(OFF_SWA_K, SWA_KV_W)
    wv_a, bv_a = cols(OFF_SWA_V, SWA_KV_W)
    wq_b, bq_b = cols(OFF_MOBA_Q, MOBA_W)
    wk_b, bk_b = cols(OFF_MOBA_K, MOBA_W)
    wv_b, bv_b = cols(OFF_MOBA_V, MOBA_W)

    def dup_kv(t):
        parts = [t[..., i * HEAD_DIM:(i + 1) * HEAD_DIM] for i in range(SWA_KV_HEADS)]
        return jnp.concatenate([p for p in parts for _ in range(2)], axis=-1)

    qs = ATTN_SCALE * LOG2E
    wn = jnp.concatenate([wq_a * qs, dup_kv(wk_a), wq_b * qs, wk_b], axis=1).astype(bf16)
    bn = jnp.concatenate([bq_a * qs, dup_kv(bk_a), bq_b * qs, bk_b])[None, :].astype(f32)
    wt = jnp.concatenate([wv_a, wv_b], axis=1).T.astype(bf16)
    bt = jnp.concatenate([bv_a, bv_b])[:, None].astype(f32)

    qk, vt, eg_b, eu_b, ed_b = _inproj(x, wn, bn, wt, bt, w_expert_gate[0], w_expert_up[0], w_expert_down[0])

    bias_a, bias_b = _attention_bias_tables(rel_bias_table)
    y_a = _swa(attn_sinks[0].astype(f32), qk, vt, bias_a)
    y_b = _moba(qk, vt, bias_b)

    wg, bg = cols(OFF_GATE, 2 * D_MODEL)
    wr = jnp.zeros((ROUTER_ROWS, D), f32)
    wr = wr.at[0:N_GROUPS].set(w_group_router[0].T).at[8:8 + N_EXPERTS].set(w_expert_router[0].T)
    wr_hi = wr.astype(bf16)
    wr_lo = (wr - wr_hi.astype(f32)).astype(bf16)
    br = jnp.zeros((ROUTER_ROWS,), f32)
    br = br.at[0:N_GROUPS].set(b_group_router[0]).at[8:8 + N_EXPERTS].set(b_expert_router[0])[:, None]
    merge_args = (
        x.reshape(T, D), y_a, y_b,
        wg.astype(bf16), bg[None, :].astype(f32), w_branch_swa[0].astype(bf16), w_branch_moba[0].astype(bf16),
        w_out[0].astype(bf16), ln1_gain[0][None, :].astype(f32), ln1_bias[0][None, :].astype(f32),
        jnp.concatenate([wr_hi, wr_lo], axis=0), br)
    g2 = ln2_gain[0][None, :].astype(f32)
    b2 = ln2_bias[0][None, :].astype(f32)

    Tp = T // MOE_PARTS
    out = None
    for part in range(MOE_PARTS):
        x1, x1p, rinfo, counts = _merge(*merge_args, part, MOE_PARTS)
        sizes = counts[:, 0].astype(jnp.int32)
        padded_start, blk_expert, blk_valid, cap = _block_plan(sizes, Tp)
        dest = _dest(rinfo, padded_start)
        xs = _sc_scatter_rows(x1p, dest[0], dest[1], cap)
        y_buf = _experts(blk_expert, blk_valid, xs, eg_b, eu_b, ed_b)
        ypair = _sc_gather_rows(y_buf, dest[0:EXPERT_TOPK].reshape(-1))
        out = _final(x1, ypair, rinfo, g2, b2, out, part, MOE_PARTS)
    return out.reshape(B, S, D)
```

```python
import functools
import math

import numpy as np
import jax
import jax.numpy as jnp
from jax import lax
from jax.experimental import pallas as pl
from jax.experimental.pallas import tpu as pltpu
from jax.experimental.pallas import tpu_sc as plsc

D_MODEL = 1024
HEAD_DIM = 64
SWA_Q_HEADS = 8
SWA_KV_HEADS = 2
SWA_GROUP = SWA_Q_HEADS // SWA_KV_HEADS
SWA_WINDOW = 128
SWA_BLOCK = 128
MOBA_HEADS = 8
MOBA_BLOCK = 256
MOBA_TOPK = 3
MOBA_LOOKAHEAD = 2
BF16_SUBLANES = 16
MOBA_VROWS = HEAD_DIM + BF16_SUBLANES
REL_BUCKETS = 32
REL_MAX_DIST = 128
N_GROUPS = 4
EXPERTS_PER_GROUP = 8
N_EXPERTS = N_GROUPS * EXPERTS_PER_GROUP
EXPERT_TOPK = 2
D_EXPERT = 512
LN_EPS = 1e-5
DEPTH = 1
DEEPNORM_ALPHA = (2.0 * DEPTH) ** 0.25
NEG_INF = -1e30
ATTN_SCALE = HEAD_DIM ** -0.5
LOG2E = math.log2(math.e)

SWA_Q_W = SWA_Q_HEADS * HEAD_DIM
SWA_KV_W = SWA_KV_HEADS * HEAD_DIM
MOBA_W = MOBA_HEADS * HEAD_DIM
OFF_SWA_Q = 0
OFF_SWA_K = OFF_SWA_Q + SWA_Q_W
OFF_SWA_V = OFF_SWA_K + SWA_KV_W
OFF_MOBA_Q = OFF_SWA_V + SWA_KV_W
OFF_MOBA_K = OFF_MOBA_Q + MOBA_W
OFF_MOBA_V = OFF_MOBA_K + MOBA_W
OFF_GATE = OFF_MOBA_V + MOBA_W

LANES = 128
QK_SWA_Q = 0
QK_SWA_K = QK_SWA_Q + SWA_Q_W
QK_MOBA_Q = QK_SWA_K + SWA_KV_HEADS * LANES
QK_MOBA_K = QK_MOBA_Q + MOBA_W
QK_W = QK_MOBA_K + MOBA_W
VT_SWA = 0
VT_MOBA = VT_SWA + SWA_KV_W
VT_W = VT_MOBA + MOBA_W

MOE_TM = 512
MOE_PARTS = 2
SC_CHUNK = 128
ROUTER_ROWS = 8 + N_EXPERTS
VMEM_LIMIT = 56 * 1024 * 1024

_NT = (((1,), (1,)), ((), ()))
_TN = (((0,), (0,)), ((), ()))


def _rel_bucket_np(dist):
    n = np.maximum(dist, 0)
    max_exact = REL_BUCKETS // 2
    nf = np.maximum(n, 1).astype(np.float32)
    large = max_exact + (np.log(nf / np.float32(max_exact)) / np.float32(math.log(REL_MAX_DIST / max_exact))
                         * np.float32(REL_BUCKETS - max_exact)).astype(np.int32)
    large = np.minimum(large, REL_BUCKETS - 1)
    return np.where(n < max_exact, n, large).astype(np.int32)


def _inproj_kernel(x_ref, wn_ref, bn_ref, wt_ref, bt_ref, eg_ref, eu_ref, ed_ref,
                   qk_ref, vt_ref, egb_ref, eub_ref, edb_ref):
    xb = x_ref[0].astype(jnp.bfloat16)
    qk = jnp.dot(xb, wn_ref[...], preferred_element_type=jnp.float32) + bn_ref[...]
    qk_ref[0] = qk.astype(jnp.bfloat16)
    vt = lax.dot_general(wt_ref[...], xb, _NT, preferred_element_type=jnp.float32) + bt_ref[...]
    vt_ref[0] = vt.astype(jnp.bfloat16)
    egb_ref[...] = eg_ref[...].astype(jnp.bfloat16)
    eub_ref[...] = eu_ref[...].astype(jnp.bfloat16)
    edb_ref[...] = ed_ref[...].astype(jnp.bfloat16)


def _inproj(x, wn, bn, wt, bt, w_gate, w_up, w_down, tm=512):
    B, S, D = x.shape
    nt = S // tm
    steps = B * nt

    def sliced(w):
        E, R, C = w.shape
        assert (E * R) % (steps * BF16_SUBLANES) == 0
        return w.reshape(steps, E * R // steps, C)

    eg, eu, ed = sliced(w_gate), sliced(w_up), sliced(w_down)
    espec = lambda w: pl.BlockSpec((1,) + w.shape[1:], lambda b, i: (b * nt + i, 0, 0))
    qk, vt, egb, eub, edb = pl.pallas_call(
        _inproj_kernel,
        grid=(B, nt),
        in_specs=[
            pl.BlockSpec((1, tm, D), lambda b, i: (b, i, 0)),
            pl.BlockSpec((D, QK_W), lambda b, i: (0, 0)),
            pl.BlockSpec((1, QK_W), lambda b, i: (0, 0)),
            pl.BlockSpec((VT_W, D), lambda b, i: (0, 0)),
            pl.BlockSpec((VT_W, 1), lambda b, i: (0, 0)),
            espec(eg), espec(eu), espec(ed),
        ],
        out_specs=[
            pl.BlockSpec((1, tm, QK_W), lambda b, i: (b, i, 0)),
            pl.BlockSpec((1, VT_W, tm), lambda b, i: (b, 0, i)),
            espec(eg), espec(eu), espec(ed),
        ],
        out_shape=[
            jax.ShapeDtypeStruct((B, S, QK_W), jnp.bfloat16),
            jax.ShapeDtypeStruct((B, VT_W, S), jnp.bfloat16),
            jax.ShapeDtypeStruct(eg.shape, jnp.bfloat16),
            jax.ShapeDtypeStruct(eu.shape, jnp.bfloat16),
            jax.ShapeDtypeStruct(ed.shape, jnp.bfloat16),
        ],
        compiler_params=pltpu.CompilerParams(
            dimension_semantics=("parallel", "parallel"), vmem_limit_bytes=VMEM_LIMIT),
        name="inproj",
    )(x, wn, bn, wt, bt, eg, eu, ed)
    return qk, vt, egb.reshape(w_gate.shape), eub.reshape(w_up.shape), edb.reshape(w_down.shape)


def _swa_kernel(sink_ref, q_ref, k_ref, vt_ref, bias_ref, o_ref, vaug_ref, *s_bufs):
    S = q_ref.shape[1]
    nblk = S // SWA_BLOCK
    half = HEAD_DIM
    vt = vt_ref[0]
    rows = lax.broadcasted_iota(jnp.int32, vt.shape, 0)
    one = jnp.ones_like(vt)
    vaug_ref[0] = jnp.where(rows < half, vt, one)
    vaug_ref[1] = jnp.where(rows < half, one, vt)
    lane = lax.broadcasted_iota(jnp.int32, (SWA_BLOCK, LANES), 1)
    col = lax.broadcasted_iota(jnp.int32, (1, 2 * SWA_BLOCK), 1)
    npair = SWA_Q_HEADS // 2

    def expand(specs):
        return [(qs, ks, nk, boff, pair, (2 * pair) // SWA_GROUP) for qs, ks, nk, boff in specs for pair in range(npair)]

    def score(specs):
        ss = []
        for qs, ks, nk, boff, pair, kv in expand(specs):
            qblk = q_ref[0, pl.ds(qs, SWA_BLOCK), pair * LANES:(pair + 1) * LANES]
            zero = jnp.zeros_like(qblk)
            q2 = jnp.concatenate([jnp.where(lane < half, qblk, zero), jnp.where(lane >= half, qblk, zero)], axis=0)
            kblk = k_ref[0, pl.ds(ks, nk), kv * LANES:(kv + 1) * LANES]
            s = lax.dot_general(kblk, q2, _NT, preferred_element_type=jnp.float32)
            ss.append(s + bias_ref[pair, boff:boff + nk, :])
        return ss

    def finish(specs, ss):
        items = expand(specs)
        stats = []
        for (qs, ks, nk, boff, pair, kv), s in zip(items, ss):
            sink = jnp.where(col < SWA_BLOCK, sink_ref[2 * pair], sink_ref[2 * pair + 1]) * LOG2E
            m = jnp.maximum(_colmax(s), sink)
            stats.append((jnp.exp2(s - m).astype(jnp.bfloat16), jnp.exp2(sink - m)))
        rs = [jnp.dot(vaug_ref[kv, :, pl.ds(ks, nk)], p, preferred_element_type=jnp.float32)
              for (qs, ks, nk, boff, pair, kv), (p, _) in zip(items, stats)]
        outs = []
        for (qs, ks, nk, boff, pair, kv), (_, esink), r in zip(items, stats, rs):
            l = (r[half:half + 1, :] if kv == 0 else r[0:1, :]) + esink
            o = (r[0:half, :] if kv == 0 else r[half:, :]) / l
            outs.append(jnp.concatenate([o[:, :SWA_BLOCK], o[:, SWA_BLOCK:]], axis=0))
        for i, (qs, _, _, _) in enumerate(specs):
            ot = jnp.concatenate(outs[i * npair:(i + 1) * npair], axis=0)
            o_ref[0, :, pl.ds(qs, SWA_BLOCK)] = ot.astype(jnp.bfloat16)

    first = [(0, 0, SWA_BLOCK, SWA_BLOCK), (SWA_BLOCK, 0, 2 * SWA_BLOCK, 0)]
    finish(first, score(first))

    assert nblk % 2 == 0
    n_groups = nblk // 2 - 1
    nbuf = len(s_bufs)

    def group_specs(g):
        specs = []
        for n in (2 * g + 2, 2 * g + 3):
            qs = pl.multiple_of(n * SWA_BLOCK, SWA_BLOCK)
            specs.append((qs, pl.multiple_of(qs - SWA_BLOCK, SWA_BLOCK), 2 * SWA_BLOCK, 0))
        return specs

    def score_to(g, buf):
        for k, s in enumerate(score(group_specs(g))):
            buf[k] = s

    def finish_from(g, buf):
        finish(group_specs(g), [buf[k] for k in range(2 * npair)])

    for g in range(min(2, n_groups)):
        score_to(g, s_bufs[g])

    def body(t, carry):
        for j in range(nbuf):
            score_to(nbuf * t + 2 + j, s_bufs[(2 + j) % nbuf])
            finish_from(nbuf * t + j, s_bufs[j])
        return carry

    n_trips = max(n_groups - 2, 0) // nbuf
    lax.fori_loop(0, n_trips, body, 0)
    for g in range(nbuf * n_trips, n_groups):
        if g + 2 < n_groups:
            score_to(g + 2, s_bufs[(g + 2) % nbuf])
        finish_from(g, s_bufs[g % nbuf])


def _swa(sinks, qk, vt, bias):
    B, S, _ = qk.shape
    return pl.pallas_call(
        _swa_kernel,
        grid_spec=pltpu.PrefetchScalarGridSpec(
            num_scalar_prefetch=1,
            grid=(B,),
            in_specs=[
                pl.BlockSpec((1, S, SWA_Q_W), lambda b, s: (b, 0, QK_SWA_Q // SWA_Q_W)),
                pl.BlockSpec((1, S, 2 * LANES), lambda b, s: (b, 0, QK_SWA_K // (2 * LANES))),
                pl.BlockSpec((1, SWA_KV_W, S), lambda b, s: (b, VT_SWA // SWA_KV_W, 0)),
                pl.BlockSpec((SWA_Q_HEADS // 2, 2 * SWA_BLOCK, 2 * SWA_BLOCK), lambda b, s: (0, 0, 0)),
            ],
            out_specs=pl.BlockSpec((1, SWA_Q_W, S), lambda b, s: (b, 0, 0)),
            scratch_shapes=[pltpu.VMEM((SWA_KV_HEADS, SWA_KV_W, S), jnp.bfloat16)]
            + [pltpu.VMEM((SWA_Q_HEADS, 2 * SWA_BLOCK, 2 * SWA_BLOCK), jnp.float32)] * 4,
        ),
        out_shape=jax.ShapeDtypeStruct((B, SWA_Q_W, S), jnp.bfloat16),
        compiler_params=pltpu.CompilerParams(
            dimension_semantics=("parallel",), vmem_limit_bytes=VMEM_LIMIT),
        name="swa",
    )(sinks, qk, qk, vt, bias)


def _colmax(s):
    while s.shape[0] > 8:
        h = s.shape[0] // 2
        s = jnp.maximum(s[:h], s[h:])
    return jnp.max(s, axis=0, keepdims=True)


def _moba_kernel(fqb_ref, fc_ref, q_ref, k_ref, vt_ref, bias_ref, o_ref,
                 vaug_ref, gate_ref, rank_ref, sel_ref, far_ref, m_scr, acc_scr, sa_scr, sb_scr, sc_scr, sd_scr):
    S = q_ref.shape[1]
    nb = S // MOBA_BLOCK
    half = HEAD_DIM
    BLK = MOBA_BLOCK

    ones_rows = jnp.ones((MOBA_VROWS - half, S), jnp.bfloat16)
    for h in range(2):
        vaug_ref[h, 0:half, :] = vt_ref[0, h * half:(h + 1) * half, :]
        vaug_ref[h, half:, :] = ones_rows

    kf = k_ref[0].astype(jnp.float32).reshape(nb, BLK, LANES)
    kmean = jnp.sum(kf, axis=1) * (1.0 / BLK)
    k_hi = kmean.astype(jnp.bfloat16)
    k_lo = (kmean - k_hi.astype(jnp.float32)).astype(jnp.bfloat16)
    kcat = jnp.concatenate([k_hi, k_lo], axis=0)

    lane_q = lax.broadcasted_iota(jnp.int32, (S, LANES), 1)
    brow = lax.broadcasted_iota(jnp.int32, (nb, S), 0)
    qblk_of = lax.broadcasted_iota(jnp.int32, (nb, S), 1) // BLK
    qall = q_ref[0]
    for h in range(2):
        qh = jnp.where((lane_q < half) if h == 0 else (lane_q >= half), qall, jnp.zeros_like(qall))
        g2 = lax.dot_general(kcat, qh, _NT, preferred_element_type=jnp.float32)
        gate_ref[...] = g2[0:nb] + g2[nb:2 * nb]
        rank_ref[...] = jnp.zeros_like(rank_ref)
        for m in range(nb - 1):
            lo = (m + 1) * BLK
            G = gate_ref[:, lo:]
            gm = gate_ref[m:m + 1, lo:]
            ge = jnp.where(gm >= G, 1.0, 0.0)
            gt = jnp.where(gm > G, 1.0, 0.0)
            brow_m = lax.broadcasted_iota(jnp.int32, (nb, S - lo), 0)
            rank_ref[:, lo:] += jnp.where(brow_m > m, ge, gt)
        top = rank_ref[...] < float(MOBA_TOPK)
        sel_ref[h] = jnp.where((brow < qblk_of) & top, 0.0, NEG_INF)
        far_ref[h] = jnp.where((brow < qblk_of - 1) & top, 0.0, NEG_INF)

    lane = lax.broadcasted_iota(jnp.int32, (BLK, LANES), 1)

    def head_q(qs, h):
        qblk = q_ref[0, pl.ds(qs, BLK), :]
        return jnp.where((lane < half) if h == 0 else (lane >= half), qblk, jnp.zeros_like(qblk))

    m0 = jnp.full((1, BLK), NEG_INF, jnp.float32)
    acc0 = jnp.zeros((MOBA_VROWS, BLK), jnp.float32)

    def item(h, qb, qs, ks, nk, bias, sel_a, sel_b, m, acc):
        return dict(h=h, qb=qb, qs=qs, ks=ks, nk=nk, bias=bias, sel_a=sel_a, sel_b=sel_b, m=m, acc=acc)

    def near_item(qb, h):
        qs = pl.multiple_of(qb * BLK, BLK)
        ps = pl.multiple_of(qs - BLK, BLK)
        sel_prev = sel_ref[h, pl.ds(qb - 1, 1), pl.ds(qs, BLK)]
        return item(h, qb, qs, ps, 2 * BLK, bias_ref[h], sel_prev, None, m0, acc0)

    def far_item(i, h):
        qb = fqb_ref[i]
        c = fc_ref[i]
        qs = pl.multiple_of(qb * BLK, BLK)
        ks = pl.multiple_of(c * (2 * BLK), 2 * BLK)
        sel_a = far_ref[h, pl.ds(2 * c, 1), pl.ds(qs, BLK)]
        sel_b = far_ref[h, pl.ds(2 * c + 1, 1), pl.ds(qs, BLK)]
        return item(h, qb, qs, ks, 2 * BLK, None, sel_a, sel_b, m_scr[h, qb], acc_scr[h, qb])

    def scores(it):
        kslab = k_ref[0, pl.ds(it["ks"], it["nk"]), :]
        s = lax.dot_general(kslab, head_q(it["qs"], it["h"]), _NT,
                            preferred_element_type=jnp.float32)
        return s if it["bias"] is None else s + it["bias"]

    def probs(it, s):
        m, sel_a, sel_b = it["m"], it["sel_a"], it["sel_b"]
        if it["nk"] == BLK:
            m_new = jnp.maximum(m, _colmax(s))
            p = jnp.exp2(s - m_new)
        else:
            cm_a = _colmax(s[:BLK]) + sel_a
            cm_b = _colmax(s[BLK:])
            if sel_b is not None:
                cm_b = cm_b + sel_b
            m_new = jnp.maximum(m, jnp.maximum(cm_a, cm_b))
            p_a = jnp.exp2(s[:BLK] - (m_new - sel_a))
            p_b = jnp.exp2(s[BLK:] - (m_new if sel_b is None else m_new - sel_b))
            p = jnp.concatenate([p_a, p_b], axis=0)
        return m_new, jnp.exp2(m - m_new), p.astype(jnp.bfloat16)

    def run(items):
        ss = [scores(it) for it in items]
        ps = [probs(it, s) for it, s in zip(items, ss)]
        pvs = [jnp.dot(vaug_ref[it["h"], :, pl.ds(it["ks"], it["nk"])], p, preferred_element_type=jnp.float32)
               for it, (_, _, p) in zip(items, ps)]
        for it, (m_new, alpha, _), pv in zip(items, ps, pvs):
            m_scr[it["h"], it["qb"]] = m_new
            acc_scr[it["h"], it["qb"]] = it["acc"] * alpha + pv

    run([item(h, 0, 0, 0, BLK, bias_ref[h, BLK:, :], None, None, m0, acc0) for h in range(2)]
        + [near_item(nb - 1, h) for h in range(2)])

    bufs = (sa_scr, sb_scr, sc_scr, sd_scr)
    nbuf = len(bufs)

    def finish(items, s_buf):
        ps = [probs(it, s_buf[k]) for k, it in enumerate(items)]
        pvs = [jnp.dot(vaug_ref[it["h"], :, pl.ds(it["ks"], it["nk"])], p, preferred_element_type=jnp.float32)
               for it, (_, _, p) in zip(items, ps)]
        for it, (m_new, alpha, _), pv in zip(items, ps, pvs):
            m_scr[it["h"], it["qb"]] = m_new
            acc_scr[it["h"], it["qb"]] = it["acc"] * alpha + pv

    def pipelined(n_groups, score, items_of):
        for g in range(min(MOBA_LOOKAHEAD, n_groups)):
            score(g, bufs[g])

        def body(t, carry):
            for j in range(nbuf):
                score(nbuf * t + MOBA_LOOKAHEAD + j, bufs[(MOBA_LOOKAHEAD + j) % nbuf])
                finish(items_of(nbuf * t + j), bufs[j])
            return carry

        n_trips = max(n_groups - MOBA_LOOKAHEAD, 0) // nbuf
        lax.fori_loop(0, n_trips, body, 0)
        for g in range(nbuf * n_trips, n_groups):
            if g + MOBA_LOOKAHEAD < n_groups:
                score(g + MOBA_LOOKAHEAD, bufs[(g + MOBA_LOOKAHEAD) % nbuf])
            finish(items_of(g), bufs[g % nbuf])

    near_pairs = nb // 2 - 1

    def near_group(g):
        return [(qb, h) for qb in (g + 1, g + 1 + near_pairs) for h in range(2)]

    def near_score(g, s_buf):
        for k, (qb, h) in enumerate(near_group(g)):
            qs = pl.multiple_of(qb * BLK, BLK)
            ps = pl.multiple_of(qs - BLK, BLK)
            s_buf[k] = lax.dot_general(k_ref[0, pl.ds(ps, 2 * BLK), :], head_q(qs, h), _NT,
                                       preferred_element_type=jnp.float32) + bias_ref[h]

    pipelined(near_pairs, near_score, lambda g: [near_item(qb, h) for qb, h in near_group(g)])

    n_far_groups = fqb_ref.shape[0] // 2

    def far_group(g):
        return [(j, h) for j in (g, g + n_far_groups) for h in range(2)]

    def far_score(g, s_buf):
        for k, (j, h) in enumerate(far_group(g)):
            qs = pl.multiple_of(fqb_ref[j] * BLK, BLK)
            ks = pl.multiple_of(fc_ref[j] * (2 * BLK), 2 * BLK)
            s_buf[k] = lax.dot_general(k_ref[0, pl.ds(ks, 2 * BLK), :], head_q(qs, h), _NT,
                                       preferred_element_type=jnp.float32)

    pipelined(n_far_groups, far_score, lambda g: [far_item(j, h) for j, h in far_group(g)])

    def out_body(t, carry):
        for qb in (2 * t, 2 * t + 1):
            qs = pl.multiple_of(qb * BLK, BLK)
            a0 = acc_scr[0, qb]
            a1 = acc_scr[1, qb]
            ot = jnp.concatenate([a0[0:half] / a0[half:half + 1, :], a1[0:half] / a1[half:half + 1, :]],
                                 axis=0)
            o_ref[0, :, pl.ds(qs, BLK)] = ot.astype(jnp.bfloat16)
        return carry

    lax.fori_loop(0, nb // 2, out_body, 0)


def _moba_far_items(nb):
    items = [(qb, c) for qb in range(2, nb) for c in range(qb // 2)]
    n = len(items)
    assert n % 2 == 0 and all(items[i][0] != items[i + n // 2][0] for i in range(n // 2))
    return np.array([it[0] for it in items], np.int32), np.array([it[1] for it in items], np.int32)


def _moba(qk, vt, bias):
    B, S, _ = qk.shape
    nb = S // MOBA_BLOCK
    npair = MOBA_W // LANES
    far_qb, far_c = _moba_far_items(nb)
    return pl.pallas_call(
        _moba_kernel,
        grid_spec=pltpu.PrefetchScalarGridSpec(
            num_scalar_prefetch=2,
            grid=(B, npair),
            in_specs=[
                pl.BlockSpec((1, S, LANES), lambda b, p, fq, fc: (b, 0, QK_MOBA_Q // LANES + p)),
                pl.BlockSpec((1, S, LANES), lambda b, p, fq, fc: (b, 0, QK_MOBA_K // LANES + p)),
                pl.BlockSpec((1, LANES, S), lambda b, p, fq, fc: (b, VT_MOBA // LANES + p, 0)),
                pl.BlockSpec((2, 2 * MOBA_BLOCK, MOBA_BLOCK), lambda b, p, fq, fc: (p, 0, 0)),
            ],
            out_specs=pl.BlockSpec((1, LANES, S), lambda b, p, fq, fc: (b, p, 0)),
            scratch_shapes=[
                pltpu.VMEM((2, MOBA_VROWS, S), jnp.bfloat16),
                pltpu.VMEM((nb, S), jnp.float32),
                pltpu.VMEM((nb, S), jnp.float32),
                pltpu.VMEM((2, nb, S), jnp.float32),
                pltpu.VMEM((2, nb, S), jnp.float32),
                pltpu.VMEM((2, nb, 1, MOBA_BLOCK), jnp.float32),
                pltpu.VMEM((2, nb, MOBA_VROWS, MOBA_BLOCK), jnp.float32),
            ] + [pltpu.VMEM((4, 2 * MOBA_BLOCK, MOBA_BLOCK), jnp.float32)] * 4,
        ),
        out_shape=jax.ShapeDtypeStruct((B, MOBA_W, S), jnp.bfloat16),
        compiler_params=pltpu.CompilerParams(
            dimension_semantics=("parallel", "parallel"), vmem_limit_bytes=VMEM_LIMIT),
        name="moba",
    )(jnp.asarray(far_qb), jnp.asarray(far_c), qk, qk, vt, bias)


def _layer_norm(h, gain, bias):
    mu = jnp.mean(h, axis=-1, keepdims=True)
    c = h - mu
    var = jnp.mean(c * c, axis=-1, keepdims=True)
    return c * lax.rsqrt(var + LN_EPS) * gain + bias


def _pack_bf16_pair(a, b):
    ia = lax.bitcast_convert_type(a.astype(jnp.bfloat16).astype(jnp.float32), jnp.int32)
    ib = lax.bitcast_convert_type(b.astype(jnp.bfloat16).astype(jnp.float32), jnp.int32)
    return lax.shift_right_logical(ia, 16) | ib


def _unpack_bf16_pair(w):
    lo = lax.bitcast_convert_type(lax.shift_left(w, 16), jnp.float32)
    hi = lax.bitcast_convert_type(w & jnp.int32(-65536), jnp.float32)
    return lo.astype(jnp.bfloat16), hi.astype(jnp.bfloat16)


def _merge_kernel(x_ref, ya_ref, yb_ref, wg_ref, bg_ref, wa_ref, wb_ref, wo_ref, g1_ref, b1_ref,
                  wr_ref, br_ref, tri_ref, x1_ref, x1p_ref, r_ref, cnt_ref):
    @pl.when(pl.program_id(0) == 0)
    def _():
        cnt_ref[...] = jnp.zeros_like(cnt_ref)

    ts = tri_ref.shape[0]
    subs = [pl.ds(r0, ts) for r0 in range(0, x_ref.shape[0], ts)]
    pre = [_merge_matmuls(x_ref[rows, :], ya_ref[0, :, rows], yb_ref[0, :, rows], wg_ref, bg_ref, wa_ref, wb_ref, wo_ref)
           for rows in subs]
    for rows, h in zip(subs, pre):
        _merge_route(h, rows, g1_ref, b1_ref, wr_ref, br_ref, tri_ref, x1_ref, x1p_ref, r_ref, cnt_ref)


def _merge_matmuls(x, ya_t, yb_t, wg_ref, bg_ref, wa_ref, wb_ref, wo_ref):
    xb = x.astype(jnp.bfloat16)
    z = jnp.dot(xb, wg_ref[...], preferred_element_type=jnp.float32) + bg_ref[...]
    gates = 1.0 / (1.0 + jnp.exp(-z))
    pa = lax.dot_general(ya_t, wa_ref[...], _TN, preferred_element_type=jnp.float32)
    pb = lax.dot_general(yb_t, wb_ref[...], _TN, preferred_element_type=jnp.float32)
    merged = gates[:, :D_MODEL] * pa + gates[:, D_MODEL:] * pb
    mixed = jnp.dot(merged.astype(jnp.bfloat16), wo_ref[...], preferred_element_type=jnp.float32)
    return DEEPNORM_ALPHA * x + mixed


def _merge_route(h, rows, g1_ref, b1_ref, wr_ref, br_ref, tri_ref, x1_ref, x1p_ref, r_ref, cnt_ref):
    x1 = _layer_norm(h, g1_ref[...], b1_ref[...])
    x1_ref[rows, :] = x1
    x1_hi = x1.astype(jnp.bfloat16)
    x1p_ref[rows, :] = _pack_bf16_pair(x1[:, :D_MODEL // 2], x1[:, D_MODEL // 2:])
    x1_lo = (x1 - x1_hi.astype(jnp.float32)).astype(jnp.bfloat16)

    R = ROUTER_ROWS
    l1 = lax.dot_general(wr_ref[...], x1_hi, _NT, preferred_element_type=jnp.float32)
    l2 = lax.dot_general(wr_ref[0:R, :], x1_lo, _NT, preferred_element_type=jnp.float32)
    L = l1[0:R] + l1[R:2 * R] + l2 + br_ref[...]
    tm = x1.shape[0]
    row = lax.broadcasted_iota(jnp.int32, (8, tm), 0)
    big = jnp.float32(-3e38)
    gl = jnp.where(row < N_GROUPS, L[0:8], big)
    gmax = jnp.max(gl, axis=0, keepdims=True)
    g_idx = jnp.min(jnp.where(gl == gmax, row, 8), axis=0, keepdims=True)
    gsum = jnp.sum(jnp.where(row < N_GROUPS, jnp.exp(gl - gmax), 0.0), axis=0, keepdims=True)
    g_prob = 1.0 / gsum
    E = L[8 + 8 * (N_GROUPS - 1):8 + 8 * N_GROUPS]
    for g in range(N_GROUPS - 2, -1, -1):
        E = jnp.where(g_idx == g, L[8 + 8 * g:16 + 8 * g], E)
    t0 = jnp.max(E, axis=0, keepdims=True)
    loc0 = jnp.min(jnp.where(E == t0, row, 8), axis=0, keepdims=True)
    E2 = jnp.where(row == loc0, big, E)
    t1 = jnp.max(E2, axis=0, keepdims=True)
    loc1 = jnp.min(jnp.where(E2 == t1, row, 8), axis=0, keepdims=True)
    ex = jnp.exp(t1 - t0)
    w0 = g_prob / (1.0 + ex)
    w1 = g_prob * ex / (1.0 + ex)
    e0i = g_idx * EXPERTS_PER_GROUP + loc0
    e1i = g_idx * EXPERTS_PER_GROUP + loc1

    erow = lax.broadcasted_iota(jnp.int32, (N_EXPERTS, tm), 0)
    oh0 = jnp.where(erow == e0i, 1.0, 0.0)
    oh1 = jnp.where(erow == e1i, 1.0, 0.0)
    both = oh0 + oh1
    prefix = jnp.dot(both.astype(jnp.bfloat16), tri_ref[...], preferred_element_type=jnp.float32)
    prefix = prefix + cnt_ref[:, 0:1]
    rank0 = jnp.sum(oh0 * prefix, axis=0, keepdims=True)
    rank1 = jnp.sum(oh1 * prefix, axis=0, keepdims=True)
    cnt_ref[...] = cnt_ref[...] + jnp.sum(both, axis=1, keepdims=True)

    vals = (e0i.astype(jnp.float32), e1i.astype(jnp.float32), w0, w1, rank0, rank1)
    out = jnp.zeros((8, tm), jnp.float32)
    for k, v in enumerate(vals):
        out = jnp.where(row == k, v, out)
    r_ref[:, rows] = out


def _merge(x2, ya, yb, wg, bg, wa, wb, wo, g1, b1, wr, br, part, n_parts, tm=512, ts=256):
    D = x2.shape[1]
    T = x2.shape[0] // n_parts
    S = ya.shape[2]
    per_seq = S // tm
    assert per_seq * tm == S
    off = part * (T // tm)
    const = lambda i: (0, 0)
    seq_tile = lambda i: ((i + off) // per_seq, 0, (i + off) % per_seq)
    tri = jnp.triu(jnp.ones((ts, ts), jnp.bfloat16), k=1)
    return pl.pallas_call(
        _merge_kernel,
        grid=(T // tm,),
        in_specs=[
            pl.BlockSpec((tm, D), lambda i: (i + off, 0)),
            pl.BlockSpec((1, SWA_Q_W, tm), seq_tile),
            pl.BlockSpec((1, MOBA_W, tm), seq_tile),
            pl.BlockSpec((D, 2 * D), const),
            pl.BlockSpec((1, 2 * D), const),
            pl.BlockSpec((SWA_Q_W, D), const),
            pl.BlockSpec((MOBA_W, D), const),
            pl.BlockSpec((D, D), const),
            pl.BlockSpec((1, D), const),
            pl.BlockSpec((1, D), const),
            pl.BlockSpec((2 * ROUTER_ROWS, D), const),
            pl.BlockSpec((ROUTER_ROWS, 1), const),
            pl.BlockSpec((ts, ts), const),
        ],
        out_specs=[
            pl.BlockSpec((tm, D), lambda i: (i, 0)),
            pl.BlockSpec((tm, D // 2), lambda i: (i, 0)),
            pl.BlockSpec((8, tm), lambda i: (0, i)),
            pl.BlockSpec((N_EXPERTS, LANES), const),
        ],
        out_shape=[
            jax.ShapeDtypeStruct((T, D), jnp.float32),
            jax.ShapeDtypeStruct((T, D // 2), jnp.int32),
            jax.ShapeDtypeStruct((8, T), jnp.float32),
            jax.ShapeDtypeStruct((N_EXPERTS, LANES), jnp.float32),
        ],
        compiler_params=pltpu.CompilerParams(
            dimension_semantics=("arbitrary",), vmem_limit_bytes=VMEM_LIMIT),
        name="merge",
    )(x2, ya, yb, wg, bg, wa, wb, wo, g1, b1, wr, br, tri)


def _dest_kernel(r_ref, ps_ref, d_ref):
    tt = r_ref.shape[1]
    erow = lax.broadcasted_iota(jnp.int32, (N_EXPERTS, tt), 0)
    row = lax.broadcasted_iota(jnp.int32, (8, tt), 0)
    ps = ps_ref[...]
    out = jnp.zeros((8, tt), jnp.float32)
    for k in range(EXPERT_TOPK):
        e = r_ref[k:k + 1, :].astype(jnp.int32)
        start = jnp.sum(jnp.where(erow == e, ps, 0.0), axis=0, keepdims=True)
        out = jnp.where(row == k, start + r_ref[4 + k:5 + k, :], out)
    d_ref[...] = out.astype(jnp.int32)


def _dest(rinfo, padded_start, tt=8192):
    T = rinfo.shape[1]
    tt = min(tt, T)
    return pl.pallas_call(
        _dest_kernel,
        grid=(T // tt,),
        in_specs=[pl.BlockSpec((8, tt), lambda i: (0, i)), pl.BlockSpec((N_EXPERTS, 1), lambda i: (0, 0))],
        out_specs=pl.BlockSpec((8, tt), lambda i: (0, i)),
        out_shape=jax.ShapeDtypeStruct((8, T), jnp.int32),
        compiler_params=pltpu.CompilerParams(dimension_semantics=("parallel",)),
        name="dest",
    )(rinfo, padded_start.astype(jnp.float32)[:, None])


def _sc_workers():
    info = plsc.get_sparse_core_info()
    return info.num_cores, info.num_subcores


def _sc_scatter_rows(src, dest0, dest1, cap):
    T, W = src.shape
    nc, ns = _sc_workers()
    per_w = T // (nc * ns)
    assert per_w * nc * ns == T and per_w % SC_CHUNK == 0
    mesh = plsc.VectorSubcoreMesh(core_axis_name="c", subcore_axis_name="s")

    @functools.partial(
        pl.kernel, mesh=mesh,
        out_type=jax.ShapeDtypeStruct((cap, W), src.dtype),
        scratch_types=[pltpu.VMEM((SC_CHUNK,), jnp.int32), pltpu.VMEM((SC_CHUNK, W), src.dtype)],
    )
    def scatter(src_hbm, d0_hbm, d1_hbm, out_hbm, idx_v, rows_v):
        wid = lax.axis_index("s") * nc + lax.axis_index("c")
        base = wid * per_w

        @pl.loop(0, per_w // SC_CHUNK)
        def _(c):
            off = pl.multiple_of(base + c * SC_CHUNK, SC_CHUNK)
            pltpu.sync_copy(src_hbm.at[pl.ds(off, SC_CHUNK)], rows_v)
            for d_hbm in (d0_hbm, d1_hbm):
                pltpu.sync_copy(d_hbm.at[pl.ds(off, SC_CHUNK)], idx_v)
                pltpu.sync_copy(rows_v, out_hbm.at[idx_v])

    return scatter(src, dest0, dest1)


def _sc_gather_rows(table, idx):
    N = idx.shape[0]
    W = table.shape[1]
    nc, ns = _sc_workers()
    per_w = N // (nc * ns)
    assert per_w * nc * ns == N and per_w % SC_CHUNK == 0
    mesh = plsc.VectorSubcoreMesh(core_axis_name="c", subcore_axis_name="s")

    @functools.partial(
        pl.kernel, mesh=mesh,
        out_type=jax.ShapeDtypeStruct((N, W), table.dtype),
        scratch_types=[pltpu.VMEM((SC_CHUNK,), jnp.int32), pltpu.VMEM((SC_CHUNK, W), table.dtype)],
    )
    def gather(table_hbm, idx_hbm, out_hbm, idx_v, rows_v):
        wid = lax.axis_index("s") * nc + lax.axis_index("c")
        base = wid * per_w

        @pl.loop(0, per_w // SC_CHUNK)
        def _(c):
            off = pl.multiple_of(base + c * SC_CHUNK, SC_CHUNK)
            pltpu.sync_copy(idx_hbm.at[pl.ds(off, SC_CHUNK)], idx_v)
            pltpu.sync_copy(table_hbm.at[idx_v], rows_v)
            pltpu.sync_copy(rows_v, out_hbm.at[pl.ds(off, SC_CHUNK)])

    return gather(table, idx)


def _expert_kernel(be_ref, nv_ref, first_ref, slot_ref, nxt_ref, x_ref, wg_hbm, wu_hbm, wd_hbm, y_ref,
                   wg_buf, wu_buf, wd_buf, sem):
    i = pl.program_id(0)
    nv = nv_ref[i]
    slot = slot_ref[i]

    def weight_copies(e, s):
        return [pltpu.make_async_copy(hbm.at[e], buf.at[s], sem.at[s, k])
                for k, (hbm, buf) in enumerate(((wg_hbm, wg_buf), (wu_hbm, wu_buf), (wd_hbm, wd_buf)))]

    @pl.when(i == 0)
    def _():
        for c in weight_copies(be_ref[0], 0):
            c.start()

    @pl.when(first_ref[i] == 1)
    def _():
        @pl.when(nxt_ref[i] >= 0)
        def _():
            for c in weight_copies(nxt_ref[i], 1 - slot):
                c.start()

        for c in weight_copies(be_ref[i], slot):
            c.wait()

    @pl.when(nv > 0)
    def _():
        lo, hi = _unpack_bf16_pair(x_ref[...])
        xb = jnp.concatenate([lo, hi], axis=1)
        rows = lax.broadcasted_iota(jnp.int32, xb.shape, 0)
        xb = jnp.where(rows < nv, xb, jnp.zeros_like(xb))
        g = jnp.dot(xb, wg_buf[slot], preferred_element_type=jnp.float32)
        u = jnp.dot(xb, wu_buf[slot], preferred_element_type=jnp.float32)
        act = (g / (1.0 + jnp.exp(-g))) * u
        y = jnp.dot(act.astype(jnp.bfloat16), wd_buf[slot], preferred_element_type=jnp.float32)
        y_ref[...] = _pack_bf16_pair(y[:, :D_MODEL // 2], y[:, D_MODEL // 2:])

    @pl.when(nv <= 0)
    def _():
        y_ref[...] = jnp.zeros_like(y_ref)


def _experts(blk_expert, blk_valid, xs, wg, wu, wd):
    cap, DW = xs.shape
    D = 2 * DW
    n_blocks = cap // MOE_TM
    blk = jnp.arange(n_blocks, dtype=jnp.int32)
    first = (blk == 0) | (blk_expert != jnp.roll(blk_expert, 1))
    slot = (jnp.cumsum(first.astype(jnp.int32)) - 1) % 2
    first_at = jnp.where(first, blk, n_blocks)
    next_first = jnp.concatenate([lax.cummin(first_at[::-1])[::-1][1:], jnp.full((1,), n_blocks, jnp.int32)])
    nxt = jnp.where(next_first < n_blocks, blk_expert[jnp.minimum(next_first, n_blocks - 1)], -1)
    hbm = pl.BlockSpec(memory_space=pl.ANY)
    return pl.pallas_call(
        _expert_kernel,
        grid_spec=pltpu.PrefetchScalarGridSpec(
            num_scalar_prefetch=5,
            grid=(n_blocks,),
            in_specs=[pl.BlockSpec((MOE_TM, DW), lambda i, *_: (i, 0)), hbm, hbm, hbm],
            out_specs=pl.BlockSpec((MOE_TM, DW), lambda i, *_: (i, 0)),
            scratch_shapes=[
                pltpu.VMEM((2, D, D_EXPERT), jnp.bfloat16),
                pltpu.VMEM((2, D, D_EXPERT), jnp.bfloat16),
                pltpu.VMEM((2, D_EXPERT, D), jnp.bfloat16),
                pltpu.SemaphoreType.DMA((2, 3)),
            ],
        ),
        out_shape=jax.ShapeDtypeStruct((cap, DW), jnp.int32),
        compiler_params=pltpu.CompilerParams(
            dimension_semantics=("arbitrary",), vmem_limit_bytes=VMEM_LIMIT),
        name="experts",
    )(blk_expert, blk_valid, first.astype(jnp.int32), slot.astype(jnp.int32), nxt.astype(jnp.int32), xs, wg, wu, wd)


def _final_kernel(x1_ref, y0_ref, y1_ref, w_ref, g2_ref, b2_ref, *rest):
    o_ref = rest[-1]
    w = w_ref[...].T
    halves = []
    for part in range(2):
        y0 = _unpack_bf16_pair(y0_ref[...])[part].astype(jnp.float32)
        y1 = _unpack_bf16_pair(y1_ref[...])[part].astype(jnp.float32)
        halves.append(y0 * w[:, 2:3] + y1 * w[:, 3:4])
    moe = jnp.concatenate(halves, axis=1)
    o_ref[...] = _layer_norm(DEEPNORM_ALPHA * x1_ref[...] + moe, g2_ref[...], b2_ref[...])


def _final(x1, ypair, rinfo, g2, b2, out_prev, part, n_parts, tm=1024):
    T, D = x1.shape
    const = lambda i: (0, 0)
    nt = T // tm
    off = part * nt
    in_specs = [
        pl.BlockSpec((tm, D), lambda i: (i, 0)),
        pl.BlockSpec((tm, D // 2), lambda i: (i, 0)),
        pl.BlockSpec((tm, D // 2), lambda i: (i + nt, 0)),
        pl.BlockSpec((8, tm), lambda i: (0, i)),
        pl.BlockSpec((1, D), const),
        pl.BlockSpec((1, D), const),
    ]
    args = [x1, ypair, ypair, rinfo, g2, b2]
    aliases = {}
    if out_prev is not None:
        in_specs.append(pl.BlockSpec(memory_space=pl.ANY))
        args.append(out_prev)
        aliases = {len(args) - 1: 0}
    return pl.pallas_call(
        _final_kernel,
        grid=(nt,),
        in_specs=in_specs,
        out_specs=pl.BlockSpec((tm, D), lambda i: (i + off, 0)),
        out_shape=jax.ShapeDtypeStruct((n_parts * T, D), jnp.float32),
        input_output_aliases=aliases,
        compiler_params=pltpu.CompilerParams(
            dimension_semantics=("parallel",), vmem_limit_bytes=VMEM_LIMIT),
        name="final",
    )(*args)


def _bias_kernel(tab_ref, sub_ref, idx_ref, o_ref):
    h = pl.program_id(0)
    idx = idx_ref[...]
    acc = jnp.full(idx.shape, NEG_INF, jnp.float32)
    for b in range(REL_BUCKETS):
        acc = jnp.where(idx == b, (tab_ref[h, b] - sub_ref[h]) * LOG2E, acc)
    o_ref[0] = acc


def _bias_expand(table_hb, sub_h, idx):
    H = table_hb.shape[0]
    K, Q = idx.shape
    return pl.pallas_call(
        _bias_kernel,
        grid_spec=pltpu.PrefetchScalarGridSpec(
            num_scalar_prefetch=2,
            grid=(H,),
            in_specs=[pl.BlockSpec((K, Q), lambda h, t, s: (0, 0))],
            out_specs=pl.BlockSpec((1, K, Q), lambda h, t, s: (h, 0, 0)),
        ),
        out_shape=jax.ShapeDtypeStruct((H, K, Q), jnp.float32),
        compiler_params=pltpu.CompilerParams(dimension_semantics=("parallel",)),
        name="bias",
    )(table_hb, sub_h, jnp.asarray(idx))


def _attention_bias_tables(rel_bias_table):
    rel = rel_bias_table.astype(jnp.float32)
    tab_a = rel[:, :SWA_Q_HEADS].T
    tab_b = rel[:, SWA_Q_HEADS:].T
    kj = np.arange(2 * SWA_BLOCK)[:, None]
    qi = np.arange(SWA_BLOCK)[None, :]
    dist = SWA_BLOCK + qi - kj
    idx_a = np.where((dist >= 0) & (dist < SWA_WINDOW), _rel_bucket_np(dist), -1).astype(np.int32)
    bias_a = _bias_expand(tab_a, jnp.zeros((SWA_Q_HEADS,), jnp.float32), idx_a)
    bias_a = bias_a.reshape(SWA_Q_HEADS // 2, 2, 2 * SWA_BLOCK, SWA_BLOCK).transpose(0, 2, 1, 3)
    bias_a = bias_a.reshape(SWA_Q_HEADS // 2, 2 * SWA_BLOCK, 2 * SWA_BLOCK)
    j = np.arange(MOBA_BLOCK)[:, None]
    i = np.arange(MOBA_BLOCK)[None, :]
    d_own = i - j
    idx_own = np.where(d_own >= 0, _rel_bucket_np(d_own), -1)
    idx_prev = _rel_bucket_np(MOBA_BLOCK + i - j)
    idx_b = np.concatenate([idx_prev, idx_own], axis=0).astype(np.int32)
    bias_b = _bias_expand(tab_b, tab_b[:, REL_BUCKETS - 1], idx_b)
    return bias_a, bias_b


def _block_plan(sizes, n_tok):
    n_assign = n_tok * EXPERT_TOPK
    padded = ((sizes + MOE_TM - 1) // MOE_TM) * MOE_TM
    eid = jnp.arange(N_EXPERTS, dtype=jnp.int32)
    padded_end = jnp.sum(jnp.where(eid[None, :] <= eid[:, None], padded[None, :], 0), axis=1)
    padded_start = padded_end - padded
    cap = -(-n_assign // MOE_TM) * MOE_TM + N_EXPERTS * MOE_TM
    blk_start = jnp.arange(cap // MOE_TM, dtype=jnp.int32) * MOE_TM
    blk_expert = jnp.minimum(
        jnp.sum(padded_end[None, :] <= blk_start[:, None], axis=1), N_EXPERTS - 1).astype(jnp.int32)
    mine = blk_expert[:, None] == eid[None, :]
    size_b = jnp.sum(jnp.where(mine, sizes[None, :], 0), axis=1)
    start_b = jnp.sum(jnp.where(mine, padded_start[None, :], 0), axis=1)
    blk_valid = jnp.clip(size_b - (blk_start - start_b), 0, MOE_TM)
    blk_valid = jnp.where(blk_start < padded_end[-1], blk_valid, 0).astype(jnp.int32)
    return padded_start, blk_expert, blk_valid, cap


def kernel(x, w_in, b_in, attn_sinks, rel_bias_table, w_branch_swa, w_branch_moba, w_out, ln1_gain, ln1_bias,
           w_group_router, b_group_router, w_expert_router, b_expert_router, w_expert_gate, w_expert_up,
           w_expert_down, ln2_gain, ln2_bias):
    assert w_in.shape[0] == DEPTH == 1
    B, S, D = x.shape
    T = B * S
    bf16 = jnp.bfloat16
    f32 = jnp.float32
    w = w_in[0]
    b = b_in[0]

    def cols(off, width):
        return w[:, off:off + width], b[off:off + width]

    wq_a, bq_a = cols(OFF_SWA_Q, SWA_Q_W)
    wk_a, bk_a = cols(OFF_SWA_K, SWA_KV_W)
    wv_a, bv_a = cols(OFF_SWA_V, SWA_KV_W)
    wq_b, bq_b = cols(OFF_MOBA_Q, MOBA_W)
    wk_b, bk_b = cols(OFF_MOBA_K, MOBA_W)
    wv_b, bv_b = cols(OFF_MOBA_V, MOBA_W)

    def dup_kv(t):
        parts = [t[..., i * HEAD_DIM:(i + 1) * HEAD_DIM] for i in range(SWA_KV_HEADS)]
        return jnp.concatenate([p for p in parts for _ in range(2)], axis=-1)

    qs = ATTN_SCALE * LOG2E
    wn = jnp.concatenate([wq_a * qs, dup_kv(wk_a), wq_b * qs, wk_b], axis=1).astype(bf16)
    bn = jnp.concatenate([bq_a * qs, dup_kv(bk_a), bq_b * qs, bk_b])[None, :].astype(f32)
    wt = jnp.concatenate([wv_a, wv_b], axis=1).T.astype(bf16)
    bt = jnp.concatenate([bv_a, bv_b])[:, None].astype(f32)

    qk, vt, eg_b, eu_b, ed_b = _inproj(x, wn, bn, wt, bt, w_expert_gate[0], w_expert_up[0], w_expert_down[0])

    bias_a, bias_b = _attention_bias_tables(rel_bias_table)
    y_a = _swa(attn_sinks[0].astype(f32), qk, vt, bias_a)
    y_b = _moba(qk, vt, bias_b)

    wg, bg = cols(OFF_GATE, 2 * D_MODEL)
    wr = jnp.zeros((ROUTER_ROWS, D), f32)
    wr = wr.at[0:N_GROUPS].set(w_group_router[0].T).at[8:8 + N_EXPERTS].set(w_expert_router[0].T)
    wr_hi = wr.astype(bf16)
    wr_lo = (wr - wr_hi.astype(f32)).astype(bf16)
    br = jnp.zeros((ROUTER_ROWS,), f32)
    br = br.at[0:N_GROUPS].set(b_group_router[0]).at[8:8 + N_EXPERTS].set(b_expert_router[0])[:, None]
    merge_args = (
        x.reshape(T, D), y_a, y_b,
        wg.astype(bf16), bg[None, :].astype(f32), w_branch_swa[0].astype(bf16), w_branch_moba[0].astype(bf16),
        w_out[0].astype(bf16), ln1_gain[0][None, :].astype(f32), ln1_bias[0][None, :].astype(f32),
        jnp.concatenate([wr_hi, wr_lo], axis=0), br)
    g2 = ln2_gain[0][None, :].astype(f32)
    b2 = ln2_bias[0][None, :].astype(f32)

    Tp = T // MOE_PARTS
    out = None
    for part in range(MOE_PARTS):
        x1, x1p, rinfo, counts = _merge(*merge_args, part, MOE_PARTS)
        sizes = counts[:, 0].astype(jnp.int32)
        padded_start, blk_expert, blk_valid, cap = _block_plan(sizes, Tp)
        dest = _dest(rinfo, padded_start)
        xs = _sc_scatter_rows(x1p, dest[0], dest[1], cap)
        y_buf = _experts(blk_expert, blk_valid, xs, eg_b, eu_b, ed_b)
        ypair = _sc_gather_rows(y_buf, dest[0:EXPERT_TOPK].reshape(-1))
        out = _final(x1, ypair, rinfo, g2, b2, out, part, MOE_PARTS)
    return out.reshape(B, S, D)
```

```python
import functools
import math

import numpy as np
import jax
import jax.numpy as jnp
from jax import lax
from jax.experimental import pallas as pl
from jax.experimental.pallas import tpu as pltpu
from jax.experimental.pallas import tpu_sc as plsc

D_MODEL = 1024
HEAD_DIM = 64
SWA_Q_HEADS = 8
SWA_KV_HEADS = 2
SWA_GROUP = SWA_Q_HEADS // SWA_KV_HEADS
SWA_WINDOW = 128
SWA_BLOCK = 128
MOBA_HEADS = 8
MOBA_BLOCK = 256
MOBA_TOPK = 3
MOBA_LOOKAHEAD = 2
BF16_SUBLANES = 16
MOBA_VROWS = HEAD_DIM + BF16_SUBLANES
REL_BUCKETS = 32
REL_MAX_DIST = 128
N_GROUPS = 4
EXPERTS_PER_GROUP = 8
N_EXPERTS = N_GROUPS * EXPERTS_PER_GROUP
EXPERT_TOPK = 2
D_EXPERT = 512
LN_EPS = 1e-5
DEPTH = 1
DEEPNORM_ALPHA = (2.0 * DEPTH) ** 0.25
NEG_INF = -1e30
ATTN_SCALE = HEAD_DIM ** -0.5
LOG2E = math.log2(math.e)

SWA_Q_W = SWA_Q_HEADS * HEAD_DIM
SWA_KV_W = SWA_KV_HEADS * HEAD_DIM
MOBA_W = MOBA_HEADS * HEAD_DIM
OFF_SWA_Q = 0
OFF_SWA_K = OFF_SWA_Q + SWA_Q_W
OFF_SWA_V = OFF_SWA_K + SWA_KV_W
OFF_MOBA_Q = OFF_SWA_V + SWA_KV_W
OFF_MOBA_K = OFF_MOBA_Q + MOBA_W
OFF_MOBA_V = OFF_MOBA_K + MOBA_W
OFF_GATE = OFF_MOBA_V + MOBA_W

LANES = 128
QK_SWA_Q = 0
QK_SWA_K = QK_SWA_Q + SWA_Q_W
QK_MOBA_Q = QK_SWA_K + SWA_KV_HEADS * LANES
QK_MOBA_K = QK_MOBA_Q + MOBA_W
QK_W = QK_MOBA_K + MOBA_W
VT_SWA = 0
VT_MOBA = VT_SWA + SWA_KV_W
VT_W = VT_MOBA + MOBA_W

MOE_TM = 512
MOE_PARTS = 2
SC_CHUNK = 128
ROUTER_ROWS = 8 + N_EXPERTS
VMEM_LIMIT = 56 * 1024 * 1024

_NT = (((1,), (1,)), ((), ()))
_TN = (((0,), (0,)), ((), ()))


def _rel_bucket_np(dist):
    n = np.maximum(dist, 0)
    max_exact = REL_BUCKETS // 2
    nf = np.maximum(n, 1).astype(np.float32)
    large = max_exact + (np.log(nf / np.float32(max_exact)) / np.float32(math.log(REL_MAX_DIST / max_exact))
                         * np.float32(REL_BUCKETS - max_exact)).astype(np.int32)
    large = np.minimum(large, REL_BUCKETS - 1)
    return np.where(n < max_exact, n, large).astype(np.int32)


def _inproj_kernel(x_ref, wn_ref, bn_ref, wt_ref, bt_ref, eg_ref, eu_ref, ed_ref,
                   qk_ref, vt_ref, egb_ref, eub_ref, edb_ref):
    xb = x_ref[0].astype(jnp.bfloat16)
    qk = jnp.dot(xb, wn_ref[...], preferred_element_type=jnp.float32) + bn_ref[...]
    qk_ref[0] = qk.astype(jnp.bfloat16)
    vt = lax.dot_general(wt_ref[...], xb, _NT, preferred_element_type=jnp.float32) + bt_ref[...]
    vt_ref[0] = vt.astype(jnp.bfloat16)
    egb_ref[...] = eg_ref[...].astype(jnp.bfloat16)
    eub_ref[...] = eu_ref[...].astype(jnp.bfloat16)
    edb_ref[...] = ed_ref[...].astype(jnp.bfloat16)


def _inproj(x, wn, bn, wt, bt, w_gate, w_up, w_down, tm=512):
    B, S, D = x.shape
    nt = S // tm
    steps = B * nt

    def sliced(w):
        E, R, C = w.shape
        assert (E * R) % (steps * BF16_SUBLANES) == 0
        return w.reshape(steps, E * R // steps, C)

    eg, eu, ed = sliced(w_gate), sliced(w_up), sliced(w_down)
    espec = lambda w: pl.BlockSpec((1,) + w.shape[1:], lambda b, i: (b * nt + i, 0, 0))
    qk, vt, egb, eub, edb = pl.pallas_call(
        _inproj_kernel,
        grid=(B, nt),
        in_specs=[
            pl.BlockSpec((1, tm, D), lambda b, i: (b, i, 0)),
            pl.BlockSpec((D, QK_W), lambda b, i: (0, 0)),
            pl.BlockSpec((1, QK_W), lambda b, i: (0, 0)),
            pl.BlockSpec((VT_W, D), lambda b, i: (0, 0)),
            pl.BlockSpec((VT_W, 1), lambda b, i: (0, 0)),
            espec(eg), espec(eu), espec(ed),
        ],
        out_specs=[
            pl.BlockSpec((1, tm, QK_W), lambda b, i: (b, i, 0)),
            pl.BlockSpec((1, VT_W, tm), lambda b, i: (b, 0, i)),
            espec(eg), espec(eu), espec(ed),
        ],
        out_shape=[
            jax.ShapeDtypeStruct((B, S, QK_W), jnp.bfloat16),
            jax.ShapeDtypeStruct((B, VT_W, S), jnp.bfloat16),
            jax.ShapeDtypeStruct(eg.shape, jnp.bfloat16),
            jax.ShapeDtypeStruct(eu.shape, jnp.bfloat16),
            jax.ShapeDtypeStruct(ed.shape, jnp.bfloat16),
        ],
        compiler_params=pltpu.CompilerParams(
            dimension_semantics=("parallel", "parallel"), vmem_limit_bytes=VMEM_LIMIT),
        name="inproj",
    )(x, wn, bn, wt, bt, eg, eu, ed)
    return qk, vt, egb.reshape(w_gate.shape), eub.reshape(w_up.shape), edb.reshape(w_down.shape)


def _swa_kernel(sink_ref, q_ref, k_ref, vt_ref, bias_ref, o_ref, vaug_ref, *s_bufs):
    S = q_ref.shape[1]
    nblk = S // SWA_BLOCK
    half = HEAD_DIM
    vt = vt_ref[0]
    rows = lax.broadcasted_iota(jnp.int32, vt.shape, 0)
    one = jnp.ones_like(vt)
    vaug_ref[0] = jnp.where(rows < half, vt, one)
    vaug_ref[1] = jnp.where(rows < half, one, vt)
    lane = lax.broadcasted_iota(jnp.int32, (SWA_BLOCK, LANES), 1)
    col = lax.broadcasted_iota(jnp.int32, (1, 2 * SWA_BLOCK), 1)
    npair = SWA_Q_HEADS // 2

    def expand(specs):
        return [(qs, ks, nk, boff, pair, (2 * pair) // SWA_GROUP) for qs, ks, nk, boff in specs for pair in range(npair)]

    def score(specs):
        ss = []
        for qs, ks, nk, boff, pair, kv in expand(specs):
            qblk = q_ref[0, pl.ds(qs, SWA_BLOCK), pair * LANES:(pair + 1) * LANES]
            zero = jnp.zeros_like(qblk)
            q2 = jnp.concatenate([jnp.where(lane < half, qblk, zero), jnp.where(lane >= half, qblk, zero)], axis=0)
            kblk = k_ref[0, pl.ds(ks, nk), kv * LANES:(kv + 1) * LANES]
            s = lax.dot_general(kblk, q2, _NT, preferred_element_type=jnp.float32)
            ss.append(s + bias_ref[pair, boff:boff + nk, :])
        return ss

    def finish(specs, ss):
        items = expand(specs)
        stats = []
        for (qs, ks, nk, boff, pair, kv), s in zip(items, ss):
            sink = jnp.where(col < SWA_BLOCK, sink_ref[2 * pair], sink_ref[2 * pair + 1]) * LOG2E
            m = jnp.maximum(_colmax(s), sink)
            stats.append((jnp.exp2(s - m).astype(jnp.bfloat16), jnp.exp2(sink - m)))
        rs = [jnp.dot(vaug_ref[kv, :, pl.ds(ks, nk)], p, preferred_element_type=jnp.float32)
              for (qs, ks, nk, boff, pair, kv), (p, _) in zip(items, stats)]
        outs = []
        for (qs, ks, nk, boff, pair, kv), (_, esink), r in zip(items, stats, rs):
            l = (r[half:half + 1, :] if kv == 0 else r[0:1, :]) + esink
            o = (r[0:half, :] if kv == 0 else r[half:, :]) / l
            outs.append(jnp.concatenate([o[:, :SWA_BLOCK], o[:, SWA_BLOCK:]], axis=0))
        for i, (qs, _, _, _) in enumerate(specs):
            ot = jnp.concatenate(outs[i * npair:(i + 1) * npair], axis=0)
            o_ref[0, :, pl.ds(qs, SWA_BLOCK)] = ot.astype(jnp.bfloat16)

    first = [(0, 0, SWA_BLOCK, SWA_BLOCK), (SWA_BLOCK, 0, 2 * SWA_BLOCK, 0)]
    finish(first, score(first))

    assert nblk % 2 == 0
    n_groups = nblk // 2 - 1
    nbuf = len(s_bufs)

    def group_specs(g):
        specs = []
        for n in (2 * g + 2, 2 * g + 3):
            qs = pl.multiple_of(n * SWA_BLOCK, SWA_BLOCK)
            specs.append((qs, pl.multiple_of(qs - SWA_BLOCK, SWA_BLOCK), 2 * SWA_BLOCK, 0))
        return specs

    def score_to(g, buf):
        for k, s in enumerate(score(group_specs(g))):
            buf[k] = s

    def finish_from(g, buf):
        finish(group_specs(g), [buf[k] for k in range(2 * npair)])

    for g in range(min(2, n_groups)):
        score_to(g, s_bufs[g])

    def body(t, carry):
        for j in range(nbuf):
            score_to(nbuf * t + 2 + j, s_bufs[(2 + j) % nbuf])
            finish_from(nbuf * t + j, s_bufs[j])
        return carry

    n_trips = max(n_groups - 2, 0) // nbuf
    lax.fori_loop(0, n_trips, body, 0)
    for g in range(nbuf * n_trips, n_groups):
        if g + 2 < n_groups:
            score_to(g + 2, s_bufs[(g + 2) % nbuf])
        finish_from(g, s_bufs[g % nbuf])


def _swa(sinks, qk, vt, bias):
    B, S, _ = qk.shape
    return pl.pallas_call(
        _swa_kernel,
        grid_spec=pltpu.PrefetchScalarGridSpec(
            num_scalar_prefetch=1,
            grid=(B,),
            in_specs=[
                pl.BlockSpec((1, S, SWA_Q_W), lambda b, s: (b, 0, QK_SWA_Q // SWA_Q_W)),
                pl.BlockSpec((1, S, 2 * LANES), lambda b, s: (b, 0, QK_SWA_K // (2 * LANES))),
                pl.BlockSpec((1, SWA_KV_W, S), lambda b, s: (b, VT_SWA // SWA_KV_W, 0)),
                pl.BlockSpec((SWA_Q_HEADS // 2, 2 * SWA_BLOCK, 2 * SWA_BLOCK), lambda b, s: (0, 0, 0)),
            ],
            out_specs=pl.BlockSpec((1, SWA_Q_W, S), lambda b, s: (b, 0, 0)),
            scratch_shapes=[pltpu.VMEM((SWA_KV_HEADS, SWA_KV_W, S), jnp.bfloat16)]
            + [pltpu.VMEM((SWA_Q_HEADS, 2 * SWA_BLOCK, 2 * SWA_BLOCK), jnp.float32)] * 4,
        ),
        out_shape=jax.ShapeDtypeStruct((B, SWA_Q_W, S), jnp.bfloat16),
        compiler_params=pltpu.CompilerParams(
            dimension_semantics=("parallel",), vmem_limit_bytes=VMEM_LIMIT),
        name="swa",
    )(sinks, qk, qk, vt, bias)


def _colmax(s):
    while s.shape[0] > 8:
        h = s.shape[0] // 2
        s = jnp.maximum(s[:h], s[h:])
    return jnp.max(s, axis=0, keepdims=True)


def _moba_kernel(fqb_ref, fc_ref, q_ref, k_ref, vt_ref, bias_ref, o_ref,
                 vaug_ref, gate_ref, rank_ref, sel_ref, far_ref, m_scr, acc_scr, sa_scr, sb_scr, sc_scr, sd_scr):
    S = q_ref.shape[1]
    nb = S // MOBA_BLOCK
    half = HEAD_DIM
    BLK = MOBA_BLOCK

    ones_rows = jnp.ones((MOBA_VROWS - half, S), jnp.bfloat16)
    for h in range(2):
        vaug_ref[h, 0:half, :] = vt_ref[0, h * half:(h + 1) * half, :]
        vaug_ref[h, half:, :] = ones_rows

    kf = k_ref[0].astype(jnp.float32).reshape(nb, BLK, LANES)
    kmean = jnp.sum(kf, axis=1) * (1.0 / BLK)
    k_hi = kmean.astype(jnp.bfloat16)
    k_lo = (kmean - k_hi.astype(jnp.float32)).astype(jnp.bfloat16)
    kcat = jnp.concatenate([k_hi, k_lo], axis=0)

    lane_q = lax.broadcasted_iota(jnp.int32, (S, LANES), 1)
    brow = lax.broadcasted_iota(jnp.int32, (nb, S), 0)
    qblk_of = lax.broadcasted_iota(jnp.int32, (nb, S), 1) // BLK
    qall = q_ref[0]
    for h in range(2):
        qh = jnp.where((lane_q < half) if h == 0 else (lane_q >= half), qall, jnp.zeros_like(qall))
        g2 = lax.dot_general(kcat, qh, _NT, preferred_element_type=jnp.float32)
        gate_ref[...] = g2[0:nb] + g2[nb:2 * nb]
        rank_ref[...] = jnp.zeros_like(rank_ref)
        for m in range(nb - 1):
            lo = (m + 1) * BLK
            G = gate_ref[:, lo:]
            gm = gate_ref[m:m + 1, lo:]
            ge = jnp.where(gm >= G, 1.0, 0.0)
            gt = jnp.where(gm > G, 1.0, 0.0)
            brow_m = lax.broadcasted_iota(jnp.int32, (nb, S - lo), 0)
            rank_ref[:, lo:] += jnp.where(brow_m > m, ge, gt)
        top = rank_ref[...] < float(MOBA_TOPK)
        sel_ref[h] = jnp.where((brow < qblk_of) & top, 0.0, NEG_INF)
        far_ref[h] = jnp.where((brow < qblk_of - 1) & top, 0.0, NEG_INF)

    lane = lax.broadcasted_iota(jnp.int32, (BLK, LANES), 1)

    def head_q(qs, h):
        qblk = q_ref[0, pl.ds(qs, BLK), :]
        return jnp.where((lane < half) if h == 0 else (lane >= half), qblk, jnp.zeros_like(qblk))

    m0 = jnp.full((1, BLK), NEG_INF, jnp.float32)
    acc0 = jnp.zeros((MOBA_VROWS, BLK), jnp.float32)

    def item(h, qb, qs, ks, nk, bias, sel_a, sel_b, m, acc):
        return dict(h=h, qb=qb, qs=qs, ks=ks, nk=nk, bias=bias, sel_a=sel_a, sel_b=sel_b, m=m, acc=acc)

    def near_item(qb, h):
        qs = pl.multiple_of(qb * BLK, BLK)
        ps = pl.multiple_of(qs - BLK, BLK)
        sel_prev = sel_ref[h, pl.ds(qb - 1, 1), pl.ds(qs, BLK)]
        return item(h, qb, qs, ps, 2 * BLK, bias_ref[h], sel_prev, None, m0, acc0)

    def far_item(i, h):
        qb = fqb_ref[i]
        c = fc_ref[i]
        qs = pl.multiple_of(qb * BLK, BLK)
        ks = pl.multiple_of(c * (2 * BLK), 2 * BLK)
        sel_a = far_ref[h, pl.ds(2 * c, 1), pl.ds(qs, BLK)]
        sel_b = far_ref[h, pl.ds(2 * c + 1, 1), pl.ds(qs, BLK)]
        return item(h, qb, qs, ks, 2 * BLK, None, sel_a, sel_b, m_scr[h, qb], acc_scr[h, qb])

    def scores(it):
        kslab = k_ref[0, pl.ds(it["ks"], it["nk"]), :]
        s = lax.dot_general(kslab, head_q(it["qs"], it["h"]), _NT,
                            preferred_element_type=jnp.float32)
        return s if it["bias"] is None else s + it["bias"]

    def probs(it, s):
        m, sel_a, sel_b = it["m"], it["sel_a"], it["sel_b"]
        if it["nk"] == BLK:
            m_new = jnp.maximum(m, _colmax(s))
            p = jnp.exp2(s - m_new)
        else:
            cm_a = _colmax(s[:BLK]) + sel_a
            cm_b = _colmax(s[BLK:])
            if sel_b is not None:
                cm_b = cm_b + sel_b
            m_new = jnp.maximum(m, jnp.maximum(cm_a, cm_b))
            p_a = jnp.exp2(s[:BLK] - (m_new - sel_a))
            p_b = jnp.exp2(s[BLK:] - (m_new if sel_b is None else m_new - sel_b))
            p = jnp.concatenate([p_a, p_b], axis=0)
        return m_new, jnp.exp2(m - m_new), p.astype(jnp.bfloat16)

    def run(items):
        ss = [scores(it) for it in items]
        ps = [probs(it, s) for it, s in zip(items, ss)]
        pvs = [jnp.dot(vaug_ref[it["h"], :, pl.ds(it["ks"], it["nk"])], p, preferred_element_type=jnp.float32)
               for it, (_, _, p) in zip(items, ps)]
        for it, (m_new, alpha, _), pv in zip(items, ps, pvs):
            m_scr[it["h"], it["qb"]] = m_new
            acc_scr[it["h"], it["qb"]] = it["acc"] * alpha + pv

    run([item(h, 0, 0, 0, BLK, bias_ref[h, BLK:, :], None, None, m0, acc0) for h in range(2)]
        + [near_item(nb - 1, h) for h in range(2)])

    bufs = (sa_scr, sb_scr, sc_scr, sd_scr)
    nbuf = len(bufs)

    def finish(items, s_buf):
        ps = [probs(it, s_buf[k]) for k, it in enumerate(items)]
        pvs = [jnp.dot(vaug_ref[it["h"], :, pl.ds(it["ks"], it["nk"])], p, preferred_element_type=jnp.float32)
               for it, (_, _, p) in zip(items, ps)]
        for it, (m_new, alpha, _), pv in zip(items, ps, pvs):
            m_scr[it["h"], it["qb"]] = m_new
            acc_scr[it["h"], it["qb"]] = it["acc"] * alpha + pv

    def pipelined(n_groups, score, items_of):
        for g in range(min(MOBA_LOOKAHEAD, n_groups)):
            score(g, bufs[g])

        def body(t, carry):
            for j in range(nbuf):
                score(nbuf * t + MOBA_LOOKAHEAD + j, bufs[(MOBA_LOOKAHEAD + j) % nbuf])
                finish(items_of(nbuf * t + j), bufs[j])
            return carry

        n_trips = max(n_groups - MOBA_LOOKAHEAD, 0) // nbuf
        lax.fori_loop(0, n_trips, body, 0)
        for g in range(nbuf * n_trips, n_groups):
            if g + MOBA_LOOKAHEAD < n_groups:
                score(g + MOBA_LOOKAHEAD, bufs[(g + MOBA_LOOKAHEAD) % nbuf])
            finish(items_of(g), bufs[g % nbuf])

    near_pairs = nb // 2 - 1

    def near_group(g):
        return [(qb, h) for qb in (g + 1, g + 1 + near_pairs) for h in range(2)]

    def near_score(g, s_buf):
        for k, (qb, h) in enumerate(near_group(g)):
            qs = pl.multiple_of(qb * BLK, BLK)
            ps = pl.multiple_of(qs - BLK, BLK)
            s_buf[k] = lax.dot_general(k_ref[0, pl.ds(ps, 2 * BLK), :], head_q(qs, h), _NT,
                                       preferred_element_type=jnp.float32) + bias_ref[h]

    pipelined(near_pairs, near_score, lambda g: [near_item(qb, h) for qb, h in near_group(g)])

    n_far_groups = fqb_ref.shape[0] // 2

    def far_group(g):
        return [(j, h) for j in (g, g + n_far_groups) for h in range(2)]

    def far_score(g, s_buf):
        for k, (j, h) in enumerate(far_group(g)):
            qs = pl.multiple_of(fqb_ref[j] * BLK, BLK)
            ks = pl.multiple_of(fc_ref[j] * (2 * BLK), 2 * BLK)
            s_buf[k] = lax.dot_general(k_ref[0, pl.ds(ks, 2 * BLK), :], head_q(qs, h), _NT,
                                       preferred_element_type=jnp.float32)

    pipelined(n_far_groups, far_score, lambda g: [far_item(j, h) for j, h in far_group(g)])

    def out_body(t, carry):
        for qb in (2 * t, 2 * t + 1):
            qs = pl.multiple_of(qb * BLK, BLK)
            a0 = acc_scr[0, qb]
            a1 = acc_scr[1, qb]
            ot = jnp.concatenate([a0[0:half] / a0[half:half + 1, :], a1[0:half] / a1[half:half + 1, :]],
                                 axis=0)
            o_ref[0, :, pl.ds(qs, BLK)] = ot.astype(jnp.bfloat16)
        return carry

    lax.fori_loop(0, nb // 2, out_body, 0)


def _moba_far_items(nb):
    items = [(qb, c) for qb in range(2, nb) for c in range(qb // 2)]
    n = len(items)
    assert n % 2 == 0 and all(items[i][0] != items[i + n // 2][0] for i in range(n // 2))
    return np.array([it[0] for it in items], np.int32), np.array([it[1] for it in items], np.int32)


def _moba(qk, vt, bias):
    B, S, _ = qk.shape
    nb = S // MOBA_BLOCK
    npair = MOBA_W // LANES
    far_qb, far_c = _moba_far_items(nb)
    return pl.pallas_call(
        _moba_kernel,
        grid_spec=pltpu.PrefetchScalarGridSpec(
            num_scalar_prefetch=2,
            grid=(B, npair),
            in_specs=[
                pl.BlockSpec((1, S, LANES), lambda b, p, fq, fc: (b, 0, QK_MOBA_Q // LANES + p)),
                pl.BlockSpec((1, S, LANES), lambda b, p, fq, fc: (b, 0, QK_MOBA_K // LANES + p)),
                pl.BlockSpec((1, LANES, S), lambda b, p, fq, fc: (b, VT_MOBA // LANES + p, 0)),
                pl.BlockSpec((2, 2 * MOBA_BLOCK, MOBA_BLOCK), lambda b, p, fq, fc: (p, 0, 0)),
            ],
            out_specs=pl.BlockSpec((1, LANES, S), lambda b, p, fq, fc: (b, p, 0)),
            scratch_shapes=[
                pltpu.VMEM((2, MOBA_VROWS, S), jnp.bfloat16),
                pltpu.VMEM((nb, S), jnp.float32),
                pltpu.VMEM((nb, S), jnp.float32),
                pltpu.VMEM((2, nb, S), jnp.float32),
                pltpu.VMEM((2, nb, S), jnp.float32),
                pltpu.VMEM((2, nb, 1, MOBA_BLOCK), jnp.float32),
                pltpu.VMEM((2, nb, MOBA_VROWS, MOBA_BLOCK), jnp.float32),
            ] + [pltpu.VMEM((4, 2 * MOBA_BLOCK, MOBA_BLOCK), jnp.float32)] * 4,
        ),
        out_shape=jax.ShapeDtypeStruct((B, MOBA_W, S), jnp.bfloat16),
        compiler_params=pltpu.CompilerParams(
            dimension_semantics=("parallel", "parallel"), vmem_limit_bytes=VMEM_LIMIT),
        name="moba",
    )(jnp.asarray(far_qb), jnp.asarray(far_c), qk, qk, vt, bias)


def _layer_norm(h, gain, bias):
    mu = jnp.mean(h, axis=-1, keepdims=True)
    c = h - mu
    var = jnp.mean(c * c, axis=-1, keepdims=True)
    return c * lax.rsqrt(var + LN_EPS) * gain + bias


def _pack_bf16_pair(a, b):
    ia = lax.bitcast_convert_type(a.astype(jnp.bfloat16).astype(jnp.float32), jnp.int32)
    ib = lax.bitcast_convert_type(b.astype(jnp.bfloat16).astype(jnp.float32), jnp.int32)
    return lax.shift_right_logical(ia, 16) | ib


def _unpack_bf16_pair(w):
    lo = lax.bitcast_convert_type(lax.shift_left(w, 16), jnp.float32)
    hi = lax.bitcast_convert_type(w & jnp.int32(-65536), jnp.float32)
    return lo.astype(jnp.bfloat16), hi.astype(jnp.bfloat16)


def _merge_kernel(x_ref, ya_ref, yb_ref, wg_ref, bg_ref, wa_ref, wb_ref, wo_ref, g1_ref, b1_ref,
                  wr_ref, br_ref, tri_ref, x1_ref, x1p_ref, r_ref, cnt_ref):
    @pl.when(pl.program_id(0) == 0)
    def _():
        cnt_ref[...] = jnp.zeros_like(cnt_ref)

    ts = tri_ref.shape[0]
    subs = [pl.ds(r0, ts) for r0 in range(0, x_ref.shape[0], ts)]
    pre = [_merge_matmuls(x_ref[rows, :], ya_ref[0, :, rows], yb_ref[0, :, rows], wg_ref, bg_ref, wa_ref, wb_ref, wo_ref)
           for rows in subs]
    for rows, h in zip(subs, pre):
        _merge_route(h, rows, g1_ref, b1_ref, wr_ref, br_ref, tri_ref, x1_ref, x1p_ref, r_ref, cnt_ref)


def _merge_matmuls(x, ya_t, yb_t, wg_ref, bg_ref, wa_ref, wb_ref, wo_ref):
    xb = x.astype(jnp.bfloat16)
    z = jnp.dot(xb, wg_ref[...], preferred_element_type=jnp.float32) + bg_ref[...]
    gates = 1.0 / (1.0 + jnp.exp(-z))
    pa = lax.dot_general(ya_t, wa_ref[...], _TN, preferred_element_type=jnp.float32)
    pb = lax.dot_general(yb_t, wb_ref[...], _TN, preferred_element_type=jnp.float32)
    merged = gates[:, :D_MODEL] * pa + gates[:, D_MODEL:] * pb
    mixed = jnp.dot(merged.astype(jnp.bfloat16), wo_ref[...], preferred_element_type=jnp.float32)
    return DEEPNORM_ALPHA * x + mixed


def _merge_route(h, rows, g1_ref, b1_ref, wr_ref, br_ref, tri_ref, x1_ref, x1p_ref, r_ref, cnt_ref):
    x1 = _layer_norm(h, g1_ref[...], b1_ref[...])
    x1_ref[rows, :] = x1
    x1_hi = x1.astype(jnp.bfloat16)
    x1p_ref[rows, :] = _pack_bf16_pair(x1[:, :D_MODEL // 2], x1[:, D_MODEL // 2:])
    x1_lo = (x1 - x1_hi.astype(jnp.float32)).astype(jnp.bfloat16)

    R = ROUTER_ROWS
    l1 = lax.dot_general(wr_ref[...], x1_hi, _NT, preferred_element_type=jnp.float32)
    l2 = lax.dot_general(wr_ref[0:R, :], x1_lo, _NT, preferred_element_type=jnp.float32)
    L = l1[0:R] + l1[R:2 * R] + l2 + br_ref[...]
    tm = x1.shape[0]
    row = lax.broadcasted_iota(jnp.int32, (8, tm), 0)
    big = jnp.float32(-3e38)
    gl = jnp.where(row < N_GROUPS, L[0:8], big)
    gmax = jnp.max(gl, axis=0, keepdims=True)
    g_idx = jnp.min(jnp.where(gl == gmax, row, 8), axis=0, keepdims=True)
    gsum = jnp.sum(jnp.where(row < N_GROUPS, jnp.exp(gl - gmax), 0.0), axis=0, keepdims=True)
    g_prob = 1.0 / gsum
    E = L[8 + 8 * (N_GROUPS - 1):8 + 8 * N_GROUPS]
    for g in range(N_GROUPS - 2, -1, -1):
        E = jnp.where(g_idx == g, L[8 + 8 * g:16 + 8 * g], E)
    t0 = jnp.max(E, axis=0, keepdims=True)
    loc0 = jnp.min(jnp.where(E == t0, row, 8), axis=0, keepdims=True)
    E2 = jnp.where(row == loc0, big, E)
    t1 = jnp.max(E2, axis=0, keepdims=True)
    loc1 = jnp.min(jnp.where(E2 == t1, row, 8), axis=0, keepdims=True)
    ex = jnp.exp(t1 - t0)
    w0 = g_prob / (1.0 + ex)
    w1 = g_prob * ex / (1.0 + ex)
    e0i = g_idx * EXPERTS_PER_GROUP + loc0
    e1i = g_idx * EXPERTS_PER_GROUP + loc1

    erow = lax.broadcasted_iota(jnp.int32, (N_EXPERTS, tm), 0)
    oh0 = jnp.where(erow == e0i, 1.0, 0.0)
    oh1 = jnp.where(erow == e1i, 1.0, 0.0)
    both = oh0 + oh1
    prefix = jnp.dot(both.astype(jnp.bfloat16), tri_ref[...], preferred_element_type=jnp.float32)
    prefix = prefix + cnt_ref[:, 0:1]
    rank0 = jnp.sum(oh0 * prefix, axis=0, keepdims=True)
    rank1 = jnp.sum(oh1 * prefix, axis=0, keepdims=True)
    cnt_ref[...] = cnt_ref[...] + jnp.sum(both, axis=1, keepdims=True)

    vals = (e0i.astype(jnp.float32), e1i.astype(jnp.float32), w0, w1, rank0, rank1)
    out = jnp.zeros((8, tm), jnp.float32)
    for k, v in enumerate(vals):
        out = jnp.where(row == k, v, out)
    r_ref[:, rows] = out


def _merge(x2, ya, yb, wg, bg, wa, wb, wo, g1, b1, wr, br, part, n_parts, tm=512, ts=256):
    D = x2.shape[1]
    T = x2.shape[0] // n_parts
    S = ya.shape[2]
    per_seq = S // tm
    assert per_seq * tm == S
    off = part * (T // tm)
    const = lambda i: (0, 0)
    seq_tile = lambda i: ((i + off) // per_seq, 0, (i + off) % per_seq)
    tri = jnp.triu(jnp.ones((ts, ts), jnp.bfloat16), k=1)
    return pl.pallas_call(
        _merge_kernel,
        grid=(T // tm,),
        in_specs=[
            pl.BlockSpec((tm, D), lambda i: (i + off, 0)),
            pl.BlockSpec((1, SWA_Q_W, tm), seq_tile),
            pl.BlockSpec((1, MOBA_W, tm), seq_tile),
            pl.BlockSpec((D, 2 * D), const),
            pl.BlockSpec((1, 2 * D), const),
            pl.BlockSpec((SWA_Q_W, D), const),
            pl.BlockSpec((MOBA_W, D), const),
            pl.BlockSpec((D, D), const),
            pl.BlockSpec((1, D), const),
            pl.BlockSpec((1, D), const),
            pl.BlockSpec((2 * ROUTER_ROWS, D), const),
            pl.BlockSpec((ROUTER_ROWS, 1), const),
            pl.BlockSpec((ts, ts), const),
        ],
        out_specs=[
            pl.BlockSpec((tm, D), lambda i: (i, 0)),
            pl.BlockSpec((tm, D // 2), lambda i: (i, 0)),
            pl.BlockSpec((8, tm), lambda i: (0, i)),
            pl.BlockSpec((N_EXPERTS, LANES), const),
        ],
        out_shape=[
            jax.ShapeDtypeStruct((T, D), jnp.float32),
            jax.ShapeDtypeStruct((T, D // 2), jnp.int32),
            jax.ShapeDtypeStruct((8, T), jnp.float32),
            jax.ShapeDtypeStruct((N_EXPERTS, LANES), jnp.float32),
        ],
        compiler_params=pltpu.CompilerParams(
            dimension_semantics=("arbitrary",), vmem_limit_bytes=VMEM_LIMIT),
        name="merge",
    )(x2, ya, yb, wg, bg, wa, wb, wo, g1, b1, wr, br, tri)


def _dest_kernel(r_ref, ps_ref, d_ref):
    tt = r_ref.shape[1]
    erow = lax.broadcasted_iota(jnp.int32, (N_EXPERTS, tt), 0)
    row = lax.broadcasted_iota(jnp.int32, (8, tt), 0)
    ps = ps_ref[...]
    out = jnp.zeros((8, tt), jnp.float32)
    for k in range(EXPERT_TOPK):
        e = r_ref[k:k + 1, :].astype(jnp.int32)
        start = jnp.sum(jnp.where(erow == e, ps, 0.0), axis=0, keepdims=True)
        out = jnp.where(row == k, start + r_ref[4 + k:5 + k, :], out)
    d_ref[...] = out.astype(jnp.int32)


def _dest(rinfo, padded_start, tt=8192):
    T = rinfo.shape[1]
    tt = min(tt, T)
    return pl.pallas_call(
        _dest_kernel,
        grid=(T // tt,),
        in_specs=[pl.BlockSpec((8, tt), lambda i: (0, i)), pl.BlockSpec((N_EXPERTS, 1), lambda i: (0, 0))],
        out_specs=pl.BlockSpec((8, tt), lambda i: (0, i)),
        out_shape=jax.ShapeDtypeStruct((8, T), jnp.int32),
        compiler_params=pltpu.CompilerParams(dimension_semantics=("parallel",)),
        name="dest",
    )(rinfo, padded_start.astype(jnp.float32)[:, None])


def _sc_workers():
    info = plsc.get_sparse_core_info()
    return info.num_cores, info.num_subcores


def _sc_scatter_rows(src, dest0, dest1, cap):
    T, W = src.shape
    nc, ns = _sc_workers()
    per_w = T // (nc * ns)
    assert per_w * nc * ns == T and per_w % SC_CHUNK == 0
    mesh = plsc.VectorSubcoreMesh(core_axis_name="c", subcore_axis_name="s")

    @functools.partial(
        pl.kernel, mesh=mesh,
        out_type=jax.ShapeDtypeStruct((cap, W), src.dtype),
        scratch_types=[pltpu.VMEM((SC_CHUNK,), jnp.int32), pltpu.VMEM((SC_CHUNK, W), src.dtype)],
    )
    def scatter(src_hbm, d0_hbm, d1_hbm, out_hbm, idx_v, rows_v):
        wid = lax.axis_index("s") * nc + lax.axis_index("c")
        base = wid * per_w

        @pl.loop(0, per_w // SC_CHUNK)
        def _(c):
            off = pl.multiple_of(base + c * SC_CHUNK, SC_CHUNK)
            pltpu.sync_copy(src_hbm.at[pl.ds(off, SC_CHUNK)], rows_v)
            for d_hbm in (d0_hbm, d1_hbm):
                pltpu.sync_copy(d_hbm.at[pl.ds(off, SC_CHUNK)], idx_v)
                pltpu.sync_copy(rows_v, out_hbm.at[idx_v])

    return scatter(src, dest0, dest1)


def _sc_gather_rows(table, idx):
    N = idx.shape[0]
    W = table.shape[1]
    nc, ns = _sc_workers()
    per_w = N // (nc * ns)
    assert per_w * nc * ns == N and per_w % SC_CHUNK == 0
    mesh = plsc.VectorSubcoreMesh(core_axis_name="c", subcore_axis_name="s")

    @functools.partial(
        pl.kernel, mesh=mesh,
        out_type=jax.ShapeDtypeStruct((N, W), table.dtype),
        scratch_types=[pltpu.VMEM((SC_CHUNK,), jnp.int32), pltpu.VMEM((SC_CHUNK, W), table.dtype)],
    )
    def gather(table_hbm, idx_hbm, out_hbm, idx_v, rows_v):
        wid = lax.axis_index("s") * nc + lax.axis_index("c")
        base = wid * per_w

        @pl.loop(0, per_w // SC_CHUNK)
        def _(c):
            off = pl.multiple_of(base + c * SC_CHUNK, SC_CHUNK)
            pltpu.sync_copy(idx_hbm.at[pl.ds(off, SC_CHUNK)], idx_v)
            pltpu.sync_copy(table_hbm.at[idx_v], rows_v)
            pltpu.sync_copy(rows_v, out_hbm.at[pl.ds(off, SC_CHUNK)])

    return gather(table, idx)


def _expert_kernel(be_ref, nv_ref, first_ref, slot_ref, nxt_ref, x_ref, wg_hbm, wu_hbm, wd_hbm, y_ref,
                   wg_buf, wu_buf, wd_buf, sem):
    i = pl.program_id(0)
    nv = nv_ref[i]
    slot = slot_ref[i]

    def weight_copies(e, s):
        return [pltpu.make_async_copy(hbm.at[e], buf.at[s], sem.at[s, k])
                for k, (hbm, buf) in enumerate(((wg_hbm, wg_buf), (wu_hbm, wu_buf), (wd_hbm, wd_buf)))]

    @pl.when(i == 0)
    def _():
        for c in weight_copies(be_ref[0], 0):
            c.start()

    @pl.when(first_ref[i] == 1)
    def _():
        @pl.when(nxt_ref[i] >= 0)
        def _():
            for c in weight_copies(nxt_ref[i], 1 - slot):
                c.start()

        for c in weight_copies(be_ref[i], slot):
            c.wait()

    @pl.when(nv > 0)
    def _():
        lo, hi = _unpack_bf16_pair(x_ref[...])
        xb = jnp.concatenate([lo, hi], axis=1)
        rows = lax.broadcasted_iota(jnp.int32, xb.shape, 0)
        xb = jnp.where(rows < nv, xb, jnp.zeros_like(xb))
        g = jnp.dot(xb, wg_buf[slot], preferred_element_type=jnp.float32)
        u = jnp.dot(xb, wu_buf[slot], preferred_element_type=jnp.float32)
        act = (g / (1.0 + jnp.exp(-g))) * u
        y = jnp.dot(act.astype(jnp.bfloat16), wd_buf[slot], preferred_element_type=jnp.float32)
        y_ref[...] = _pack_bf16_pair(y[:, :D_MODEL // 2], y[:, D_MODEL // 2:])

    @pl.when(nv <= 0)
    def _():
        y_ref[...] = jnp.zeros_like(y_ref)


def _experts(plan, xs, wg, wu, wd):
    cap, DW = xs.shape
    D = 2 * DW
    n_blocks = cap // MOE_TM
    hbm = pl.BlockSpec(memory_space=pl.ANY)
    return pl.pallas_call(
        _expert_kernel,
        grid_spec=pltpu.PrefetchScalarGridSpec(
            num_scalar_prefetch=5,
            grid=(n_blocks,),
            in_specs=[pl.BlockSpec((MOE_TM, DW), lambda i, *_: (i, 0)), hbm, hbm, hbm],
            out_specs=pl.BlockSpec((MOE_TM, DW), lambda i, *_: (i, 0)),
            scratch_shapes=[
                pltpu.VMEM((2, D, D_EXPERT), jnp.bfloat16),
                pltpu.VMEM((2, D, D_EXPERT), jnp.bfloat16),
                pltpu.VMEM((2, D_EXPERT, D), jnp.bfloat16),
                pltpu.SemaphoreType.DMA((2, 3)),
            ],
        ),
        out_shape=jax.ShapeDtypeStruct((cap, DW), jnp.int32),
        compiler_params=pltpu.CompilerParams(
            dimension_semantics=("arbitrary",), vmem_limit_bytes=VMEM_LIMIT),
        name="experts",
    )(*plan, xs, wg, wu, wd)


def _final_kernel(x1_ref, y0_ref, y1_ref, w_ref, g2_ref, b2_ref, *rest):
    o_ref = rest[-1]
    w = w_ref[...].T
    halves = []
    for part in range(2):
        y0 = _unpack_bf16_pair(y0_ref[...])[part].astype(jnp.float32)
        y1 = _unpack_bf16_pair(y1_ref[...])[part].astype(jnp.float32)
        halves.append(y0 * w[:, 2:3] + y1 * w[:, 3:4])
    moe = jnp.concatenate(halves, axis=1)
    o_ref[...] = _layer_norm(DEEPNORM_ALPHA * x1_ref[...] + moe, g2_ref[...], b2_ref[...])


def _final(x1, ypair, rinfo, g2, b2, out_prev, part, n_parts, tm=1024):
    T, D = x1.shape
    const = lambda i: (0, 0)
    nt = T // tm
    off = part * nt
    in_specs = [
        pl.BlockSpec((tm, D), lambda i: (i, 0)),
        pl.BlockSpec((tm, D // 2), lambda i: (i, 0)),
        pl.BlockSpec((tm, D // 2), lambda i: (i + nt, 0)),
        pl.BlockSpec((8, tm), lambda i: (0, i)),
        pl.BlockSpec((1, D), const),
        pl.BlockSpec((1, D), const),
    ]
    args = [x1, ypair, ypair, rinfo, g2, b2]
    aliases = {}
    if out_prev is not None:
        in_specs.append(pl.BlockSpec(memory_space=pl.ANY))
        args.append(out_prev)
        aliases = {len(args) - 1: 0}
    return pl.pallas_call(
        _final_kernel,
        grid=(nt,),
        in_specs=in_specs,
        out_specs=pl.BlockSpec((tm, D), lambda i: (i + off, 0)),
        out_shape=jax.ShapeDtypeStruct((n_parts * T, D), jnp.float32),
        input_output_aliases=aliases,
        compiler_params=pltpu.CompilerParams(
            dimension_semantics=("parallel",), vmem_limit_bytes=VMEM_LIMIT),
        name="final",
    )(*args)


def _bias_kernel(tab_ref, sub_ref, idx_ref, o_ref):
    h = pl.program_id(0)
    idx = idx_ref[...]
    acc = jnp.full(idx.shape, NEG_INF, jnp.float32)
    for b in range(REL_BUCKETS):
        acc = jnp.where(idx == b, (tab_ref[h, b] - sub_ref[h]) * LOG2E, acc)
    o_ref[0] = acc


def _bias_expand(table_hb, sub_h, idx):
    H = table_hb.shape[0]
    K, Q = idx.shape
    return pl.pallas_call(
        _bias_kernel,
        grid_spec=pltpu.PrefetchScalarGridSpec(
            num_scalar_prefetch=2,
            grid=(H,),
            in_specs=[pl.BlockSpec((K, Q), lambda h, t, s: (0, 0))],
            out_specs=pl.BlockSpec((1, K, Q), lambda h, t, s: (h, 0, 0)),
        ),
        out_shape=jax.ShapeDtypeStruct((H, K, Q), jnp.float32),
        compiler_params=pltpu.CompilerParams(dimension_semantics=("parallel",)),
        name="bias",
    )(table_hb, sub_h, jnp.asarray(idx))


def _attention_bias_tables(rel_bias_table):
    rel = rel_bias_table.astype(jnp.float32)
    tab_a = rel[:, :SWA_Q_HEADS].T
    tab_b = rel[:, SWA_Q_HEADS:].T
    kj = np.arange(2 * SWA_BLOCK)[:, None]
    qi = np.arange(SWA_BLOCK)[None, :]
    dist = SWA_BLOCK + qi - kj
    idx_a = np.where((dist >= 0) & (dist < SWA_WINDOW), _rel_bucket_np(dist), -1).astype(np.int32)
    bias_a = _bias_expand(tab_a, jnp.zeros((SWA_Q_HEADS,), jnp.float32), idx_a)
    bias_a = bias_a.reshape(SWA_Q_HEADS // 2, 2, 2 * SWA_BLOCK, SWA_BLOCK).transpose(0, 2, 1, 3)
    bias_a = bias_a.reshape(SWA_Q_HEADS // 2, 2 * SWA_BLOCK, 2 * SWA_BLOCK)
    j = np.arange(MOBA_BLOCK)[:, None]
    i = np.arange(MOBA_BLOCK)[None, :]
    d_own = i - j
    idx_own = np.where(d_own >= 0, _rel_bucket_np(d_own), -1)
    idx_prev = _rel_bucket_np(MOBA_BLOCK + i - j)
    idx_b = np.concatenate([idx_prev, idx_own], axis=0).astype(np.int32)
    bias_b = _bias_expand(tab_b, tab_b[:, REL_BUCKETS - 1], idx_b)
    return bias_a, bias_b


def _block_plan(sizes, n_tok):
    n_assign = n_tok * EXPERT_TOPK
    padded = ((sizes + MOE_TM - 1) // MOE_TM) * MOE_TM
    eid = jnp.arange(N_EXPERTS, dtype=jnp.int32)
    padded_end = jnp.sum(jnp.where(eid[None, :] <= eid[:, None], padded[None, :], 0), axis=1)
    padded_start = padded_end - padded
    cap = -(-n_assign // MOE_TM) * MOE_TM + N_EXPERTS * MOE_TM
    blk_start = jnp.arange(cap // MOE_TM, dtype=jnp.int32) * MOE_TM
    used = blk_start < padded_end[-1]
    nonempty = sizes > 0
    blk_expert = jnp.where(used, jnp.sum(padded_end[None, :] <= blk_start[:, None], axis=1),
                           jnp.max(jnp.where(nonempty, eid, 0))).astype(jnp.int32)
    mine = blk_expert[:, None] == eid[None, :]
    size_b = jnp.sum(jnp.where(mine, sizes[None, :], 0), axis=1)
    start_b = jnp.sum(jnp.where(mine, padded_start[None, :], 0), axis=1)
    blk_valid = jnp.where(used, jnp.clip(size_b - (blk_start - start_b), 0, MOE_TM), 0).astype(jnp.int32)
    blk_first = (used & (blk_start == start_b)).astype(jnp.int32)
    earlier = nonempty[None, :] & (eid[None, :] < blk_expert[:, None])
    blk_slot = (jnp.sum(earlier, axis=1) % 2).astype(jnp.int32)
    later = jnp.where(nonempty[None, :] & (eid[None, :] > eid[:, None]), eid[None, :], N_EXPERTS)
    next_of = jnp.min(later, axis=1)
    blk_next = jnp.sum(jnp.where(mine, jnp.where(next_of < N_EXPERTS, next_of, -1)[None, :], 0), axis=1).astype(jnp.int32)
    return padded_start, (blk_expert, blk_valid, blk_first, blk_slot, blk_next), cap


def kernel(x, w_in, b_in, attn_sinks, rel_bias_table, w_branch_swa, w_branch_moba, w_out, ln1_gain, ln1_bias,
           w_group_router, b_group_router, w_expert_router, b_expert_router, w_expert_gate, w_expert_up,
           w_expert_down, ln2_gain, ln2_bias):
    assert w_in.shape[0] == DEPTH == 1
    B, S, D = x.shape
    T = B * S
    bf16 = jnp.bfloat16
    f32 = jnp.float32
    w = w_in[0]
    b = b_in[0]

    def cols(off, width):
        return w[:, off:off + width], b[off:off + width]

    wq_a, bq_a = cols(OFF_SWA_Q, SWA_Q_W)
    wk_a, bk_a = cols(OFF_SWA_K, SWA_KV_W)
    wv_a, bv_a = cols(OFF_SWA_V, SWA_KV_W)
    wq_b, bq_b = cols(OFF_MOBA_Q, MOBA_W)
    wk_b, bk_b = cols(OFF_MOBA_K, MOBA_W)
    wv_b, bv_b = cols(OFF_MOBA_V, MOBA_W)

    def dup_kv(t):
        parts = [t[..., i * HEAD_DIM:(i + 1) * HEAD_DIM] for i in range(SWA_KV_HEADS)]
        return jnp.concatenate([p for p in parts for _ in range(2)], axis=-1)

    qs = ATTN_SCALE * LOG2E
    wn = jnp.concatenate([wq_a * qs, dup_kv(wk_a), wq_b * qs, wk_b], axis=1).astype(bf16)
    bn = jnp.concatenate([bq_a * qs, dup_kv(bk_a), bq_b * qs, bk_b])[None, :].astype(f32)
    wt = jnp.concatenate([wv_a, wv_b], axis=1).T.astype(bf16)
    bt = jnp.concatenate([bv_a, bv_b])[:, None].astype(f32)

    qk, vt, eg_b, eu_b, ed_b = _inproj(x, wn, bn, wt, bt, w_expert_gate[0], w_expert_up[0], w_expert_down[0])

    bias_a, bias_b = _attention_bias_tables(rel_bias_table)
    y_a = _swa(attn_sinks[0].astype(f32), qk, vt, bias_a)
    y_b = _moba(qk, vt, bias_b)

    wg, bg = cols(OFF_GATE, 2 * D_MODEL)
    wr = jnp.zeros((ROUTER_ROWS, D), f32)
    wr = wr.at[0:N_GROUPS].set(w_group_router[0].T).at[8:8 + N_EXPERTS].set(w_expert_router[0].T)
    wr_hi = wr.astype(bf16)
    wr_lo = (wr - wr_hi.astype(f32)).astype(bf16)
    br = jnp.zeros((ROUTER_ROWS,), f32)
    br = br.at[0:N_GROUPS].set(b_group_router[0]).at[8:8 + N_EXPERTS].set(b_expert_router[0])[:, None]
    merge_args = (
        x.reshape(T, D), y_a, y_b,
        wg.astype(bf16), bg[None, :].astype(f32), w_branch_swa[0].astype(bf16), w_branch_moba[0].astype(bf16),
        w_out[0].astype(bf16), ln1_gain[0][None, :].astype(f32), ln1_bias[0][None, :].astype(f32),
        jnp.concatenate([wr_hi, wr_lo], axis=0), br)
    g2 = ln2_gain[0][None, :].astype(f32)
    b2 = ln2_bias[0][None, :].astype(f32)

    Tp = T // MOE_PARTS
    out = None
    for part in range(MOE_PARTS):
        x1, x1p, rinfo, counts = _merge(*merge_args, part, MOE_PARTS)
        sizes = counts[:, 0].astype(jnp.int32)
        padded_start, plan, cap = _block_plan(sizes, Tp)
        dest = _dest(rinfo, padded_start)
        xs = _sc_scatter_rows(x1p, dest[0], dest[1], cap)
        y_buf = _experts(plan, xs, eg_b, eu_b, ed_b)
        ypair = _sc_gather_rows(y_buf, dest[0:EXPERT_TOPK].reshape(-1))
        out = _final(x1, ypair, rinfo, g2, b2, out, part, MOE_PARTS)
    return out.reshape(B, S, D)
```

```python
import functools
import math

import numpy as np
import jax
import jax.numpy as jnp
from jax import lax
from jax.experimental import pallas as pl
from jax.experimental.pallas import tpu as pltpu
from jax.experimental.pallas import tpu_sc as plsc

D_MODEL = 1024
HEAD_DIM = 64
SWA_Q_HEADS = 8
SWA_KV_HEADS = 2
SWA_GROUP = SWA_Q_HEADS // SWA_KV_HEADS
SWA_WINDOW = 128
SWA_BLOCK = 128
MOBA_HEADS = 8
MOBA_BLOCK = 256
MOBA_TOPK = 3
MOBA_LOOKAHEAD = 2
MOBA_UNROLL = 8
BF16_SUBLANES = 16
MOBA_VROWS = HEAD_DIM + BF16_SUBLANES
REL_BUCKETS = 32
REL_MAX_DIST = 128
N_GROUPS = 4
EXPERTS_PER_GROUP = 8
N_EXPERTS = N_GROUPS * EXPERTS_PER_GROUP
EXPERT_TOPK = 2
D_EXPERT = 512
LN_EPS = 1e-5
DEPTH = 1
DEEPNORM_ALPHA = (2.0 * DEPTH) ** 0.25
NEG_INF = -1e30
ATTN_SCALE = HEAD_DIM ** -0.5
LOG2E = math.log2(math.e)

SWA_Q_W = SWA_Q_HEADS * HEAD_DIM
SWA_KV_W = SWA_KV_HEADS * HEAD_DIM
MOBA_W = MOBA_HEADS * HEAD_DIM
OFF_SWA_Q = 0
OFF_SWA_K = OFF_SWA_Q + SWA_Q_W
OFF_SWA_V = OFF_SWA_K + SWA_KV_W
OFF_MOBA_Q = OFF_SWA_V + SWA_KV_W
OFF_MOBA_K = OFF_MOBA_Q + MOBA_W
OFF_MOBA_V = OFF_MOBA_K + MOBA_W
OFF_GATE = OFF_MOBA_V + MOBA_W

LANES = 128
QK_SWA_Q = 0
QK_SWA_K = QK_SWA_Q + SWA_Q_W
QK_MOBA_Q = QK_SWA_K + SWA_KV_HEADS * LANES
QK_MOBA_K = QK_MOBA_Q + MOBA_W
QK_W = QK_MOBA_K + MOBA_W
VT_SWA = 0
VT_MOBA = VT_SWA + SWA_KV_W
VT_W = VT_MOBA + MOBA_W

MOE_TM = 512
MOE_PARTS = 2
SC_CHUNK = 128
ROUTER_ROWS = 8 + N_EXPERTS
VMEM_LIMIT = 56 * 1024 * 1024

_NT = (((1,), (1,)), ((), ()))
_TN = (((0,), (0,)), ((), ()))


def _rel_bucket_np(dist):
    n = np.maximum(dist, 0)
    max_exact = REL_BUCKETS // 2
    nf = np.maximum(n, 1).astype(np.float32)
    large = max_exact + (np.log(nf / np.float32(max_exact)) / np.float32(math.log(REL_MAX_DIST / max_exact))
                         * np.float32(REL_BUCKETS - max_exact)).astype(np.int32)
    large = np.minimum(large, REL_BUCKETS - 1)
    return np.where(n < max_exact, n, large).astype(np.int32)


def _inproj_kernel(x_ref, wn_ref, bn_ref, wt_ref, bt_ref, eg_ref, eu_ref, ed_ref,
                   qk_ref, vt_ref, egb_ref, eub_ref, edb_ref):
    xb = x_ref[0].astype(jnp.bfloat16)
    qk = jnp.dot(xb, wn_ref[...], preferred_element_type=jnp.float32) + bn_ref[...]
    qk_ref[0] = qk.astype(jnp.bfloat16)
    vt = lax.dot_general(wt_ref[...], xb, _NT, preferred_element_type=jnp.float32) + bt_ref[...]
    vt_ref[0] = vt.astype(jnp.bfloat16)
    egb_ref[...] = eg_ref[...].astype(jnp.bfloat16)
    eub_ref[...] = eu_ref[...].astype(jnp.bfloat16)
    edb_ref[...] = ed_ref[...].astype(jnp.bfloat16)


def _inproj(x, wn, bn, wt, bt, w_gate, w_up, w_down, tm=1024):
    B, S, D = x.shape
    nt = S // tm
    steps = B * nt

    def sliced(w):
        E, R, C = w.shape
        assert (E * R) % (steps * BF16_SUBLANES) == 0
        return w.reshape(steps, E * R // steps, C)

    eg, eu, ed = sliced(w_gate), sliced(w_up), sliced(w_down)
    espec = lambda w: pl.BlockSpec((1,) + w.shape[1:], lambda b, i: (b * nt + i, 0, 0))
    qk, vt, egb, eub, edb = pl.pallas_call(
        _inproj_kernel,
        grid=(B, nt),
        in_specs=[
            pl.BlockSpec((1, tm, D), lambda b, i: (b, i, 0)),
            pl.BlockSpec((D, QK_W), lambda b, i: (0, 0)),
            pl.BlockSpec((1, QK_W), lambda b, i: (0, 0)),
            pl.BlockSpec((VT_W, D), lambda b, i: (0, 0)),
            pl.BlockSpec((VT_W, 1), lambda b, i: (0, 0)),
            espec(eg), espec(eu), espec(ed),
        ],
        out_specs=[
            pl.BlockSpec((1, tm, QK_W), lambda b, i: (b, i, 0)),
            pl.BlockSpec((1, VT_W, tm), lambda b, i: (b, 0, i)),
            espec(eg), espec(eu), espec(ed),
        ],
        out_shape=[
            jax.ShapeDtypeStruct((B, S, QK_W), jnp.bfloat16),
            jax.ShapeDtypeStruct((B, VT_W, S), jnp.bfloat16),
            jax.ShapeDtypeStruct(eg.shape, jnp.bfloat16),
            jax.ShapeDtypeStruct(eu.shape, jnp.bfloat16),
            jax.ShapeDtypeStruct(ed.shape, jnp.bfloat16),
        ],
        compiler_params=pltpu.CompilerParams(
            dimension_semantics=("parallel", "parallel"), vmem_limit_bytes=VMEM_LIMIT),
        name="inproj",
    )(x, wn, bn, wt, bt, eg, eu, ed)
    return qk, vt, egb.reshape(w_gate.shape), eub.reshape(w_up.shape), edb.reshape(w_down.shape)


def _swa_kernel(sink_ref, q_ref, k_ref, vt_ref, bias_ref, o_ref, vaug_ref, *s_bufs):
    S = q_ref.shape[1]
    nblk = S // SWA_BLOCK
    half = HEAD_DIM
    vt = vt_ref[0]
    rows = lax.broadcasted_iota(jnp.int32, vt.shape, 0)
    one = jnp.ones_like(vt)
    vaug_ref[0] = jnp.where(rows < half, vt, one)
    vaug_ref[1] = jnp.where(rows < half, one, vt)
    lane = lax.broadcasted_iota(jnp.int32, (SWA_BLOCK, LANES), 1)
    col = lax.broadcasted_iota(jnp.int32, (1, 2 * SWA_BLOCK), 1)
    npair = SWA_Q_HEADS // 2

    def expand(specs):
        return [(qs, ks, nk, boff, pair, (2 * pair) // SWA_GROUP) for qs, ks, nk, boff in specs for pair in range(npair)]

    def score(specs):
        ss = []
        for qs, ks, nk, boff, pair, kv in expand(specs):
            qblk = q_ref[0, pl.ds(qs, SWA_BLOCK), pair * LANES:(pair + 1) * LANES]
            zero = jnp.zeros_like(qblk)
            q2 = jnp.concatenate([jnp.where(lane < half, qblk, zero), jnp.where(lane >= half, qblk, zero)], axis=0)
            kblk = k_ref[0, pl.ds(ks, nk), kv * LANES:(kv + 1) * LANES]
            s = lax.dot_general(kblk, q2, _NT, preferred_element_type=jnp.float32)
            ss.append(s + bias_ref[pair, boff:boff + nk, :])
        return ss

    def finish(specs, ss):
        items = expand(specs)
        stats = []
        for (qs, ks, nk, boff, pair, kv), s in zip(items, ss):
            sink = jnp.where(col < SWA_BLOCK, sink_ref[2 * pair], sink_ref[2 * pair + 1]) * LOG2E
            m = jnp.maximum(_colmax(s), sink)
            stats.append((jnp.exp2(s - m).astype(jnp.bfloat16), jnp.exp2(sink - m)))
        rs = [jnp.dot(vaug_ref[kv, :, pl.ds(ks, nk)], p, preferred_element_type=jnp.float32)
              for (qs, ks, nk, boff, pair, kv), (p, _) in zip(items, stats)]
        outs = []
        for (qs, ks, nk, boff, pair, kv), (_, esink), r in zip(items, stats, rs):
            l = (r[half:half + 1, :] if kv == 0 else r[0:1, :]) + esink
            o = (r[0:half, :] if kv == 0 else r[half:, :]) / l
            outs.append(jnp.concatenate([o[:, :SWA_BLOCK], o[:, SWA_BLOCK:]], axis=0))
        for i, (qs, _, _, _) in enumerate(specs):
            ot = jnp.concatenate(outs[i * npair:(i + 1) * npair], axis=0)
            o_ref[0, :, pl.ds(qs, SWA_BLOCK)] = ot.astype(jnp.bfloat16)

    first = [(0, 0, SWA_BLOCK, SWA_BLOCK), (SWA_BLOCK, 0, 2 * SWA_BLOCK, 0)]
    finish(first, score(first))

    assert nblk % 2 == 0
    n_groups = nblk // 2 - 1
    nbuf = len(s_bufs)

    def group_specs(g):
        specs = []
        for n in (2 * g + 2, 2 * g + 3):
            qs = pl.multiple_of(n * SWA_BLOCK, SWA_BLOCK)
            specs.append((qs, pl.multiple_of(qs - SWA_BLOCK, SWA_BLOCK), 2 * SWA_BLOCK, 0))
        return specs

    def score_to(g, buf):
        for k, s in enumerate(score(group_specs(g))):
            buf[k] = s

    def finish_from(g, buf):
        finish(group_specs(g), [buf[k] for k in range(2 * npair)])

    for g in range(min(2, n_groups)):
        score_to(g, s_bufs[g])

    def body(t, carry):
        for j in range(nbuf):
            score_to(nbuf * t + 2 + j, s_bufs[(2 + j) % nbuf])
            finish_from(nbuf * t + j, s_bufs[j])
        return carry

    n_trips = max(n_groups - 2, 0) // nbuf
    lax.fori_loop(0, n_trips, body, 0)
    for g in range(nbuf * n_trips, n_groups):
        if g + 2 < n_groups:
            score_to(g + 2, s_bufs[(g + 2) % nbuf])
        finish_from(g, s_bufs[g % nbuf])


def _swa(sinks, qk, vt, bias):
    B, S, _ = qk.shape
    return pl.pallas_call(
        _swa_kernel,
        grid_spec=pltpu.PrefetchScalarGridSpec(
            num_scalar_prefetch=1,
            grid=(B,),
            in_specs=[
                pl.BlockSpec((1, S, SWA_Q_W), lambda b, s: (b, 0, QK_SWA_Q // SWA_Q_W)),
                pl.BlockSpec((1, S, 2 * LANES), lambda b, s: (b, 0, QK_SWA_K // (2 * LANES))),
                pl.BlockSpec((1, SWA_KV_W, S), lambda b, s: (b, VT_SWA // SWA_KV_W, 0)),
                pl.BlockSpec((SWA_Q_HEADS // 2, 2 * SWA_BLOCK, 2 * SWA_BLOCK), lambda b, s: (0, 0, 0)),
            ],
            out_specs=pl.BlockSpec((1, SWA_Q_W, S), lambda b, s: (b, 0, 0)),
            scratch_shapes=[pltpu.VMEM((SWA_KV_HEADS, SWA_KV_W, S), jnp.bfloat16)]
            + [pltpu.VMEM((SWA_Q_HEADS, 2 * SWA_BLOCK, 2 * SWA_BLOCK), jnp.float32)] * 4,
        ),
        out_shape=jax.ShapeDtypeStruct((B, SWA_Q_W, S), jnp.bfloat16),
        compiler_params=pltpu.CompilerParams(
            dimension_semantics=("parallel",), vmem_limit_bytes=VMEM_LIMIT),
        name="swa",
    )(sinks, qk, qk, vt, bias)


def _colmax(s):
    while s.shape[0] > 8:
        h = s.shape[0] // 2
        s = jnp.maximum(s[:h], s[h:])
    return jnp.max(s, axis=0, keepdims=True)


def _moba_kernel(fqb_ref, fc_ref, q_ref, k_ref, vt_ref, bias_ref, o_ref,
                 vaug_ref, gate_ref, rank_ref, sel_ref, far_ref, m_scr, acc_scr, sa_scr, sb_scr, sc_scr, sd_scr):
    S = q_ref.shape[1]
    nb = S // MOBA_BLOCK
    half = HEAD_DIM
    BLK = MOBA_BLOCK

    ones_rows = jnp.ones((MOBA_VROWS - half, S), jnp.bfloat16)
    for h in range(2):
        vaug_ref[h, 0:half, :] = vt_ref[0, h * half:(h + 1) * half, :]
        vaug_ref[h, half:, :] = ones_rows

    kf = k_ref[0].astype(jnp.float32).reshape(nb, BLK, LANES)
    kmean = jnp.sum(kf, axis=1) * (1.0 / BLK)
    k_hi = kmean.astype(jnp.bfloat16)
    k_lo = (kmean - k_hi.astype(jnp.float32)).astype(jnp.bfloat16)
    kcat = jnp.concatenate([k_hi, k_lo], axis=0)

    lane_q = lax.broadcasted_iota(jnp.int32, (S, LANES), 1)
    brow = lax.broadcasted_iota(jnp.int32, (nb, S), 0)
    qblk_of = lax.broadcasted_iota(jnp.int32, (nb, S), 1) // BLK
    qall = q_ref[0]
    for h in range(2):
        qh = jnp.where((lane_q < half) if h == 0 else (lane_q >= half), qall, jnp.zeros_like(qall))
        g2 = lax.dot_general(kcat, qh, _NT, preferred_element_type=jnp.float32)
        gate_ref[...] = g2[0:nb] + g2[nb:2 * nb]
        rank_ref[...] = jnp.zeros_like(rank_ref)
        for m in range(nb - 1):
            lo = (m + 1) * BLK
            G = gate_ref[:, lo:]
            gm = gate_ref[m:m + 1, lo:]
            ge = jnp.where(gm >= G, 1.0, 0.0)
            gt = jnp.where(gm > G, 1.0, 0.0)
            brow_m = lax.broadcasted_iota(jnp.int32, (nb, S - lo), 0)
            rank_ref[:, lo:] += jnp.where(brow_m > m, ge, gt)
        top = rank_ref[...] < float(MOBA_TOPK)
        sel_ref[h] = jnp.where((brow < qblk_of) & top, 0.0, NEG_INF)
        far_ref[h] = jnp.where((brow < qblk_of - 1) & top, 0.0, NEG_INF)

    lane = lax.broadcasted_iota(jnp.int32, (BLK, LANES), 1)

    def head_q(qs, h):
        qblk = q_ref[0, pl.ds(qs, BLK), :]
        return jnp.where((lane < half) if h == 0 else (lane >= half), qblk, jnp.zeros_like(qblk))

    m0 = jnp.full((1, BLK), NEG_INF, jnp.float32)
    acc0 = jnp.zeros((MOBA_VROWS, BLK), jnp.float32)

    def item(h, qb, qs, ks, nk, bias, sel_a, sel_b, m, acc):
        return dict(h=h, qb=qb, qs=qs, ks=ks, nk=nk, bias=bias, sel_a=sel_a, sel_b=sel_b, m=m, acc=acc)

    def near_item(qb, h):
        qs = pl.multiple_of(qb * BLK, BLK)
        ps = pl.multiple_of(qs - BLK, BLK)
        sel_prev = sel_ref[h, pl.ds(qb - 1, 1), pl.ds(qs, BLK)]
        return item(h, qb, qs, ps, 2 * BLK, bias_ref[h], sel_prev, None, m0, acc0)

    def far_item(i, h):
        qb = fqb_ref[i]
        c = fc_ref[i]
        qs = pl.multiple_of(qb * BLK, BLK)
        ks = pl.multiple_of(c * (2 * BLK), 2 * BLK)
        sel_a = far_ref[h, pl.ds(2 * c, 1), pl.ds(qs, BLK)]
        sel_b = far_ref[h, pl.ds(2 * c + 1, 1), pl.ds(qs, BLK)]
        return item(h, qb, qs, ks, 2 * BLK, None, sel_a, sel_b, m_scr[h, qb], acc_scr[h, qb])

    def scores(it):
        kslab = k_ref[0, pl.ds(it["ks"], it["nk"]), :]
        s = lax.dot_general(kslab, head_q(it["qs"], it["h"]), _NT,
                            preferred_element_type=jnp.float32)
        return s if it["bias"] is None else s + it["bias"]

    def probs(it, s):
        m, sel_a, sel_b = it["m"], it["sel_a"], it["sel_b"]
        if it["nk"] == BLK:
            m_new = jnp.maximum(m, _colmax(s))
            p = jnp.exp2(s - m_new)
        else:
            cm_a = _colmax(s[:BLK]) + sel_a
            cm_b = _colmax(s[BLK:])
            if sel_b is not None:
                cm_b = cm_b + sel_b
            m_new = jnp.maximum(m, jnp.maximum(cm_a, cm_b))
            p_a = jnp.exp2(s[:BLK] - (m_new - sel_a))
            p_b = jnp.exp2(s[BLK:] - (m_new if sel_b is None else m_new - sel_b))
            p = jnp.concatenate([p_a, p_b], axis=0)
        return m_new, jnp.exp2(m - m_new), p.astype(jnp.bfloat16)

    def run(items):
        ss = [scores(it) for it in items]
        ps = [probs(it, s) for it, s in zip(items, ss)]
        pvs = [jnp.dot(vaug_ref[it["h"], :, pl.ds(it["ks"], it["nk"])], p, preferred_element_type=jnp.float32)
               for it, (_, _, p) in zip(items, ps)]
        for it, (m_new, alpha, _), pv in zip(items, ps, pvs):
            m_scr[it["h"], it["qb"]] = m_new
            acc_scr[it["h"], it["qb"]] = it["acc"] * alpha + pv

    run([item(h, 0, 0, 0, BLK, bias_ref[h, BLK:, :], None, None, m0, acc0) for h in range(2)]
        + [near_item(nb - 1, h) for h in range(2)])

    bufs = (sa_scr, sb_scr, sc_scr, sd_scr)
    nbuf = len(bufs)

    def finish(items, s_buf):
        ps = [probs(it, s_buf[k]) for k, it in enumerate(items)]
        pvs = [jnp.dot(vaug_ref[it["h"], :, pl.ds(it["ks"], it["nk"])], p, preferred_element_type=jnp.float32)
               for it, (_, _, p) in zip(items, ps)]
        for it, (m_new, alpha, _), pv in zip(items, ps, pvs):
            m_scr[it["h"], it["qb"]] = m_new
            acc_scr[it["h"], it["qb"]] = it["acc"] * alpha + pv

    def pipelined(n_groups, score, items_of):
        for g in range(min(MOBA_LOOKAHEAD, n_groups)):
            score(g, bufs[g])

        def body(t, carry):
            for j in range(MOBA_UNROLL):
                score(MOBA_UNROLL * t + MOBA_LOOKAHEAD + j, bufs[(MOBA_LOOKAHEAD + j) % nbuf])
                finish(items_of(MOBA_UNROLL * t + j), bufs[j % nbuf])
            return carry

        n_trips = max(n_groups - MOBA_LOOKAHEAD, 0) // MOBA_UNROLL
        lax.fori_loop(0, n_trips, body, 0)
        for g in range(MOBA_UNROLL * n_trips, n_groups):
            if g + MOBA_LOOKAHEAD < n_groups:
                score(g + MOBA_LOOKAHEAD, bufs[(g + MOBA_LOOKAHEAD) % nbuf])
            finish(items_of(g), bufs[g % nbuf])

    near_pairs = nb // 2 - 1

    def near_group(g):
        return [(qb, h) for qb in (g + 1, g + 1 + near_pairs) for h in range(2)]

    def near_score(g, s_buf):
        for k, (qb, h) in enumerate(near_group(g)):
            qs = pl.multiple_of(qb * BLK, BLK)
            ps = pl.multiple_of(qs - BLK, BLK)
            s_buf[k] = lax.dot_general(k_ref[0, pl.ds(ps, 2 * BLK), :], head_q(qs, h), _NT,
                                       preferred_element_type=jnp.float32) + bias_ref[h]

    pipelined(near_pairs, near_score, lambda g: [near_item(qb, h) for qb, h in near_group(g)])

    n_far_groups = fqb_ref.shape[0] // 2

    def far_group(g):
        return [(j, h) for j in (g, g + n_far_groups) for h in range(2)]

    def far_score(g, s_buf):
        for k, (j, h) in enumerate(far_group(g)):
            qs = pl.multiple_of(fqb_ref[j] * BLK, BLK)
            ks = pl.multiple_of(fc_ref[j] * (2 * BLK), 2 * BLK)
            s_buf[k] = lax.dot_general(k_ref[0, pl.ds(ks, 2 * BLK), :], head_q(qs, h), _NT,
                                       preferred_element_type=jnp.float32)

    pipelined(n_far_groups, far_score, lambda g: [far_item(j, h) for j, h in far_group(g)])

    def out_body(t, carry):
        for qb in (2 * t, 2 * t + 1):
            qs = pl.multiple_of(qb * BLK, BLK)
            a0 = acc_scr[0, qb]
            a1 = acc_scr[1, qb]
            ot = jnp.concatenate([a0[0:half] / a0[half:half + 1, :], a1[0:half] / a1[half:half + 1, :]],
                                 axis=0)
            o_ref[0, :, pl.ds(qs, BLK)] = ot.astype(jnp.bfloat16)
        return carry

    lax.fori_loop(0, nb // 2, out_body, 0)


def _moba_far_items(nb):
    items = [(qb, c) for qb in range(2, nb) for c in range(qb // 2)]
    n = len(items)
    assert n % 2 == 0 and all(items[i][0] != items[i + n // 2][0] for i in range(n // 2))
    return np.array([it[0] for it in items], np.int32), np.array([it[1] for it in items], np.int32)


def _moba(qk, vt, bias):
    B, S, _ = qk.shape
    nb = S // MOBA_BLOCK
    npair = MOBA_W // LANES
    far_qb, far_c = _moba_far_items(nb)
    return pl.pallas_call(
        _moba_kernel,
        grid_spec=pltpu.PrefetchScalarGridSpec(
            num_scalar_prefetch=2,
            grid=(B, npair),
            in_specs=[
                pl.BlockSpec((1, S, LANES), lambda b, p, fq, fc: (b, 0, QK_MOBA_Q // LANES + p)),
                pl.BlockSpec((1, S, LANES), lambda b, p, fq, fc: (b, 0, QK_MOBA_K // LANES + p)),
                pl.BlockSpec((1, LANES, S), lambda b, p, fq, fc: (b, VT_MOBA // LANES + p, 0)),
                pl.BlockSpec((2, 2 * MOBA_BLOCK, MOBA_BLOCK), lambda b, p, fq, fc: (p, 0, 0)),
            ],
            out_specs=pl.BlockSpec((1, LANES, S), lambda b, p, fq, fc: (b, p, 0)),
            scratch_shapes=[
                pltpu.VMEM((2, MOBA_VROWS, S), jnp.bfloat16),
                pltpu.VMEM((nb, S), jnp.float32),
                pltpu.VMEM((nb, S), jnp.float32),
                pltpu.VMEM((2, nb, S), jnp.float32),
                pltpu.VMEM((2, nb, S), jnp.float32),
                pltpu.VMEM((2, nb, 1, MOBA_BLOCK), jnp.float32),
                pltpu.VMEM((2, nb, MOBA_VROWS, MOBA_BLOCK), jnp.float32),
            ] + [pltpu.VMEM((4, 2 * MOBA_BLOCK, MOBA_BLOCK), jnp.float32)] * 4,
        ),
        out_shape=jax.ShapeDtypeStruct((B, MOBA_W, S), jnp.bfloat16),
        compiler_params=pltpu.CompilerParams(
            dimension_semantics=("parallel", "parallel"), vmem_limit_bytes=VMEM_LIMIT),
        name="moba",
    )(jnp.asarray(far_qb), jnp.asarray(far_c), qk, qk, vt, bias)


def _layer_norm(h, gain, bias):
    mu = jnp.mean(h, axis=-1, keepdims=True)
    c = h - mu
    var = jnp.mean(c * c, axis=-1, keepdims=True)
    return c * lax.rsqrt(var + LN_EPS) * gain + bias


def _pack_bf16_pair(a, b):
    ia = lax.bitcast_convert_type(a.astype(jnp.bfloat16).astype(jnp.float32), jnp.int32)
    ib = lax.bitcast_convert_type(b.astype(jnp.bfloat16).astype(jnp.float32), jnp.int32)
    return lax.shift_right_logical(ia, 16) | ib


def _unpack_bf16_pair(w):
    lo = lax.bitcast_convert_type(lax.shift_left(w, 16), jnp.float32)
    hi = lax.bitcast_convert_type(w & jnp.int32(-65536), jnp.float32)
    return lo.astype(jnp.bfloat16), hi.astype(jnp.bfloat16)


def _merge_kernel(x_ref, ya_ref, yb_ref, wg_ref, bg_ref, wa_ref, wb_ref, wo_ref, g1_ref, b1_ref,
                  wr_ref, br_ref, tri_ref, x1_ref, x1p_ref, r_ref, cnt_ref):
    @pl.when(pl.program_id(0) == 0)
    def _():
        cnt_ref[...] = jnp.zeros_like(cnt_ref)

    ts = tri_ref.shape[0]
    subs = [pl.ds(r0, ts) for r0 in range(0, x_ref.shape[0], ts)]
    pre = [_merge_matmuls(x_ref[rows, :], ya_ref[0, :, rows], yb_ref[0, :, rows], wg_ref, bg_ref, wa_ref, wb_ref, wo_ref)
           for rows in subs]
    for rows, h in zip(subs, pre):
        _merge_route(h, rows, g1_ref, b1_ref, wr_ref, br_ref, tri_ref, x1_ref, x1p_ref, r_ref, cnt_ref)


def _merge_matmuls(x, ya_t, yb_t, wg_ref, bg_ref, wa_ref, wb_ref, wo_ref):
    xb = x.astype(jnp.bfloat16)
    z = jnp.dot(xb, wg_ref[...], preferred_element_type=jnp.float32) + bg_ref[...]
    gates = 1.0 / (1.0 + jnp.exp(-z))
    pa = lax.dot_general(ya_t, wa_ref[...], _TN, preferred_element_type=jnp.float32)
    pb = lax.dot_general(yb_t, wb_ref[...], _TN, preferred_element_type=jnp.float32)
    merged = gates[:, :D_MODEL] * pa + gates[:, D_MODEL:] * pb
    mixed = jnp.dot(merged.astype(jnp.bfloat16), wo_ref[...], preferred_element_type=jnp.float32)
    return DEEPNORM_ALPHA * x + mixed


def _merge_route(h, rows, g1_ref, b1_ref, wr_ref, br_ref, tri_ref, x1_ref, x1p_ref, r_ref, cnt_ref):
    x1 = _layer_norm(h, g1_ref[...], b1_ref[...])
    x1_ref[rows, :] = x1
    x1_hi = x1.astype(jnp.bfloat16)
    x1p_ref[rows, :] = _pack_bf16_pair(x1[:, :D_MODEL // 2], x1[:, D_MODEL // 2:])
    x1_lo = (x1 - x1_hi.astype(jnp.float32)).astype(jnp.bfloat16)

    R = ROUTER_ROWS
    l1 = lax.dot_general(wr_ref[...], x1_hi, _NT, preferred_element_type=jnp.float32)
    l2 = lax.dot_general(wr_ref[0:R, :], x1_lo, _NT, preferred_element_type=jnp.float32)
    L = l1[0:R] + l1[R:2 * R] + l2 + br_ref[...]
    tm = x1.shape[0]
    row = lax.broadcasted_iota(jnp.int32, (8, tm), 0)
    big = jnp.float32(-3e38)
    gl = jnp.where(row < N_GROUPS, L[0:8], big)
    gmax = jnp.max(gl, axis=0, keepdims=True)
    g_idx = jnp.min(jnp.where(gl == gmax, row, 8), axis=0, keepdims=True)
    gsum = jnp.sum(jnp.where(row < N_GROUPS, jnp.exp(gl - gmax), 0.0), axis=0, keepdims=True)
    g_prob = 1.0 / gsum
    E = L[8 + 8 * (N_GROUPS - 1):8 + 8 * N_GROUPS]
    for g in range(N_GROUPS - 2, -1, -1):
        E = jnp.where(g_idx == g, L[8 + 8 * g:16 + 8 * g], E)
    t0 = jnp.max(E, axis=0, keepdims=True)
    loc0 = jnp.min(jnp.where(E == t0, row, 8), axis=0, keepdims=True)
    E2 = jnp.where(row == loc0, big, E)
    t1 = jnp.max(E2, axis=0, keepdims=True)
    loc1 = jnp.min(jnp.where(E2 == t1, row, 8), axis=0, keepdims=True)
    ex = jnp.exp(t1 - t0)
    w0 = g_prob / (1.0 + ex)
    w1 = g_prob * ex / (1.0 + ex)
    e0i = g_idx * EXPERTS_PER_GROUP + loc0
    e1i = g_idx * EXPERTS_PER_GROUP + loc1

    erow = lax.broadcasted_iota(jnp.int32, (N_EXPERTS, tm), 0)
    oh0 = jnp.where(erow == e0i, 1.0, 0.0)
    oh1 = jnp.where(erow == e1i, 1.0, 0.0)
    both = oh0 + oh1
    prefix = jnp.dot(both.astype(jnp.bfloat16), tri_ref[...], preferred_element_type=jnp.float32)
    prefix = prefix + cnt_ref[:, 0:1]
    rank0 = jnp.sum(oh0 * prefix, axis=0, keepdims=True)
    rank1 = jnp.sum(oh1 * prefix, axis=0, keepdims=True)
    cnt_ref[...] = cnt_ref[...] + jnp.sum(both, axis=1, keepdims=True)

    vals = (e0i.astype(jnp.float32), e1i.astype(jnp.float32), w0, w1, rank0, rank1)
    out = jnp.zeros((8, tm), jnp.float32)
    for k, v in enumerate(vals):
        out = jnp.where(row == k, v, out)
    r_ref[:, rows] = out


def _merge(x2, ya, yb, wg, bg, wa, wb, wo, g1, b1, wr, br, part, n_parts, tm=512, ts=256):
    D = x2.shape[1]
    T = x2.shape[0] // n_parts
    S = ya.shape[2]
    per_seq = S // tm
    assert per_seq * tm == S
    off = part * (T // tm)
    const = lambda i: (0, 0)
    seq_tile = lambda i: ((i + off) // per_seq, 0, (i + off) % per_seq)
    tri = jnp.triu(jnp.ones((ts, ts), jnp.bfloat16), k=1)
    return pl.pallas_call(
        _merge_kernel,
        grid=(T // tm,),
        in_specs=[
            pl.BlockSpec((tm, D), lambda i: (i + off, 0)),
            pl.BlockSpec((1, SWA_Q_W, tm), seq_tile),
            pl.BlockSpec((1, MOBA_W, tm), seq_tile),
            pl.BlockSpec((D, 2 * D), const),
            pl.BlockSpec((1, 2 * D), const),
            pl.BlockSpec((SWA_Q_W, D), const),
            pl.BlockSpec((MOBA_W, D), const),
            pl.BlockSpec((D, D), const),
            pl.BlockSpec((1, D), const),
            pl.BlockSpec((1, D), const),
            pl.BlockSpec((2 * ROUTER_ROWS, D), const),
            pl.BlockSpec((ROUTER_ROWS, 1), const),
            pl.BlockSpec((ts, ts), const),
        ],
        out_specs=[
            pl.BlockSpec((tm, D), lambda i: (i, 0)),
            pl.BlockSpec((tm, D // 2), lambda i: (i, 0)),
            pl.BlockSpec((8, tm), lambda i: (0, i)),
            pl.BlockSpec((N_EXPERTS, LANES), const),
        ],
        out_shape=[
            jax.ShapeDtypeStruct((T, D), jnp.float32),
            jax.ShapeDtypeStruct((T, D // 2), jnp.int32),
            jax.ShapeDtypeStruct((8, T), jnp.float32),
            jax.ShapeDtypeStruct((N_EXPERTS, LANES), jnp.float32),
        ],
        compiler_params=pltpu.CompilerParams(
            dimension_semantics=("arbitrary",), vmem_limit_bytes=VMEM_LIMIT),
        name="merge",
    )(x2, ya, yb, wg, bg, wa, wb, wo, g1, b1, wr, br, tri)


def _dest_kernel(r_ref, ps_ref, d_ref):
    tt = r_ref.shape[1]
    erow = lax.broadcasted_iota(jnp.int32, (N_EXPERTS, tt), 0)
    row = lax.broadcasted_iota(jnp.int32, (8, tt), 0)
    ps = ps_ref[...]
    out = jnp.zeros((8, tt), jnp.float32)
    for k in range(EXPERT_TOPK):
        e = r_ref[k:k + 1, :].astype(jnp.int32)
        start = jnp.sum(jnp.where(erow == e, ps, 0.0), axis=0, keepdims=True)
        out = jnp.where(row == k, start + r_ref[4 + k:5 + k, :], out)
    d_ref[...] = out.astype(jnp.int32)


def _dest(rinfo, padded_start, tt=8192):
    T = rinfo.shape[1]
    tt = min(tt, T)
    return pl.pallas_call(
        _dest_kernel,
        grid=(T // tt,),
        in_specs=[pl.BlockSpec((8, tt), lambda i: (0, i)), pl.BlockSpec((N_EXPERTS, 1), lambda i: (0, 0))],
        out_specs=pl.BlockSpec((8, tt), lambda i: (0, i)),
        out_shape=jax.ShapeDtypeStruct((8, T), jnp.int32),
        compiler_params=pltpu.CompilerParams(dimension_semantics=("parallel",)),
        name="dest",
    )(rinfo, padded_start.astype(jnp.float32)[:, None])


def _sc_workers():
    info = plsc.get_sparse_core_info()
    return info.num_cores, info.num_subcores


def _sc_scatter_rows(src, dest0, dest1, cap):
    T, W = src.shape
    nc, ns = _sc_workers()
    per_w = T // (nc * ns)
    assert per_w * nc * ns == T and per_w % SC_CHUNK == 0
    mesh = plsc.VectorSubcoreMesh(core_axis_name="c", subcore_axis_name="s")

    @functools.partial(
        pl.kernel, mesh=mesh,
        out_type=jax.ShapeDtypeStruct((cap, W), src.dtype),
        scratch_types=[pltpu.VMEM((SC_CHUNK,), jnp.int32), pltpu.VMEM((SC_CHUNK, W), src.dtype)],
    )
    def scatter(src_hbm, d0_hbm, d1_hbm, out_hbm, idx_v, rows_v):
        wid = lax.axis_index("s") * nc + lax.axis_index("c")
        base = wid * per_w

        @pl.loop(0, per_w // SC_CHUNK)
        def _(c):
            off = pl.multiple_of(base + c * SC_CHUNK, SC_CHUNK)
            pltpu.sync_copy(src_hbm.at[pl.ds(off, SC_CHUNK)], rows_v)
            for d_hbm in (d0_hbm, d1_hbm):
                pltpu.sync_copy(d_hbm.at[pl.ds(off, SC_CHUNK)], idx_v)
                pltpu.sync_copy(rows_v, out_hbm.at[idx_v])

    return scatter(src, dest0, dest1)


def _sc_gather_rows(table, idx):
    N = idx.shape[0]
    W = table.shape[1]
    nc, ns = _sc_workers()
    per_w = N // (nc * ns)
    assert per_w * nc * ns == N and per_w % SC_CHUNK == 0
    mesh = plsc.VectorSubcoreMesh(core_axis_name="c", subcore_axis_name="s")

    @functools.partial(
        pl.kernel, mesh=mesh,
        out_type=jax.ShapeDtypeStruct((N, W), table.dtype),
        scratch_types=[pltpu.VMEM((SC_CHUNK,), jnp.int32), pltpu.VMEM((SC_CHUNK, W), table.dtype)],
    )
    def gather(table_hbm, idx_hbm, out_hbm, idx_v, rows_v):
        wid = lax.axis_index("s") * nc + lax.axis_index("c")
        base = wid * per_w

        @pl.loop(0, per_w // SC_CHUNK)
        def _(c):
            off = pl.multiple_of(base + c * SC_CHUNK, SC_CHUNK)
            pltpu.sync_copy(idx_hbm.at[pl.ds(off, SC_CHUNK)], idx_v)
            pltpu.sync_copy(table_hbm.at[idx_v], rows_v)
            pltpu.sync_copy(rows_v, out_hbm.at[pl.ds(off, SC_CHUNK)])

    return gather(table, idx)


def _expert_kernel(be_ref, nv_ref, first_ref, slot_ref, nxt_ref, x_ref, wg_hbm, wu_hbm, wd_hbm, y_ref,
                   wg_buf, wu_buf, wd_buf, sem):
    i = pl.program_id(0)
    nv = nv_ref[i]
    slot = slot_ref[i]

    def weight_copies(e, s):
        return [pltpu.make_async_copy(hbm.at[e], buf.at[s], sem.at[s, k])
                for k, (hbm, buf) in enumerate(((wg_hbm, wg_buf), (wu_hbm, wu_buf), (wd_hbm, wd_buf)))]

    @pl.when(i == 0)
    def _():
        for c in weight_copies(be_ref[0], 0):
            c.start()

    @pl.when(first_ref[i] == 1)
    def _():
        @pl.when(nxt_ref[i] >= 0)
        def _():
            for c in weight_copies(nxt_ref[i], 1 - slot):
                c.start()

        for c in weight_copies(be_ref[i], slot):
            c.wait()

    @pl.when(nv > 0)
    def _():
        lo, hi = _unpack_bf16_pair(x_ref[...])
        xb = jnp.concatenate([lo, hi], axis=1)
        rows = lax.broadcasted_iota(jnp.int32, xb.shape, 0)
        xb = jnp.where(rows < nv, xb, jnp.zeros_like(xb))
        g = jnp.dot(xb, wg_buf[slot], preferred_element_type=jnp.float32)
        u = jnp.dot(xb, wu_buf[slot], preferred_element_type=jnp.float32)
        act = (g / (1.0 + jnp.exp(-g))) * u
        y = jnp.dot(act.astype(jnp.bfloat16), wd_buf[slot], preferred_element_type=jnp.float32)
        y_ref[...] = _pack_bf16_pair(y[:, :D_MODEL // 2], y[:, D_MODEL // 2:])

    @pl.when(nv <= 0)
    def _():
        y_ref[...] = jnp.zeros_like(y_ref)


def _experts(plan, xs, wg, wu, wd):
    cap, DW = xs.shape
    D = 2 * DW
    n_blocks = cap // MOE_TM
    hbm = pl.BlockSpec(memory_space=pl.ANY)
    return pl.pallas_call(
        _expert_kernel,
        grid_spec=pltpu.PrefetchScalarGridSpec(
            num_scalar_prefetch=5,
            grid=(n_blocks,),
            in_specs=[pl.BlockSpec((MOE_TM, DW), lambda i, *_: (i, 0)), hbm, hbm, hbm],
            out_specs=pl.BlockSpec((MOE_TM, DW), lambda i, *_: (i, 0)),
            scratch_shapes=[
                pltpu.VMEM((2, D, D_EXPERT), jnp.bfloat16),
                pltpu.VMEM((2, D, D_EXPERT), jnp.bfloat16),
                pltpu.VMEM((2, D_EXPERT, D), jnp.bfloat16),
                pltpu.SemaphoreType.DMA((2, 3)),
            ],
        ),
        out_shape=jax.ShapeDtypeStruct((cap, DW), jnp.int32),
        compiler_params=pltpu.CompilerParams(
            dimension_semantics=("arbitrary",), vmem_limit_bytes=VMEM_LIMIT),
        name="experts",
    )(*plan, xs, wg, wu, wd)


def _final_kernel(x1_ref, y0_ref, y1_ref, w_ref, g2_ref, b2_ref, *rest):
    o_ref = rest[-1]
    w = w_ref[...].T
    halves = []
    for part in range(2):
        y0 = _unpack_bf16_pair(y0_ref[...])[part].astype(jnp.float32)
        y1 = _unpack_bf16_pair(y1_ref[...])[part].astype(jnp.float32)
        halves.append(y0 * w[:, 2:3] + y1 * w[:, 3:4])
    moe = jnp.concatenate(halves, axis=1)
    o_ref[...] = _layer_norm(DEEPNORM_ALPHA * x1_ref[...] + moe, g2_ref[...], b2_ref[...])


def _final(x1, ypair, rinfo, g2, b2, out_prev, part, n_parts, tm=1024):
    T, D = x1.shape
    const = lambda i: (0, 0)
    nt = T // tm
    off = part * nt
    in_specs = [
        pl.BlockSpec((tm, D), lambda i: (i, 0)),
        pl.BlockSpec((tm, D // 2), lambda i: (i, 0)),
        pl.BlockSpec((tm, D // 2), lambda i: (i + nt, 0)),
        pl.BlockSpec((8, tm), lambda i: (0, i)),
        pl.BlockSpec((1, D), const),
        pl.BlockSpec((1, D), const),
    ]
    args = [x1, ypair, ypair, rinfo, g2, b2]
    aliases = {}
    if out_prev is not None:
        in_specs.append(pl.BlockSpec(memory_space=pl.ANY))
        args.append(out_prev)
        aliases = {len(args) - 1: 0}
    return pl.pallas_call(
        _final_kernel,
        grid=(nt,),
        in_specs=in_specs,
        out_specs=pl.BlockSpec((tm, D), lambda i: (i + off, 0)),
        out_shape=jax.ShapeDtypeStruct((n_parts * T, D), jnp.float32),
        input_output_aliases=aliases,
        compiler_params=pltpu.CompilerParams(
            dimension_semantics=("parallel",), vmem_limit_bytes=VMEM_LIMIT),
        name="final",
    )(*args)


def _bias_kernel(tab_ref, sub_ref, idx_ref, o_ref):
    h = pl.program_id(0)
    idx = idx_ref[...]
    acc = jnp.full(idx.shape, NEG_INF, jnp.float32)
    for b in range(REL_BUCKETS):
        acc = jnp.where(idx == b, (tab_ref[h, b] - sub_ref[h]) * LOG2E, acc)
    o_ref[0] = acc


def _bias_expand(table_hb, sub_h, idx):
    H = table_hb.shape[0]
    K, Q = idx.shape
    return pl.pallas_call(
        _bias_kernel,
        grid_spec=pltpu.PrefetchScalarGridSpec(
            num_scalar_prefetch=2,
            grid=(H,),
            in_specs=[pl.BlockSpec((K, Q), lambda h, t, s: (0, 0))],
            out_specs=pl.BlockSpec((1, K, Q), lambda h, t, s: (h, 0, 0)),
        ),
        out_shape=jax.ShapeDtypeStruct((H, K, Q), jnp.float32),
        compiler_params=pltpu.CompilerParams(dimension_semantics=("parallel",)),
        name="bias",
    )(table_hb, sub_h, jnp.asarray(idx))


def _attention_bias_tables(rel_bias_table):
    rel = rel_bias_table.astype(jnp.float32)
    tab_a = rel[:, :SWA_Q_HEADS].T
    tab_b = rel[:, SWA_Q_HEADS:].T
    kj = np.arange(2 * SWA_BLOCK)[:, None]
    qi = np.arange(SWA_BLOCK)[None, :]
    dist = SWA_BLOCK + qi - kj
    idx_a = np.where((dist >= 0) & (dist < SWA_WINDOW), _rel_bucket_np(dist), -1).astype(np.int32)
    bias_a = _bias_expand(tab_a, jnp.zeros((SWA_Q_HEADS,), jnp.float32), idx_a)
    bias_a = bias_a.reshape(SWA_Q_HEADS // 2, 2, 2 * SWA_BLOCK, SWA_BLOCK).transpose(0, 2, 1, 3)
    bias_a = bias_a.reshape(SWA_Q_HEADS // 2, 2 * SWA_BLOCK, 2 * SWA_BLOCK)
    j = np.arange(MOBA_BLOCK)[:, None]
    i = np.arange(MOBA_BLOCK)[None, :]
    d_own = i - j
    idx_own = np.where(d_own >= 0, _rel_bucket_np(d_own), -1)
    idx_prev = _rel_bucket_np(MOBA_BLOCK + i - j)
    idx_b = np.concatenate([idx_prev, idx_own], axis=0).astype(np.int32)
    bias_b = _bias_expand(tab_b, tab_b[:, REL_BUCKETS - 1], idx_b)
    return bias_a, bias_b


def _block_plan(sizes, n_tok):
    n_assign = n_tok * EXPERT_TOPK
    padded = ((sizes + MOE_TM - 1) // MOE_TM) * MOE_TM
    eid = jnp.arange(N_EXPERTS, dtype=jnp.int32)
    padded_end = jnp.sum(jnp.where(eid[None, :] <= eid[:, None], padded[None, :], 0), axis=1)
    padded_start = padded_end - padded
    cap = -(-n_assign // MOE_TM) * MOE_TM + N_EXPERTS * MOE_TM
    blk_start = jnp.arange(cap // MOE_TM, dtype=jnp.int32) * MOE_TM
    used = blk_start < padded_end[-1]
    nonempty = sizes > 0
    blk_expert = jnp.where(used, jnp.sum(padded_end[None, :] <= blk_start[:, None], axis=1),
                           jnp.max(jnp.where(nonempty, eid, 0))).astype(jnp.int32)
    mine = blk_expert[:, None] == eid[None, :]
    size_b = jnp.sum(jnp.where(mine, sizes[None, :], 0), axis=1)
    start_b = jnp.sum(jnp.where(mine, padded_start[None, :], 0), axis=1)
    blk_valid = jnp.where(used, jnp.clip(size_b - (blk_start - start_b), 0, MOE_TM), 0).astype(jnp.int32)
    blk_first = (used & (blk_start == start_b)).astype(jnp.int32)
    earlier = nonempty[None, :] & (eid[None, :] < blk_expert[:, None])
    blk_slot = (jnp.sum(earlier, axis=1) % 2).astype(jnp.int32)
    later = jnp.where(nonempty[None, :] & (eid[None, :] > eid[:, None]), eid[None, :], N_EXPERTS)
    next_of = jnp.min(later, axis=1)
    blk_next = jnp.sum(jnp.where(mine, jnp.where(next_of < N_EXPERTS, next_of, -1)[None, :], 0), axis=1).astype(jnp.int32)
    return padded_start, (blk_expert, blk_valid, blk_first, blk_slot, blk_next), cap


def kernel(x, w_in, b_in, attn_sinks, rel_bias_table, w_branch_swa, w_branch_moba, w_out, ln1_gain, ln1_bias,
           w_group_router, b_group_router, w_expert_router, b_expert_router, w_expert_gate, w_expert_up,
           w_expert_down, ln2_gain, ln2_bias):
    assert w_in.shape[0] == DEPTH == 1
    B, S, D = x.shape
    T = B * S
    bf16 = jnp.bfloat16
    f32 = jnp.float32
    w = w_in[0]
    b = b_in[0]

    def cols(off, width):
        return w[:, off:off + width], b[off:off + width]

    wq_a, bq_a = cols(OFF_SWA_Q, SWA_Q_W)
    wk_a, bk_a = cols(OFF_SWA_K, SWA_KV_W)
    wv_a, bv_a = cols(OFF_SWA_V, SWA_KV_W)
    wq_b, bq_b = cols(OFF_MOBA_Q, MOBA_W)
    wk_b, bk_b = cols(OFF_MOBA_K, MOBA_W)
    wv_b, bv_b = cols(OFF_MOBA_V, MOBA_W)

    def dup_kv(t):
        parts = [t[..., i * HEAD_DIM:(i + 1) * HEAD_DIM] for i in range(SWA_KV_HEADS)]
        return jnp.concatenate([p for p in parts for _ in range(2)], axis=-1)

    qs = ATTN_SCALE * LOG2E
    wn = jnp.concatenate([wq_a * qs, dup_kv(wk_a), wq_b * qs, wk_b], axis=1).astype(bf16)
    bn = jnp.concatenate([bq_a * qs, dup_kv(bk_a), bq_b * qs, bk_b])[None, :].astype(f32)
    wt = jnp.concatenate([wv_a, wv_b], axis=1).T.astype(bf16)
    bt = jnp.concatenate([bv_a, bv_b])[:, None].astype(f32)

    qk, vt, eg_b, eu_b, ed_b = _inproj(x, wn, bn, wt, bt, w_expert_gate[0], w_expert_up[0], w_expert_down[0])

    bias_a, bias_b = _attention_bias_tables(rel_bias_table)
    y_a = _swa(attn_sinks[0].astype(f32), qk, vt, bias_a)
    y_b = _moba(qk, vt, bias_b)

    wg, bg = cols(OFF_GATE, 2 * D_MODEL)
    wr = jnp.zeros((ROUTER_ROWS, D), f32)
    wr = wr.at[0:N_GROUPS].set(w_group_router[0].T).at[8:8 + N_EXPERTS].set(w_expert_router[0].T)
    wr_hi = wr.astype(bf16)
    wr_lo = (wr - wr_hi.astype(f32)).astype(bf16)
    br = jnp.zeros((ROUTER_ROWS,), f32)
    br = br.at[0:N_GROUPS].set(b_group_router[0]).at[8:8 + N_EXPERTS].set(b_expert_router[0])[:, None]
    merge_args = (
        x.reshape(T, D), y_a, y_b,
        wg.astype(bf16), bg[None, :].astype(f32), w_branch_swa[0].astype(bf16), w_branch_moba[0].astype(bf16),
        w_out[0].astype(bf16), ln1_gain[0][None, :].astype(f32), ln1_bias[0][None, :].astype(f32),
        jnp.concatenate([wr_hi, wr_lo], axis=0), br)
    g2 = ln2_gain[0][None, :].astype(f32)
    b2 = ln2_bias[0][None, :].astype(f32)

    Tp = T // MOE_PARTS
    out = None
    for part in range(MOE_PARTS):
        x1, x1p, rinfo, counts = _merge(*merge_args, part, MOE_PARTS)
        sizes = counts[:, 0].astype(jnp.int32)
        padded_start, plan, cap = _block_plan(sizes, Tp)
        dest = _dest(rinfo, padded_start)
        xs = _sc_scatter_rows(x1p, dest[0], dest[1], cap)
        y_buf = _experts(plan, xs, eg_b, eu_b, ed_b)
        ypair = _sc_gather_rows(y_buf, dest[0:EXPERT_TOPK].reshape(-1))
        out = _final(x1, ypair, rinfo, g2, b2, out, part, MOE_PARTS)
    return out.reshape(B, S, D)
```

```python
import functools
import math

import numpy as np
import jax
import jax.numpy as jnp
from jax import lax
from jax.experimental import pallas as pl
from jax.experimental.pallas import tpu as pltpu
from jax.experimental.pallas import tpu_sc as plsc

D_MODEL = 1024
HEAD_DIM = 64
SWA_Q_HEADS = 8
SWA_KV_HEADS = 2
SWA_GROUP = SWA_Q_HEADS // SWA_KV_HEADS
SWA_WINDOW = 128
SWA_BLOCK = 128
MOBA_HEADS = 8
MOBA_BLOCK = 256
MOBA_TOPK = 3
MOBA_LOOKAHEAD = 2
MOBA_UNROLL = 8
BF16_SUBLANES = 16
MOBA_VROWS = HEAD_DIM + BF16_SUBLANES
REL_BUCKETS = 32
REL_MAX_DIST = 128
N_GROUPS = 4
EXPERTS_PER_GROUP = 8
N_EXPERTS = N_GROUPS * EXPERTS_PER_GROUP
EXPERT_TOPK = 2
D_EXPERT = 512
LN_EPS = 1e-5
DEPTH = 1
DEEPNORM_ALPHA = (2.0 * DEPTH) ** 0.25
NEG_INF = -1e30
ATTN_SCALE = HEAD_DIM ** -0.5
LOG2E = math.log2(math.e)

SWA_Q_W = SWA_Q_HEADS * HEAD_DIM
SWA_KV_W = SWA_KV_HEADS * HEAD_DIM
MOBA_W = MOBA_HEADS * HEAD_DIM
OFF_SWA_Q = 0
OFF_SWA_K = OFF_SWA_Q + SWA_Q_W
OFF_SWA_V = OFF_SWA_K + SWA_KV_W
OFF_MOBA_Q = OFF_SWA_V + SWA_KV_W
OFF_MOBA_K = OFF_MOBA_Q + MOBA_W
OFF_MOBA_V = OFF_MOBA_K + MOBA_W
OFF_GATE = OFF_MOBA_V + MOBA_W

LANES = 128
QK_SWA_Q = 0
QK_SWA_K = QK_SWA_Q + SWA_Q_W
QK_MOBA_Q = QK_SWA_K + SWA_KV_HEADS * LANES
QK_MOBA_K = QK_MOBA_Q + MOBA_W
QK_W = QK_MOBA_K + MOBA_W
VT_SWA = 0
VT_MOBA = VT_SWA + SWA_KV_W
VT_W = VT_MOBA + MOBA_W

MOE_TM = 512
MOE_PARTS = 2
SC_CHUNK = 128
ROUTER_ROWS = 8 + N_EXPERTS
VMEM_LIMIT = 56 * 1024 * 1024

_NT = (((1,), (1,)), ((), ()))
_TN = (((0,), (0,)), ((), ()))


def _rel_bucket_np(dist):
    n = np.maximum(dist, 0)
    max_exact = REL_BUCKETS // 2
    nf = np.maximum(n, 1).astype(np.float32)
    large = max_exact + (np.log(nf / np.float32(max_exact)) / np.float32(math.log(REL_MAX_DIST / max_exact))
                         * np.float32(REL_BUCKETS - max_exact)).astype(np.int32)
    large = np.minimum(large, REL_BUCKETS - 1)
    return np.where(n < max_exact, n, large).astype(np.int32)


def _inproj_kernel(x_ref, wn_ref, bn_ref, wt_ref, bt_ref, eg_ref, eu_ref, ed_ref,
                   qk_ref, vt_ref, egb_ref, eub_ref, edb_ref):
    xb = x_ref[0].astype(jnp.bfloat16)
    qk = jnp.dot(xb, wn_ref[...], preferred_element_type=jnp.float32) + bn_ref[...]
    qk_ref[0] = qk.astype(jnp.bfloat16)
    vt = lax.dot_general(wt_ref[...], xb, _NT, preferred_element_type=jnp.float32) + bt_ref[...]
    vt_ref[0] = vt.astype(jnp.bfloat16)
    egb_ref[...] = eg_ref[...].astype(jnp.bfloat16)
    eub_ref[...] = eu_ref[...].astype(jnp.bfloat16)
    edb_ref[...] = ed_ref[...].astype(jnp.bfloat16)


def _inproj(x, wn, bn, wt, bt, w_gate, w_up, w_down, tm=1024):
    B, S, D = x.shape
    nt = S // tm
    steps = B * nt

    def sliced(w):
        E, R, C = w.shape
        assert (E * R) % (steps * BF16_SUBLANES) == 0
        return w.reshape(steps, E * R // steps, C)

    eg, eu, ed = sliced(w_gate), sliced(w_up), sliced(w_down)
    espec = lambda w: pl.BlockSpec((1,) + w.shape[1:], lambda b, i: (b * nt + i, 0, 0))
    qk, vt, egb, eub, edb = pl.pallas_call(
        _inproj_kernel,
        grid=(B, nt),
        in_specs=[
            pl.BlockSpec((1, tm, D), lambda b, i: (b, i, 0)),
            pl.BlockSpec((D, QK_W), lambda b, i: (0, 0)),
            pl.BlockSpec((1, QK_W), lambda b, i: (0, 0)),
            pl.BlockSpec((VT_W, D), lambda b, i: (0, 0)),
            pl.BlockSpec((VT_W, 1), lambda b, i: (0, 0)),
            espec(eg), espec(eu), espec(ed),
        ],
        out_specs=[
            pl.BlockSpec((1, tm, QK_W), lambda b, i: (b, i, 0)),
            pl.BlockSpec((1, VT_W, tm), lambda b, i: (b, 0, i)),
            espec(eg), espec(eu), espec(ed),
        ],
        out_shape=[
            jax.ShapeDtypeStruct((B, S, QK_W), jnp.bfloat16),
            jax.ShapeDtypeStruct((B, VT_W, S), jnp.bfloat16),
            jax.ShapeDtypeStruct(eg.shape, jnp.bfloat16),
            jax.ShapeDtypeStruct(eu.shape, jnp.bfloat16),
            jax.ShapeDtypeStruct(ed.shape, jnp.bfloat16),
        ],
        compiler_params=pltpu.CompilerParams(
            dimension_semantics=("parallel", "parallel"), vmem_limit_bytes=VMEM_LIMIT),
        name="inproj",
    )(x, wn, bn, wt, bt, eg, eu, ed)
    return qk, vt, egb.reshape(w_gate.shape), eub.reshape(w_up.shape), edb.reshape(w_down.shape)


def _swa_kernel(sink_ref, q_ref, k_ref, vt_ref, bias_ref, o_ref, vaug_ref, *s_bufs):
    S = q_ref.shape[1]
    nblk = S // SWA_BLOCK
    half = HEAD_DIM
    vt = vt_ref[0]
    rows = lax.broadcasted_iota(jnp.int32, vt.shape, 0)
    one = jnp.ones_like(vt)
    vaug_ref[0] = jnp.where(rows < half, vt, one)
    vaug_ref[1] = jnp.where(rows < half, one, vt)
    lane = lax.broadcasted_iota(jnp.int32, (SWA_BLOCK, LANES), 1)
    col = lax.broadcasted_iota(jnp.int32, (1, 2 * SWA_BLOCK), 1)
    npair = SWA_Q_HEADS // 2

    def expand(specs):
        return [(qs, ks, nk, boff, pair, (2 * pair) // SWA_GROUP) for qs, ks, nk, boff in specs for pair in range(npair)]

    def score(specs):
        ss = []
        for qs, ks, nk, boff, pair, kv in expand(specs):
            qblk = q_ref[0, pl.ds(qs, SWA_BLOCK), pair * LANES:(pair + 1) * LANES]
            zero = jnp.zeros_like(qblk)
            q2 = jnp.concatenate([jnp.where(lane < half, qblk, zero), jnp.where(lane >= half, qblk, zero)], axis=0)
            kblk = k_ref[0, pl.ds(ks, nk), kv * LANES:(kv + 1) * LANES]
            s = lax.dot_general(kblk, q2, _NT, preferred_element_type=jnp.float32)
            ss.append(s + bias_ref[pair, boff:boff + nk, :])
        return ss

    def finish(specs, ss):
        items = expand(specs)
        stats = []
        for (qs, ks, nk, boff, pair, kv), s in zip(items, ss):
            sink = jnp.where(col < SWA_BLOCK, sink_ref[2 * pair], sink_ref[2 * pair + 1]) * LOG2E
            m = jnp.maximum(_colmax(s), sink)
            stats.append((jnp.exp2(s - m).astype(jnp.bfloat16), jnp.exp2(sink - m)))
        rs = [jnp.dot(vaug_ref[kv, :, pl.ds(ks, nk)], p, preferred_element_type=jnp.float32)
              for (qs, ks, nk, boff, pair, kv), (p, _) in zip(items, stats)]
        outs = []
        for (qs, ks, nk, boff, pair, kv), (_, esink), r in zip(items, stats, rs):
            l = (r[half:half + 1, :] if kv == 0 else r[0:1, :]) + esink
            o = (r[0:half, :] if kv == 0 else r[half:, :]) / l
            outs.append(jnp.concatenate([o[:, :SWA_BLOCK], o[:, SWA_BLOCK:]], axis=0))
        for i, (qs, _, _, _) in enumerate(specs):
            ot = jnp.concatenate(outs[i * npair:(i + 1) * npair], axis=0)
            o_ref[0, :, pl.ds(qs, SWA_BLOCK)] = ot.astype(jnp.bfloat16)

    first = [(0, 0, SWA_BLOCK, SWA_BLOCK), (SWA_BLOCK, 0, 2 * SWA_BLOCK, 0)]
    finish(first, score(first))

    assert nblk % 2 == 0
    n_groups = nblk // 2 - 1
    nbuf = len(s_bufs)

    def group_specs(g):
        specs = []
        for n in (2 * g + 2, 2 * g + 3):
            qs = pl.multiple_of(n * SWA_BLOCK, SWA_BLOCK)
            specs.append((qs, pl.multiple_of(qs - SWA_BLOCK, SWA_BLOCK), 2 * SWA_BLOCK, 0))
        return specs

    def score_to(g, buf):
        for k, s in enumerate(score(group_specs(g))):
            buf[k] = s

    def finish_from(g, buf):
        finish(group_specs(g), [buf[k] for k in range(2 * npair)])

    for g in range(min(2, n_groups)):
        score_to(g, s_bufs[g])

    def body(t, carry):
        for j in range(nbuf):
            score_to(nbuf * t + 2 + j, s_bufs[(2 + j) % nbuf])
            finish_from(nbuf * t + j, s_bufs[j])
        return carry

    n_trips = max(n_groups - 2, 0) // nbuf
    lax.fori_loop(0, n_trips, body, 0)
    for g in range(nbuf * n_trips, n_groups):
        if g + 2 < n_groups:
            score_to(g + 2, s_bufs[(g + 2) % nbuf])
        finish_from(g, s_bufs[g % nbuf])


def _swa(sinks, qk, vt, bias):
    B, S, _ = qk.shape
    return pl.pallas_call(
        _swa_kernel,
        grid_spec=pltpu.PrefetchScalarGridSpec(
            num_scalar_prefetch=1,
            grid=(B,),
            in_specs=[
                pl.BlockSpec((1, S, SWA_Q_W), lambda b, s: (b, 0, QK_SWA_Q // SWA_Q_W)),
                pl.BlockSpec((1, S, 2 * LANES), lambda b, s: (b, 0, QK_SWA_K // (2 * LANES))),
                pl.BlockSpec((1, SWA_KV_W, S), lambda b, s: (b, VT_SWA // SWA_KV_W, 0)),
                pl.BlockSpec((SWA_Q_HEADS // 2, 2 * SWA_BLOCK, 2 * SWA_BLOCK), lambda b, s: (0, 0, 0)),
            ],
            out_specs=pl.BlockSpec((1, SWA_Q_W, S), lambda b, s: (b, 0, 0)),
            scratch_shapes=[pltpu.VMEM((SWA_KV_HEADS, SWA_KV_W, S), jnp.bfloat16)]
            + [pltpu.VMEM((SWA_Q_HEADS, 2 * SWA_BLOCK, 2 * SWA_BLOCK), jnp.float32)] * 4,
        ),
        out_shape=jax.ShapeDtypeStruct((B, SWA_Q_W, S), jnp.bfloat16),
        compiler_params=pltpu.CompilerParams(
            dimension_semantics=("parallel",), vmem_limit_bytes=VMEM_LIMIT),
        name="swa",
    )(sinks, qk, qk, vt, bias)


def _colmax(s):
    while s.shape[0] > 8:
        h = s.shape[0] // 2
        s = jnp.maximum(s[:h], s[h:])
    return jnp.max(s, axis=0, keepdims=True)


def _moba_kernel(fqb_ref, fc_ref, q_ref, k_ref, vt_ref, bias_ref, o_ref,
                 vaug_ref, gate_ref, rank_ref, sel_ref, far_ref, m_scr, acc_scr, sa_scr, sb_scr, sc_scr, sd_scr):
    S = q_ref.shape[1]
    nb = S // MOBA_BLOCK
    half = HEAD_DIM
    BLK = MOBA_BLOCK

    ones_rows = jnp.ones((MOBA_VROWS - half, S), jnp.bfloat16)
    for h in range(2):
        vaug_ref[h, 0:half, :] = vt_ref[0, h * half:(h + 1) * half, :]
        vaug_ref[h, half:, :] = ones_rows

    kf = k_ref[0].astype(jnp.float32).reshape(nb, BLK, LANES)
    kmean = jnp.sum(kf, axis=1) * (1.0 / BLK)
    k_hi = kmean.astype(jnp.bfloat16)
    k_lo = (kmean - k_hi.astype(jnp.float32)).astype(jnp.bfloat16)
    kcat = jnp.concatenate([k_hi, k_lo], axis=0)

    lane_q = lax.broadcasted_iota(jnp.int32, (S, LANES), 1)
    brow = lax.broadcasted_iota(jnp.int32, (nb, S), 0)
    qblk_of = lax.broadcasted_iota(jnp.int32, (nb, S), 1) // BLK
    qall = q_ref[0]
    for h in range(2):
        qh = jnp.where((lane_q < half) if h == 0 else (lane_q >= half), qall, jnp.zeros_like(qall))
        g2 = lax.dot_general(kcat, qh, _NT, preferred_element_type=jnp.float32)
        gate_ref[...] = g2[0:nb] + g2[nb:2 * nb]
        rank_ref[...] = jnp.zeros_like(rank_ref)
        for m in range(nb - 1):
            lo = (m + 1) * BLK
            G = gate_ref[:, lo:]
            gm = gate_ref[m:m + 1, lo:]
            ge = jnp.where(gm >= G, 1.0, 0.0)
            gt = jnp.where(gm > G, 1.0, 0.0)
            brow_m = lax.broadcasted_iota(jnp.int32, (nb, S - lo), 0)
            rank_ref[:, lo:] += jnp.where(brow_m > m, ge, gt)
        top = rank_ref[...] < float(MOBA_TOPK)
        sel_ref[h] = jnp.where((brow < qblk_of) & top, 0.0, NEG_INF)
        far_ref[h] = jnp.where((brow < qblk_of - 1) & top, 0.0, NEG_INF)

    lane = lax.broadcasted_iota(jnp.int32, (BLK, LANES), 1)

    def head_q(qs, h):
        qblk = q_ref[0, pl.ds(qs, BLK), :]
        return jnp.where((lane < half) if h == 0 else (lane >= half), qblk, jnp.zeros_like(qblk))

    m0 = jnp.full((1, BLK), NEG_INF, jnp.float32)
    acc0 = jnp.zeros((MOBA_VROWS, BLK), jnp.float32)

    def item(h, qb, qs, ks, nk, bias, sel_a, sel_b, m, acc):
        return dict(h=h, qb=qb, qs=qs, ks=ks, nk=nk, bias=bias, sel_a=sel_a, sel_b=sel_b, m=m, acc=acc)

    def near_item(qb, h):
        qs = pl.multiple_of(qb * BLK, BLK)
        ps = pl.multiple_of(qs - BLK, BLK)
        sel_prev = sel_ref[h, pl.ds(qb - 1, 1), pl.ds(qs, BLK)]
        return item(h, qb, qs, ps, 2 * BLK, bias_ref[h], sel_prev, None, m0, acc0)

    def far_item(i, h):
        qb = fqb_ref[i]
        c = fc_ref[i]
        qs = pl.multiple_of(qb * BLK, BLK)
        ks = pl.multiple_of(c * (2 * BLK), 2 * BLK)
        sel_a = far_ref[h, pl.ds(2 * c, 1), pl.ds(qs, BLK)]
        sel_b = far_ref[h, pl.ds(2 * c + 1, 1), pl.ds(qs, BLK)]
        return item(h, qb, qs, ks, 2 * BLK, None, sel_a, sel_b, m_scr[h, qb], acc_scr[h, qb])

    def scores(it):
        kslab = k_ref[0, pl.ds(it["ks"], it["nk"]), :]
        s = lax.dot_general(kslab, head_q(it["qs"], it["h"]), _NT,
                            preferred_element_type=jnp.float32)
        return s if it["bias"] is None else s + it["bias"]

    def probs(it, s):
        m, sel_a, sel_b = it["m"], it["sel_a"], it["sel_b"]
        if it["nk"] == BLK:
            m_new = jnp.maximum(m, _colmax(s))
            p = jnp.exp2(s - m_new)
        else:
            cm_a = _colmax(s[:BLK]) + sel_a
            cm_b = _colmax(s[BLK:])
            if sel_b is not None:
                cm_b = cm_b + sel_b
            m_new = jnp.maximum(m, jnp.maximum(cm_a, cm_b))
            p_a = jnp.exp2(s[:BLK] - (m_new - sel_a))
            p_b = jnp.exp2(s[BLK:] - (m_new if sel_b is None else m_new - sel_b))
            p = jnp.concatenate([p_a, p_b], axis=0)
        return m_new, jnp.exp2(m - m_new), p.astype(jnp.bfloat16)

    def run(items):
        ss = [scores(it) for it in items]
        ps = [probs(it, s) for it, s in zip(items, ss)]
        pvs = [jnp.dot(vaug_ref[it["h"], :, pl.ds(it["ks"], it["nk"])], p, preferred_element_type=jnp.float32)
               for it, (_, _, p) in zip(items, ps)]
        for it, (m_new, alpha, _), pv in zip(items, ps, pvs):
            m_scr[it["h"], it["qb"]] = m_new
            acc_scr[it["h"], it["qb"]] = it["acc"] * alpha + pv

    run([item(h, 0, 0, 0, BLK, bias_ref[h, BLK:, :], None, None, m0, acc0) for h in range(2)]
        + [near_item(nb - 1, h) for h in range(2)])

    bufs = (sa_scr, sb_scr, sc_scr, sd_scr)
    nbuf = len(bufs)

    def finish(items, s_buf):
        ps = [probs(it, s_buf[k]) for k, it in enumerate(items)]
        pvs = [jnp.dot(vaug_ref[it["h"], :, pl.ds(it["ks"], it["nk"])], p, preferred_element_type=jnp.float32)
               for it, (_, _, p) in zip(items, ps)]
        for it, (m_new, alpha, _), pv in zip(items, ps, pvs):
            m_scr[it["h"], it["qb"]] = m_new
            acc_scr[it["h"], it["qb"]] = it["acc"] * alpha + pv

    def pipelined(n_groups, score, items_of):
        for g in range(min(MOBA_LOOKAHEAD, n_groups)):
            score(g, bufs[g])

        def body(t, carry):
            for j in range(MOBA_UNROLL):
                score(MOBA_UNROLL * t + MOBA_LOOKAHEAD + j, bufs[(MOBA_LOOKAHEAD + j) % nbuf])
                finish(items_of(MOBA_UNROLL * t + j), bufs[j % nbuf])
            return carry

        n_trips = max(n_groups - MOBA_LOOKAHEAD, 0) // MOBA_UNROLL
        lax.fori_loop(0, n_trips, body, 0)
        for g in range(MOBA_UNROLL * n_trips, n_groups):
            if g + MOBA_LOOKAHEAD < n_groups:
                score(g + MOBA_LOOKAHEAD, bufs[(g + MOBA_LOOKAHEAD) % nbuf])
            finish(items_of(g), bufs[g % nbuf])

    near_pairs = nb // 2 - 1

    def near_group(g):
        return [(qb, h) for qb in (g + 1, g + 1 + near_pairs) for h in range(2)]

    def near_score(g, s_buf):
        for k, (qb, h) in enumerate(near_group(g)):
            qs = pl.multiple_of(qb * BLK, BLK)
            ps = pl.multiple_of(qs - BLK, BLK)
            s_buf[k] = lax.dot_general(k_ref[0, pl.ds(ps, 2 * BLK), :], head_q(qs, h), _NT,
                                       preferred_element_type=jnp.float32) + bias_ref[h]

    pipelined(near_pairs, near_score, lambda g: [near_item(qb, h) for qb, h in near_group(g)])

    n_far_groups = fqb_ref.shape[0] // 2

    def far_group(g):
        return [(j, h) for j in (g, g + n_far_groups) for h in range(2)]

    def far_score(g, s_buf):
        for k, (j, h) in enumerate(far_group(g)):
            qs = pl.multiple_of(fqb_ref[j] * BLK, BLK)
            ks = pl.multiple_of(fc_ref[j] * (2 * BLK), 2 * BLK)
            s_buf[k] = lax.dot_general(k_ref[0, pl.ds(ks, 2 * BLK), :], head_q(qs, h), _NT,
                                       preferred_element_type=jnp.float32)

    pipelined(n_far_groups, far_score, lambda g: [far_item(j, h) for j, h in far_group(g)])

    def out_body(t, carry):
        for qb in (2 * t, 2 * t + 1):
            qs = pl.multiple_of(qb * BLK, BLK)
            a0 = acc_scr[0, qb]
            a1 = acc_scr[1, qb]
            ot = jnp.concatenate([a0[0:half] / a0[half:half + 1, :], a1[0:half] / a1[half:half + 1, :]],
                                 axis=0)
            o_ref[0, :, pl.ds(qs, BLK)] = ot.astype(jnp.bfloat16)
        return carry

    lax.fori_loop(0, nb // 2, out_body, 0)


def _moba_far_items(nb):
    items = [(qb, c) for qb in range(2, nb) for c in range(qb // 2)]
    n = len(items)
    assert n % 2 == 0 and all(items[i][0] != items[i + n // 2][0] for i in range(n // 2))
    return np.array([it[0] for it in items], np.int32), np.array([it[1] for it in items], np.int32)


def _moba(qk, vt, bias):
    B, S, _ = qk.shape
    nb = S // MOBA_BLOCK
    npair = MOBA_W // LANES
    far_qb, far_c = _moba_far_items(nb)
    return pl.pallas_call(
        _moba_kernel,
        grid_spec=pltpu.PrefetchScalarGridSpec(
            num_scalar_prefetch=2,
            grid=(B, npair),
            in_specs=[
                pl.BlockSpec((1, S, LANES), lambda b, p, fq, fc: (b, 0, QK_MOBA_Q // LANES + p)),
                pl.BlockSpec((1, S, LANES), lambda b, p, fq, fc: (b, 0, QK_MOBA_K // LANES + p)),
                pl.BlockSpec((1, LANES, S), lambda b, p, fq, fc: (b, VT_MOBA // LANES + p, 0)),
                pl.BlockSpec((2, 2 * MOBA_BLOCK, MOBA_BLOCK), lambda b, p, fq, fc: (p, 0, 0)),
            ],
            out_specs=pl.BlockSpec((1, LANES, S), lambda b, p, fq, fc: (b, p, 0)),
            scratch_shapes=[
                pltpu.VMEM((2, MOBA_VROWS, S), jnp.bfloat16),
                pltpu.VMEM((nb, S), jnp.float32),
                pltpu.VMEM((nb, S), jnp.float32),
                pltpu.VMEM((2, nb, S), jnp.float32),
                pltpu.VMEM((2, nb, S), jnp.float32),
                pltpu.VMEM((2, nb, 1, MOBA_BLOCK), jnp.float32),
                pltpu.VMEM((2, nb, MOBA_VROWS, MOBA_BLOCK), jnp.float32),
            ] + [pltpu.VMEM((4, 2 * MOBA_BLOCK, MOBA_BLOCK), jnp.float32)] * 4,
        ),
        out_shape=jax.ShapeDtypeStruct((B, MOBA_W, S), jnp.bfloat16),
        compiler_params=pltpu.CompilerParams(
            dimension_semantics=("parallel", "parallel"), vmem_limit_bytes=VMEM_LIMIT),
        name="moba",
    )(jnp.asarray(far_qb), jnp.asarray(far_c), qk, qk, vt, bias)


def _layer_norm(h, gain, bias):
    mu = jnp.mean(h, axis=-1, keepdims=True)
    c = h - mu
    var = jnp.mean(c * c, axis=-1, keepdims=True)
    return c * lax.rsqrt(var + LN_EPS) * gain + bias


def _pack_bf16_pair(a, b):
    ia = lax.bitcast_convert_type(a.astype(jnp.bfloat16).astype(jnp.float32), jnp.int32)
    ib = lax.bitcast_convert_type(b.astype(jnp.bfloat16).astype(jnp.float32), jnp.int32)
    return lax.shift_right_logical(ia, 16) | ib


def _unpack_bf16_pair(w):
    lo = lax.bitcast_convert_type(lax.shift_left(w, 16), jnp.float32)
    hi = lax.bitcast_convert_type(w & jnp.int32(-65536), jnp.float32)
    return lo.astype(jnp.bfloat16), hi.astype(jnp.bfloat16)


def _merge_kernel(x_ref, ya_ref, yb_ref, wg_ref, bg_ref, wa_ref, wb_ref, wo_ref, g1_ref, b1_ref,
                  wr_ref, br_ref, tri_ref, x1_ref, x1p_ref, r_ref, cnt_ref):
    @pl.when(pl.program_id(0) == 0)
    def _():
        cnt_ref[...] = jnp.zeros_like(cnt_ref)

    ts = tri_ref.shape[0]
    subs = [pl.ds(r0, ts) for r0 in range(0, x_ref.shape[0], ts)]
    pre = [_merge_matmuls(x_ref[rows, :], ya_ref[0, :, rows], yb_ref[0, :, rows], wg_ref, bg_ref, wa_ref, wb_ref, wo_ref)
           for rows in subs]
    for rows, h in zip(subs, pre):
        _merge_route(h, rows, g1_ref, b1_ref, wr_ref, br_ref, tri_ref, x1_ref, x1p_ref, r_ref, cnt_ref)


def _merge_matmuls(x, ya_t, yb_t, wg_ref, bg_ref, wa_ref, wb_ref, wo_ref):
    xb = x.astype(jnp.bfloat16)
    z = jnp.dot(xb, wg_ref[...], preferred_element_type=jnp.float32) + bg_ref[...]
    gates = 1.0 / (1.0 + jnp.exp(-z))
    pa = lax.dot_general(ya_t, wa_ref[...], _TN, preferred_element_type=jnp.float32)
    pb = lax.dot_general(yb_t, wb_ref[...], _TN, preferred_element_type=jnp.float32)
    merged = gates[:, :D_MODEL] * pa + gates[:, D_MODEL:] * pb
    mixed = jnp.dot(merged.astype(jnp.bfloat16), wo_ref[...], preferred_element_type=jnp.float32)
    return DEEPNORM_ALPHA * x + mixed


def _merge_route(h, rows, g1_ref, b1_ref, wr_ref, br_ref, tri_ref, x1_ref, x1p_ref, r_ref, cnt_ref):
    x1 = _layer_norm(h, g1_ref[...], b1_ref[...])
    x1_ref[rows, :] = x1
    x1_hi = x1.astype(jnp.bfloat16)
    x1p_ref[rows, :] = _pack_bf16_pair(x1[:, :D_MODEL // 2], x1[:, D_MODEL // 2:])
    x1_lo = (x1 - x1_hi.astype(jnp.float32)).astype(jnp.bfloat16)

    R = ROUTER_ROWS
    l1 = lax.dot_general(wr_ref[...], x1_hi, _NT, preferred_element_type=jnp.float32)
    l2 = lax.dot_general(wr_ref[0:R, :], x1_lo, _NT, preferred_element_type=jnp.float32)
    L = l1[0:R] + l1[R:2 * R] + l2 + br_ref[...]
    tm = x1.shape[0]
    row = lax.broadcasted_iota(jnp.int32, (8, tm), 0)
    big = jnp.float32(-3e38)
    gl = jnp.where(row < N_GROUPS, L[0:8], big)
    gmax = jnp.max(gl, axis=0, keepdims=True)
    g_idx = jnp.min(jnp.where(gl == gmax, row, 8), axis=0, keepdims=True)
    gsum = jnp.sum(jnp.where(row < N_GROUPS, jnp.exp(gl - gmax), 0.0), axis=0, keepdims=True)
    g_prob = 1.0 / gsum
    E = L[8 + 8 * (N_GROUPS - 1):8 + 8 * N_GROUPS]
    for g in range(N_GROUPS - 2, -1, -1):
        E = jnp.where(g_idx == g, L[8 + 8 * g:16 + 8 * g], E)
    t0 = jnp.max(E, axis=0, keepdims=True)
    loc0 = jnp.min(jnp.where(E == t0, row, 8), axis=0, keepdims=True)
    E2 = jnp.where(row == loc0, big, E)
    t1 = jnp.max(E2, axis=0, keepdims=True)
    loc1 = jnp.min(jnp.where(E2 == t1, row, 8), axis=0, keepdims=True)
    ex = jnp.exp(t1 - t0)
    w0 = g_prob / (1.0 + ex)
    w1 = g_prob * ex / (1.0 + ex)
    e0i = g_idx * EXPERTS_PER_GROUP + loc0
    e1i = g_idx * EXPERTS_PER_GROUP + loc1

    erow = lax.broadcasted_iota(jnp.int32, (N_EXPERTS, tm), 0)
    oh0 = jnp.where(erow == e0i, 1.0, 0.0)
    oh1 = jnp.where(erow == e1i, 1.0, 0.0)
    both = oh0 + oh1
    prefix = jnp.dot(both.astype(jnp.bfloat16), tri_ref[...], preferred_element_type=jnp.float32)
    prefix = prefix + cnt_ref[:, 0:1]
    rank0 = jnp.sum(oh0 * prefix, axis=0, keepdims=True)
    rank1 = jnp.sum(oh1 * prefix, axis=0, keepdims=True)
    cnt_ref[...] = cnt_ref[...] + jnp.sum(both, axis=1, keepdims=True)

    vals = (e0i.astype(jnp.float32), e1i.astype(jnp.float32), w0, w1, rank0, rank1)
    out = jnp.zeros((8, tm), jnp.float32)
    for k, v in enumerate(vals):
        out = jnp.where(row == k, v, out)
    r_ref[:, rows] = out


def _merge(x2, ya, yb, wg, bg, wa, wb, wo, g1, b1, wr, br, part, n_parts, tm=1024, ts=256):
    D = x2.shape[1]
    T = x2.shape[0] // n_parts
    S = ya.shape[2]
    per_seq = S // tm
    assert per_seq * tm == S
    off = part * (T // tm)
    const = lambda i: (0, 0)
    seq_tile = lambda i: ((i + off) // per_seq, 0, (i + off) % per_seq)
    tri = jnp.triu(jnp.ones((ts, ts), jnp.bfloat16), k=1)
    return pl.pallas_call(
        _merge_kernel,
        grid=(T // tm,),
        in_specs=[
            pl.BlockSpec((tm, D), lambda i: (i + off, 0)),
            pl.BlockSpec((1, SWA_Q_W, tm), seq_tile),
            pl.BlockSpec((1, MOBA_W, tm), seq_tile),
            pl.BlockSpec((D, 2 * D), const),
            pl.BlockSpec((1, 2 * D), const),
            pl.BlockSpec((SWA_Q_W, D), const),
            pl.BlockSpec((MOBA_W, D), const),
            pl.BlockSpec((D, D), const),
            pl.BlockSpec((1, D), const),
            pl.BlockSpec((1, D), const),
            pl.BlockSpec((2 * ROUTER_ROWS, D), const),
            pl.BlockSpec((ROUTER_ROWS, 1), const),
            pl.BlockSpec((ts, ts), const),
        ],
        out_specs=[
            pl.BlockSpec((tm, D), lambda i: (i, 0)),
            pl.BlockSpec((tm, D // 2), lambda i: (i, 0)),
            pl.BlockSpec((8, tm), lambda i: (0, i)),
            pl.BlockSpec((N_EXPERTS, LANES), const),
        ],
        out_shape=[
            jax.ShapeDtypeStruct((T, D), jnp.float32),
            jax.ShapeDtypeStruct((T, D // 2), jnp.int32),
            jax.ShapeDtypeStruct((8, T), jnp.float32),
            jax.ShapeDtypeStruct((N_EXPERTS, LANES), jnp.float32),
        ],
        compiler_params=pltpu.CompilerParams(
            dimension_semantics=("arbitrary",), vmem_limit_bytes=VMEM_LIMIT),
        name="merge",
    )(x2, ya, yb, wg, bg, wa, wb, wo, g1, b1, wr, br, tri)


def _dest_kernel(r_ref, ps_ref, d_ref):
    tt = r_ref.shape[1]
    erow = lax.broadcasted_iota(jnp.int32, (N_EXPERTS, tt), 0)
    row = lax.broadcasted_iota(jnp.int32, (8, tt), 0)
    ps = ps_ref[...]
    out = jnp.zeros((8, tt), jnp.float32)
    for k in range(EXPERT_TOPK):
        e = r_ref[k:k + 1, :].astype(jnp.int32)
        start = jnp.sum(jnp.where(erow == e, ps, 0.0), axis=0, keepdims=True)
        out = jnp.where(row == k, start + r_ref[4 + k:5 + k, :], out)
    d_ref[...] = out.astype(jnp.int32)


def _dest(rinfo, padded_start, tt=8192):
    T = rinfo.shape[1]
    tt = min(tt, T)
    return pl.pallas_call(
        _dest_kernel,
        grid=(T // tt,),
        in_specs=[pl.BlockSpec((8, tt), lambda i: (0, i)), pl.BlockSpec((N_EXPERTS, 1), lambda i: (0, 0))],
        out_specs=pl.BlockSpec((8, tt), lambda i: (0, i)),
        out_shape=jax.ShapeDtypeStruct((8, T), jnp.int32),
        compiler_params=pltpu.CompilerParams(dimension_semantics=("parallel",)),
        name="dest",
    )(rinfo, padded_start.astype(jnp.float32)[:, None])


def _sc_workers():
    info = plsc.get_sparse_core_info()
    return info.num_cores, info.num_subcores


def _sc_scatter_rows(src, dest0, dest1, cap):
    T, W = src.shape
    nc, ns = _sc_workers()
    per_w = T // (nc * ns)
    assert per_w * nc * ns == T and per_w % SC_CHUNK == 0
    mesh = plsc.VectorSubcoreMesh(core_axis_name="c", subcore_axis_name="s")

    @functools.partial(
        pl.kernel, mesh=mesh,
        out_type=jax.ShapeDtypeStruct((cap, W), src.dtype),
        scratch_types=[pltpu.VMEM((SC_CHUNK,), jnp.int32), pltpu.VMEM((SC_CHUNK, W), src.dtype)],
    )
    def scatter(src_hbm, d0_hbm, d1_hbm, out_hbm, idx_v, rows_v):
        wid = lax.axis_index("s") * nc + lax.axis_index("c")
        base = wid * per_w

        @pl.loop(0, per_w // SC_CHUNK)
        def _(c):
            off = pl.multiple_of(base + c * SC_CHUNK, SC_CHUNK)
            pltpu.sync_copy(src_hbm.at[pl.ds(off, SC_CHUNK)], rows_v)
            for d_hbm in (d0_hbm, d1_hbm):
                pltpu.sync_copy(d_hbm.at[pl.ds(off, SC_CHUNK)], idx_v)
                pltpu.sync_copy(rows_v, out_hbm.at[idx_v])

    return scatter(src, dest0, dest1)


def _sc_gather_rows(table, idx):
    N = idx.shape[0]
    W = table.shape[1]
    nc, ns = _sc_workers()
    per_w = N // (nc * ns)
    assert per_w * nc * ns == N and per_w % SC_CHUNK == 0
    mesh = plsc.VectorSubcoreMesh(core_axis_name="c", subcore_axis_name="s")

    @functools.partial(
        pl.kernel, mesh=mesh,
        out_type=jax.ShapeDtypeStruct((N, W), table.dtype),
        scratch_types=[pltpu.VMEM((SC_CHUNK,), jnp.int32), pltpu.VMEM((SC_CHUNK, W), table.dtype)],
    )
    def gather(table_hbm, idx_hbm, out_hbm, idx_v, rows_v):
        wid = lax.axis_index("s") * nc + lax.axis_index("c")
        base = wid * per_w

        @pl.loop(0, per_w // SC_CHUNK)
        def _(c):
            off = pl.multiple_of(base + c * SC_CHUNK, SC_CHUNK)
            pltpu.sync_copy(idx_hbm.at[pl.ds(off, SC_CHUNK)], idx_v)
            pltpu.sync_copy(table_hbm.at[idx_v], rows_v)
            pltpu.sync_copy(rows_v, out_hbm.at[pl.ds(off, SC_CHUNK)])

    return gather(table, idx)


def _expert_kernel(be_ref, nv_ref, first_ref, slot_ref, nxt_ref, x_ref, wg_hbm, wu_hbm, wd_hbm, y_ref,
                   wg_buf, wu_buf, wd_buf, sem):
    i = pl.program_id(0)
    nv = nv_ref[i]
    slot = slot_ref[i]

    def weight_copies(e, s):
        return [pltpu.make_async_copy(hbm.at[e], buf.at[s], sem.at[s, k])
                for k, (hbm, buf) in enumerate(((wg_hbm, wg_buf), (wu_hbm, wu_buf), (wd_hbm, wd_buf)))]

    @pl.when(i == 0)
    def _():
        for c in weight_copies(be_ref[0], 0):
            c.start()

    @pl.when(first_ref[i] == 1)
    def _():
        @pl.when(nxt_ref[i] >= 0)
        def _():
            for c in weight_copies(nxt_ref[i], 1 - slot):
                c.start()

        for c in weight_copies(be_ref[i], slot):
            c.wait()

    @pl.when(nv > 0)
    def _():
        lo, hi = _unpack_bf16_pair(x_ref[...])
        xb = jnp.concatenate([lo, hi], axis=1)
        rows = lax.broadcasted_iota(jnp.int32, xb.shape, 0)
        xb = jnp.where(rows < nv, xb, jnp.zeros_like(xb))
        g = jnp.dot(xb, wg_buf[slot], preferred_element_type=jnp.float32)
        u = jnp.dot(xb, wu_buf[slot], preferred_element_type=jnp.float32)
        act = (g / (1.0 + jnp.exp(-g))) * u
        y = jnp.dot(act.astype(jnp.bfloat16), wd_buf[slot], preferred_element_type=jnp.float32)
        y_ref[...] = _pack_bf16_pair(y[:, :D_MODEL // 2], y[:, D_MODEL // 2:])

    @pl.when(nv <= 0)
    def _():
        y_ref[...] = jnp.zeros_like(y_ref)


def _experts(plan, xs, wg, wu, wd):
    cap, DW = xs.shape
    D = 2 * DW
    n_blocks = cap // MOE_TM
    hbm = pl.BlockSpec(memory_space=pl.ANY)
    return pl.pallas_call(
        _expert_kernel,
        grid_spec=pltpu.PrefetchScalarGridSpec(
            num_scalar_prefetch=5,
            grid=(n_blocks,),
            in_specs=[pl.BlockSpec((MOE_TM, DW), lambda i, *_: (i, 0)), hbm, hbm, hbm],
            out_specs=pl.BlockSpec((MOE_TM, DW), lambda i, *_: (i, 0)),
            scratch_shapes=[
                pltpu.VMEM((2, D, D_EXPERT), jnp.bfloat16),
                pltpu.VMEM((2, D, D_EXPERT), jnp.bfloat16),
                pltpu.VMEM((2, D_EXPERT, D), jnp.bfloat16),
                pltpu.SemaphoreType.DMA((2, 3)),
            ],
        ),
        out_shape=jax.ShapeDtypeStruct((cap, DW), jnp.int32),
        compiler_params=pltpu.CompilerParams(
            dimension_semantics=("arbitrary",), vmem_limit_bytes=VMEM_LIMIT),
        name="experts",
    )(*plan, xs, wg, wu, wd)


def _final_kernel(x1_ref, y0_ref, y1_ref, w_ref, g2_ref, b2_ref, *rest):
    o_ref = rest[-1]
    w = w_ref[...].T
    halves = []
    for part in range(2):
        y0 = _unpack_bf16_pair(y0_ref[...])[part].astype(jnp.float32)
        y1 = _unpack_bf16_pair(y1_ref[...])[part].astype(jnp.float32)
        halves.append(y0 * w[:, 2:3] + y1 * w[:, 3:4])
    moe = jnp.concatenate(halves, axis=1)
    o_ref[...] = _layer_norm(DEEPNORM_ALPHA * x1_ref[...] + moe, g2_ref[...], b2_ref[...])


def _final(x1, ypair, rinfo, g2, b2, out_prev, part, n_parts, tm=1024):
    T, D = x1.shape
    const = lambda i: (0, 0)
    nt = T // tm
    off = part * nt
    in_specs = [
        pl.BlockSpec((tm, D), lambda i: (i, 0)),
        pl.BlockSpec((tm, D // 2), lambda i: (i, 0)),
        pl.BlockSpec((tm, D // 2), lambda i: (i + nt, 0)),
        pl.BlockSpec((8, tm), lambda i: (0, i)),
        pl.BlockSpec((1, D), const),
        pl.BlockSpec((1, D), const),
    ]
    args = [x1, ypair, ypair, rinfo, g2, b2]
    aliases = {}
    if out_prev is not None:
        in_specs.append(pl.BlockSpec(memory_space=pl.ANY))
        args.append(out_prev)
        aliases = {len(args) - 1: 0}
    return pl.pallas_call(
        _final_kernel,
        grid=(nt,),
        in_specs=in_specs,
        out_specs=pl.BlockSpec((tm, D), lambda i: (i + off, 0)),
        out_shape=jax.ShapeDtypeStruct((n_parts * T, D), jnp.float32),
        input_output_aliases=aliases,
        compiler_params=pltpu.CompilerParams(
            dimension_semantics=("parallel",), vmem_limit_bytes=VMEM_LIMIT),
        name="final",
    )(*args)


def _bias_kernel(tab_ref, sub_ref, idx_ref, o_ref):
    h = pl.program_id(0)
    idx = idx_ref[...]
    acc = jnp.full(idx.shape, NEG_INF, jnp.float32)
    for b in range(REL_BUCKETS):
        acc = jnp.where(idx == b, (tab_ref[h, b] - sub_ref[h]) * LOG2E, acc)
    o_ref[0] = acc


def _bias_expand(table_hb, sub_h, idx):
    H = table_hb.shape[0]
    K, Q = idx.shape
    return pl.pallas_call(
        _bias_kernel,
        grid_spec=pltpu.PrefetchScalarGridSpec(
            num_scalar_prefetch=2,
            grid=(H,),
            in_specs=[pl.BlockSpec((K, Q), lambda h, t, s: (0, 0))],
            out_specs=pl.BlockSpec((1, K, Q), lambda h, t, s: (h, 0, 0)),
        ),
        out_shape=jax.ShapeDtypeStruct((H, K, Q), jnp.float32),
        compiler_params=pltpu.CompilerParams(dimension_semantics=("parallel",)),
        name="bias",
    )(table_hb, sub_h, jnp.asarray(idx))


def _attention_bias_tables(rel_bias_table):
    rel = rel_bias_table.astype(jnp.float32)
    tab_a = rel[:, :SWA_Q_HEADS].T
    tab_b = rel[:, SWA_Q_HEADS:].T
    kj = np.arange(2 * SWA_BLOCK)[:, None]
    qi = np.arange(SWA_BLOCK)[None, :]
    dist = SWA_BLOCK + qi - kj
    idx_a = np.where((dist >= 0) & (dist < SWA_WINDOW), _rel_bucket_np(dist), -1).astype(np.int32)
    bias_a = _bias_expand(tab_a, jnp.zeros((SWA_Q_HEADS,), jnp.float32), idx_a)
    bias_a = bias_a.reshape(SWA_Q_HEADS // 2, 2, 2 * SWA_BLOCK, SWA_BLOCK).transpose(0, 2, 1, 3)
    bias_a = bias_a.reshape(SWA_Q_HEADS // 2, 2 * SWA_BLOCK, 2 * SWA_BLOCK)
    j = np.arange(MOBA_BLOCK)[:, None]
    i = np.arange(MOBA_BLOCK)[None, :]
    d_own = i - j
    idx_own = np.where(d_own >= 0, _rel_bucket_np(d_own), -1)
    idx_prev = _rel_bucket_np(MOBA_BLOCK + i - j)
    idx_b = np.concatenate([idx_prev, idx_own], axis=0).astype(np.int32)
    bias_b = _bias_expand(tab_b, tab_b[:, REL_BUCKETS - 1], idx_b)
    return bias_a, bias_b


def _block_plan(sizes, n_tok):
    n_assign = n_tok * EXPERT_TOPK
    padded = ((sizes + MOE_TM - 1) // MOE_TM) * MOE_TM
    eid = jnp.arange(N_EXPERTS, dtype=jnp.int32)
    padded_end = jnp.sum(jnp.where(eid[None, :] <= eid[:, None], padded[None, :], 0), axis=1)
    padded_start = padded_end - padded
    cap = -(-n_assign // MOE_TM) * MOE_TM + N_EXPERTS * MOE_TM
    blk_start = jnp.arange(cap // MOE_TM, dtype=jnp.int32) * MOE_TM
    used = blk_start < padded_end[-1]
    nonempty = sizes > 0
    blk_expert = jnp.where(used, jnp.sum(padded_end[None, :] <= blk_start[:, None], axis=1),
                           jnp.max(jnp.where(nonempty, eid, 0))).astype(jnp.int32)
    mine = blk_expert[:, None] == eid[None, :]
    size_b = jnp.sum(jnp.where(mine, sizes[None, :], 0), axis=1)
    start_b = jnp.sum(jnp.where(mine, padded_start[None, :], 0), axis=1)
    blk_valid = jnp.where(used, jnp.clip(size_b - (blk_start - start_b), 0, MOE_TM), 0).astype(jnp.int32)
    blk_first = (used & (blk_start == start_b)).astype(jnp.int32)
    earlier = nonempty[None, :] & (eid[None, :] < blk_expert[:, None])
    blk_slot = (jnp.sum(earlier, axis=1) % 2).astype(jnp.int32)
    later = jnp.where(nonempty[None, :] & (eid[None, :] > eid[:, None]), eid[None, :], N_EXPERTS)
    next_of = jnp.min(later, axis=1)
    blk_next = jnp.sum(jnp.where(mine, jnp.where(next_of < N_EXPERTS, next_of, -1)[None, :], 0), axis=1).astype(jnp.int32)
    return padded_start, (blk_expert, blk_valid, blk_first, blk_slot, blk_next), cap


def kernel(x, w_in, b_in, attn_sinks, rel_bias_table, w_branch_swa, w_branch_moba, w_out, ln1_gain, ln1_bias,
           w_group_router, b_group_router, w_expert_router, b_expert_router, w_expert_gate, w_expert_up,
           w_expert_down, ln2_gain, ln2_bias):
    assert w_in.shape[0] == DEPTH == 1
    B, S, D = x.shape
    T = B * S
    bf16 = jnp.bfloat16
    f32 = jnp.float32
    w = w_in[0]
    b = b_in[0]

    def cols(off, width):
        return w[:, off:off + width], b[off:off + width]

    wq_a, bq_a = cols(OFF_SWA_Q, SWA_Q_W)
    wk_a, bk_a = cols(OFF_SWA_K, SWA_KV_W)
    wv_a, bv_a = cols(OFF_SWA_V, SWA_KV_W)
    wq_b, bq_b = cols(OFF_MOBA_Q, MOBA_W)
    wk_b, bk_b = cols(OFF_MOBA_K, MOBA_W)
    wv_b, bv_b = cols(OFF_MOBA_V, MOBA_W)

    def dup_kv(t):
        parts = [t[..., i * HEAD_DIM:(i + 1) * HEAD_DIM] for i in range(SWA_KV_HEADS)]
        return jnp.concatenate([p for p in parts for _ in range(2)], axis=-1)

    qs = ATTN_SCALE * LOG2E
    wn = jnp.concatenate([wq_a * qs, dup_kv(wk_a), wq_b * qs, wk_b], axis=1).astype(bf16)
    bn = jnp.concatenate([bq_a * qs, dup_kv(bk_a), bq_b * qs, bk_b])[None, :].astype(f32)
    wt = jnp.concatenate([wv_a, wv_b], axis=1).T.astype(bf16)
    bt = jnp.concatenate([bv_a, bv_b])[:, None].astype(f32)

    qk, vt, eg_b, eu_b, ed_b = _inproj(x, wn, bn, wt, bt, w_expert_gate[0], w_expert_up[0], w_expert_down[0])

    bias_a, bias_b = _attention_bias_tables(rel_bias_table)
    y_a = _swa(attn_sinks[0].astype(f32), qk, vt, bias_a)
    y_b = _moba(qk, vt, bias_b)

    wg, bg = cols(OFF_GATE, 2 * D_MODEL)
    wr = jnp.zeros((ROUTER_ROWS, D), f32)
    wr = wr.at[0:N_GROUPS].set(w_group_router[0].T).at[8:8 + N_EXPERTS].set(w_expert_router[0].T)
    wr_hi = wr.astype(bf16)
    wr_lo = (wr - wr_hi.astype(f32)).astype(bf16)
    br = jnp.zeros((ROUTER_ROWS,), f32)
    br = br.at[0:N_GROUPS].set(b_group_router[0]).at[8:8 + N_EXPERTS].set(b_expert_router[0])[:, None]
    merge_args = (
        x.reshape(T, D), y_a, y_b,
        wg.astype(bf16), bg[None, :].astype(f32), w_branch_swa[0].astype(bf16), w_branch_moba[0].astype(bf16),
        w_out[0].astype(bf16), ln1_gain[0][None, :].astype(f32), ln1_bias[0][None, :].astype(f32),
        jnp.concatenate([wr_hi, wr_lo], axis=0), br)
    g2 = ln2_gain[0][None, :].astype(f32)
    b2 = ln2_bias[0][None, :].astype(f32)

    Tp = T // MOE_PARTS
    out = None
    for part in range(MOE_PARTS):
        x1, x1p, rinfo, counts = _merge(*merge_args, part, MOE_PARTS)
        sizes = counts[:, 0].astype(jnp.int32)
        padded_start, plan, cap = _block_plan(sizes, Tp)
        dest = _dest(rinfo, padded_start)
        xs = _sc_scatter_rows(x1p, dest[0], dest[1], cap)
        y_buf = _experts(plan, xs, eg_b, eu_b, ed_b)
        ypair = _sc_gather_rows(y_buf, dest[0:EXPERT_TOPK].reshape(-1))
        out = _final(x1, ypair, rinfo, g2, b2, out, part, MOE_PARTS)
    return out.reshape(B, S, D)
```

```python
import functools
import math

import numpy as np
import jax
import jax.numpy as jnp
from jax import lax
from jax.experimental import pallas as pl
from jax.experimental.pallas import tpu as pltpu
from jax.experimental.pallas import tpu_sc as plsc

D_MODEL = 1024
HEAD_DIM = 64
SWA_Q_HEADS = 8
SWA_KV_HEADS = 2
SWA_GROUP = SWA_Q_HEADS // SWA_KV_HEADS
SWA_WINDOW = 128
SWA_BLOCK = 128
MOBA_HEADS = 8
MOBA_BLOCK = 256
MOBA_TOPK = 3
MOBA_LOOKAHEAD = 2
MOBA_UNROLL = 8
BF16_SUBLANES = 16
MOBA_VROWS = HEAD_DIM + BF16_SUBLANES
REL_BUCKETS = 32
REL_MAX_DIST = 128
N_GROUPS = 4
EXPERTS_PER_GROUP = 8
N_EXPERTS = N_GROUPS * EXPERTS_PER_GROUP
EXPERT_TOPK = 2
D_EXPERT = 512
LN_EPS = 1e-5
DEPTH = 1
DEEPNORM_ALPHA = (2.0 * DEPTH) ** 0.25
NEG_INF = -1e30
ATTN_SCALE = HEAD_DIM ** -0.5
LOG2E = math.log2(math.e)

SWA_Q_W = SWA_Q_HEADS * HEAD_DIM
SWA_KV_W = SWA_KV_HEADS * HEAD_DIM
MOBA_W = MOBA_HEADS * HEAD_DIM
OFF_SWA_Q = 0
OFF_SWA_K = OFF_SWA_Q + SWA_Q_W
OFF_SWA_V = OFF_SWA_K + SWA_KV_W
OFF_MOBA_Q = OFF_SWA_V + SWA_KV_W
OFF_MOBA_K = OFF_MOBA_Q + MOBA_W
OFF_MOBA_V = OFF_MOBA_K + MOBA_W
OFF_GATE = OFF_MOBA_V + MOBA_W

LANES = 128
QK_SWA_Q = 0
QK_SWA_K = QK_SWA_Q + SWA_Q_W
QK_MOBA_Q = QK_SWA_K + SWA_KV_HEADS * LANES
QK_MOBA_K = QK_MOBA_Q + MOBA_W
QK_W = QK_MOBA_K + MOBA_W
VT_SWA = 0
VT_MOBA = VT_SWA + SWA_KV_W
VT_W = VT_MOBA + MOBA_W

MOE_TM = 512
MOE_PARTS = 2
SC_CHUNK = 128
ROUTER_ROWS = 8 + N_EXPERTS
VMEM_LIMIT = 56 * 1024 * 1024

_NT = (((1,), (1,)), ((), ()))
_TN = (((0,), (0,)), ((), ()))


def _rel_bucket_np(dist):
    n = np.maximum(dist, 0)
    max_exact = REL_BUCKETS // 2
    nf = np.maximum(n, 1).astype(np.float32)
    large = max_exact + (np.log(nf / np.float32(max_exact)) / np.float32(math.log(REL_MAX_DIST / max_exact))
                         * np.float32(REL_BUCKETS - max_exact)).astype(np.int32)
    large = np.minimum(large, REL_BUCKETS - 1)
    return np.where(n < max_exact, n, large).astype(np.int32)


def _inproj_kernel(x_ref, wn_ref, bn_ref, wt_ref, bt_ref, eg_ref, eu_ref, ed_ref,
                   qk_ref, vt_ref, egb_ref, eub_ref, edb_ref):
    xb = x_ref[0].astype(jnp.bfloat16)
    qk = jnp.dot(xb, wn_ref[...], preferred_element_type=jnp.float32) + bn_ref[...]
    qk_ref[0] = qk.astype(jnp.bfloat16)
    vt = lax.dot_general(wt_ref[...], xb, _NT, preferred_element_type=jnp.float32) + bt_ref[...]
    vt_ref[0] = vt.astype(jnp.bfloat16)
    egb_ref[...] = eg_ref[...].astype(jnp.bfloat16)
    eub_ref[...] = eu_ref[...].astype(jnp.bfloat16)
    edb_ref[...] = ed_ref[...].astype(jnp.bfloat16)


def _inproj(x, wn, bn, wt, bt, w_gate, w_up, w_down, tm=1024):
    B, S, D = x.shape
    nt = S // tm
    steps = B * nt

    def sliced(w):
        E, R, C = w.shape
        assert (E * R) % (steps * BF16_SUBLANES) == 0
        return w.reshape(steps, E * R // steps, C)

    eg, eu, ed = sliced(w_gate), sliced(w_up), sliced(w_down)
    espec = lambda w: pl.BlockSpec((1,) + w.shape[1:], lambda b, i: (b * nt + i, 0, 0))
    qk, vt, egb, eub, edb = pl.pallas_call(
        _inproj_kernel,
        grid=(B, nt),
        in_specs=[
            pl.BlockSpec((1, tm, D), lambda b, i: (b, i, 0)),
            pl.BlockSpec((D, QK_W), lambda b, i: (0, 0)),
            pl.BlockSpec((1, QK_W), lambda b, i: (0, 0)),
            pl.BlockSpec((VT_W, D), lambda b, i: (0, 0)),
            pl.BlockSpec((VT_W, 1), lambda b, i: (0, 0)),
            espec(eg), espec(eu), espec(ed),
        ],
        out_specs=[
            pl.BlockSpec((1, tm, QK_W), lambda b, i: (b, i, 0)),
            pl.BlockSpec((1, VT_W, tm), lambda b, i: (b, 0, i)),
            espec(eg), espec(eu), espec(ed),
        ],
        out_shape=[
            jax.ShapeDtypeStruct((B, S, QK_W), jnp.bfloat16),
            jax.ShapeDtypeStruct((B, VT_W, S), jnp.bfloat16),
            jax.ShapeDtypeStruct(eg.shape, jnp.bfloat16),
            jax.ShapeDtypeStruct(eu.shape, jnp.bfloat16),
            jax.ShapeDtypeStruct(ed.shape, jnp.bfloat16),
        ],
        compiler_params=pltpu.CompilerParams(
            dimension_semantics=("parallel", "parallel"), vmem_limit_bytes=VMEM_LIMIT),
        name="inproj",
    )(x, wn, bn, wt, bt, eg, eu, ed)
    return qk, vt, egb.reshape(w_gate.shape), eub.reshape(w_up.shape), edb.reshape(w_down.shape)


def _expand_bias(tab_ref, head, sub, idx):
    acc = jnp.full(idx.shape, NEG_INF, jnp.float32)
    for b in range(REL_BUCKETS):
        acc = jnp.where(idx == b, (tab_ref[head, b] - sub) * LOG2E, acc)
    return acc


def _swa_kernel(sink_ref, tab_ref, q_ref, k_ref, vt_ref, idx_ref, o_ref, vaug_ref, bias_ref, *s_bufs):
    S = q_ref.shape[1]
    nblk = S // SWA_BLOCK
    half = HEAD_DIM

    @pl.when(pl.program_id(0) == 0)
    def _():
        for h in range(SWA_Q_HEADS):
            bias_ref[h // 2, :, (h % 2) * SWA_BLOCK:(h % 2 + 1) * SWA_BLOCK] = _expand_bias(tab_ref, h, 0.0, idx_ref[...])

    vt = vt_ref[0]
    rows = lax.broadcasted_iota(jnp.int32, vt.shape, 0)
    one = jnp.ones_like(vt)
    vaug_ref[0] = jnp.where(rows < half, vt, one)
    vaug_ref[1] = jnp.where(rows < half, one, vt)
    lane = lax.broadcasted_iota(jnp.int32, (SWA_BLOCK, LANES), 1)
    col = lax.broadcasted_iota(jnp.int32, (1, 2 * SWA_BLOCK), 1)
    npair = SWA_Q_HEADS // 2

    def expand(specs):
        return [(qs, ks, nk, boff, pair, (2 * pair) // SWA_GROUP) for qs, ks, nk, boff in specs for pair in range(npair)]

    def score(specs):
        ss = []
        for qs, ks, nk, boff, pair, kv in expand(specs):
            qblk = q_ref[0, pl.ds(qs, SWA_BLOCK), pair * LANES:(pair + 1) * LANES]
            zero = jnp.zeros_like(qblk)
            q2 = jnp.concatenate([jnp.where(lane < half, qblk, zero), jnp.where(lane >= half, qblk, zero)], axis=0)
            kblk = k_ref[0, pl.ds(ks, nk), kv * LANES:(kv + 1) * LANES]
            s = lax.dot_general(kblk, q2, _NT, preferred_element_type=jnp.float32)
            ss.append(s + bias_ref[pair, boff:boff + nk, :])
        return ss

    def finish(specs, ss):
        items = expand(specs)
        stats = []
        for (qs, ks, nk, boff, pair, kv), s in zip(items, ss):
            sink = jnp.where(col < SWA_BLOCK, sink_ref[2 * pair], sink_ref[2 * pair + 1]) * LOG2E
            m = jnp.maximum(_colmax(s), sink)
            stats.append((jnp.exp2(s - m).astype(jnp.bfloat16), jnp.exp2(sink - m)))
        rs = [jnp.dot(vaug_ref[kv, :, pl.ds(ks, nk)], p, preferred_element_type=jnp.float32)
              for (qs, ks, nk, boff, pair, kv), (p, _) in zip(items, stats)]
        outs = []
        for (qs, ks, nk, boff, pair, kv), (_, esink), r in zip(items, stats, rs):
            l = (r[half:half + 1, :] if kv == 0 else r[0:1, :]) + esink
            o = (r[0:half, :] if kv == 0 else r[half:, :]) / l
            outs.append(jnp.concatenate([o[:, :SWA_BLOCK], o[:, SWA_BLOCK:]], axis=0))
        for i, (qs, _, _, _) in enumerate(specs):
            ot = jnp.concatenate(outs[i * npair:(i + 1) * npair], axis=0)
            o_ref[0, :, pl.ds(qs, SWA_BLOCK)] = ot.astype(jnp.bfloat16)

    first = [(0, 0, SWA_BLOCK, SWA_BLOCK), (SWA_BLOCK, 0, 2 * SWA_BLOCK, 0)]
    finish(first, score(first))

    assert nblk % 2 == 0
    n_groups = nblk // 2 - 1
    nbuf = len(s_bufs)

    def group_specs(g):
        specs = []
        for n in (2 * g + 2, 2 * g + 3):
            qs = pl.multiple_of(n * SWA_BLOCK, SWA_BLOCK)
            specs.append((qs, pl.multiple_of(qs - SWA_BLOCK, SWA_BLOCK), 2 * SWA_BLOCK, 0))
        return specs

    def score_to(g, buf):
        for k, s in enumerate(score(group_specs(g))):
            buf[k] = s

    def finish_from(g, buf):
        finish(group_specs(g), [buf[k] for k in range(2 * npair)])

    for g in range(min(2, n_groups)):
        score_to(g, s_bufs[g])

    def body(t, carry):
        for j in range(nbuf):
            score_to(nbuf * t + 2 + j, s_bufs[(2 + j) % nbuf])
            finish_from(nbuf * t + j, s_bufs[j])
        return carry

    n_trips = max(n_groups - 2, 0) // nbuf
    lax.fori_loop(0, n_trips, body, 0)
    for g in range(nbuf * n_trips, n_groups):
        if g + 2 < n_groups:
            score_to(g + 2, s_bufs[(g + 2) % nbuf])
        finish_from(g, s_bufs[g % nbuf])


def _swa(sinks, table_hb, qk, vt):
    B, S, _ = qk.shape
    kj = np.arange(2 * SWA_BLOCK)[:, None]
    qi = np.arange(SWA_BLOCK)[None, :]
    dist = SWA_BLOCK + qi - kj
    idx = np.where((dist >= 0) & (dist < SWA_WINDOW), _rel_bucket_np(dist), -1).astype(np.int32)
    return pl.pallas_call(
        _swa_kernel,
        grid_spec=pltpu.PrefetchScalarGridSpec(
            num_scalar_prefetch=2,
            grid=(B,),
            in_specs=[
                pl.BlockSpec((1, S, SWA_Q_W), lambda b, s, t: (b, 0, QK_SWA_Q // SWA_Q_W)),
                pl.BlockSpec((1, S, 2 * LANES), lambda b, s, t: (b, 0, QK_SWA_K // (2 * LANES))),
                pl.BlockSpec((1, SWA_KV_W, S), lambda b, s, t: (b, VT_SWA // SWA_KV_W, 0)),
                pl.BlockSpec((2 * SWA_BLOCK, SWA_BLOCK), lambda b, s, t: (0, 0)),
            ],
            out_specs=pl.BlockSpec((1, SWA_Q_W, S), lambda b, s, t: (b, 0, 0)),
            scratch_shapes=[pltpu.VMEM((SWA_KV_HEADS, SWA_KV_W, S), jnp.bfloat16),
                            pltpu.VMEM((SWA_Q_HEADS // 2, 2 * SWA_BLOCK, 2 * SWA_BLOCK), jnp.float32)]
            + [pltpu.VMEM((SWA_Q_HEADS, 2 * SWA_BLOCK, 2 * SWA_BLOCK), jnp.float32)] * 4,
        ),
        out_shape=jax.ShapeDtypeStruct((B, SWA_Q_W, S), jnp.bfloat16),
        compiler_params=pltpu.CompilerParams(
            dimension_semantics=("arbitrary",), vmem_limit_bytes=VMEM_LIMIT),
        name="swa",
    )(sinks, table_hb, qk, qk, vt, jnp.asarray(idx))


def _colmax(s):
    while s.shape[0] > 8:
        h = s.shape[0] // 2
        s = jnp.maximum(s[:h], s[h:])
    return jnp.max(s, axis=0, keepdims=True)


def _moba_kernel(fqb_ref, fc_ref, tab_ref, q_ref, k_ref, vt_ref, idx_ref, o_ref,
                 vaug_ref, gate_ref, rank_ref, sel_ref, far_ref, m_scr, acc_scr, sa_scr, sb_scr, sc_scr, sd_scr,
                 bias_all):
    S = q_ref.shape[1]
    nb = S // MOBA_BLOCK
    half = HEAD_DIM
    BLK = MOBA_BLOCK
    pair = pl.program_id(1)
    bias_ref = bias_all.at[pair]

    @pl.when(pl.program_id(0) == 0)
    def _():
        for h in range(2):
            head = 2 * pair + h
            bias_ref[h] = _expand_bias(tab_ref, head, tab_ref[head, REL_BUCKETS - 1], idx_ref[...])

    ones_rows = jnp.ones((MOBA_VROWS - half, S), jnp.bfloat16)
    for h in range(2):
        vaug_ref[h, 0:half, :] = vt_ref[0, h * half:(h + 1) * half, :]
        vaug_ref[h, half:, :] = ones_rows

    kf = k_ref[0].astype(jnp.float32).reshape(nb, BLK, LANES)
    kmean = jnp.sum(kf, axis=1) * (1.0 / BLK)
    k_hi = kmean.astype(jnp.bfloat16)
    k_lo = (kmean - k_hi.astype(jnp.float32)).astype(jnp.bfloat16)
    kcat = jnp.concatenate([k_hi, k_lo], axis=0)

    lane_q = lax.broadcasted_iota(jnp.int32, (S, LANES), 1)
    brow = lax.broadcasted_iota(jnp.int32, (nb, S), 0)
    qblk_of = lax.broadcasted_iota(jnp.int32, (nb, S), 1) // BLK
    qall = q_ref[0]
    for h in range(2):
        qh = jnp.where((lane_q < half) if h == 0 else (lane_q >= half), qall, jnp.zeros_like(qall))
        g2 = lax.dot_general(kcat, qh, _NT, preferred_element_type=jnp.float32)
        gate_ref[...] = g2[0:nb] + g2[nb:2 * nb]
        rank_ref[...] = jnp.zeros_like(rank_ref)
        for m in range(nb - 1):
            lo = (m + 1) * BLK
            G = gate_ref[:, lo:]
            gm = gate_ref[m:m + 1, lo:]
            ge = jnp.where(gm >= G, 1.0, 0.0)
            gt = jnp.where(gm > G, 1.0, 0.0)
            brow_m = lax.broadcasted_iota(jnp.int32, (nb, S - lo), 0)
            rank_ref[:, lo:] += jnp.where(brow_m > m, ge, gt)
        top = rank_ref[...] < float(MOBA_TOPK)
        sel_ref[h] = jnp.where((brow < qblk_of) & top, 0.0, NEG_INF)
        far_ref[h] = jnp.where((brow < qblk_of - 1) & top, 0.0, NEG_INF)

    lane = lax.broadcasted_iota(jnp.int32, (BLK, LANES), 1)

    def head_q(qs, h):
        qblk = q_ref[0, pl.ds(qs, BLK), :]
        return jnp.where((lane < half) if h == 0 else (lane >= half), qblk, jnp.zeros_like(qblk))

    m0 = jnp.full((1, BLK), NEG_INF, jnp.float32)
    acc0 = jnp.zeros((MOBA_VROWS, BLK), jnp.float32)

    def item(h, qb, qs, ks, nk, bias, sel_a, sel_b, m, acc):
        return dict(h=h, qb=qb, qs=qs, ks=ks, nk=nk, bias=bias, sel_a=sel_a, sel_b=sel_b, m=m, acc=acc)

    def near_item(qb, h):
        qs = pl.multiple_of(qb * BLK, BLK)
        ps = pl.multiple_of(qs - BLK, BLK)
        sel_prev = sel_ref[h, pl.ds(qb - 1, 1), pl.ds(qs, BLK)]
        return item(h, qb, qs, ps, 2 * BLK, bias_ref[h], sel_prev, None, m0, acc0)

    def far_item(i, h):
        qb = fqb_ref[i]
        c = fc_ref[i]
        qs = pl.multiple_of(qb * BLK, BLK)
        ks = pl.multiple_of(c * (2 * BLK), 2 * BLK)
        sel_a = far_ref[h, pl.ds(2 * c, 1), pl.ds(qs, BLK)]
        sel_b = far_ref[h, pl.ds(2 * c + 1, 1), pl.ds(qs, BLK)]
        return item(h, qb, qs, ks, 2 * BLK, None, sel_a, sel_b, m_scr[h, qb], acc_scr[h, qb])

    def scores(it):
        kslab = k_ref[0, pl.ds(it["ks"], it["nk"]), :]
        s = lax.dot_general(kslab, head_q(it["qs"], it["h"]), _NT,
                            preferred_element_type=jnp.float32)
        return s if it["bias"] is None else s + it["bias"]

    def probs(it, s):
        m, sel_a, sel_b = it["m"], it["sel_a"], it["sel_b"]
        if it["nk"] == BLK:
            m_new = jnp.maximum(m, _colmax(s))
            p = jnp.exp2(s - m_new)
        else:
            cm_a = _colmax(s[:BLK]) + sel_a
            cm_b = _colmax(s[BLK:])
            if sel_b is not None:
                cm_b = cm_b + sel_b
            m_new = jnp.maximum(m, jnp.maximum(cm_a, cm_b))
            p_a = jnp.exp2(s[:BLK] - (m_new - sel_a))
            p_b = jnp.exp2(s[BLK:] - (m_new if sel_b is None else m_new - sel_b))
            p = jnp.concatenate([p_a, p_b], axis=0)
        return m_new, jnp.exp2(m - m_new), p.astype(jnp.bfloat16)

    def run(items):
        ss = [scores(it) for it in items]
        ps = [probs(it, s) for it, s in zip(items, ss)]
        pvs = [jnp.dot(vaug_ref[it["h"], :, pl.ds(it["ks"], it["nk"])], p, preferred_element_type=jnp.float32)
               for it, (_, _, p) in zip(items, ps)]
        for it, (m_new, alpha, _), pv in zip(items, ps, pvs):
            m_scr[it["h"], it["qb"]] = m_new
            acc_scr[it["h"], it["qb"]] = it["acc"] * alpha + pv

    run([item(h, 0, 0, 0, BLK, bias_ref[h, BLK:, :], None, None, m0, acc0) for h in range(2)]
        + [near_item(nb - 1, h) for h in range(2)])

    bufs = (sa_scr, sb_scr, sc_scr, sd_scr)
    nbuf = len(bufs)

    def finish(items, s_buf):
        ps = [probs(it, s_buf[k]) for k, it in enumerate(items)]
        pvs = [jnp.dot(vaug_ref[it["h"], :, pl.ds(it["ks"], it["nk"])], p, preferred_element_type=jnp.float32)
               for it, (_, _, p) in zip(items, ps)]
        for it, (m_new, alpha, _), pv in zip(items, ps, pvs):
            m_scr[it["h"], it["qb"]] = m_new
            acc_scr[it["h"], it["qb"]] = it["acc"] * alpha + pv

    def pipelined(n_groups, score, items_of):
        for g in range(min(MOBA_LOOKAHEAD, n_groups)):
            score(g, bufs[g])

        def body(t, carry):
            for j in range(MOBA_UNROLL):
                score(MOBA_UNROLL * t + MOBA_LOOKAHEAD + j, bufs[(MOBA_LOOKAHEAD + j) % nbuf])
                finish(items_of(MOBA_UNROLL * t + j), bufs[j % nbuf])
            return carry

        n_trips = max(n_groups - MOBA_LOOKAHEAD, 0) // MOBA_UNROLL
        lax.fori_loop(0, n_trips, body, 0)
        for g in range(MOBA_UNROLL * n_trips, n_groups):
            if g + MOBA_LOOKAHEAD < n_groups:
                score(g + MOBA_LOOKAHEAD, bufs[(g + MOBA_LOOKAHEAD) % nbuf])
            finish(items_of(g), bufs[g % nbuf])

    near_pairs = nb // 2 - 1

    def near_group(g):
        return [(qb, h) for qb in (g + 1, g + 1 + near_pairs) for h in range(2)]

    def near_score(g, s_buf):
        for k, (qb, h) in enumerate(near_group(g)):
            qs = pl.multiple_of(qb * BLK, BLK)
            ps = pl.multiple_of(qs - BLK, BLK)
            s_buf[k] = lax.dot_general(k_ref[0, pl.ds(ps, 2 * BLK), :], head_q(qs, h), _NT,
                                       preferred_element_type=jnp.float32) + bias_ref[h]

    pipelined(near_pairs, near_score, lambda g: [near_item(qb, h) for qb, h in near_group(g)])

    n_far_groups = fqb_ref.shape[0] // 2

    def far_group(g):
        return [(j, h) for j in (g, g + n_far_groups) for h in range(2)]

    def far_score(g, s_buf):
        for k, (j, h) in enumerate(far_group(g)):
            qs = pl.multiple_of(fqb_ref[j] * BLK, BLK)
            ks = pl.multiple_of(fc_ref[j] * (2 * BLK), 2 * BLK)
            s_buf[k] = lax.dot_general(k_ref[0, pl.ds(ks, 2 * BLK), :], head_q(qs, h), _NT,
                                       preferred_element_type=jnp.float32)

    pipelined(n_far_groups, far_score, lambda g: [far_item(j, h) for j, h in far_group(g)])

    def out_body(t, carry):
        for qb in (2 * t, 2 * t + 1):
            qs = pl.multiple_of(qb * BLK, BLK)
            a0 = acc_scr[0, qb]
            a1 = acc_scr[1, qb]
            ot = jnp.concatenate([a0[0:half] / a0[half:half + 1, :], a1[0:half] / a1[half:half + 1, :]],
                                 axis=0)
            o_ref[0, :, pl.ds(qs, BLK)] = ot.astype(jnp.bfloat16)
        return carry

    lax.fori_loop(0, nb // 2, out_body, 0)


def _moba_far_items(nb):
    items = [(qb, c) for qb in range(2, nb) for c in range(qb // 2)]
    n = len(items)
    assert n % 2 == 0 and all(items[i][0] != items[i + n // 2][0] for i in range(n // 2))
    return np.array([it[0] for it in items], np.int32), np.array([it[1] for it in items], np.int32)


def _moba(table_hb, qk, vt):
    B, S, _ = qk.shape
    nb = S // MOBA_BLOCK
    npair = MOBA_W // LANES
    far_qb, far_c = _moba_far_items(nb)
    j = np.arange(MOBA_BLOCK)[:, None]
    i = np.arange(MOBA_BLOCK)[None, :]
    idx_own = np.where(i - j >= 0, _rel_bucket_np(i - j), -1)
    idx = np.concatenate([_rel_bucket_np(MOBA_BLOCK + i - j), idx_own], axis=0).astype(np.int32)
    return pl.pallas_call(
        _moba_kernel,
        grid_spec=pltpu.PrefetchScalarGridSpec(
            num_scalar_prefetch=3,
            grid=(B, npair),
            in_specs=[
                pl.BlockSpec((1, S, LANES), lambda b, p, fq, fc, t: (b, 0, QK_MOBA_Q // LANES + p)),
                pl.BlockSpec((1, S, LANES), lambda b, p, fq, fc, t: (b, 0, QK_MOBA_K // LANES + p)),
                pl.BlockSpec((1, LANES, S), lambda b, p, fq, fc, t: (b, VT_MOBA // LANES + p, 0)),
                pl.BlockSpec((2 * MOBA_BLOCK, MOBA_BLOCK), lambda b, p, fq, fc, t: (0, 0)),
            ],
            out_specs=pl.BlockSpec((1, LANES, S), lambda b, p, fq, fc, t: (b, p, 0)),
            scratch_shapes=[
                pltpu.VMEM((2, MOBA_VROWS, S), jnp.bfloat16),
                pltpu.VMEM((nb, S), jnp.float32),
                pltpu.VMEM((nb, S), jnp.float32),
                pltpu.VMEM((2, nb, S), jnp.float32),
                pltpu.VMEM((2, nb, S), jnp.float32),
                pltpu.VMEM((2, nb, 1, MOBA_BLOCK), jnp.float32),
                pltpu.VMEM((2, nb, MOBA_VROWS, MOBA_BLOCK), jnp.float32),
            ] + [pltpu.VMEM((4, 2 * MOBA_BLOCK, MOBA_BLOCK), jnp.float32)] * 4
            + [pltpu.VMEM((npair, 2, 2 * MOBA_BLOCK, MOBA_BLOCK), jnp.float32)],
        ),
        out_shape=jax.ShapeDtypeStruct((B, MOBA_W, S), jnp.bfloat16),
        compiler_params=pltpu.CompilerParams(
            dimension_semantics=("arbitrary", "arbitrary"), vmem_limit_bytes=VMEM_LIMIT),
        name="moba",
    )(jnp.asarray(far_qb), jnp.asarray(far_c), table_hb, qk, qk, vt, jnp.asarray(idx))


def _layer_norm(h, gain, bias):
    mu = jnp.mean(h, axis=-1, keepdims=True)
    c = h - mu
    var = jnp.mean(c * c, axis=-1, keepdims=True)
    return c * lax.rsqrt(var + LN_EPS) * gain + bias


def _pack_bf16_pair(a, b):
    ia = lax.bitcast_convert_type(a.astype(jnp.bfloat16).astype(jnp.float32), jnp.int32)
    ib = lax.bitcast_convert_type(b.astype(jnp.bfloat16).astype(jnp.float32), jnp.int32)
    return lax.shift_right_logical(ia, 16) | ib


def _unpack_bf16_pair(w):
    lo = lax.bitcast_convert_type(lax.shift_left(w, 16), jnp.float32)
    hi = lax.bitcast_convert_type(w & jnp.int32(-65536), jnp.float32)
    return lo.astype(jnp.bfloat16), hi.astype(jnp.bfloat16)


def _merge_kernel(x_ref, ya_ref, yb_ref, wg_ref, bg_ref, wa_ref, wb_ref, wo_ref, g1_ref, b1_ref,
                  wr_ref, br_ref, tri_ref, x1_ref, x1p_ref, r_ref, cnt_ref):
    @pl.when(pl.program_id(0) == 0)
    def _():
        cnt_ref[...] = jnp.zeros_like(cnt_ref)

    ts = tri_ref.shape[0]
    subs = [pl.ds(r0, ts) for r0 in range(0, x_ref.shape[0], ts)]
    pre = [_merge_matmuls(x_ref[rows, :], ya_ref[0, :, rows], yb_ref[0, :, rows], wg_ref, bg_ref, wa_ref, wb_ref, wo_ref)
           for rows in subs]
    for rows, h in zip(subs, pre):
        _merge_route(h, rows, g1_ref, b1_ref, wr_ref, br_ref, tri_ref, x1_ref, x1p_ref, r_ref, cnt_ref)


def _merge_matmuls(x, ya_t, yb_t, wg_ref, bg_ref, wa_ref, wb_ref, wo_ref):
    xb = x.astype(jnp.bfloat16)
    z = jnp.dot(xb, wg_ref[...], preferred_element_type=jnp.float32) + bg_ref[...]
    gates = 1.0 / (1.0 + jnp.exp(-z))
    pa = lax.dot_general(ya_t, wa_ref[...], _TN, preferred_element_type=jnp.float32)
    pb = lax.dot_general(yb_t, wb_ref[...], _TN, preferred_element_type=jnp.float32)
    merged = gates[:, :D_MODEL] * pa + gates[:, D_MODEL:] * pb
    mixed = jnp.dot(merged.astype(jnp.bfloat16), wo_ref[...], preferred_element_type=jnp.float32)
    return DEEPNORM_ALPHA * x + mixed


def _merge_route(h, rows, g1_ref, b1_ref, wr_ref, br_ref, tri_ref, x1_ref, x1p_ref, r_ref, cnt_ref):
    x1 = _layer_norm(h, g1_ref[...], b1_ref[...])
    x1_ref[rows, :] = x1
    x1_hi = x1.astype(jnp.bfloat16)
    x1p_ref[rows, :] = _pack_bf16_pair(x1[:, :D_MODEL // 2], x1[:, D_MODEL // 2:])
    x1_lo = (x1 - x1_hi.astype(jnp.float32)).astype(jnp.bfloat16)

    R = ROUTER_ROWS
    l1 = lax.dot_general(wr_ref[...], x1_hi, _NT, preferred_element_type=jnp.float32)
    l2 = lax.dot_general(wr_ref[0:R, :], x1_lo, _NT, preferred_element_type=jnp.float32)
    L = l1[0:R] + l1[R:2 * R] + l2 + br_ref[...]
    tm = x1.shape[0]
    row = lax.broadcasted_iota(jnp.int32, (8, tm), 0)
    big = jnp.float32(-3e38)
    gl = jnp.where(row < N_GROUPS, L[0:8], big)
    gmax = jnp.max(gl, axis=0, keepdims=True)
    g_idx = jnp.min(jnp.where(gl == gmax, row, 8), axis=0, keepdims=True)
    gsum = jnp.sum(jnp.where(row < N_GROUPS, jnp.exp(gl - gmax), 0.0), axis=0, keepdims=True)
    g_prob = 1.0 / gsum
    E = L[8 + 8 * (N_GROUPS - 1):8 + 8 * N_GROUPS]
    for g in range(N_GROUPS - 2, -1, -1):
        E = jnp.where(g_idx == g, L[8 + 8 * g:16 + 8 * g], E)
    t0 = jnp.max(E, axis=0, keepdims=True)
    loc0 = jnp.min(jnp.where(E == t0, row, 8), axis=0, keepdims=True)
    E2 = jnp.where(row == loc0, big, E)
    t1 = jnp.max(E2, axis=0, keepdims=True)
    loc1 = jnp.min(jnp.where(E2 == t1, row, 8), axis=0, keepdims=True)
    ex = jnp.exp(t1 - t0)
    w0 = g_prob / (1.0 + ex)
    w1 = g_prob * ex / (1.0 + ex)
    e0i = g_idx * EXPERTS_PER_GROUP + loc0
    e1i = g_idx * EXPERTS_PER_GROUP + loc1

    erow = lax.broadcasted_iota(jnp.int32, (N_EXPERTS, tm), 0)
    oh0 = jnp.where(erow == e0i, 1.0, 0.0)
    oh1 = jnp.where(erow == e1i, 1.0, 0.0)
    both = oh0 + oh1
    prefix = jnp.dot(both.astype(jnp.bfloat16), tri_ref[...], preferred_element_type=jnp.float32)
    prefix = prefix + cnt_ref[:, 0:1]
    rank0 = jnp.sum(oh0 * prefix, axis=0, keepdims=True)
    rank1 = jnp.sum(oh1 * prefix, axis=0, keepdims=True)
    cnt_ref[...] = cnt_ref[...] + jnp.sum(both, axis=1, keepdims=True)

    vals = (e0i.astype(jnp.float32), e1i.astype(jnp.float32), w0, w1, rank0, rank1)
    out = jnp.zeros((8, tm), jnp.float32)
    for k, v in enumerate(vals):
        out = jnp.where(row == k, v, out)
    r_ref[:, rows] = out


def _merge(x2, ya, yb, wg, bg, wa, wb, wo, g1, b1, wr, br, part, n_parts, tm=1024, ts=256):
    D = x2.shape[1]
    T = x2.shape[0] // n_parts
    S = ya.shape[2]
    per_seq = S // tm
    assert per_seq * tm == S
    off = part * (T // tm)
    const = lambda i: (0, 0)
    seq_tile = lambda i: ((i + off) // per_seq, 0, (i + off) % per_seq)
    tri = jnp.triu(jnp.ones((ts, ts), jnp.bfloat16), k=1)
    return pl.pallas_call(
        _merge_kernel,
        grid=(T // tm,),
        in_specs=[
            pl.BlockSpec((tm, D), lambda i: (i + off, 0)),
            pl.BlockSpec((1, SWA_Q_W, tm), seq_tile),
            pl.BlockSpec((1, MOBA_W, tm), seq_tile),
            pl.BlockSpec((D, 2 * D), const),
            pl.BlockSpec((1, 2 * D), const),
            pl.BlockSpec((SWA_Q_W, D), const),
            pl.BlockSpec((MOBA_W, D), const),
            pl.BlockSpec((D, D), const),
            pl.BlockSpec((1, D), const),
            pl.BlockSpec((1, D), const),
            pl.BlockSpec((2 * ROUTER_ROWS, D), const),
            pl.BlockSpec((ROUTER_ROWS, 1), const),
            pl.BlockSpec((ts, ts), const),
        ],
        out_specs=[
            pl.BlockSpec((tm, D), lambda i: (i, 0)),
            pl.BlockSpec((tm, D // 2), lambda i: (i, 0)),
            pl.BlockSpec((8, tm), lambda i: (0, i)),
            pl.BlockSpec((N_EXPERTS, LANES), const),
        ],
        out_shape=[
            jax.ShapeDtypeStruct((T, D), jnp.float32),
            jax.ShapeDtypeStruct((T, D // 2), jnp.int32),
            jax.ShapeDtypeStruct((8, T), jnp.float32),
            jax.ShapeDtypeStruct((N_EXPERTS, LANES), jnp.float32),
        ],
        compiler_params=pltpu.CompilerParams(
            dimension_semantics=("arbitrary",), vmem_limit_bytes=VMEM_LIMIT),
        name="merge",
    )(x2, ya, yb, wg, bg, wa, wb, wo, g1, b1, wr, br, tri)


def _dest_kernel(r_ref, ps_ref, d_ref):
    tt = r_ref.shape[1]
    erow = lax.broadcasted_iota(jnp.int32, (N_EXPERTS, tt), 0)
    row = lax.broadcasted_iota(jnp.int32, (8, tt), 0)
    ps = ps_ref[...]
    out = jnp.zeros((8, tt), jnp.float32)
    for k in range(EXPERT_TOPK):
        e = r_ref[k:k + 1, :].astype(jnp.int32)
        start = jnp.sum(jnp.where(erow == e, ps, 0.0), axis=0, keepdims=True)
        out = jnp.where(row == k, start + r_ref[4 + k:5 + k, :], out)
    d_ref[...] = out.astype(jnp.int32)


def _dest(rinfo, padded_start, tt=8192):
    T = rinfo.shape[1]
    tt = min(tt, T)
    return pl.pallas_call(
        _dest_kernel,
        grid=(T // tt,),
        in_specs=[pl.BlockSpec((8, tt), lambda i: (0, i)), pl.BlockSpec((N_EXPERTS, 1), lambda i: (0, 0))],
        out_specs=pl.BlockSpec((8, tt), lambda i: (0, i)),
        out_shape=jax.ShapeDtypeStruct((8, T), jnp.int32),
        compiler_params=pltpu.CompilerParams(dimension_semantics=("parallel",)),
        name="dest",
    )(rinfo, padded_start.astype(jnp.float32)[:, None])


def _sc_workers():
    info = plsc.get_sparse_core_info()
    return info.num_cores, info.num_subcores


def _sc_scatter_rows(src, dest0, dest1, cap):
    T, W = src.shape
    nc, ns = _sc_workers()
    per_w = T // (nc * ns)
    assert per_w * nc * ns == T and per_w % SC_CHUNK == 0
    mesh = plsc.VectorSubcoreMesh(core_axis_name="c", subcore_axis_name="s")

    @functools.partial(
        pl.kernel, mesh=mesh,
        out_type=jax.ShapeDtypeStruct((cap, W), src.dtype),
        scratch_types=[pltpu.VMEM((SC_CHUNK,), jnp.int32), pltpu.VMEM((SC_CHUNK, W), src.dtype)],
    )
    def scatter(src_hbm, d0_hbm, d1_hbm, out_hbm, idx_v, rows_v):
        wid = lax.axis_index("s") * nc + lax.axis_index("c")
        base = wid * per_w

        @pl.loop(0, per_w // SC_CHUNK)
        def _(c):
            off = pl.multiple_of(base + c * SC_CHUNK, SC_CHUNK)
            pltpu.sync_copy(src_hbm.at[pl.ds(off, SC_CHUNK)], rows_v)
            for d_hbm in (d0_hbm, d1_hbm):
                pltpu.sync_copy(d_hbm.at[pl.ds(off, SC_CHUNK)], idx_v)
                pltpu.sync_copy(rows_v, out_hbm.at[idx_v])

    return scatter(src, dest0, dest1)


def _sc_gather_rows(table, idx):
    N = idx.shape[0]
    W = table.shape[1]
    nc, ns = _sc_workers()
    per_w = N // (nc * ns)
    assert per_w * nc * ns == N and per_w % SC_CHUNK == 0
    mesh = plsc.VectorSubcoreMesh(core_axis_name="c", subcore_axis_name="s")

    @functools.partial(
        pl.kernel, mesh=mesh,
        out_type=jax.ShapeDtypeStruct((N, W), table.dtype),
        scratch_types=[pltpu.VMEM((SC_CHUNK,), jnp.int32), pltpu.VMEM((SC_CHUNK, W), table.dtype)],
    )
    def gather(table_hbm, idx_hbm, out_hbm, idx_v, rows_v):
        wid = lax.axis_index("s") * nc + lax.axis_index("c")
        base = wid * per_w

        @pl.loop(0, per_w // SC_CHUNK)
        def _(c):
            off = pl.multiple_of(base + c * SC_CHUNK, SC_CHUNK)
            pltpu.sync_copy(idx_hbm.at[pl.ds(off, SC_CHUNK)], idx_v)
            pltpu.sync_copy(table_hbm.at[idx_v], rows_v)
            pltpu.sync_copy(rows_v, out_hbm.at[pl.ds(off, SC_CHUNK)])

    return gather(table, idx)


def _expert_kernel(be_ref, nv_ref, first_ref, slot_ref, nxt_ref, x_ref, wg_hbm, wu_hbm, wd_hbm, y_ref,
                   wg_buf, wu_buf, wd_buf, sem):
    i = pl.program_id(0)
    nv = nv_ref[i]
    slot = slot_ref[i]

    def weight_copies(e, s):
        return [pltpu.make_async_copy(hbm.at[e], buf.at[s], sem.at[s, k])
                for k, (hbm, buf) in enumerate(((wg_hbm, wg_buf), (wu_hbm, wu_buf), (wd_hbm, wd_buf)))]

    @pl.when(i == 0)
    def _():
        for c in weight_copies(be_ref[0], 0):
            c.start()

    @pl.when(first_ref[i] == 1)
    def _():
        @pl.when(nxt_ref[i] >= 0)
        def _():
            for c in weight_copies(nxt_ref[i], 1 - slot):
                c.start()

        for c in weight_copies(be_ref[i], slot):
            c.wait()

    @pl.when(nv > 0)
    def _():
        lo, hi = _unpack_bf16_pair(x_ref[...])
        xb = jnp.concatenate([lo, hi], axis=1)
        rows = lax.broadcasted_iota(jnp.int32, xb.shape, 0)
        xb = jnp.where(rows < nv, xb, jnp.zeros_like(xb))
        g = jnp.dot(xb, wg_buf[slot], preferred_element_type=jnp.float32)
        u = jnp.dot(xb, wu_buf[slot], preferred_element_type=jnp.float32)
        act = (g / (1.0 + jnp.exp(-g))) * u
        y = jnp.dot(act.astype(jnp.bfloat16), wd_buf[slot], preferred_element_type=jnp.float32)
        y_ref[...] = _pack_bf16_pair(y[:, :D_MODEL // 2], y[:, D_MODEL // 2:])

    @pl.when(nv <= 0)
    def _():
        y_ref[...] = jnp.zeros_like(y_ref)


def _experts(plan, xs, wg, wu, wd):
    cap, DW = xs.shape
    D = 2 * DW
    n_blocks = cap // MOE_TM
    hbm = pl.BlockSpec(memory_space=pl.ANY)
    return pl.pallas_call(
        _expert_kernel,
        grid_spec=pltpu.PrefetchScalarGridSpec(
            num_scalar_prefetch=5,
            grid=(n_blocks,),
            in_specs=[pl.BlockSpec((MOE_TM, DW), lambda i, *_: (i, 0)), hbm, hbm, hbm],
            out_specs=pl.BlockSpec((MOE_TM, DW), lambda i, *_: (i, 0)),
            scratch_shapes=[
                pltpu.VMEM((2, D, D_EXPERT), jnp.bfloat16),
                pltpu.VMEM((2, D, D_EXPERT), jnp.bfloat16),
                pltpu.VMEM((2, D_EXPERT, D), jnp.bfloat16),
                pltpu.SemaphoreType.DMA((2, 3)),
            ],
        ),
        out_shape=jax.ShapeDtypeStruct((cap, DW), jnp.int32),
        compiler_params=pltpu.CompilerParams(
            dimension_semantics=("arbitrary",), vmem_limit_bytes=VMEM_LIMIT),
        name="experts",
    )(*plan, xs, wg, wu, wd)


def _final_kernel(x1_ref, y0_ref, y1_ref, w_ref, g2_ref, b2_ref, *rest):
    o_ref = rest[-1]
    w = w_ref[...].T
    halves = []
    for part in range(2):
        y0 = _unpack_bf16_pair(y0_ref[...])[part].astype(jnp.float32)
        y1 = _unpack_bf16_pair(y1_ref[...])[part].astype(jnp.float32)
        halves.append(y0 * w[:, 2:3] + y1 * w[:, 3:4])
    moe = jnp.concatenate(halves, axis=1)
    o_ref[...] = _layer_norm(DEEPNORM_ALPHA * x1_ref[...] + moe, g2_ref[...], b2_ref[...])


def _final(x1, ypair, rinfo, g2, b2, out_prev, part, n_parts, tm=1024):
    T, D = x1.shape
    const = lambda i: (0, 0)
    nt = T // tm
    off = part * nt
    in_specs = [
        pl.BlockSpec((tm, D), lambda i: (i, 0)),
        pl.BlockSpec((tm, D // 2), lambda i: (i, 0)),
        pl.BlockSpec((tm, D // 2), lambda i: (i + nt, 0)),
        pl.BlockSpec((8, tm), lambda i: (0, i)),
        pl.BlockSpec((1, D), const),
        pl.BlockSpec((1, D), const),
    ]
    args = [x1, ypair, ypair, rinfo, g2, b2]
    aliases = {}
    if out_prev is not None:
        in_specs.append(pl.BlockSpec(memory_space=pl.ANY))
        args.append(out_prev)
        aliases = {len(args) - 1: 0}
    return pl.pallas_call(
        _final_kernel,
        grid=(nt,),
        in_specs=in_specs,
        out_specs=pl.BlockSpec((tm, D), lambda i: (i + off, 0)),
        out_shape=jax.ShapeDtypeStruct((n_parts * T, D), jnp.float32),
        input_output_aliases=aliases,
        compiler_params=pltpu.CompilerParams(
            dimension_semantics=("parallel",), vmem_limit_bytes=VMEM_LIMIT),
        name="final",
    )(*args)


def _block_plan(sizes, n_tok):
    n_assign = n_tok * EXPERT_TOPK
    padded = ((sizes + MOE_TM - 1) // MOE_TM) * MOE_TM
    eid = jnp.arange(N_EXPERTS, dtype=jnp.int32)
    padded_end = jnp.sum(jnp.where(eid[None, :] <= eid[:, None], padded[None, :], 0), axis=1)
    padded_start = padded_end - padded
    cap = -(-n_assign // MOE_TM) * MOE_TM + N_EXPERTS * MOE_TM
    blk_start = jnp.arange(cap // MOE_TM, dtype=jnp.int32) * MOE_TM
    used = blk_start < padded_end[-1]
    nonempty = sizes > 0
    blk_expert = jnp.where(used, jnp.sum(padded_end[None, :] <= blk_start[:, None], axis=1),
                           jnp.max(jnp.where(nonempty, eid, 0))).astype(jnp.int32)
    mine = blk_expert[:, None] == eid[None, :]
    size_b = jnp.sum(jnp.where(mine, sizes[None, :], 0), axis=1)
    start_b = jnp.sum(jnp.where(mine, padded_start[None, :], 0), axis=1)
    blk_valid = jnp.where(used, jnp.clip(size_b - (blk_start - start_b), 0, MOE_TM), 0).astype(jnp.int32)
    blk_first = (used & (blk_start == start_b)).astype(jnp.int32)
    earlier = nonempty[None, :] & (eid[None, :] < blk_expert[:, None])
    blk_slot = (jnp.sum(earlier, axis=1) % 2).astype(jnp.int32)
    later = jnp.where(nonempty[None, :] & (eid[None, :] > eid[:, None]), eid[None, :], N_EXPERTS)
    next_of = jnp.min(later, axis=1)
    blk_next = jnp.sum(jnp.where(mine, jnp.where(next_of < N_EXPERTS, next_of, -1)[None, :], 0), axis=1).astype(jnp.int32)
    return padded_start, (blk_expert, blk_valid, blk_first, blk_slot, blk_next), cap


def kernel(x, w_in, b_in, attn_sinks, rel_bias_table, w_branch_swa, w_branch_moba, w_out, ln1_gain, ln1_bias,
           w_group_router, b_group_router, w_expert_router, b_expert_router, w_expert_gate, w_expert_up,
           w_expert_down, ln2_gain, ln2_bias):
    assert w_in.shape[0] == DEPTH == 1
    B, S, D = x.shape
    T = B * S
    bf16 = jnp.bfloat16
    f32 = jnp.float32
    w = w_in[0]
    b = b_in[0]

    def cols(off, width):
        return w[:, off:off + width], b[off:off + width]

    wq_a, bq_a = cols(OFF_SWA_Q, SWA_Q_W)
    wk_a, bk_a = cols(OFF_SWA_K, SWA_KV_W)
    wv_a, bv_a = cols(OFF_SWA_V, SWA_KV_W)
    wq_b, bq_b = cols(OFF_MOBA_Q, MOBA_W)
    wk_b, bk_b = cols(OFF_MOBA_K, MOBA_W)
    wv_b, bv_b = cols(OFF_MOBA_V, MOBA_W)

    def dup_kv(t):
        parts = [t[..., i * HEAD_DIM:(i + 1) * HEAD_DIM] for i in range(SWA_KV_HEADS)]
        return jnp.concatenate([p for p in parts for _ in range(2)], axis=-1)

    qs = ATTN_SCALE * LOG2E
    wn = jnp.concatenate([wq_a * qs, dup_kv(wk_a), wq_b * qs, wk_b], axis=1).astype(bf16)
    bn = jnp.concatenate([bq_a * qs, dup_kv(bk_a), bq_b * qs, bk_b])[None, :].astype(f32)
    wt = jnp.concatenate([wv_a, wv_b], axis=1).T.astype(bf16)
    bt = jnp.concatenate([bv_a, bv_b])[:, None].astype(f32)

    qk, vt, eg_b, eu_b, ed_b = _inproj(x, wn, bn, wt, bt, w_expert_gate[0], w_expert_up[0], w_expert_down[0])

    rel = rel_bias_table.astype(f32).T
    y_a = _swa(attn_sinks[0].astype(f32), rel[:SWA_Q_HEADS], qk, vt)
    y_b = _moba(rel[SWA_Q_HEADS:], qk, vt)

    wg, bg = cols(OFF_GATE, 2 * D_MODEL)
    wr = jnp.zeros((ROUTER_ROWS, D), f32)
    wr = wr.at[0:N_GROUPS].set(w_group_router[0].T).at[8:8 + N_EXPERTS].set(w_expert_router[0].T)
    wr_hi = wr.astype(bf16)
    wr_lo = (wr - wr_hi.astype(f32)).astype(bf16)
    br = jnp.zeros((ROUTER_ROWS,), f32)
    br = br.at[0:N_GROUPS].set(b_group_router[0]).at[8:8 + N_EXPERTS].set(b_expert_router[0])[:, None]
    merge_args = (
        x.reshape(T, D), y_a, y_b,
        wg.astype(bf16), bg[None, :].astype(f32), w_branch_swa[0].astype(bf16), w_branch_moba[0].astype(bf16),
        w_out[0].astype(bf16), ln1_gain[0][None, :].astype(f32), ln1_bias[0][None, :].astype(f32),
        jnp.concatenate([wr_hi, wr_lo], axis=0), br)
    g2 = ln2_gain[0][None, :].astype(f32)
    b2 = ln2_bias[0][None, :].astype(f32)

    Tp = T // MOE_PARTS
    out = None
    for part in range(MOE_PARTS):
        x1, x1p, rinfo, counts = _merge(*merge_args, part, MOE_PARTS)
        sizes = counts[:, 0].astype(jnp.int32)
        padded_start, plan, cap = _block_plan(sizes, Tp)
        dest = _dest(rinfo, padded_start)
        xs = _sc_scatter_rows(x1p, dest[0], dest[1], cap)
        y_buf = _experts(plan, xs, eg_b, eu_b, ed_b)
        ypair = _sc_gather_rows(y_buf, dest[0:EXPERT_TOPK].reshape(-1))
        out = _final(x1, ypair, rinfo, g2, b2, out, part, MOE_PARTS)
    return out.reshape(B, S, D)
```
